```python
import jax, jax.numpy as jnp
from jax import lax
import numpy as np

D_MODEL = 1024
BATCH = 1
SEQ = 16384
DEPTH = 1
DEC_BATCH = 128
DEC_SEQ = 4
PAST_LEN = 16384
PAGE_SIZE = 128

N_HEADS = 8
QK_NOPE = 64
QK_ROPE = 32
V_DIM = 64
Q_LORA = 256
KV_LORA = 128
ATTN_WIDTH = N_HEADS * V_DIM
SM_SCALE = (QK_NOPE + QK_ROPE) ** -0.5
ROPE_THETA = 10000.0
Q_BLOCK = 128
CONV_CH = D_MODEL - ATTN_WIDTH
CONV_W = 31
IN_COLS = Q_LORA + KV_LORA + QK_ROPE + 2 * CONV_CH
N_EXPERTS = 32
TOP_K = 4
D_FF = D_MODEL
SWIGLU_LIMIT = 7.0
SWIGLU_ALPHA = 1.702
EPS = 1e-6
NEG_INF = -1e30

kernel_name = 'hymba_mla_conformer_moe_step'


def _rms(x, g):
    xf = x.astype(jnp.float32)
    y = xf * lax.rsqrt(jnp.mean(xf * xf, axis=-1, keepdims=True) + EPS)
    return (y * g.astype(jnp.float32)).astype(x.dtype)


def _layernorm(x, g, b):
    xf = x.astype(jnp.float32)
    mu = jnp.mean(xf, axis=-1, keepdims=True)
    xc = xf - mu
    var = jnp.mean(xc * xc, axis=-1, keepdims=True)
    return (xc * lax.rsqrt(var + EPS) * g.astype(jnp.float32) + b.astype(jnp.float32)).astype(x.dtype)


def _rope_tables(pos):
    inv = ROPE_THETA ** (-jnp.arange(0, QK_ROPE, 2, dtype=jnp.float32) / QK_ROPE)
    ang = pos.astype(jnp.float32)[:, None] * inv[None, :]
    return jnp.cos(ang), jnp.sin(ang)


def _rope(x, cos, sin):
    x1, x2 = jnp.split(x.astype(jnp.float32), 2, axis=-1)
    return jnp.concatenate([x1 * cos - x2 * sin, x1 * sin + x2 * cos], axis=-1).astype(x.dtype)


def _adaln(c, w_ada, b_ada):
    mod = jax.nn.silu(c) @ w_ada + b_ada
    return jnp.split(mod[:, None, :], 6, axis=-1)


def _mixer_inputs(h, pos, w_in, g_q_lat, w_q_up, g_q_nope, g_q_rope, g_kv_lat, g_k_rope):
    b, t = h.shape[:2]
    proj = h @ w_in
    q_lat, c_kv, k_pe, u = jnp.split(proj, [Q_LORA, Q_LORA + KV_LORA, Q_LORA + KV_LORA + QK_ROPE], axis=-1)
    q = (_rms(q_lat, g_q_lat) @ w_q_up).reshape(b, t, N_HEADS, QK_NOPE + QK_ROPE)
    cos, sin = _rope_tables(pos)
    q_nope = _rms(q[..., :QK_NOPE], g_q_nope)
    q_pe = _rope(_rms(q[..., QK_NOPE:], g_q_rope), cos[:, None, :], sin[:, None, :])
    c_kv = _rms(c_kv, g_kv_lat)
    k_pe = _rope(_rms(k_pe, g_k_rope), cos, sin)
    u_a, u_b = jnp.split(u, 2, axis=-1)
    return q_nope, q_pe, c_kv, k_pe, u_a * jax.nn.sigmoid(u_b)


def _expand_kv(c_kv, w_kv_up, g_k_nope):
    kv = jnp.einsum('btc,chd->bthd', c_kv, w_kv_up)
    return _rms(kv[..., :QK_NOPE], g_k_nope), kv[..., QK_NOPE:]


def _attend(q_nope, q_pe, q_pos, k_nope, k_pe, v, k_pos):
    s = (jnp.einsum('bqhd,bkhd->bhqk', q_nope, k_nope)
         + jnp.einsum('bqhr,bkr->bhqk', q_pe, k_pe)).astype(jnp.float32) * SM_SCALE
    s = jnp.where(k_pos[None, :] <= q_pos[:, None], s, NEG_INF)
    p = jax.nn.softmax(s, axis=-1).astype(v.dtype)
    return jnp.einsum('bhqk,bkhd->bqhd', p, v)


def _prompt_attention(q_nope, q_pe, k_nope, k_pe, v, pos):
    b, s = q_nope.shape[:2]
    nb = s // Q_BLOCK

    def blocks(a):
        return jnp.moveaxis(a.reshape((b, nb, Q_BLOCK) + a.shape[2:]), 1, 0)

    def one(xs):
        qn, qr, qp = xs
        return _attend(qn, qr, qp, k_nope, k_pe, v, pos)

    out = lax.map(one, (blocks(q_nope), blocks(q_pe), pos.reshape(nb, Q_BLOCK)))
    return jnp.moveaxis(out, 0, 1).reshape(b, s, N_HEADS, V_DIM)


def _sample_attention(q_nope, q_pe, ckv_new, kpe_new, ckv_pool, kpe_pool, page_table, w_kv_up, g_k_nope, q_pos):
    past = page_table.shape[1] * ckv_pool.shape[1]
    k_pos = jnp.arange(past + q_nope.shape[1])

    def one(xs):
        pt, qn, qr, cn, kn = xs
        ckv = jnp.concatenate([ckv_pool[pt].reshape(past, KV_LORA).astype(cn.dtype), cn], axis=0)[None]
        kpe = jnp.concatenate([kpe_pool[pt].reshape(past, QK_ROPE).astype(kn.dtype), kn], axis=0)[None]
        k_nope, v = _expand_kv(ckv, w_kv_up, g_k_nope)
        return _attend(qn[None], qr[None], q_pos, k_nope, kpe, v, k_pos)[0]

    return lax.map(one, (page_table, q_nope, q_pe, ckv_new, kpe_new))


def _conv_module(u_ext, w_dw, b_dw, g_ln, b_ln):
    y = lax.conv_general_dilated(u_ext, w_dw[:, None, :], window_strides=(1,), padding='VALID',
                                 dimension_numbers=('NWC', 'WIO', 'NWC'),
                                 feature_group_count=CONV_CH) + b_dw
    return jax.nn.silu(_layernorm(y, g_ln, b_ln))


def _merge(attn, conv, g_out_attn, g_out_conv, w_out):
    b, t = attn.shape[:2]
    m = jnp.concatenate([_rms(attn.reshape(b, t, ATTN_WIDTH), g_out_attn), _rms(conv, g_out_conv)], axis=-1)
    return m @ w_out


def _moe(h, w_router, b_router, w_up, b_up, w_down, b_down):
    logits = (h @ w_router + b_router).astype(jnp.float32)
    top_v, top_i = lax.top_k(logits, TOP_K)
    wts = jax.nn.softmax(top_v, axis=-1)
    combine = jnp.sum(jax.nn.one_hot(top_i, N_EXPERTS, dtype=jnp.float32) * wts[..., None], axis=1)

    def expert(acc, xs):
        w1, b1, w2, b2, ce = xs
        z = h @ w1 + b1
        zg, zl = jnp.split(z, 2, axis=-1)
        zg = jnp.minimum(zg, SWIGLU_LIMIT)
        zl = jnp.clip(zl, -SWIGLU_LIMIT, SWIGLU_LIMIT)
        act = zg * jax.nn.sigmoid(SWIGLU_ALPHA * zg) * (zl + 1)
        return acc + ce[:, None].astype(h.dtype) * (act @ w2 + b2), None

    out, _ = lax.scan(expert, jnp.zeros_like(h), (w_up, b_up, w_down, b_down, combine.T))
    return out


def setup_inputs(seed: int = 0) -> dict:
    key = jax.random.key(seed)
    ks = jax.random.split(key, 40)
    f32 = jnp.float32
    n_pages = PAST_LEN // PAGE_SIZE
    n_used = DEC_BATCH * n_pages
    n_pool = n_used + (n_used + 3) // 4

    def nrm(k, shape, scale):
        return jax.random.normal(k, shape, f32) * scale

    def gain(k, shape):
        return 1.0 + nrm(k, shape, 0.02)

    L = DEPTH
    page_table = jax.random.permutation(ks[5], n_pool)[:n_used].reshape(DEC_BATCH, n_pages).astype(jnp.int32)
    return {
        'x_prompt': nrm(ks[0], (BATCH, SEQ, D_MODEL), 1.0),
        'x_sample': nrm(ks[1], (DEC_BATCH, DEC_SEQ, D_MODEL), 1.0),
        'cache_ckv': nrm(ks[2], (L, n_pool, PAGE_SIZE, KV_LORA), 1.0),
        'cache_kpe': nrm(ks[3], (L, n_pool, PAGE_SIZE, QK_ROPE), 1.0),
        'state_conv': nrm(ks[4], (L, DEC_BATCH, CONV_W - 1, CONV_CH), 0.5),
        'page_table': page_table,
        'c_prompt': nrm(ks[6], (BATCH, D_MODEL), 1.0),
        'c_sample': nrm(ks[7], (DEC_BATCH, D_MODEL), 1.0),
        'w_ada': nrm(ks[8], (L, D_MODEL, 6 * D_MODEL), 0.5 * D_MODEL ** -0.5),
        'b_ada': nrm(ks[9], (L, 6 * D_MODEL), 0.02),
        'g_norm_mix': gain(ks[10], (L, D_MODEL)),
        'g_norm_ffn': gain(ks[11], (L, D_MODEL)),
        'w_in': nrm(ks[12], (L, D_MODEL, IN_COLS), D_MODEL ** -0.5),
        'g_q_lat': gain(ks[13], (L, Q_LORA)),
        'w_q_up': nrm(ks[14], (L, Q_LORA, N_HEADS * (QK_NOPE + QK_ROPE)), Q_LORA ** -0.5),
        'g_q_nope': gain(ks[15], (L, QK_NOPE)),
        'g_q_rope': gain(ks[16], (L, QK_ROPE)),
        'g_kv_lat': gain(ks[17], (L, KV_LORA)),
        'g_k_rope': gain(ks[18], (L, QK_ROPE)),
        'w_kv_up': nrm(ks[19], (L, KV_LORA, N_HEADS, QK_NOPE + V_DIM), KV_LORA ** -0.5),
        'g_k_nope': gain(ks[20], (L, QK_NOPE)),
        'w_dw': nrm(ks[21], (L, CONV_W, CONV_CH), CONV_W ** -0.5),
        'b_dw': nrm(ks[22], (L, CONV_CH), 0.02),
        'g_conv_ln': gain(ks[23], (L, CONV_CH)),
        'b_conv_ln': nrm(ks[24], (L, CONV_CH), 0.02),
        'g_out_attn': gain(ks[25], (L, ATTN_WIDTH)),
        'g_out_conv': gain(ks[26], (L, CONV_CH)),
        'w_out': nrm(ks[27], (L, D_MODEL, D_MODEL), D_MODEL ** -0.5),
        'w_router': nrm(ks[28], (L, D_MODEL, N_EXPERTS), D_MODEL ** -0.5),
        'b_router': nrm(ks[29], (L, N_EXPERTS), 0.01),
        'w_exp_up': nrm(ks[30], (L, N_EXPERTS, D_MODEL, 2 * D_FF), D_MODEL ** -0.5),
        'b_exp_up': nrm(ks[31], (L, N_EXPERTS, 2 * D_FF), 0.02),
        'w_exp_down': nrm(ks[32], (L, N_EXPERTS, D_FF, D_MODEL), D_FF ** -0.5),
        'b_exp_down': nrm(ks[33], (L, N_EXPERTS, D_MODEL), 0.02),
    }


def reference(x_prompt, x_sample, cache_ckv, cache_kpe, state_conv, page_table, c_prompt, c_sample,
              w_ada, b_ada, g_norm_mix, g_norm_ffn, w_in, g_q_lat, w_q_up, g_q_nope, g_q_rope,
              g_kv_lat, g_k_rope, w_kv_up, g_k_nope, w_dw, b_dw, g_conv_ln, b_conv_ln,
              g_out_attn, g_out_conv, w_out, w_router, b_router, w_exp_up, b_exp_up,
              w_exp_down, b_exp_down):
    b, s, d = x_prompt.shape
    db, t = x_sample.shape[:2]
    past = page_table.shape[1] * cache_ckv.shape[2]
    pos_p = jnp.arange(s)
    pos_s = past + jnp.arange(t)
    y_p, y_s = x_prompt, x_sample
    ckv_p_l, kpe_p_l, conv_p_l, ckv_s_l, kpe_s_l, conv_s_l = [], [], [], [], [], []
    for l in range(DEPTH):
        sh_mp, sc_mp, gt_mp, sh_fp, sc_fp, gt_fp = _adaln(c_prompt, w_ada[l], b_ada[l])
        sh_ms, sc_ms, gt_ms, sh_fs, sc_fs, gt_fs = _adaln(c_sample, w_ada[l], b_ada[l])
        h_p = _rms(y_p, g_norm_mix[l]) * (1 + sc_mp) + sh_mp
        h_s = _rms(y_s, g_norm_mix[l]) * (1 + sc_ms) + sh_ms
        qn_p, qr_p, ckv_p, kpe_p, u_p = _mixer_inputs(h_p, pos_p, w_in[l], g_q_lat[l], w_q_up[l], g_q_nope[l],
                                                      g_q_rope[l], g_kv_lat[l], g_k_rope[l])
        qn_s, qr_s, ckv_s, kpe_s, u_s = _mixer_inputs(h_s, pos_s, w_in[l], g_q_lat[l], w_q_up[l], g_q_nope[l],
                                                      g_q_rope[l], g_kv_lat[l], g_k_rope[l])
        kn_p, v_p = _expand_kv(ckv_p, w_kv_up[l], g_k_nope[l])
        attn_p = _prompt_attention(qn_p, qr_p, kn_p, kpe_p, v_p, pos_p)
        attn_s = _sample_attention(qn_s, qr_s, ckv_s, kpe_s, cache_ckv[l], cache_kpe[l], page_table,
                                   w_kv_up[l], g_k_nope[l], pos_s)
        u_ext_p = jnp.concatenate([jnp.zeros((b, CONV_W - 1, CONV_CH), u_p.dtype), u_p], axis=1)
        u_ext_s = jnp.concatenate([state_conv[l].astype(u_s.dtype), u_s], axis=1)
        conv_p = _conv_module(u_ext_p, w_dw[l], b_dw[l], g_conv_ln[l], b_conv_ln[l])
        conv_s = _conv_module(u_ext_s, w_dw[l], b_dw[l], g_conv_ln[l], b_conv_ln[l])
        y_p = y_p + gt_mp * _merge(attn_p, conv_p, g_out_attn[l], g_out_conv[l], w_out[l])
        y_s = y_s + gt_ms * _merge(attn_s, conv_s, g_out_attn[l], g_out_conv[l], w_out[l])
        f_p = _rms(y_p, g_norm_ffn[l]) * (1 + sc_fp) + sh_fp
        f_s = _rms(y_s, g_norm_ffn[l]) * (1 + sc_fs) + sh_fs
        tok = jnp.concatenate([f_p.reshape(b * s, d), f_s.reshape(db * t, d)], axis=0)
        moe = _moe(tok, w_router[l], b_router[l], w_exp_up[l], b_exp_up[l], w_exp_down[l], b_exp_down[l])
        y_p = y_p + gt_fp * moe[:b * s].reshape(b, s, d)
        y_s = y_s + gt_fs * moe[b * s:].reshape(db, t, d)
        ckv_p_l.append(ckv_p)
        kpe_p_l.append(kpe_p)
        conv_p_l.append(u_ext_p[:, -(CONV_W - 1):])
        ckv_s_l.append(ckv_s)
        kpe_s_l.append(kpe_s)
        conv_s_l.append(u_ext_s[:, -(CONV_W - 1):])
    ckv_prompt = jnp.stack(ckv_p_l)
    kpe_prompt = jnp.stack(kpe_p_l)
    conv_prompt = jnp.stack(conv_p_l)
    ckv_sample = jnp.stack(ckv_s_l)
    kpe_sample = jnp.stack(kpe_s_l)
    conv_sample = jnp.stack(conv_s_l)
    return (y_p, y_s, ckv_prompt, kpe_prompt, conv_prompt, ckv_sample, kpe_sample, conv_sample)
```

```python
import functools

import numpy as np
import jax
import jax.numpy as jnp
from jax import lax
from jax.experimental import pallas as pl
from jax.experimental.pallas import tpu as pltpu
from jax.experimental.pallas import tpu_sc as plsc

F32 = jnp.float32
BF16 = jnp.bfloat16
I32 = jnp.int32
HIGHEST = lax.Precision.HIGHEST

D_MODEL = 1024
N_HEADS = 8
QK_NOPE = 64
QK_ROPE = 32
V_DIM = 64
Q_LORA = 256
KV_LORA = 128
ATTN_WIDTH = N_HEADS * V_DIM
CONV_CH = D_MODEL - ATTN_WIDTH
CONV_W = 31
N_EXPERTS = 32
TOP_K = 4
D_FF = D_MODEL
SWIGLU_LIMIT = 7.0
SWIGLU_ALPHA = 1.702
EPS = 1e-6
NEG_INF = -1e30
ROPE_THETA = 10000.0
SM_SCALE = (QK_NOPE + QK_ROPE) ** -0.5

LANES = 128
HEAD_PAD = LANES
ROPE_LO = QK_NOPE
ROPE_HALF = QK_ROPE // 2

TOK_TILE = 256
FA_TQ = 1024
FA_TK = 512
PAGES_PER_STEP = 16
FFN_TILE = 256
ROUTE_TILE = 512
SC_ROW = 256
SC_WINDOW = 128
VMEM_LIMIT = 56 * 1024 * 1024


def _cparams(sem, vmem=None):
    return pltpu.CompilerParams(dimension_semantics=sem, vmem_limit_bytes=vmem)


def _rsqrt_mean(x, n):
    return lax.rsqrt(jnp.sum(x * x, axis=-1, keepdims=True) * (1.0 / n) + EPS)


def _ada_kernel(c_ref, w_ref, b_ref, o_ref):
    c = c_ref[...]
    s = c * jax.nn.sigmoid(c)
    o_ref[...] = jnp.dot(s, w_ref[...], precision=HIGHEST, preferred_element_type=F32) + b_ref[...]


def _adaln(c_all, w_ada, b_ada):
    rows = c_all.shape[0]
    n_out = w_ada.shape[1]
    return pl.pallas_call(
        _ada_kernel,
        grid=(n_out // D_MODEL,),
        in_specs=[pl.BlockSpec((rows, D_MODEL), lambda j: (0, 0)),
                  pl.BlockSpec((D_MODEL, D_MODEL), lambda j: (0, j)),
                  pl.BlockSpec((1, D_MODEL), lambda j: (0, j))],
        out_specs=pl.BlockSpec((rows, D_MODEL), lambda j: (0, j)),
        out_shape=jax.ShapeDtypeStruct((rows, n_out), F32),
        compiler_params=_cparams(("arbitrary",)),
        name="adaln",
    )(c_all, w_ada, b_ada.reshape(1, n_out))


def _group_norm_rope(x, m_grp, gain, cos_t, sin_t, first_half):
    ms = jnp.dot((x * x).astype(BF16), m_grp, preferred_element_type=F32)
    xn = x * lax.rsqrt(ms + EPS) * gain
    swapped = jnp.where(first_half, pltpu.roll(xn, LANES - ROPE_HALF, 1), pltpu.roll(xn, ROPE_HALF, 1))
    return xn * cos_t + swapped * sin_t


def _mix_kernel(x_ref, sh_ref, sc_ref, gmix_ref, win_ref, gql_ref, wq_ref, gq_ref, m_ref, cos_ref, sin_ref,
                gkv_ref, gk_ref, gkpe_ref, wk_ref, wv_ref,
                q_out, ckv_out, kpe_out, u_out, k_out, v_out):
    x = x_ref[...]
    h = x * _rsqrt_mean(x, D_MODEL) * gmix_ref[...]
    h = h * (1.0 + sc_ref[...]) + sh_ref[...]
    proj = jnp.dot(h.astype(BF16), win_ref[...], preferred_element_type=F32)
    q_lat = proj[:, :Q_LORA]
    ckv_raw = proj[:, Q_LORA:Q_LORA + KV_LORA]
    kpe_blk = proj[:, Q_LORA + KV_LORA:Q_LORA + KV_LORA + LANES]
    glu_lo = Q_LORA + KV_LORA + LANES
    u_out[...] = proj[:, glu_lo:glu_lo + CONV_CH] * jax.nn.sigmoid(proj[:, glu_lo + CONV_CH:glu_lo + 2 * CONV_CH])

    m_grp = m_ref[...]
    cos_t = cos_ref[...]
    sin_t = sin_ref[...]
    lane = lax.broadcasted_iota(I32, (1, LANES), 1)
    first_half = lane < ROPE_LO + ROPE_HALF

    q_lat_n = q_lat * _rsqrt_mean(q_lat, Q_LORA) * gql_ref[...]
    q = jnp.dot(q_lat_n.astype(BF16), wq_ref[...], preferred_element_type=F32)
    gq = gq_ref[...]
    for hd in range(N_HEADS):
        qh = _group_norm_rope(q[:, hd * HEAD_PAD:(hd + 1) * HEAD_PAD], m_grp, gq, cos_t, sin_t, first_half)
        q_out[:, hd * HEAD_PAD:(hd + 1) * HEAD_PAD] = (qh * SM_SCALE).astype(BF16)

    ckv_n = ckv_raw * _rsqrt_mean(ckv_raw, KV_LORA) * gkv_ref[...]
    ckv_out[...] = ckv_n
    kpe_r = _group_norm_rope(kpe_blk, m_grp, gkpe_ref[...], cos_t, sin_t, first_half)
    kpe_out[...] = kpe_r[:, ROPE_LO:ROPE_LO + QK_ROPE]

    ckv_b = ckv_n.astype(BF16)
    kexp = jnp.dot(ckv_b, wk_ref[...], preferred_element_type=F32)
    gk = gk_ref[...]
    for hd in range(N_HEADS):
        kh = kexp[:, hd * HEAD_PAD:(hd + 1) * HEAD_PAD]
        ms = jnp.dot((kh * kh).astype(BF16), m_grp, preferred_element_type=F32)
        k_out[:, hd * HEAD_PAD:(hd + 1) * HEAD_PAD] = (kh * lax.rsqrt(ms + EPS) * gk + kpe_r).astype(BF16)
    v_out[...] = jnp.dot(ckv_b, wv_ref[...], preferred_element_type=F32).astype(BF16)


def _mod_index(n_prompt_tiles):
    return lambda i: (jnp.where(i < n_prompt_tiles, 0, i - n_prompt_tiles + 1), 0)


def _mixer_inputs(x_all, sh, sc, n_prompt_tiles, p):
    n = x_all.shape[0]
    tm = TOK_TILE
    const = lambda i: (0, 0)
    row = lambda i: (i, 0)
    mod = _mod_index(n_prompt_tiles)
    hw = N_HEADS * HEAD_PAD
    in_cols = p["w_in"].shape[1]
    return pl.pallas_call(
        _mix_kernel,
        grid=(n // tm,),
        in_specs=[pl.BlockSpec((tm, D_MODEL), row),
                  pl.BlockSpec((tm, D_MODEL), mod),
                  pl.BlockSpec((tm, D_MODEL), mod),
                  pl.BlockSpec((1, D_MODEL), const),
                  pl.BlockSpec((D_MODEL, in_cols), const),
                  pl.BlockSpec((1, Q_LORA), const),
                  pl.BlockSpec((Q_LORA, hw), const),
                  pl.BlockSpec((1, LANES), const),
                  pl.BlockSpec((LANES, LANES), const),
                  pl.BlockSpec((tm, LANES), row),
                  pl.BlockSpec((tm, LANES), row),
                  pl.BlockSpec((1, KV_LORA), const),
                  pl.BlockSpec((1, LANES), const),
                  pl.BlockSpec((1, LANES), const),
                  pl.BlockSpec((KV_LORA, hw), const),
                  pl.BlockSpec((KV_LORA, ATTN_WIDTH), const)],
        out_specs=[pl.BlockSpec((tm, hw), row),
                   pl.BlockSpec((tm, KV_LORA), row),
                   pl.BlockSpec((tm, QK_ROPE), row),
                   pl.BlockSpec((tm, CONV_CH), row),
                   pl.BlockSpec((tm, hw), row),
                   pl.BlockSpec((tm, ATTN_WIDTH), row)],
        out_shape=[jax.ShapeDtypeStruct((n, hw), BF16),
                   jax.ShapeDtypeStruct((n, KV_LORA), F32),
                   jax.ShapeDtypeStruct((n, QK_ROPE), F32),
                   jax.ShapeDtypeStruct((n, CONV_CH), F32),
                   jax.ShapeDtypeStruct((n, hw), BF16),
                   jax.ShapeDtypeStruct((n, ATTN_WIDTH), BF16)],
        compiler_params=_cparams(("parallel",), VMEM_LIMIT),
        name="mixer_inputs",
    )(x_all, sh, sc, p["g_norm_mix"], p["w_in"], p["g_q_lat"], p["w_q_up"], p["gain_q"], p["m_grp"],
      p["cos_t"], p["sin_t"], p["g_kv_lat"], p["gain_k"], p["gain_kpe"], p["w_k"], p["w_v"])


def _fa_kernel(qt_ref, kt_ref, q_ref, k_ref, v_ref, o_ref, m_sc, l_sc, acc_sc):
    t = pl.program_id(1)
    qi = qt_ref[t]
    ki = kt_ref[t]
    last_k = (qi + 1) * (FA_TQ // FA_TK) - 1

    @pl.when(ki == 0)
    def _():
        m_sc[...] = jnp.full(m_sc.shape, NEG_INF, F32)
        l_sc[...] = jnp.zeros(l_sc.shape, F32)
        acc_sc[...] = jnp.zeros(acc_sc.shape, F32)

    def step(masked):
        if masked:
            row = qi * FA_TQ + lax.broadcasted_iota(I32, (FA_TQ, FA_TK), 0)
            col = ki * FA_TK + lax.broadcasted_iota(I32, (FA_TQ, FA_TK), 1)
            visible = col <= row
        for hh in range(2):
            q = q_ref[:, hh * HEAD_PAD:(hh + 1) * HEAD_PAD]
            k = k_ref[:, hh * HEAD_PAD:(hh + 1) * HEAD_PAD]
            v = v_ref[:, hh * V_DIM:(hh + 1) * V_DIM]
            s = lax.dot_general(q, k, (((1,), (1,)), ((), ())), preferred_element_type=F32)
            if masked:
                s = jnp.where(visible, s, NEG_INF)
            m_prev = m_sc[hh]
            m_new = jnp.maximum(m_prev, jnp.max(s, axis=-1, keepdims=True))
            alpha = jnp.exp(m_prev - m_new)
            pr = jnp.exp(s - m_new)
            l_sc[hh] = alpha * l_sc[hh] + jnp.sum(pr, axis=-1, keepdims=True)
            acc_sc[hh] = alpha * acc_sc[hh] + jnp.dot(pr.astype(BF16), v, preferred_element_type=F32)
            m_sc[hh] = m_new

    @pl.when(ki * FA_TK + FA_TK - 1 <= qi * FA_TQ)
    def _():
        step(False)

    @pl.when(ki * FA_TK + FA_TK - 1 > qi * FA_TQ)
    def _():
        step(True)

    @pl.when(ki == last_k)
    def _():
        for hh in range(2):
            o_ref[:, hh * V_DIM:(hh + 1) * V_DIM] = acc_sc[hh] / l_sc[hh]


def _prompt_attention(q_all, k_all, v_all, seq):
    nq = seq // FA_TQ
    ratio = FA_TQ // FA_TK
    qt, kt = [], []
    for qi in range(nq):
        for ki in range((qi + 1) * ratio):
            qt.append(qi)
            kt.append(ki)
    qt = jnp.asarray(np.array(qt, np.int32))
    kt = jnp.asarray(np.array(kt, np.int32))
    n_pairs = int(qt.shape[0])
    grid_spec = pltpu.PrefetchScalarGridSpec(
        num_scalar_prefetch=2,
        grid=(N_HEADS // 2, n_pairs),
        in_specs=[pl.BlockSpec((FA_TQ, 2 * HEAD_PAD), lambda hp, t, qt, kt: (qt[t], hp)),
                  pl.BlockSpec((FA_TK, 2 * HEAD_PAD), lambda hp, t, qt, kt: (kt[t], hp)),
                  pl.BlockSpec((FA_TK, 2 * V_DIM), lambda hp, t, qt, kt: (kt[t], hp))],
        out_specs=pl.BlockSpec((FA_TQ, 2 * V_DIM), lambda hp, t, qt, kt: (qt[t], hp)),
        scratch_shapes=[pltpu.VMEM((2, FA_TQ, 1), F32),
                        pltpu.VMEM((2, FA_TQ, 1), F32),
                        pltpu.VMEM((2, FA_TQ, V_DIM), F32)],
    )
    return pl.pallas_call(
        _fa_kernel,
        grid_spec=grid_spec,
        out_shape=jax.ShapeDtypeStruct((seq, ATTN_WIDTH), F32),
        compiler_params=_cparams(("parallel", "arbitrary"), VMEM_LIMIT),
        name="prompt_attention",
    )(qt, kt, q_all, k_all, v_all)


def _sattn_kernel(pt_ref, q_ref, wabs_ref, wkt_ref, wv_ref, ckvn_ref, kpen_ref, *rest):
    ckv_pages = rest[:PAGES_PER_STEP]
    kpe_pages = rest[PAGES_PER_STEP:2 * PAGES_PER_STEP]
    o_ref, m_sc, l_sc, acc_sc, qa_sc = rest[2 * PAGES_PER_STEP:]
    kb = pl.program_id(1)
    n_kb = pl.num_programs(1)
    rows = 4 * N_HEADS

    @pl.when(kb == 0)
    def _():
        m_sc[...] = jnp.full(m_sc.shape, NEG_INF, F32)
        l_sc[...] = jnp.zeros(l_sc.shape, F32)
        acc_sc[...] = jnp.zeros(acc_sc.shape, F32)
        q4 = q_ref[...].astype(F32)
        head_of_lane = lax.broadcasted_iota(I32, (N_HEADS, N_HEADS * HEAD_PAD), 1) // HEAD_PAD
        head_of_row = lax.broadcasted_iota(I32, (N_HEADS, N_HEADS * HEAD_PAD), 0)
        own = head_of_lane == head_of_row
        qbd = jnp.concatenate(
            [jnp.where(own, jnp.broadcast_to(q4[qq:qq + 1, :], own.shape), 0.0) for qq in range(4)], axis=0)
        qa_sc[...] = jnp.dot(qbd.astype(BF16), wabs_ref[...], preferred_element_type=F32).astype(BF16)

    def block(ckv_b, kpe_b, mask):
        nt = (((1,), (1,)), ((), ()))
        kn_t = lax.dot_general(wkt_ref[...], ckv_b, nt, preferred_element_type=F32)
        keys = kn_t.shape[1]
        ss = jnp.sum((kn_t * kn_t).reshape(N_HEADS, QK_NOPE, keys), axis=1)
        r8 = lax.rsqrt(ss * (1.0 / QK_NOPE) + EPS)
        qa = qa_sc[...]
        a = lax.dot_general(qa[:, :KV_LORA], ckv_b, nt, preferred_element_type=F32)
        b = lax.dot_general(qa[:, KV_LORA:KV_LORA + QK_ROPE], kpe_b, nt, preferred_element_type=F32)
        s = a * jnp.concatenate([r8] * 4, axis=0) + b
        if mask is not None:
            s = jnp.where(mask, s, NEG_INF)
        m_prev = m_sc[...]
        m_new = jnp.maximum(m_prev, jnp.max(s, axis=-1, keepdims=True))
        alpha = jnp.exp(m_prev - m_new)
        pr = jnp.exp(s - m_new)
        l_sc[...] = alpha * l_sc[...] + jnp.sum(pr, axis=-1, keepdims=True)
        acc_sc[...] = alpha * acc_sc[...] + jnp.dot(pr.astype(BF16), ckv_b, preferred_element_type=F32)
        m_sc[...] = m_new

    ckv_b = jnp.concatenate([r[...].astype(BF16) for r in ckv_pages], axis=0)
    kpe_b = jnp.concatenate([r[...].astype(BF16) for r in kpe_pages], axis=0)
    block(ckv_b, kpe_b, None)

    @pl.when(kb == n_kb - 1)
    def _():
        key = lax.broadcasted_iota(I32, (rows, LANES), 1)
        qry = lax.broadcasted_iota(I32, (rows, LANES), 0) // N_HEADS
        block(ckvn_ref[...].astype(BF16), kpen_ref[...].astype(BF16), key <= qry)
        lat = acc_sc[...] / l_sc[...]
        o_all = jnp.dot(lat.astype(BF16), wv_ref[...], preferred_element_type=F32)
        head_of_col = lax.broadcasted_iota(I32, (rows, ATTN_WIDTH), 1) // V_DIM
        head_of_row = lax.broadcasted_iota(I32, (rows, ATTN_WIDTH), 0) % N_HEADS
        o_own = jnp.where(head_of_col == head_of_row, o_all, 0.0)
        o_ref[...] = jnp.sum(o_own.reshape(4, N_HEADS, ATTN_WIDTH), axis=1)


def _sample_attention(page_table, q_s, ckv_new_pad, kpe_new_pad, cache_ckv, cache_kpe, p):
    n_seq, n_pages = page_table.shape
    page = cache_ckv.shape[2]
    n_kb = n_pages // PAGES_PER_STEP
    hw = N_HEADS * HEAD_PAD

    def ckv_map(j):
        return lambda b, kb, pt: (0, pt[b * n_pages + kb * PAGES_PER_STEP + j], 0, 0)

    per_seq3 = lambda b, kb, pt: (b, 0, 0)
    const = lambda b, kb, pt: (0, 0)
    in_specs = [pl.BlockSpec((None, 4, hw), per_seq3),
                pl.BlockSpec((hw, 2 * LANES), const),
                pl.BlockSpec((N_HEADS * QK_NOPE, KV_LORA), const),
                pl.BlockSpec((KV_LORA, ATTN_WIDTH), const),
                pl.BlockSpec((None, LANES, KV_LORA), per_seq3),
                pl.BlockSpec((None, LANES, QK_ROPE), per_seq3)]
    in_specs += [pl.BlockSpec((None, None, page, KV_LORA), ckv_map(j)) for j in range(PAGES_PER_STEP)]
    in_specs += [pl.BlockSpec((None, None, page, QK_ROPE), ckv_map(j)) for j in range(PAGES_PER_STEP)]
    grid_spec = pltpu.PrefetchScalarGridSpec(
        num_scalar_prefetch=1,
        grid=(n_seq, n_kb),
        in_specs=in_specs,
        out_specs=pl.BlockSpec((None, 4, ATTN_WIDTH), per_seq3),
        scratch_shapes=[pltpu.VMEM((4 * N_HEADS, 1), F32),
                        pltpu.VMEM((4 * N_HEADS, 1), F32),
                        pltpu.VMEM((4 * N_HEADS, KV_LORA), F32),
                        pltpu.VMEM((4 * N_HEADS, 2 * LANES), BF16)],
    )
    return pl.pallas_call(
        _sattn_kernel,
        grid_spec=grid_spec,
        out_shape=jax.ShapeDtypeStruct((n_seq, 4, ATTN_WIDTH), F32),
        compiler_params=_cparams(("parallel", "arbitrary"), VMEM_LIMIT),
        name="decode_attention",
    )(page_table.reshape(-1), q_s, p["w_abs"], p["w_kt"], p["w_v"], ckv_new_pad, kpe_new_pad,
      *([cache_ckv] * PAGES_PER_STEP), *([cache_kpe] * PAGES_PER_STEP))


CONV_HALO = 32
CONV_ROWS = 64


def _conv_kernel(halo_ref, u_ref, w_ref, b_ref, o_ref, ext_sc):
    i = pl.program_id(0)
    tm = u_ref.shape[0]
    ext_sc[pl.ds(0, CONV_HALO), :] = jnp.where(i == 0, 0.0, halo_ref[...])
    ext_sc[pl.ds(CONV_HALO, tm), :] = u_ref[...]
    first = CONV_HALO - (CONV_W - 1)
    for rc in range(tm // CONV_ROWS):
        acc = jnp.broadcast_to(b_ref[...], (CONV_ROWS, CONV_CH))
        for j in range(CONV_W):
            acc = acc + ext_sc[pl.ds(rc * CONV_ROWS + first + j, CONV_ROWS), :] * w_ref[j:j + 1, :]
        o_ref[pl.ds(rc * CONV_ROWS, CONV_ROWS), :] = acc


def _prompt_conv(u_all, w_dw, b_dw):
    n = u_all.shape[0]
    tm = TOK_TILE
    per = tm // CONV_HALO
    return pl.pallas_call(
        _conv_kernel,
        grid=(n // tm,),
        in_specs=[pl.BlockSpec((CONV_HALO, CONV_CH), lambda i: (jnp.maximum(i * per - 1, 0), 0)),
                  pl.BlockSpec((tm, CONV_CH), lambda i: (i, 0)),
                  pl.BlockSpec((CONV_W, CONV_CH), lambda i: (0, 0)),
                  pl.BlockSpec((1, CONV_CH), lambda i: (0, 0))],
        out_specs=pl.BlockSpec((tm, CONV_CH), lambda i: (i, 0)),
        out_shape=jax.ShapeDtypeStruct((n, CONV_CH), F32),
        scratch_shapes=[pltpu.VMEM((CONV_HALO + tm, CONV_CH), F32)],
        compiler_params=_cparams(("parallel",)),
        name="prompt_conv",
    )(u_all, u_all, w_dw, b_dw)


def _sconv_kernel(u_ref, wt_ref, b_ref, o_ref):
    u = u_ref[...]
    for t in range(o_ref.shape[0]):
        o_ref[t] = jnp.sum(u * wt_ref[t][None, :, :], axis=1) + b_ref[...]


def _sample_conv(u_ext, w_taps, b_dw):
    n_seq, ext, _ = u_ext.shape
    t_new = w_taps.shape[0]
    sb = 8
    return pl.pallas_call(
        _sconv_kernel,
        grid=(n_seq // sb,),
        in_specs=[pl.BlockSpec((sb, ext, CONV_CH), lambda i: (i, 0, 0)),
                  pl.BlockSpec((t_new, ext, CONV_CH), lambda i: (0, 0, 0)),
                  pl.BlockSpec((1, CONV_CH), lambda i: (0, 0))],
        out_specs=pl.BlockSpec((t_new, sb, CONV_CH), lambda i: (0, i, 0)),
        out_shape=jax.ShapeDtypeStruct((t_new, n_seq, CONV_CH), F32),
        compiler_params=_cparams(("parallel",)),
        name="decode_conv",
    )(u_ext, w_taps, b_dw)


def _lane_pack(cols, dtype):
    lane = lax.broadcasted_iota(I32, (cols[0].shape[0], LANES), 1)
    out = jnp.zeros((cols[0].shape[0], LANES), dtype)
    for j, c in enumerate(cols):
        out = jnp.where(lane == j, c.astype(dtype), out)
    return out


def _merge_kernel(x_ref, attn_ref, conv_ref, gate_ref, shf_ref, scf_ref, gln_ref, bln_ref, ga_ref, gc_ref,
                  wout_ref, gffn_ref, wr_ref, br_ref, y_out, f_out, ti_out, tw_out):
    yc = conv_ref[...]
    mu = jnp.mean(yc, axis=-1, keepdims=True)
    xc = yc - mu
    var = jnp.mean(xc * xc, axis=-1, keepdims=True)
    ln = xc * lax.rsqrt(var + EPS) * gln_ref[...] + bln_ref[...]
    conv = ln * jax.nn.sigmoid(ln)
    attn = attn_ref[...]
    a_n = attn * _rsqrt_mean(attn, ATTN_WIDTH) * ga_ref[...]
    c_n = conv * _rsqrt_mean(conv, CONV_CH) * gc_ref[...]
    m = (jnp.dot(a_n.astype(BF16), wout_ref[:ATTN_WIDTH, :], preferred_element_type=F32)
         + jnp.dot(c_n.astype(BF16), wout_ref[ATTN_WIDTH:, :], preferred_element_type=F32))
    y = x_ref[...] + gate_ref[...] * m
    y_out[...] = y
    f = y * _rsqrt_mean(y, D_MODEL) * gffn_ref[...]
    f = f * (1.0 + scf_ref[...]) + shf_ref[...]
    f_out[...] = f
    logits = jnp.dot(f, wr_ref[...], precision=HIGHEST, preferred_element_type=F32) + br_ref[...]
    lane = lax.broadcasted_iota(I32, logits.shape, 1)
    vals, idxs = [], []
    for _ in range(TOP_K):
        mx = jnp.max(logits, axis=-1, keepdims=True)
        ix = jnp.min(jnp.where(logits == mx, lane, LANES), axis=-1, keepdims=True)
        vals.append(mx)
        idxs.append(ix)
        logits = jnp.where(lane == ix, NEG_INF * 4.0, logits)
    exps = [jnp.exp(v - vals[0]) for v in vals]
    tot = exps[0] + exps[1] + exps[2] + exps[3]
    ti_out[...] = _lane_pack(idxs, I32)
    tw_out[...] = _lane_pack([e / tot for e in exps], F32)


def _merge_router(x_all, attn_all, conv_all, gate, shf, scf, n_prompt_tiles, p):
    n = x_all.shape[0]
    tm = TOK_TILE
    const = lambda i: (0, 0)
    row = lambda i: (i, 0)
    mod = _mod_index(n_prompt_tiles)
    return pl.pallas_call(
        _merge_kernel,
        grid=(n // tm,),
        in_specs=[pl.BlockSpec((tm, D_MODEL), row),
                  pl.BlockSpec((tm, ATTN_WIDTH), row),
                  pl.BlockSpec((tm, CONV_CH), row),
                  pl.BlockSpec((tm, D_MODEL), mod),
                  pl.BlockSpec((tm, D_MODEL), mod),
                  pl.BlockSpec((tm, D_MODEL), mod),
                  pl.BlockSpec((1, CONV_CH), const),
                  pl.BlockSpec((1, CONV_CH), const),
                  pl.BlockSpec((1, ATTN_WIDTH), const),
                  pl.BlockSpec((1, CONV_CH), const),
                  pl.BlockSpec((D_MODEL, D_MODEL), const),
                  pl.BlockSpec((1, D_MODEL), const),
                  pl.BlockSpec((D_MODEL, LANES), const),
                  pl.BlockSpec((1, LANES), const)],
        out_specs=[pl.BlockSpec((tm, D_MODEL), row),
                   pl.BlockSpec((tm, D_MODEL), row),
                   pl.BlockSpec((tm, LANES), row),
                   pl.BlockSpec((tm, LANES), row)],
        out_shape=[jax.ShapeDtypeStruct((n, D_MODEL), F32),
                   jax.ShapeDtypeStruct((n, D_MODEL), F32),
                   jax.ShapeDtypeStruct((n, LANES), I32),
                   jax.ShapeDtypeStruct((n, LANES), F32)],
        compiler_params=_cparams(("parallel",), VMEM_LIMIT),
        name="merge_router",
    )(x_all, attn_all, conv_all, gate, shf, scf, p["g_conv_ln"], p["b_conv_ln"], p["g_out_attn"],
      p["g_out_conv"], p["w_out"], p["g_norm_ffn"], p["w_router"], p["b_router"])


def _select_lane(table, idx_col, lane):
    return jnp.sum(jnp.where(lane == idx_col, table, 0.0), axis=-1, keepdims=True)


def _rank_kernel(ti_ref, rk_out, cnt_out, carry_sc):
    i = pl.program_id(0)

    @pl.when(i == 0)
    def _():
        carry_sc[...] = jnp.zeros(carry_sc.shape, F32)

    ti = ti_ref[...]
    tn = ti.shape[0]
    lane = lax.broadcasted_iota(I32, (tn, LANES), 1)
    sel = jnp.zeros((tn, LANES), F32)
    for k in range(TOP_K):
        sel = sel + (lane == ti[:, k:k + 1]).astype(F32)
    r_i = lax.broadcasted_iota(I32, (tn, tn), 0)
    c_i = lax.broadcasted_iota(I32, (tn, tn), 1)
    below = (c_i < r_i).astype(BF16)
    rank = carry_sc[...] + jnp.dot(below, sel.astype(BF16), preferred_element_type=F32)
    rk_out[...] = _lane_pack([_select_lane(rank, ti[:, k:k + 1], lane) for k in range(TOP_K)], F32)
    carry_sc[...] = carry_sc[...] + jnp.sum(sel, axis=0, keepdims=True)
    cnt_out[...] = jnp.broadcast_to(carry_sc[...], cnt_out.shape)


def _pos_kernel(cnt_ref, ti_ref, rk_ref, pos_out, meta_out):
    cnt = cnt_ref[...]
    padded = jnp.ceil(cnt * (1.0 / FFN_TILE)) * FFN_TILE
    r_i = lax.broadcasted_iota(I32, (LANES, LANES), 0)
    c_i = lax.broadcasted_iota(I32, (LANES, LANES), 1)
    before = (r_i < c_i).astype(F32)
    offs = jnp.dot(padded, before, precision=HIGHEST, preferred_element_type=F32)
    ends = offs + padded
    ti = ti_ref[...]
    tn = ti.shape[0]
    lane = lax.broadcasted_iota(I32, (tn, LANES), 1)
    off_row = offs[0:1, :]
    rk = rk_ref[...]
    pos = [_select_lane(jnp.broadcast_to(off_row, (tn, LANES)), ti[:, k:k + 1], lane) + rk[:, k:k + 1]
           for k in range(TOP_K)]
    pos_out[...] = _lane_pack(pos, F32).astype(I32)

    @pl.when(pl.program_id(0) == 0)
    def _():
        nt = meta_out.shape[0]
        start = (lax.broadcasted_iota(I32, (nt, LANES), 0) * FFN_TILE).astype(F32)
        elane = lax.broadcasted_iota(I32, (nt, LANES), 1)
        done = jnp.where((elane < N_EXPERTS) & (jnp.broadcast_to(ends[0:1, :], (nt, LANES)) <= start), 1.0, 0.0)
        expert = jnp.minimum(jnp.sum(done, axis=-1, keepdims=True), N_EXPERTS - 1.0)
        total = jnp.sum(jnp.where(elane < N_EXPERTS, jnp.broadcast_to(padded[0:1, :], (nt, LANES)), 0.0),
                        axis=-1, keepdims=True)
        meta_out[...] = _lane_pack([expert, total * (1.0 / FFN_TILE)], F32).astype(I32)


def _route(top_i, n_tiles_max):
    n = top_i.shape[0]
    tn = ROUTE_TILE
    rk, cnt = pl.pallas_call(
        _rank_kernel,
        grid=(n // tn,),
        in_specs=[pl.BlockSpec((tn, LANES), lambda i: (i, 0))],
        out_specs=[pl.BlockSpec((tn, LANES), lambda i: (i, 0)),
                   pl.BlockSpec((8, LANES), lambda i: (0, 0))],
        out_shape=[jax.ShapeDtypeStruct((n, LANES), F32),
                   jax.ShapeDtypeStruct((8, LANES), F32)],
        scratch_shapes=[pltpu.VMEM((1, LANES), F32)],
        compiler_params=_cparams(("arbitrary",)),
        name="route_rank",
    )(top_i)
    nt_pad = -(-n_tiles_max // 8) * 8
    pos, meta = pl.pallas_call(
        _pos_kernel,
        grid=(n // tn,),
        in_specs=[pl.BlockSpec((8, LANES), lambda i: (0, 0)),
                  pl.BlockSpec((tn, LANES), lambda i: (i, 0)),
                  pl.BlockSpec((tn, LANES), lambda i: (i, 0))],
        out_specs=[pl.BlockSpec((tn, LANES), lambda i: (i, 0)),
                   pl.BlockSpec((nt_pad, LANES), lambda i: (0, 0))],
        out_shape=[jax.ShapeDtypeStruct((n, LANES), I32),
                   jax.ShapeDtypeStruct((nt_pad, LANES), I32)],
        compiler_params=_cparams(("arbitrary",)),
        name="route_pos",
    )(cnt, top_i, rk)
    return pos, meta


def _sc_scatter_rows(x, idx, n_out):
    n_src = x.shape[0]
    n = idx.shape[0]
    n_src_blk = n_src // SC_WINDOW
    mesh = plsc.VectorSubcoreMesh(core_axis_name="c", subcore_axis_name="s")

    @pl.kernel(out_type=jax.ShapeDtypeStruct((n_out, SC_ROW), x.dtype), mesh=mesh)
    def k(x_hbm, i_hbm, o_hbm):
        def body(x_vmem, i_vmem):
            pltpu.sync_copy(x_vmem, o_hbm.at[i_vmem.at[0]])

        pltpu.emit_pipeline(
            body,
            grid=(n // SC_WINDOW,),
            in_specs=[pl.BlockSpec((SC_WINDOW, SC_ROW), index_map=lambda i: (i % n_src_blk, 0)),
                      pl.BlockSpec((1, SC_WINDOW), index_map=lambda i: (0, i))],
            out_specs=[],
            core_axis_name=("c", "s"),
            dimension_semantics=(pltpu.PARALLEL,),
        )(x_hbm, i_hbm)

    return k(x, idx.reshape(1, n))


def _sc_gather_rows(x, idx):
    n = idx.shape[0]
    mesh = plsc.VectorSubcoreMesh(core_axis_name="c", subcore_axis_name="s")

    @pl.kernel(out_type=jax.ShapeDtypeStruct((n, SC_ROW), x.dtype), mesh=mesh)
    def k(x_hbm, i_hbm, o_hbm):
        def body(i_vmem, o_vmem):
            pltpu.sync_copy(x_hbm.at[i_vmem.at[0]], o_vmem)

        pltpu.emit_pipeline(
            body,
            grid=(n // SC_WINDOW,),
            in_specs=[pl.BlockSpec((1, SC_WINDOW), index_map=lambda i: (0, i))],
            out_specs=[pl.BlockSpec((SC_WINDOW, SC_ROW), index_map=lambda i: (i, 0))],
            core_axis_name=("c", "s"),
            dimension_semantics=(pltpu.PARALLEL,),
        )(i_hbm, o_hbm)

    return k(x, idx.reshape(1, n))


def _ffn_kernel(te_ref, nv_ref, x_ref, wu_ref, bu_ref, wd_ref, bd_ref, o_ref, wu_sc, wd_sc):
    t = pl.program_id(0)
    valid = t < nv_ref[0]
    prev = te_ref[jnp.maximum(t - 1, 0)]
    fresh = jnp.logical_or(t == 0, te_ref[t] != prev)

    @pl.when(jnp.logical_and(valid, fresh))
    def _():
        wu_sc[...] = wu_ref[...].astype(BF16)
        wd_sc[...] = wd_ref[...].astype(BF16)

    @pl.when(valid)
    def _():
        z = jnp.dot(x_ref[...].astype(BF16), wu_sc[...], preferred_element_type=F32) + bu_ref[...]
        zg = jnp.minimum(z[:, :D_FF], SWIGLU_LIMIT)
        zl = jnp.clip(z[:, D_FF:], -SWIGLU_LIMIT, SWIGLU_LIMIT)
        act = zg * jax.nn.sigmoid(SWIGLU_ALPHA * zg) * (zl + 1.0)
        o_ref[...] = jnp.dot(act.astype(BF16), wd_sc[...], preferred_element_type=F32) + bd_ref[...]


def _expert_ffn(tile_expert, n_valid, x_sorted, w_up, b_up, w_down, b_down):
    n_slots = x_sorted.shape[0]
    n_tiles = n_slots // FFN_TILE
    xmap = lambda t, te, nv: (jnp.minimum(t, nv[0] - 1), 0)
    emap = lambda t, te, nv: (te[t], 0, 0)
    grid_spec = pltpu.PrefetchScalarGridSpec(
        num_scalar_prefetch=2,
        grid=(n_tiles,),
        in_specs=[pl.BlockSpec((FFN_TILE, D_MODEL), xmap),
                  pl.BlockSpec((None, D_MODEL, 2 * D_FF), emap),
                  pl.BlockSpec((None, 1, 2 * D_FF), emap),
                  pl.BlockSpec((None, D_FF, D_MODEL), emap),
                  pl.BlockSpec((None, 1, D_MODEL), emap)],
        out_specs=pl.BlockSpec((FFN_TILE, D_MODEL), xmap),
        scratch_shapes=[pltpu.VMEM((D_MODEL, 2 * D_FF), BF16),
                        pltpu.VMEM((D_FF, D_MODEL), BF16)],
    )
    return pl.pallas_call(
        _ffn_kernel,
        grid_spec=grid_spec,
        out_shape=jax.ShapeDtypeStruct((n_slots, D_MODEL), F32),
        compiler_params=_cparams(("arbitrary",), VMEM_LIMIT),
        name="expert_ffn",
    )(tile_expert, n_valid, x_sorted, w_up, b_up.reshape(N_EXPERTS, 1, 2 * D_FF), w_down,
      b_down.reshape(N_EXPERTS, 1, D_MODEL))


def _combine_kernel(y_ref, g_ref, tw_ref, gate_ref, o_ref):
    tw = tw_ref[...]
    moe = g_ref[0] * tw[:, 0:1]
    for k in range(1, TOP_K):
        moe = moe + g_ref[k] * tw[:, k:k + 1]
    o_ref[...] = y_ref[...] + gate_ref[...] * moe


def _combine(y_all, gathered, top_w, gate, n_prompt_tiles):
    n = y_all.shape[0]
    tm = TOK_TILE
    row = lambda i: (i, 0)
    return pl.pallas_call(
        _combine_kernel,
        grid=(n // tm,),
        in_specs=[pl.BlockSpec((tm, D_MODEL), row),
                  pl.BlockSpec((TOP_K, tm, D_MODEL), lambda i: (0, i, 0)),
                  pl.BlockSpec((tm, LANES), row),
                  pl.BlockSpec((tm, D_MODEL), _mod_index(n_prompt_tiles))],
        out_specs=pl.BlockSpec((tm, D_MODEL), row),
        out_shape=jax.ShapeDtypeStruct((n, D_MODEL), F32),
        compiler_params=_cparams(("parallel",), VMEM_LIMIT),
        name="moe_combine",
    )(y_all, gathered, top_w, gate)


def _head_tiles(nope, rope):
    pad = jnp.zeros(nope.shape[:-1] + (HEAD_PAD - QK_NOPE - QK_ROPE,), nope.dtype)
    t = jnp.concatenate([nope, rope, pad], axis=-1)
    return t.reshape(t.shape[:-2] + (N_HEADS * HEAD_PAD,))


def _prepare(pos, w_in, g_norm_mix, g_q_lat, w_q_up, g_q_nope, g_q_rope, g_kv_lat, g_k_rope, w_kv_up, g_k_nope,
             g_conv_ln, b_conv_ln, g_out_attn, g_out_conv, w_out, g_norm_ffn, w_router, b_router):
    z32 = jnp.zeros((HEAD_PAD - QK_NOPE - QK_ROPE,), F32)
    z64 = jnp.zeros((QK_NOPE,), F32)
    d = D_MODEL
    kpe_cols = w_in[:, Q_LORA + KV_LORA:Q_LORA + KV_LORA + QK_ROPE]
    kpe_tile = jnp.concatenate([jnp.zeros((d, QK_NOPE), F32), kpe_cols, jnp.zeros((d, z32.shape[0]), F32)], axis=1)
    w_in_r = jnp.concatenate([w_in[:, :Q_LORA + KV_LORA], kpe_tile, w_in[:, Q_LORA + KV_LORA + QK_ROPE:]], axis=1)
    wq = w_q_up.reshape(Q_LORA, N_HEADS, QK_NOPE + QK_ROPE)
    w_q_r = _head_tiles(wq[..., :QK_NOPE], wq[..., QK_NOPE:])
    wk = w_kv_up[..., :QK_NOPE]
    wv = w_kv_up[..., QK_NOPE:].reshape(KV_LORA, ATTN_WIDTH)
    w_k_r = _head_tiles(wk, jnp.zeros((KV_LORA, N_HEADS, QK_ROPE), F32))
    lane = np.arange(LANES)
    grp = np.where(lane < QK_NOPE, 0, np.where(lane < QK_NOPE + QK_ROPE, 1, 2))
    m_grp = ((grp[:, None] == grp[None, :]) & (grp[:, None] < 2)).astype(np.float32)
    m_grp = m_grp / np.where(grp < 1, QK_NOPE, QK_ROPE)[None, :]
    inv = ROPE_THETA ** (-jnp.arange(0, QK_ROPE, 2, dtype=F32) / QK_ROPE)
    ang = pos.astype(F32)[:, None] * inv[None, :]
    cs, sn = jnp.cos(ang), jnp.sin(ang)
    n = pos.shape[0]
    cos_t = jnp.concatenate([jnp.ones((n, QK_NOPE), F32), cs, cs, jnp.zeros((n, z32.shape[0]), F32)], axis=1)
    sin_t = jnp.concatenate([jnp.zeros((n, QK_NOPE), F32), -sn, sn, jnp.zeros((n, z32.shape[0]), F32)], axis=1)
    wk_g = wk * g_k_nope[None, None, :]
    absorb = jnp.zeros((N_HEADS, HEAD_PAD, 2 * LANES), F32)
    absorb = absorb.at[:, :QK_NOPE, :KV_LORA].set(jnp.transpose(wk_g, (1, 2, 0)))
    absorb = absorb.at[:, ROPE_LO:ROPE_LO + QK_ROPE, KV_LORA:KV_LORA + QK_ROPE].set(
        jnp.broadcast_to(jnp.eye(QK_ROPE, dtype=F32), (N_HEADS, QK_ROPE, QK_ROPE)))
    w_kt = jnp.transpose(wk, (1, 2, 0)).reshape(N_HEADS * QK_NOPE, KV_LORA)
    wr = jnp.concatenate([w_router, jnp.zeros((d, LANES - N_EXPERTS), F32)], axis=1)
    br = jnp.concatenate([b_router, jnp.full((LANES - N_EXPERTS,), NEG_INF, F32)])
    return {
        "w_in": w_in_r.astype(BF16), "g_norm_mix": g_norm_mix.reshape(1, d), "g_q_lat": g_q_lat.reshape(1, Q_LORA),
        "w_q_up": w_q_r.astype(BF16),
        "gain_q": jnp.concatenate([g_q_nope, g_q_rope, z32]).reshape(1, LANES),
        "m_grp": jnp.asarray(m_grp, BF16), "cos_t": cos_t, "sin_t": sin_t,
        "g_kv_lat": g_kv_lat.reshape(1, KV_LORA),
        "gain_k": jnp.concatenate([g_k_nope, z64]).reshape(1, LANES),
        "gain_kpe": jnp.concatenate([z64, g_k_rope, z32]).reshape(1, LANES),
        "w_k": w_k_r.astype(BF16), "w_v": wv.astype(BF16),
        "w_abs": absorb.reshape(N_HEADS * HEAD_PAD, 2 * LANES).astype(BF16), "w_kt": w_kt.astype(BF16),
        "g_conv_ln": g_conv_ln.reshape(1, CONV_CH), "b_conv_ln": b_conv_ln.reshape(1, CONV_CH),
        "g_out_attn": g_out_attn.reshape(1, ATTN_WIDTH), "g_out_conv": g_out_conv.reshape(1, CONV_CH),
        "w_out": w_out.astype(BF16), "g_norm_ffn": g_norm_ffn.reshape(1, d),
        "w_router": wr, "b_router": br.reshape(1, LANES),
    }


def _mod_table(mod_p, mod_s, t_new):
    return jnp.concatenate([jnp.broadcast_to(mod_p, (TOK_TILE, D_MODEL)), jnp.repeat(mod_s, t_new, axis=0)], axis=0)


def kernel(x_prompt, x_sample, cache_ckv, cache_kpe, state_conv, page_table, c_prompt, c_sample, w_ada, b_ada, g_norm_mix, g_norm_ffn, w_in, g_q_lat, w_q_up, g_q_nope, g_q_rope, g_kv_lat, g_k_rope, w_kv_up, g_k_nope, w_dw, b_dw, g_conv_ln, b_conv_ln, g_out_attn, g_out_conv, w_out, w_router, b_router, w_exp_up, b_exp_up, w_exp_down, b_exp_down):
    bsz, seq, d = x_prompt.shape
    n_seq, t_new = x_sample.shape[:2]
    depth = w_ada.shape[0]
    assert bsz == 1 and depth == 1 and t_new == 4 and d == D_MODEL
    n_p = bsz * seq
    n_s = n_seq * t_new
    n = n_p + n_s
    past = page_table.shape[1] * cache_ckv.shape[2]
    n_prompt_tiles = n_p // TOK_TILE
    l = 0

    pos = jnp.concatenate([jnp.arange(seq, dtype=I32), jnp.tile(past + jnp.arange(t_new, dtype=I32), n_seq)])
    p = _prepare(pos, w_in[l], g_norm_mix[l], g_q_lat[l], w_q_up[l], g_q_nope[l], g_q_rope[l], g_kv_lat[l],
                 g_k_rope[l], w_kv_up[l], g_k_nope[l], g_conv_ln[l], b_conv_ln[l], g_out_attn[l], g_out_conv[l],
                 w_out[l], g_norm_ffn[l], w_router[l], b_router[l])

    n_c = 1 + n_seq
    c_all = jnp.concatenate([c_prompt, c_sample, jnp.zeros((-n_c % 8, d), F32)], axis=0)
    mod = _adaln(c_all, w_ada[l], b_ada[l])
    tabs = [_mod_table(mod[0:1, j * d:(j + 1) * d], mod[1:n_c, j * d:(j + 1) * d], t_new) for j in range(6)]
    sh_m, sc_m, gt_m, sh_f, sc_f, gt_f = tabs

    x_all = jnp.concatenate([x_prompt.reshape(n_p, d), x_sample.reshape(n_s, d)], axis=0)
    q_all, ckv_all, kpe_all, u_all, k_all, v_all = _mixer_inputs(x_all, sh_m, sc_m, n_prompt_tiles, p)

    attn_p = _prompt_attention(q_all, k_all, v_all, seq)
    ckv_s = ckv_all[n_p:].reshape(n_seq, t_new, KV_LORA)
    kpe_s = kpe_all[n_p:].reshape(n_seq, t_new, QK_ROPE)
    ckv_new_pad = jnp.pad(ckv_s, ((0, 0), (0, LANES - t_new), (0, 0)))
    kpe_new_pad = jnp.pad(kpe_s, ((0, 0), (0, LANES - t_new), (0, 0)))
    q_s = q_all[n_p:].reshape(n_seq, t_new, N_HEADS * HEAD_PAD)
    attn_s = _sample_attention(page_table, q_s, ckv_new_pad, kpe_new_pad, cache_ckv, cache_kpe, p)

    conv_p = _prompt_conv(u_all, w_dw[l], b_dw[l].reshape(1, CONV_CH))
    u_s = u_all[n_p:].reshape(n_seq, t_new, CONV_CH)
    u_ext_s = jnp.concatenate([state_conv[l], u_s], axis=1)
    ext = CONV_W - 1 + t_new
    ext_pad = -ext % 8
    w_taps = jnp.stack([jnp.pad(w_dw[l], ((t, t_new - 1 - t + ext_pad), (0, 0))) for t in range(t_new)])
    conv_s = _sample_conv(jnp.pad(u_ext_s, ((0, 0), (0, ext_pad), (0, 0))), w_taps, b_dw[l].reshape(1, CONV_CH))
    conv_s = jnp.transpose(conv_s, (1, 0, 2)).reshape(n_s, CONV_CH)

    attn_all = jnp.concatenate([attn_p, attn_s.reshape(n_s, ATTN_WIDTH)], axis=0)
    conv_all = jnp.concatenate([conv_p[:n_p], conv_s], axis=0)
    y_all, f_all, top_i, top_w = _merge_router(x_all, attn_all, conv_all, gt_m, sh_f, sc_f, n_prompt_tiles, p)

    n_tiles = -(-(n * TOP_K + N_EXPERTS * (FFN_TILE - 1)) // FFN_TILE)
    n_slots = n_tiles * FFN_TILE
    pos_tok, meta = _route(top_i, n_tiles)
    pieces = D_MODEL // SC_ROW
    slot = jnp.transpose(pos_tok[:, :TOP_K])
    piece_idx = (slot[:, :, None] * pieces + jnp.arange(pieces, dtype=I32)[None, None, :]).reshape(-1)
    x_sorted = _sc_scatter_rows(f_all.reshape(n * pieces, SC_ROW), piece_idx, n_slots * pieces)
    h_sorted = _expert_ffn(meta[:n_tiles, 0], meta[0:1, 1], x_sorted.reshape(n_slots, D_MODEL),
                           w_exp_up[l], b_exp_up[l], w_exp_down[l], b_exp_down[l])
    gathered = _sc_gather_rows(h_sorted.reshape(n_slots * pieces, SC_ROW), piece_idx)
    y_out = _combine(y_all, gathered.reshape(TOP_K, n, D_MODEL), top_w, gt_f, n_prompt_tiles)

    y_p = y_out[:n_p].reshape(bsz, seq, d)
    y_s = y_out[n_p:].reshape(n_seq, t_new, d)
    ckv_prompt = ckv_all[:n_p].reshape(1, bsz, seq, KV_LORA)
    kpe_prompt = kpe_all[:n_p].reshape(1, bsz, seq, QK_ROPE)
    conv_prompt = u_all[n_p - (CONV_W - 1):n_p].reshape(1, bsz, CONV_W - 1, CONV_CH)
    ckv_sample = ckv_s[None]
    kpe_sample = kpe_s[None]
    conv_sample = u_ext_s[:, t_new:][None]
    return (y_p, y_s, ckv_prompt, kpe_prompt, conv_prompt, ckv_sample, kpe_sample, conv_sample)
```

```python
import functools

import numpy as np
import jax
import jax.numpy as jnp
from jax import lax
from jax.experimental import pallas as pl
from jax.experimental.pallas import tpu as pltpu
from jax.experimental.pallas import tpu_sc as plsc

F32 = jnp.float32
BF16 = jnp.bfloat16
I32 = jnp.int32
HIGHEST = lax.Precision.HIGHEST

D_MODEL = 1024
N_HEADS = 8
QK_NOPE = 64
QK_ROPE = 32
V_DIM = 64
Q_LORA = 256
KV_LORA = 128
ATTN_WIDTH = N_HEADS * V_DIM
CONV_CH = D_MODEL - ATTN_WIDTH
CONV_W = 31
N_EXPERTS = 32
TOP_K = 4
D_FF = D_MODEL
SWIGLU_LIMIT = 7.0
SWIGLU_ALPHA = 1.702
EPS = 1e-6
NEG_INF = -1e30
ROPE_THETA = 10000.0
SM_SCALE = (QK_NOPE + QK_ROPE) ** -0.5
LOG2E = 1.4426950408889634
Q_SCALE = SM_SCALE * LOG2E

LANES = 128
HEAD_PAD = LANES
ROPE_LO = QK_NOPE
ROPE_HALF = QK_ROPE // 2

TOK_TILE = 256
FA_TQ = 1024
FA_TK = 512
FA_QC = 1024
PAGES_PER_STEP = 16
FFN_TILE = 256
ROUTE_TILE = 512
SC_ROW = 256
SC_PIECES = D_MODEL // SC_ROW
SC_WINDOW = 128
VMEM_LIMIT = 56 * 1024 * 1024


def _cparams(sem, vmem=None):
    return pltpu.CompilerParams(dimension_semantics=sem, vmem_limit_bytes=vmem)


def _rsqrt_mean(x, n):
    return lax.rsqrt(jnp.sum(x * x, axis=-1, keepdims=True) * (1.0 / n) + EPS)


def _ada_kernel(c_ref, w_ref, b_ref, o_ref):
    c = c_ref[...]
    s = c * jax.nn.sigmoid(c)
    o_ref[...] = jnp.dot(s, w_ref[...], precision=HIGHEST, preferred_element_type=F32) + b_ref[...]


def _adaln(c_all, w_ada, b_ada):
    rows = c_all.shape[0]
    n_out = w_ada.shape[1]
    return pl.pallas_call(
        _ada_kernel,
        grid=(n_out // D_MODEL,),
        in_specs=[pl.BlockSpec((rows, D_MODEL), lambda j: (0, 0)),
                  pl.BlockSpec((D_MODEL, D_MODEL), lambda j: (0, j)),
                  pl.BlockSpec((1, D_MODEL), lambda j: (0, j))],
        out_specs=pl.BlockSpec((rows, D_MODEL), lambda j: (0, j)),
        out_shape=jax.ShapeDtypeStruct((rows, n_out), F32),
        compiler_params=_cparams(("arbitrary",)),
        name="adaln",
    )(c_all, w_ada, b_ada.reshape(1, n_out))


def _group_norm_rope(x, m_grp, gain, cos_t, sin_t, first_half):
    ms = jnp.dot((x * x).astype(BF16), m_grp, preferred_element_type=F32)
    xn = x * lax.rsqrt(ms + EPS) * gain
    swapped = jnp.where(first_half, pltpu.roll(xn, LANES - ROPE_HALF, 1), pltpu.roll(xn, ROPE_HALF, 1))
    return xn * cos_t + swapped * sin_t


def _mix_kernel(x_ref, sh_ref, sc_ref, gmix_ref, win_ref, gql_ref, wq_ref, gq_ref, m_ref, cos_ref, sin_ref,
                gkv_ref, gk_ref, gkpe_ref, wk_ref, wv_ref,
                q_out, ckv_out, kpe_out, u_out, k_out, v_out):
    x = x_ref[...]
    h = x * _rsqrt_mean(x, D_MODEL) * gmix_ref[...]
    h = h * (1.0 + sc_ref[...]) + sh_ref[...]
    proj = jnp.dot(h.astype(BF16), win_ref[...], preferred_element_type=F32)
    q_lat = proj[:, :Q_LORA]
    ckv_raw = proj[:, Q_LORA:Q_LORA + KV_LORA]
    kpe_blk = proj[:, Q_LORA + KV_LORA:Q_LORA + KV_LORA + LANES]
    glu_lo = Q_LORA + KV_LORA + LANES
    u_out[...] = proj[:, glu_lo:glu_lo + CONV_CH] * jax.nn.sigmoid(proj[:, glu_lo + CONV_CH:glu_lo + 2 * CONV_CH])

    m_grp = m_ref[...]
    cos_t = cos_ref[...]
    sin_t = sin_ref[...]
    lane = lax.broadcasted_iota(I32, (1, LANES), 1)
    first_half = lane < ROPE_LO + ROPE_HALF

    q_lat_n = q_lat * _rsqrt_mean(q_lat, Q_LORA) * gql_ref[...]
    q = jnp.dot(q_lat_n.astype(BF16), wq_ref[...], preferred_element_type=F32)
    gq = gq_ref[...]
    for hd in range(N_HEADS):
        qh = _group_norm_rope(q[:, hd * HEAD_PAD:(hd + 1) * HEAD_PAD], m_grp, gq, cos_t, sin_t, first_half)
        q_out[:, hd * HEAD_PAD:(hd + 1) * HEAD_PAD] = (qh * Q_SCALE).astype(BF16)

    ckv_n = ckv_raw * _rsqrt_mean(ckv_raw, KV_LORA) * gkv_ref[...]
    ckv_out[...] = ckv_n
    kpe_r = _group_norm_rope(kpe_blk, m_grp, gkpe_ref[...], cos_t, sin_t, first_half)
    kpe_out[...] = kpe_r[:, ROPE_LO:ROPE_LO + QK_ROPE]

    ckv_b = ckv_n.astype(BF16)
    kexp = jnp.dot(ckv_b, wk_ref[...], preferred_element_type=F32)
    gk = gk_ref[...]
    for hd in range(N_HEADS):
        kh = kexp[:, hd * HEAD_PAD:(hd + 1) * HEAD_PAD]
        ms = jnp.dot((kh * kh).astype(BF16), m_grp, preferred_element_type=F32)
        k_out[:, hd * HEAD_PAD:(hd + 1) * HEAD_PAD] = (kh * lax.rsqrt(ms + EPS) * gk + kpe_r).astype(BF16)
    v_out[...] = jnp.dot(ckv_b, wv_ref[...], preferred_element_type=F32).astype(BF16)


def _mod_index(n_prompt_tiles):
    return lambda i: (jnp.where(i < n_prompt_tiles, 0, i - n_prompt_tiles + 1), 0)


def _mixer_inputs(x_all, sh, sc, n_prompt_tiles, p):
    n = x_all.shape[0]
    tm = TOK_TILE
    const = lambda i: (0, 0)
    row = lambda i: (i, 0)
    mod = _mod_index(n_prompt_tiles)
    hw = N_HEADS * HEAD_PAD
    in_cols = p["w_in"].shape[1]
    return pl.pallas_call(
        _mix_kernel,
        grid=(n // tm,),
        in_specs=[pl.BlockSpec((tm, D_MODEL), row),
                  pl.BlockSpec((tm, D_MODEL), mod),
                  pl.BlockSpec((tm, D_MODEL), mod),
                  pl.BlockSpec((1, D_MODEL), const),
                  pl.BlockSpec((D_MODEL, in_cols), const),
                  pl.BlockSpec((1, Q_LORA), const),
                  pl.BlockSpec((Q_LORA, hw), const),
                  pl.BlockSpec((1, LANES), const),
                  pl.BlockSpec((LANES, LANES), const),
                  pl.BlockSpec((tm, LANES), row),
                  pl.BlockSpec((tm, LANES), row),
                  pl.BlockSpec((1, KV_LORA), const),
                  pl.BlockSpec((1, LANES), const),
                  pl.BlockSpec((1, LANES), const),
                  pl.BlockSpec((KV_LORA, hw), const),
                  pl.BlockSpec((KV_LORA, ATTN_WIDTH), const)],
        out_specs=[pl.BlockSpec((tm, hw), row),
                   pl.BlockSpec((tm, KV_LORA), row),
                   pl.BlockSpec((tm, QK_ROPE), row),
                   pl.BlockSpec((tm, CONV_CH), row),
                   pl.BlockSpec((tm, hw), row),
                   pl.BlockSpec((tm, ATTN_WIDTH), row)],
        out_shape=[jax.ShapeDtypeStruct((n, hw), BF16),
                   jax.ShapeDtypeStruct((n, KV_LORA), F32),
                   jax.ShapeDtypeStruct((n, QK_ROPE), F32),
                   jax.ShapeDtypeStruct((n, CONV_CH), F32),
                   jax.ShapeDtypeStruct((n, hw), BF16),
                   jax.ShapeDtypeStruct((n, ATTN_WIDTH), BF16)],
        compiler_params=_cparams(("parallel",), VMEM_LIMIT),
        name="mixer_inputs",
    )(x_all, sh, sc, p["g_norm_mix"], p["w_in"], p["g_q_lat"], p["w_q_up"], p["gain_q"], p["m_grp"],
      p["cos_t"], p["sin_t"], p["g_kv_lat"], p["gain_k"], p["gain_kpe"], p["w_k"], p["w_v"])


def _fa_kernel(qt_ref, kt_ref, q_ref, k_ref, v_ref, o_ref, m_sc, l_sc, acc_sc):
    t = pl.program_id(1)
    qi = qt_ref[t]
    ki = kt_ref[t]
    last_k = (qi + 1) * (FA_TQ // FA_TK) - 1

    @pl.when(ki == 0)
    def _():
        m_sc[...] = jnp.full(m_sc.shape, NEG_INF, F32)
        l_sc[...] = jnp.zeros(l_sc.shape, F32)
        acc_sc[...] = jnp.zeros(acc_sc.shape, F32)

    def step(masked):
        if masked:
            col_minus_row = (lax.broadcasted_iota(I32, (FA_QC, FA_TK), 1)
                             - lax.broadcasted_iota(I32, (FA_QC, FA_TK), 0))
        for hh in range(2):
            k = k_ref[:, hh * HEAD_PAD:(hh + 1) * HEAD_PAD]
            v = v_ref[:, hh * V_DIM:(hh + 1) * V_DIM]
            for c in range(FA_TQ // FA_QC):
                rows = pl.ds(c * FA_QC, FA_QC)
                q = q_ref[rows, hh * HEAD_PAD:(hh + 1) * HEAD_PAD]
                s = lax.dot_general(q, k, (((1,), (1,)), ((), ())), preferred_element_type=F32)
                if masked:
                    s = jnp.where(col_minus_row <= qi * FA_TQ + c * FA_QC - ki * FA_TK, s, NEG_INF)
                m_prev = m_sc[hh, rows, :]
                m_new = jnp.maximum(m_prev, jnp.max(s, axis=-1, keepdims=True))
                alpha = jnp.exp2(m_prev - m_new)
                pr = jnp.exp2(s - jnp.concatenate([m_new] * (FA_TK // LANES), axis=1))
                l_sc[hh, rows, :] = alpha * l_sc[hh, rows, :] + jnp.sum(pr, axis=-1, keepdims=True)
                acc_sc[hh, rows, :] = (alpha[:, :V_DIM] * acc_sc[hh, rows, :]
                                       + jnp.dot(pr.astype(BF16), v, preferred_element_type=F32))
                m_sc[hh, rows, :] = m_new

    @pl.when(ki * FA_TK + FA_TK - 1 <= qi * FA_TQ)
    def _():
        step(False)

    @pl.when(ki * FA_TK + FA_TK - 1 > qi * FA_TQ)
    def _():
        step(True)

    @pl.when(ki == last_k)
    def _():
        for hh in range(2):
            o_ref[:, hh * V_DIM:(hh + 1) * V_DIM] = acc_sc[hh] / l_sc[hh][:, :V_DIM]


def _prompt_attention(q_all, k_all, v_all, seq):
    nq = seq // FA_TQ
    ratio = FA_TQ // FA_TK
    qt, kt = [], []
    for qi in range(nq):
        for ki in range((qi + 1) * ratio):
            qt.append(qi)
            kt.append(ki)
    qt = jnp.asarray(np.array(qt, np.int32))
    kt = jnp.asarray(np.array(kt, np.int32))
    n_pairs = int(qt.shape[0])
    grid_spec = pltpu.PrefetchScalarGridSpec(
        num_scalar_prefetch=2,
        grid=(N_HEADS // 2, n_pairs),
        in_specs=[pl.BlockSpec((FA_TQ, 2 * HEAD_PAD), lambda hp, t, qt, kt: (qt[t], hp)),
                  pl.BlockSpec((FA_TK, 2 * HEAD_PAD), lambda hp, t, qt, kt: (kt[t], hp)),
                  pl.BlockSpec((FA_TK, 2 * V_DIM), lambda hp, t, qt, kt: (kt[t], hp))],
        out_specs=pl.BlockSpec((FA_TQ, 2 * V_DIM), lambda hp, t, qt, kt: (qt[t], hp)),
        scratch_shapes=[pltpu.VMEM((2, FA_TQ, LANES), F32),
                        pltpu.VMEM((2, FA_TQ, LANES), F32),
                        pltpu.VMEM((2, FA_TQ, V_DIM), F32)],
    )
    return pl.pallas_call(
        _fa_kernel,
        grid_spec=grid_spec,
        out_shape=jax.ShapeDtypeStruct((seq, ATTN_WIDTH), F32),
        compiler_params=_cparams(("parallel", "arbitrary"), VMEM_LIMIT),
        name="prompt_attention",
    )(qt, kt, q_all, k_all, v_all)


def _sattn_kernel(pt_ref, q_ref, wabs_ref, wkt_ref, wv_ref, ckvn_ref, kpen_ref, *rest):
    ckv_pages = rest[:PAGES_PER_STEP]
    kpe_pages = rest[PAGES_PER_STEP:2 * PAGES_PER_STEP]
    o_ref, m_sc, l_sc, acc_sc, qa_sc = rest[2 * PAGES_PER_STEP:]
    kb = pl.program_id(1)
    n_kb = pl.num_programs(1)
    rows = 4 * N_HEADS

    @pl.when(kb == 0)
    def _():
        m_sc[...] = jnp.full(m_sc.shape, NEG_INF, F32)
        l_sc[...] = jnp.zeros(l_sc.shape, F32)
        acc_sc[...] = jnp.zeros(acc_sc.shape, F32)
        q4 = q_ref[...].astype(F32)
        head_of_lane = lax.broadcasted_iota(I32, (N_HEADS, N_HEADS * HEAD_PAD), 1) // HEAD_PAD
        head_of_row = lax.broadcasted_iota(I32, (N_HEADS, N_HEADS * HEAD_PAD), 0)
        own = head_of_lane == head_of_row
        qbd = jnp.concatenate(
            [jnp.where(own, jnp.broadcast_to(q4[qq:qq + 1, :], own.shape), 0.0) for qq in range(4)], axis=0)
        qa_sc[...] = jnp.dot(qbd.astype(BF16), wabs_ref[...], preferred_element_type=F32).astype(BF16)

    def block(ckv_b, kpe_t, mask):
        nt = (((1,), (1,)), ((), ()))
        kn_t = lax.dot_general(wkt_ref[...], ckv_b, nt, preferred_element_type=F32)
        keys = kn_t.shape[1]
        ss = jnp.sum((kn_t * kn_t).reshape(N_HEADS, QK_NOPE, keys), axis=1)
        r8 = lax.rsqrt(ss * (1.0 / QK_NOPE) + EPS)
        qa = qa_sc[...]
        a = lax.dot_general(qa[:, :KV_LORA], ckv_b, nt, preferred_element_type=F32)
        b = jnp.dot(qa[:, KV_LORA:KV_LORA + QK_ROPE], kpe_t, preferred_element_type=F32)
        s = a * jnp.concatenate([r8] * 4, axis=0) + b
        if mask is not None:
            s = jnp.where(mask, s, NEG_INF)
        m_prev = m_sc[...]
        m_new = jnp.maximum(m_prev, jnp.max(s, axis=-1, keepdims=True))
        alpha = jnp.exp2(m_prev - m_new)
        pr = jnp.exp2(s - m_new)
        l_sc[...] = alpha * l_sc[...] + jnp.sum(pr, axis=-1, keepdims=True)
        acc_sc[...] = alpha * acc_sc[...] + jnp.dot(pr.astype(BF16), ckv_b, preferred_element_type=F32)
        m_sc[...] = m_new

    ckv_b = jnp.concatenate([r[...].astype(BF16) for r in ckv_pages], axis=0)
    kpe_t = jnp.concatenate([r[...].astype(BF16) for r in kpe_pages], axis=1)
    block(ckv_b, kpe_t, None)

    @pl.when(kb == n_kb - 1)
    def _():
        key = lax.broadcasted_iota(I32, (rows, LANES), 1)
        qry = lax.broadcasted_iota(I32, (rows, LANES), 0) // N_HEADS
        block(ckvn_ref[...].astype(BF16), kpen_ref[...].astype(BF16), key <= qry)
        lat = acc_sc[...] / l_sc[...]
        o_all = jnp.dot(lat.astype(BF16), wv_ref[...], preferred_element_type=F32)
        head_of_col = lax.broadcasted_iota(I32, (rows, ATTN_WIDTH), 1) // V_DIM
        head_of_row = lax.broadcasted_iota(I32, (rows, ATTN_WIDTH), 0) % N_HEADS
        o_own = jnp.where(head_of_col == head_of_row, o_all, 0.0)
        o_ref[...] = jnp.sum(o_own.reshape(4, N_HEADS, ATTN_WIDTH), axis=1)


def _sample_attention(page_table, q_s, ckv_new_pad, kpe_new_t, cache_ckv, cache_kpe_t, p):
    n_seq, n_pages = page_table.shape
    page = cache_ckv.shape[2]
    n_kb = n_pages // PAGES_PER_STEP
    hw = N_HEADS * HEAD_PAD

    def ckv_map(j):
        return lambda b, kb, pt: (0, pt[b * n_pages + kb * PAGES_PER_STEP + j], 0, 0)

    per_seq3 = lambda b, kb, pt: (b, 0, 0)
    const = lambda b, kb, pt: (0, 0)
    in_specs = [pl.BlockSpec((None, 4, hw), per_seq3),
                pl.BlockSpec((hw, 2 * LANES), const),
                pl.BlockSpec((N_HEADS * QK_NOPE, KV_LORA), const),
                pl.BlockSpec((KV_LORA, ATTN_WIDTH), const),
                pl.BlockSpec((None, LANES, KV_LORA), per_seq3),
                pl.BlockSpec((None, QK_ROPE, LANES), per_seq3)]
    in_specs += [pl.BlockSpec((None, None, page, KV_LORA), ckv_map(j)) for j in range(PAGES_PER_STEP)]
    in_specs += [pl.BlockSpec((None, None, QK_ROPE, page), ckv_map(j)) for j in range(PAGES_PER_STEP)]
    grid_spec = pltpu.PrefetchScalarGridSpec(
        num_scalar_prefetch=1,
        grid=(n_seq, n_kb),
        in_specs=in_specs,
        out_specs=pl.BlockSpec((None, 4, ATTN_WIDTH), per_seq3),
        scratch_shapes=[pltpu.VMEM((4 * N_HEADS, 1), F32),
                        pltpu.VMEM((4 * N_HEADS, 1), F32),
                        pltpu.VMEM((4 * N_HEADS, KV_LORA), F32),
                        pltpu.VMEM((4 * N_HEADS, 2 * LANES), BF16)],
    )
    return pl.pallas_call(
        _sattn_kernel,
        grid_spec=grid_spec,
        out_shape=jax.ShapeDtypeStruct((n_seq, 4, ATTN_WIDTH), F32),
        compiler_params=_cparams(("parallel", "arbitrary"), VMEM_LIMIT),
        name="decode_attention",
    )(page_table.reshape(-1), q_s, p["w_abs"], p["w_kt"], p["w_v"], ckv_new_pad, kpe_new_t,
      *([cache_ckv] * PAGES_PER_STEP), *([cache_kpe_t] * PAGES_PER_STEP))


CONV_HALO = 32
CONV_ROWS = 64


def _conv_kernel(halo_ref, u_ref, w_ref, b_ref, o_ref, ext_sc):
    i = pl.program_id(0)
    tm = u_ref.shape[0]
    ext_sc[pl.ds(0, CONV_HALO), :] = jnp.where(i == 0, 0.0, halo_ref[...])
    ext_sc[pl.ds(CONV_HALO, tm), :] = u_ref[...]
    first = CONV_HALO - (CONV_W - 1)
    for rc in range(tm // CONV_ROWS):
        acc = jnp.broadcast_to(b_ref[...], (CONV_ROWS, CONV_CH))
        for j in range(CONV_W):
            acc = acc + ext_sc[pl.ds(rc * CONV_ROWS + first + j, CONV_ROWS), :] * w_ref[j:j + 1, :]
        o_ref[pl.ds(rc * CONV_ROWS, CONV_ROWS), :] = acc


def _prompt_conv(u_all, w_dw, b_dw):
    n = u_all.shape[0]
    tm = TOK_TILE
    per = tm // CONV_HALO
    return pl.pallas_call(
        _conv_kernel,
        grid=(n // tm,),
        in_specs=[pl.BlockSpec((CONV_HALO, CONV_CH), lambda i: (jnp.maximum(i * per - 1, 0), 0)),
                  pl.BlockSpec((tm, CONV_CH), lambda i: (i, 0)),
                  pl.BlockSpec((CONV_W, CONV_CH), lambda i: (0, 0)),
                  pl.BlockSpec((1, CONV_CH), lambda i: (0, 0))],
        out_specs=pl.BlockSpec((tm, CONV_CH), lambda i: (i, 0)),
        out_shape=jax.ShapeDtypeStruct((n, CONV_CH), F32),
        scratch_shapes=[pltpu.VMEM((CONV_HALO + tm, CONV_CH), F32)],
        compiler_params=_cparams(("parallel",)),
        name="prompt_conv",
    )(u_all, u_all, w_dw, b_dw)


def _sconv_kernel(u_ref, wt_ref, b_ref, o_ref):
    u = u_ref[...]
    for t in range(o_ref.shape[0]):
        o_ref[t] = jnp.sum(u * wt_ref[t][None, :, :], axis=1) + b_ref[...]


def _sample_conv(u_ext, w_taps, b_dw):
    n_seq, ext, _ = u_ext.shape
    t_new = w_taps.shape[0]
    sb = 8
    return pl.pallas_call(
        _sconv_kernel,
        grid=(n_seq // sb,),
        in_specs=[pl.BlockSpec((sb, ext, CONV_CH), lambda i: (i, 0, 0)),
                  pl.BlockSpec((t_new, ext, CONV_CH), lambda i: (0, 0, 0)),
                  pl.BlockSpec((1, CONV_CH), lambda i: (0, 0))],
        out_specs=pl.BlockSpec((t_new, sb, CONV_CH), lambda i: (0, i, 0)),
        out_shape=jax.ShapeDtypeStruct((t_new, n_seq, CONV_CH), F32),
        compiler_params=_cparams(("parallel",)),
        name="decode_conv",
    )(u_ext, w_taps, b_dw)


def _lane_pack(cols, dtype):
    lane = lax.broadcasted_iota(I32, (cols[0].shape[0], LANES), 1)
    out = jnp.zeros((cols[0].shape[0], LANES), dtype)
    for j, c in enumerate(cols):
        out = jnp.where(lane == j, c.astype(dtype), out)
    return out


def _merge_kernel(x_ref, attn_ref, conv_ref, gate_ref, shf_ref, scf_ref, gln_ref, bln_ref, ga_ref, gc_ref,
                  wout_ref, gffn_ref, wr_ref, br_ref, y_out, f_out, ti_out, tw_out):
    yc = conv_ref[...]
    mu = jnp.mean(yc, axis=-1, keepdims=True)
    xc = yc - mu
    var = jnp.mean(xc * xc, axis=-1, keepdims=True)
    ln = xc * lax.rsqrt(var + EPS) * gln_ref[...] + bln_ref[...]
    conv = ln * jax.nn.sigmoid(ln)
    attn = attn_ref[...]
    a_n = attn * _rsqrt_mean(attn, ATTN_WIDTH) * ga_ref[...]
    c_n = conv * _rsqrt_mean(conv, CONV_CH) * gc_ref[...]
    m = (jnp.dot(a_n.astype(BF16), wout_ref[:ATTN_WIDTH, :], preferred_element_type=F32)
         + jnp.dot(c_n.astype(BF16), wout_ref[ATTN_WIDTH:, :], preferred_element_type=F32))
    y = x_ref[...] + gate_ref[...] * m
    y_out[...] = y
    f = y * _rsqrt_mean(y, D_MODEL) * gffn_ref[...]
    f = f * (1.0 + scf_ref[...]) + shf_ref[...]
    for j in range(SC_PIECES):
        f_out[j] = f[:, j * SC_ROW:(j + 1) * SC_ROW]
    logits = jnp.dot(f, wr_ref[...], precision=HIGHEST, preferred_element_type=F32) + br_ref[...]
    lane = lax.broadcasted_iota(I32, logits.shape, 1)
    vals, idxs = [], []
    for _ in range(TOP_K):
        mx = jnp.max(logits, axis=-1, keepdims=True)
        ix = jnp.min(jnp.where(logits == mx, lane, LANES), axis=-1, keepdims=True)
        vals.append(mx)
        idxs.append(ix)
        logits = jnp.where(lane == ix, NEG_INF * 4.0, logits)
    exps = [jnp.exp(v - vals[0]) for v in vals]
    tot = exps[0] + exps[1] + exps[2] + exps[3]
    ti_out[...] = _lane_pack(idxs, I32)
    tw_out[...] = _lane_pack([e / tot for e in exps], F32)


def _merge_router(x_all, attn_all, conv_all, gate, shf, scf, n_prompt_tiles, p):
    n = x_all.shape[0]
    tm = TOK_TILE
    const = lambda i: (0, 0)
    row = lambda i: (i, 0)
    mod = _mod_index(n_prompt_tiles)
    return pl.pallas_call(
        _merge_kernel,
        grid=(n // tm,),
        in_specs=[pl.BlockSpec((tm, D_MODEL), row),
                  pl.BlockSpec((tm, ATTN_WIDTH), row),
                  pl.BlockSpec((tm, CONV_CH), row),
                  pl.BlockSpec((tm, D_MODEL), mod),
                  pl.BlockSpec((tm, D_MODEL), mod),
                  pl.BlockSpec((tm, D_MODEL), mod),
                  pl.BlockSpec((1, CONV_CH), const),
                  pl.BlockSpec((1, CONV_CH), const),
                  pl.BlockSpec((1, ATTN_WIDTH), const),
                  pl.BlockSpec((1, CONV_CH), const),
                  pl.BlockSpec((D_MODEL, D_MODEL), const),
                  pl.BlockSpec((1, D_MODEL), const),
                  pl.BlockSpec((D_MODEL, LANES), const),
                  pl.BlockSpec((1, LANES), const)],
        out_specs=[pl.BlockSpec((tm, D_MODEL), row),
                   pl.BlockSpec((SC_PIECES, tm, SC_ROW), lambda i: (0, i, 0)),
                   pl.BlockSpec((tm, LANES), row),
                   pl.BlockSpec((tm, LANES), row)],
        out_shape=[jax.ShapeDtypeStruct((n, D_MODEL), F32),
                   jax.ShapeDtypeStruct((SC_PIECES, n, SC_ROW), F32),
                   jax.ShapeDtypeStruct((n, LANES), I32),
                   jax.ShapeDtypeStruct((n, LANES), F32)],
        compiler_params=_cparams(("parallel",), VMEM_LIMIT),
        name="merge_router",
    )(x_all, attn_all, conv_all, gate, shf, scf, p["g_conv_ln"], p["b_conv_ln"], p["g_out_attn"],
      p["g_out_conv"], p["w_out"], p["g_norm_ffn"], p["w_router"], p["b_router"])


def _select_lane(table, idx_col, lane):
    return jnp.sum(jnp.where(lane == idx_col, table, 0.0), axis=-1, keepdims=True)


def _rank_kernel(ti_ref, rk_out, cnt_out, carry_sc):
    i = pl.program_id(0)

    @pl.when(i == 0)
    def _():
        carry_sc[...] = jnp.zeros(carry_sc.shape, F32)

    ti = ti_ref[...]
    tn = ti.shape[0]
    lane = lax.broadcasted_iota(I32, (tn, LANES), 1)
    sel = jnp.zeros((tn, LANES), F32)
    for k in range(TOP_K):
        sel = sel + (lane == ti[:, k:k + 1]).astype(F32)
    r_i = lax.broadcasted_iota(I32, (tn, tn), 0)
    c_i = lax.broadcasted_iota(I32, (tn, tn), 1)
    below = (c_i < r_i).astype(BF16)
    rank = carry_sc[...] + jnp.dot(below, sel.astype(BF16), preferred_element_type=F32)
    rk_out[...] = _lane_pack([_select_lane(rank, ti[:, k:k + 1], lane) for k in range(TOP_K)], F32)
    carry_sc[...] = carry_sc[...] + jnp.sum(sel, axis=0, keepdims=True)
    cnt_out[...] = jnp.broadcast_to(carry_sc[...], cnt_out.shape)


def _pos_kernel(cnt_ref, ti_ref, rk_ref, pos_out, meta_out):
    cnt = cnt_ref[...]
    padded = jnp.ceil(cnt * (1.0 / FFN_TILE)) * FFN_TILE
    r_i = lax.broadcasted_iota(I32, (LANES, LANES), 0)
    c_i = lax.broadcasted_iota(I32, (LANES, LANES), 1)
    before = (r_i < c_i).astype(F32)
    offs = jnp.dot(padded, before, precision=HIGHEST, preferred_element_type=F32)
    ends = offs + padded
    ti = ti_ref[...]
    tn = ti.shape[0]
    lane = lax.broadcasted_iota(I32, (tn, LANES), 1)
    off_row = offs[0:1, :]
    rk = rk_ref[...]
    pos = [_select_lane(jnp.broadcast_to(off_row, (tn, LANES)), ti[:, k:k + 1], lane) + rk[:, k:k + 1]
           for k in range(TOP_K)]
    pos_out[...] = _lane_pack(pos, F32).astype(I32)

    @pl.when(pl.program_id(0) == 0)
    def _():
        nt = meta_out.shape[0]
        start = (lax.broadcasted_iota(I32, (nt, LANES), 0) * FFN_TILE).astype(F32)
        elane = lax.broadcasted_iota(I32, (nt, LANES), 1)
        done = jnp.where((elane < N_EXPERTS) & (jnp.broadcast_to(ends[0:1, :], (nt, LANES)) <= start), 1.0, 0.0)
        expert = jnp.minimum(jnp.sum(done, axis=-1, keepdims=True), N_EXPERTS - 1.0)
        total = jnp.sum(jnp.where(elane < N_EXPERTS, jnp.broadcast_to(padded[0:1, :], (nt, LANES)), 0.0),
                        axis=-1, keepdims=True)
        meta_out[...] = _lane_pack([expert, total * (1.0 / FFN_TILE)], F32).astype(I32)


def _route(top_i, n_tiles_max):
    n = top_i.shape[0]
    tn = ROUTE_TILE
    rk, cnt = pl.pallas_call(
        _rank_kernel,
        grid=(n // tn,),
        in_specs=[pl.BlockSpec((tn, LANES), lambda i: (i, 0))],
        out_specs=[pl.BlockSpec((tn, LANES), lambda i: (i, 0)),
                   pl.BlockSpec((8, LANES), lambda i: (0, 0))],
        out_shape=[jax.ShapeDtypeStruct((n, LANES), F32),
                   jax.ShapeDtypeStruct((8, LANES), F32)],
        scratch_shapes=[pltpu.VMEM((1, LANES), F32)],
        compiler_params=_cparams(("arbitrary",)),
        name="route_rank",
    )(top_i)
    nt_pad = -(-n_tiles_max // 8) * 8
    pos, meta = pl.pallas_call(
        _pos_kernel,
        grid=(n // tn,),
        in_specs=[pl.BlockSpec((8, LANES), lambda i: (0, 0)),
                  pl.BlockSpec((tn, LANES), lambda i: (i, 0)),
                  pl.BlockSpec((tn, LANES), lambda i: (i, 0))],
        out_specs=[pl.BlockSpec((tn, LANES), lambda i: (i, 0)),
                   pl.BlockSpec((nt_pad, LANES), lambda i: (0, 0))],
        out_shape=[jax.ShapeDtypeStruct((n, LANES), I32),
                   jax.ShapeDtypeStruct((nt_pad, LANES), I32)],
        compiler_params=_cparams(("arbitrary",)),
        name="route_pos",
    )(cnt, top_i, rk)
    return pos, meta


def _sc_scatter_rows(x, idx, n_out):
    n_src = x.shape[0]
    n = idx.shape[0]
    n_src_blk = n_src // SC_WINDOW
    mesh = plsc.VectorSubcoreMesh(core_axis_name="c", subcore_axis_name="s")

    @pl.kernel(out_type=jax.ShapeDtypeStruct((n_out, SC_ROW), x.dtype), mesh=mesh)
    def k(x_hbm, i_hbm, o_hbm):
        def body(x_vmem, i_vmem):
            pltpu.sync_copy(x_vmem, o_hbm.at[i_vmem.at[0]])

        pltpu.emit_pipeline(
            body,
            grid=(n // SC_WINDOW,),
            in_specs=[pl.BlockSpec((SC_WINDOW, SC_ROW), index_map=lambda i: (i % n_src_blk, 0)),
                      pl.BlockSpec((1, SC_WINDOW), index_map=lambda i: (0, i))],
            out_specs=[],
            core_axis_name=("c", "s"),
            dimension_semantics=(pltpu.PARALLEL,),
        )(x_hbm, i_hbm)

    return k(x, idx.reshape(1, n))


def _sc_gather_rows(x, idx):
    n = idx.shape[0]
    mesh = plsc.VectorSubcoreMesh(core_axis_name="c", subcore_axis_name="s")

    @pl.kernel(out_type=jax.ShapeDtypeStruct((n, SC_ROW), x.dtype), mesh=mesh)
    def k(x_hbm, i_hbm, o_hbm):
        def body(i_vmem, o_vmem):
            pltpu.sync_copy(x_hbm.at[i_vmem.at[0]], o_vmem)

        pltpu.emit_pipeline(
            body,
            grid=(n // SC_WINDOW,),
            in_specs=[pl.BlockSpec((1, SC_WINDOW), index_map=lambda i: (0, i))],
            out_specs=[pl.BlockSpec((SC_WINDOW, SC_ROW), index_map=lambda i: (i, 0))],
            core_axis_name=("c", "s"),
            dimension_semantics=(pltpu.PARALLEL,),
        )(i_hbm, o_hbm)

    return k(x, idx.reshape(1, n))


def _ffn_kernel(te_ref, nv_ref, x_ref, wu_ref, bu_ref, wd_ref, bd_ref, o_ref, wu_sc, wd_sc):
    t = pl.program_id(0)
    valid = t < nv_ref[0]
    prev = te_ref[jnp.maximum(t - 1, 0)]
    fresh = jnp.logical_or(t == 0, te_ref[t] != prev)

    @pl.when(jnp.logical_and(valid, fresh))
    def _():
        wu_sc[...] = wu_ref[...].astype(BF16)
        wd_sc[...] = wd_ref[...].astype(BF16)

    @pl.when(valid)
    def _():
        x = jnp.concatenate([x_ref[j] for j in range(SC_PIECES)], axis=1)
        z = jnp.dot(x.astype(BF16), wu_sc[...], preferred_element_type=F32) + bu_ref[...]
        zg = jnp.minimum(z[:, :D_FF], SWIGLU_LIMIT)
        zl = jnp.clip(z[:, D_FF:], -SWIGLU_LIMIT, SWIGLU_LIMIT)
        act = zg * jax.nn.sigmoid(SWIGLU_ALPHA * zg) * (zl + 1.0)
        out = jnp.dot(act.astype(BF16), wd_sc[...], preferred_element_type=F32) + bd_ref[...]
        for j in range(SC_PIECES):
            o_ref[j] = out[:, j * SC_ROW:(j + 1) * SC_ROW]


def _expert_ffn(tile_expert, n_valid, x_sorted, w_up, b_up, w_down, b_down):
    n_slots = x_sorted.shape[1]
    n_tiles = n_slots // FFN_TILE
    xmap = lambda t, te, nv: (0, jnp.minimum(t, nv[0] - 1), 0)
    emap = lambda t, te, nv: (te[t], 0, 0)
    grid_spec = pltpu.PrefetchScalarGridSpec(
        num_scalar_prefetch=2,
        grid=(n_tiles,),
        in_specs=[pl.BlockSpec((SC_PIECES, FFN_TILE, SC_ROW), xmap),
                  pl.BlockSpec((None, D_MODEL, 2 * D_FF), emap),
                  pl.BlockSpec((None, 1, 2 * D_FF), emap),
                  pl.BlockSpec((None, D_FF, D_MODEL), emap),
                  pl.BlockSpec((None, 1, D_MODEL), emap)],
        out_specs=pl.BlockSpec((SC_PIECES, FFN_TILE, SC_ROW), xmap),
        scratch_shapes=[pltpu.VMEM((D_MODEL, 2 * D_FF), BF16),
                        pltpu.VMEM((D_FF, D_MODEL), BF16)],
    )
    return pl.pallas_call(
        _ffn_kernel,
        grid_spec=grid_spec,
        out_shape=jax.ShapeDtypeStruct((SC_PIECES, n_slots, SC_ROW), F32),
        compiler_params=_cparams(("arbitrary",), VMEM_LIMIT),
        name="expert_ffn",
    )(tile_expert, n_valid, x_sorted, w_up, b_up.reshape(N_EXPERTS, 1, 2 * D_FF), w_down,
      b_down.reshape(N_EXPERTS, 1, D_MODEL))


def _combine_kernel(y_ref, g_ref, tw_ref, gate_ref, o_ref):
    tw = tw_ref[...]
    rows = lambda k: jnp.concatenate([g_ref[k, j] for j in range(SC_PIECES)], axis=1)
    moe = rows(0) * tw[:, 0:1]
    for k in range(1, TOP_K):
        moe = moe + rows(k) * tw[:, k:k + 1]
    o_ref[...] = y_ref[...] + gate_ref[...] * moe


def _combine(y_all, gathered, top_w, gate, n_prompt_tiles):
    n = y_all.shape[0]
    tm = TOK_TILE
    row = lambda i: (i, 0)
    return pl.pallas_call(
        _combine_kernel,
        grid=(n // tm,),
        in_specs=[pl.BlockSpec((tm, D_MODEL), row),
                  pl.BlockSpec((TOP_K, SC_PIECES, tm, SC_ROW), lambda i: (0, 0, i, 0)),
                  pl.BlockSpec((tm, LANES), row),
                  pl.BlockSpec((tm, D_MODEL), _mod_index(n_prompt_tiles))],
        out_specs=pl.BlockSpec((tm, D_MODEL), row),
        out_shape=jax.ShapeDtypeStruct((n, D_MODEL), F32),
        compiler_params=_cparams(("parallel",), VMEM_LIMIT),
        name="moe_combine",
    )(y_all, gathered, top_w, gate)


def _head_tiles(nope, rope):
    pad = jnp.zeros(nope.shape[:-1] + (HEAD_PAD - QK_NOPE - QK_ROPE,), nope.dtype)
    t = jnp.concatenate([nope, rope, pad], axis=-1)
    return t.reshape(t.shape[:-2] + (N_HEADS * HEAD_PAD,))


def _prepare(pos, w_in, g_norm_mix, g_q_lat, w_q_up, g_q_nope, g_q_rope, g_kv_lat, g_k_rope, w_kv_up, g_k_nope,
             g_conv_ln, b_conv_ln, g_out_attn, g_out_conv, w_out, g_norm_ffn, w_router, b_router):
    z32 = jnp.zeros((HEAD_PAD - QK_NOPE - QK_ROPE,), F32)
    z64 = jnp.zeros((QK_NOPE,), F32)
    d = D_MODEL
    kpe_cols = w_in[:, Q_LORA + KV_LORA:Q_LORA + KV_LORA + QK_ROPE]
    kpe_tile = jnp.concatenate([jnp.zeros((d, QK_NOPE), F32), kpe_cols, jnp.zeros((d, z32.shape[0]), F32)], axis=1)
    w_in_r = jnp.concatenate([w_in[:, :Q_LORA + KV_LORA], kpe_tile, w_in[:, Q_LORA + KV_LORA + QK_ROPE:]], axis=1)
    wq = w_q_up.reshape(Q_LORA, N_HEADS, QK_NOPE + QK_ROPE)
    w_q_r = _head_tiles(wq[..., :QK_NOPE], wq[..., QK_NOPE:])
    wk = w_kv_up[..., :QK_NOPE]
    wv = w_kv_up[..., QK_NOPE:].reshape(KV_LORA, ATTN_WIDTH)
    w_k_r = _head_tiles(wk, jnp.zeros((KV_LORA, N_HEADS, QK_ROPE), F32))
    lane = np.arange(LANES)
    grp = np.where(lane < QK_NOPE, 0, np.where(lane < QK_NOPE + QK_ROPE, 1, 2))
    m_grp = ((grp[:, None] == grp[None, :]) & (grp[:, None] < 2)).astype(np.float32)
    m_grp = m_grp / np.where(grp < 1, QK_NOPE, QK_ROPE)[None, :]
    inv = ROPE_THETA ** (-jnp.arange(0, QK_ROPE, 2, dtype=F32) / QK_ROPE)
    ang = pos.astype(F32)[:, None] * inv[None, :]
    cs, sn = jnp.cos(ang), jnp.sin(ang)
    n = pos.shape[0]
    cos_t = jnp.concatenate([jnp.ones((n, QK_NOPE), F32), cs, cs, jnp.zeros((n, z32.shape[0]), F32)], axis=1)
    sin_t = jnp.concatenate([jnp.zeros((n, QK_NOPE), F32), -sn, sn, jnp.zeros((n, z32.shape[0]), F32)], axis=1)
    wk_g = wk * g_k_nope[None, None, :]
    absorb = jnp.zeros((N_HEADS, HEAD_PAD, 2 * LANES), F32)
    absorb = absorb.at[:, :QK_NOPE, :KV_LORA].set(jnp.transpose(wk_g, (1, 2, 0)))
    absorb = absorb.at[:, ROPE_LO:ROPE_LO + QK_ROPE, KV_LORA:KV_LORA + QK_ROPE].set(
        jnp.broadcast_to(jnp.eye(QK_ROPE, dtype=F32), (N_HEADS, QK_ROPE, QK_ROPE)))
    w_kt = jnp.transpose(wk, (1, 2, 0)).reshape(N_HEADS * QK_NOPE, KV_LORA)
    wr = jnp.concatenate([w_router, jnp.zeros((d, LANES - N_EXPERTS), F32)], axis=1)
    br = jnp.concatenate([b_router, jnp.full((LANES - N_EXPERTS,), NEG_INF, F32)])
    return {
        "w_in": w_in_r.astype(BF16), "g_norm_mix": g_norm_mix.reshape(1, d), "g_q_lat": g_q_lat.reshape(1, Q_LORA),
        "w_q_up": w_q_r.astype(BF16),
        "gain_q": jnp.concatenate([g_q_nope, g_q_rope, z32]).reshape(1, LANES),
        "m_grp": jnp.asarray(m_grp, BF16), "cos_t": cos_t, "sin_t": sin_t,
        "g_kv_lat": g_kv_lat.reshape(1, KV_LORA),
        "gain_k": jnp.concatenate([g_k_nope, z64]).reshape(1, LANES),
        "gain_kpe": jnp.concatenate([z64, g_k_rope, z32]).reshape(1, LANES),
        "w_k": w_k_r.astype(BF16), "w_v": wv.astype(BF16),
        "w_abs": absorb.reshape(N_HEADS * HEAD_PAD, 2 * LANES).astype(BF16), "w_kt": w_kt.astype(BF16),
        "g_conv_ln": g_conv_ln.reshape(1, CONV_CH), "b_conv_ln": b_conv_ln.reshape(1, CONV_CH),
        "g_out_attn": g_out_attn.reshape(1, ATTN_WIDTH), "g_out_conv": g_out_conv.reshape(1, CONV_CH),
        "w_out": w_out.astype(BF16), "g_norm_ffn": g_norm_ffn.reshape(1, d),
        "w_router": wr, "b_router": br.reshape(1, LANES),
    }


def _mod_table(mod_p, mod_s, t_new):
    return jnp.concatenate([jnp.broadcast_to(mod_p, (TOK_TILE, D_MODEL)), jnp.repeat(mod_s, t_new, axis=0)], axis=0)


def kernel(x_prompt, x_sample, cache_ckv, cache_kpe, state_conv, page_table, c_prompt, c_sample, w_ada, b_ada, g_norm_mix, g_norm_ffn, w_in, g_q_lat, w_q_up, g_q_nope, g_q_rope, g_kv_lat, g_k_rope, w_kv_up, g_k_nope, w_dw, b_dw, g_conv_ln, b_conv_ln, g_out_attn, g_out_conv, w_out, w_router, b_router, w_exp_up, b_exp_up, w_exp_down, b_exp_down):
    bsz, seq, d = x_prompt.shape
    n_seq, t_new = x_sample.shape[:2]
    depth = w_ada.shape[0]
    assert bsz == 1 and depth == 1 and t_new == 4 and d == D_MODEL
    n_p = bsz * seq
    n_s = n_seq * t_new
    n = n_p + n_s
    past = page_table.shape[1] * cache_ckv.shape[2]
    n_prompt_tiles = n_p // TOK_TILE
    l = 0

    pos = jnp.concatenate([jnp.arange(seq, dtype=I32), jnp.tile(past + jnp.arange(t_new, dtype=I32), n_seq)])
    p = _prepare(pos, w_in[l], g_norm_mix[l], g_q_lat[l], w_q_up[l], g_q_nope[l], g_q_rope[l], g_kv_lat[l],
                 g_k_rope[l], w_kv_up[l], g_k_nope[l], g_conv_ln[l], b_conv_ln[l], g_out_attn[l], g_out_conv[l],
                 w_out[l], g_norm_ffn[l], w_router[l], b_router[l])

    n_c = 1 + n_seq
    c_all = jnp.concatenate([c_prompt, c_sample, jnp.zeros((-n_c % 8, d), F32)], axis=0)
    mod = _adaln(c_all, w_ada[l], b_ada[l])
    tabs = [_mod_table(mod[0:1, j * d:(j + 1) * d], mod[1:n_c, j * d:(j + 1) * d], t_new) for j in range(6)]
    sh_m, sc_m, gt_m, sh_f, sc_f, gt_f = tabs

    x_all = jnp.concatenate([x_prompt.reshape(n_p, d), x_sample.reshape(n_s, d)], axis=0)
    q_all, ckv_all, kpe_all, u_all, k_all, v_all = _mixer_inputs(x_all, sh_m, sc_m, n_prompt_tiles, p)

    attn_p = _prompt_attention(q_all, k_all, v_all, seq)
    ckv_s = ckv_all[n_p:].reshape(n_seq, t_new, KV_LORA)
    kpe_s = kpe_all[n_p:].reshape(n_seq, t_new, QK_ROPE)
    ckv_new_pad = jnp.pad(ckv_s, ((0, 0), (0, LANES - t_new), (0, 0)))
    kpe_new_t = jnp.swapaxes(jnp.pad(kpe_s, ((0, 0), (0, LANES - t_new), (0, 0))), 1, 2)
    q_s = q_all[n_p:].reshape(n_seq, t_new, N_HEADS * HEAD_PAD)
    attn_s = _sample_attention(page_table, q_s, ckv_new_pad, kpe_new_t, cache_ckv,
                               jnp.swapaxes(cache_kpe, 2, 3), p)

    conv_p = _prompt_conv(u_all, w_dw[l], b_dw[l].reshape(1, CONV_CH))
    u_s = u_all[n_p:].reshape(n_seq, t_new, CONV_CH)
    u_ext_s = jnp.concatenate([state_conv[l], u_s], axis=1)
    ext = CONV_W - 1 + t_new
    ext_pad = -ext % 8
    w_taps = jnp.stack([jnp.pad(w_dw[l], ((t, t_new - 1 - t + ext_pad), (0, 0))) for t in range(t_new)])
    conv_s = _sample_conv(jnp.pad(u_ext_s, ((0, 0), (0, ext_pad), (0, 0))), w_taps, b_dw[l].reshape(1, CONV_CH))
    conv_s = jnp.transpose(conv_s, (1, 0, 2)).reshape(n_s, CONV_CH)

    attn_all = jnp.concatenate([attn_p, attn_s.reshape(n_s, ATTN_WIDTH)], axis=0)
    conv_all = jnp.concatenate([conv_p[:n_p], conv_s], axis=0)
    y_all, f_all, top_i, top_w = _merge_router(x_all, attn_all, conv_all, gt_m, sh_f, sc_f, n_prompt_tiles, p)

    n_tiles = -(-(n * TOP_K + N_EXPERTS * (FFN_TILE - 1)) // FFN_TILE)
    n_slots = n_tiles * FFN_TILE
    pos_tok, meta = _route(top_i, n_tiles)
    slot = jnp.transpose(pos_tok[:, :TOP_K])
    piece_idx = (slot[:, None, :] + (jnp.arange(SC_PIECES, dtype=I32) * n_slots)[None, :, None]).reshape(-1)
    x_sorted = _sc_scatter_rows(f_all.reshape(SC_PIECES * n, SC_ROW), piece_idx, SC_PIECES * n_slots)
    h_sorted = _expert_ffn(meta[:n_tiles, 0], meta[0:1, 1], x_sorted.reshape(SC_PIECES, n_slots, SC_ROW),
                           w_exp_up[l], b_exp_up[l], w_exp_down[l], b_exp_down[l])
    gathered = _sc_gather_rows(h_sorted.reshape(SC_PIECES * n_slots, SC_ROW), piece_idx)
    y_out = _combine(y_all, gathered.reshape(TOP_K, SC_PIECES, n, SC_ROW), top_w, gt_f, n_prompt_tiles)

    y_p = y_out[:n_p].reshape(bsz, seq, d)
    y_s = y_out[n_p:].reshape(n_seq, t_new, d)
    ckv_prompt = ckv_all[:n_p].reshape(1, bsz, seq, KV_LORA)
    kpe_prompt = kpe_all[:n_p].reshape(1, bsz, seq, QK_ROPE)
    conv_prompt = u_all[n_p - (CONV_W - 1):n_p].reshape(1, bsz, CONV_W - 1, CONV_CH)
    ckv_sample = ckv_s[None]
    kpe_sample = kpe_s[None]
    conv_sample = u_ext_s[:, t_new:][None]
    return (y_p, y_s, ckv_prompt, kpe_prompt, conv_prompt, ckv_sample, kpe_sample, conv_sample)
```

```python
import functools

import numpy as np
import jax
import jax.numpy as jnp
from jax import lax
from jax.experimental import pallas as pl
from jax.experimental.pallas import tpu as pltpu
from jax.experimental.pallas import tpu_sc as plsc

F32 = jnp.float32
BF16 = jnp.bfloat16
I32 = jnp.int32
HIGHEST = lax.Precision.HIGHEST

D_MODEL = 1024
N_HEADS = 8
QK_NOPE = 64
QK_ROPE = 32
V_DIM = 64
Q_LORA = 256
KV_LORA = 128
ATTN_WIDTH = N_HEADS * V_DIM
CONV_CH = D_MODEL - ATTN_WIDTH
CONV_W = 31
N_EXPERTS = 32
TOP_K = 4
D_FF = D_MODEL
SWIGLU_LIMIT = 7.0
SWIGLU_ALPHA = 1.702
EPS = 1e-6
NEG_INF = -1e30
ROPE_THETA = 10000.0
SM_SCALE = (QK_NOPE + QK_ROPE) ** -0.5
LOG2E = 1.4426950408889634
Q_SCALE = SM_SCALE * LOG2E

LANES = 128
HEAD_PAD = LANES
ROPE_LO = QK_NOPE
ROPE_HALF = QK_ROPE // 2

TOK_TILE = 256
FA_TQ = 1024
FA_TK = 1024
FA_HEADS = 4
PAGES_PER_STEP = 32
FFN_TILE = 256
ROUTE_TILE = 512
SC_ROW = 256
SC_PIECES = D_MODEL // SC_ROW
SC_WINDOW = 128
VMEM_LIMIT = 56 * 1024 * 1024


def _cparams(sem, vmem=None):
    return pltpu.CompilerParams(dimension_semantics=sem, vmem_limit_bytes=vmem)


def _rsqrt_mean(x, n):
    return lax.rsqrt(jnp.sum(x * x, axis=-1, keepdims=True) * (1.0 / n) + EPS)


def _ada_kernel(c_ref, w_ref, b_ref, o_ref):
    c = c_ref[...]
    s = c * jax.nn.sigmoid(c)
    o_ref[...] = jnp.dot(s, w_ref[...], precision=HIGHEST, preferred_element_type=F32) + b_ref[...]


def _adaln(c_all, w_ada, b_ada):
    rows = c_all.shape[0]
    n_out = w_ada.shape[1]
    return pl.pallas_call(
        _ada_kernel,
        grid=(n_out // D_MODEL,),
        in_specs=[pl.BlockSpec((rows, D_MODEL), lambda j: (0, 0)),
                  pl.BlockSpec((D_MODEL, D_MODEL), lambda j: (0, j)),
                  pl.BlockSpec((1, D_MODEL), lambda j: (0, j))],
        out_specs=pl.BlockSpec((rows, D_MODEL), lambda j: (0, j)),
        out_shape=jax.ShapeDtypeStruct((rows, n_out), F32),
        compiler_params=_cparams(("arbitrary",)),
        name="adaln",
    )(c_all, w_ada, b_ada.reshape(1, n_out))


def _group_norm_rope(x, m_grp, gain, cos_t, sin_t, first_half):
    ms = jnp.dot((x * x).astype(BF16), m_grp, preferred_element_type=F32)
    xn = x * lax.rsqrt(ms + EPS) * gain
    swapped = jnp.where(first_half, pltpu.roll(xn, LANES - ROPE_HALF, 1), pltpu.roll(xn, ROPE_HALF, 1))
    return xn * cos_t + swapped * sin_t


def _pick_rows(n_prompt_tiles, prompt_ref, decode_ref):
    return jnp.where(pl.program_id(0) < n_prompt_tiles, prompt_ref[...], decode_ref[...])


def _mix_kernel(n_prompt_tiles, xp_ref, xs_ref, sh_ref, sc_ref, gmix_ref, win_ref, gql_ref, wq_ref, gq_ref, m_ref,
                cos_ref, sin_ref, gkv_ref, gk_ref, gkpe_ref, wk_ref, wv_ref,
                q_out, ckv_out, kpe_out, u_out, k_out, v_out):
    x = _pick_rows(n_prompt_tiles, xp_ref, xs_ref)
    h = x * _rsqrt_mean(x, D_MODEL) * gmix_ref[...]
    h = h * (1.0 + sc_ref[...]) + sh_ref[...]
    proj = jnp.dot(h.astype(BF16), win_ref[...], preferred_element_type=F32)
    q_lat = proj[:, :Q_LORA]
    ckv_raw = proj[:, Q_LORA:Q_LORA + KV_LORA]
    kpe_blk = proj[:, Q_LORA + KV_LORA:Q_LORA + KV_LORA + LANES]
    glu_lo = Q_LORA + KV_LORA + LANES
    u_out[...] = proj[:, glu_lo:glu_lo + CONV_CH] * jax.nn.sigmoid(proj[:, glu_lo + CONV_CH:glu_lo + 2 * CONV_CH])

    m_grp = m_ref[...]
    cos_t = cos_ref[...]
    sin_t = sin_ref[...]
    lane = lax.broadcasted_iota(I32, (1, LANES), 1)
    first_half = lane < ROPE_LO + ROPE_HALF

    q_lat_n = q_lat * _rsqrt_mean(q_lat, Q_LORA) * gql_ref[...]
    q = jnp.dot(q_lat_n.astype(BF16), wq_ref[...], preferred_element_type=F32)
    gq = gq_ref[...]
    for hd in range(N_HEADS):
        qh = _group_norm_rope(q[:, hd * HEAD_PAD:(hd + 1) * HEAD_PAD], m_grp, gq, cos_t, sin_t, first_half)
        q_out[:, hd * HEAD_PAD:(hd + 1) * HEAD_PAD] = (qh * Q_SCALE).astype(BF16)

    ckv_n = ckv_raw * _rsqrt_mean(ckv_raw, KV_LORA) * gkv_ref[...]
    ckv_out[...] = ckv_n
    kpe_r = _group_norm_rope(kpe_blk, m_grp, gkpe_ref[...], cos_t, sin_t, first_half)
    kpe_out[...] = kpe_r[:, ROPE_LO:ROPE_LO + QK_ROPE]

    ckv_b = ckv_n.astype(BF16)
    kexp = jnp.dot(ckv_b, wk_ref[...], preferred_element_type=F32)
    gk = gk_ref[...]
    for hd in range(N_HEADS):
        kh = kexp[:, hd * HEAD_PAD:(hd + 1) * HEAD_PAD]
        ms = jnp.dot((kh * kh).astype(BF16), m_grp, preferred_element_type=F32)
        k_out[:, hd * HEAD_PAD:(hd + 1) * HEAD_PAD] = (kh * lax.rsqrt(ms + EPS) * gk + kpe_r).astype(BF16)
    v_out[...] = jnp.dot(ckv_b, wv_ref[...], preferred_element_type=F32).astype(BF16)


def _mod_index(n_prompt_tiles):
    return lambda i: (jnp.where(i < n_prompt_tiles, 0, i - n_prompt_tiles + 1), 0)


def _prompt_index(n_prompt_tiles):
    return lambda i: (jnp.minimum(i, n_prompt_tiles - 1), 0)


def _decode_index(n_prompt_tiles):
    return lambda i: (jnp.maximum(i - n_prompt_tiles, 0), 0)


def _mixer_inputs(x_p, x_s, sh, sc, p):
    n_prompt_tiles = x_p.shape[0] // TOK_TILE
    n = x_p.shape[0] + x_s.shape[0]
    tm = TOK_TILE
    const = lambda i: (0, 0)
    row = lambda i: (i, 0)
    mod = _mod_index(n_prompt_tiles)
    hw = N_HEADS * HEAD_PAD
    in_cols = p["w_in"].shape[1]
    return pl.pallas_call(
        functools.partial(_mix_kernel, n_prompt_tiles),
        grid=(n // tm,),
        in_specs=[pl.BlockSpec((tm, D_MODEL), _prompt_index(n_prompt_tiles)),
                  pl.BlockSpec((tm, D_MODEL), _decode_index(n_prompt_tiles)),
                  pl.BlockSpec((tm, D_MODEL), mod),
                  pl.BlockSpec((tm, D_MODEL), mod),
                  pl.BlockSpec((1, D_MODEL), const),
                  pl.BlockSpec((D_MODEL, in_cols), const),
                  pl.BlockSpec((1, Q_LORA), const),
                  pl.BlockSpec((Q_LORA, hw), const),
                  pl.BlockSpec((1, LANES), const),
                  pl.BlockSpec((LANES, LANES), const),
                  pl.BlockSpec((tm, LANES), row),
                  pl.BlockSpec((tm, LANES), row),
                  pl.BlockSpec((1, KV_LORA), const),
                  pl.BlockSpec((1, LANES), const),
                  pl.BlockSpec((1, LANES), const),
                  pl.BlockSpec((KV_LORA, hw), const),
                  pl.BlockSpec((KV_LORA, ATTN_WIDTH), const)],
        out_specs=[pl.BlockSpec((tm, hw), row),
                   pl.BlockSpec((tm, KV_LORA), row),
                   pl.BlockSpec((tm, QK_ROPE), row),
                   pl.BlockSpec((tm, CONV_CH), row),
                   pl.BlockSpec((tm, hw), row),
                   pl.BlockSpec((tm, ATTN_WIDTH), row)],
        out_shape=[jax.ShapeDtypeStruct((n, hw), BF16),
                   jax.ShapeDtypeStruct((n, KV_LORA), F32),
                   jax.ShapeDtypeStruct((n, QK_ROPE), F32),
                   jax.ShapeDtypeStruct((n, CONV_CH), F32),
                   jax.ShapeDtypeStruct((n, hw), BF16),
                   jax.ShapeDtypeStruct((n, ATTN_WIDTH), BF16)],
        compiler_params=_cparams(("parallel",), VMEM_LIMIT),
        name="mixer_inputs",
    )(x_p, x_s, sh, sc, p["g_norm_mix"], p["w_in"], p["g_q_lat"], p["w_q_up"], p["gain_q"], p["m_grp"],
      p["cos_t"], p["sin_t"], p["g_kv_lat"], p["gain_k"], p["gain_kpe"], p["w_k"], p["w_v"])


def _fa_kernel(qt_ref, kt_ref, q_ref, k_ref, v_ref, o_ref, m_sc, l_sc, acc_sc):
    t = pl.program_id(1)
    qi = qt_ref[t]
    ki = kt_ref[t]
    last_k = (qi + 1) * (FA_TQ // FA_TK) - 1

    @pl.when(ki == 0)
    def _():
        m_sc[...] = jnp.full(m_sc.shape, NEG_INF, F32)
        l_sc[...] = jnp.zeros(l_sc.shape, F32)
        acc_sc[...] = jnp.zeros(acc_sc.shape, F32)

    def step(masked):
        if masked:
            col_minus_row = (lax.broadcasted_iota(I32, (FA_TQ, FA_TK), 1)
                             - lax.broadcasted_iota(I32, (FA_TQ, FA_TK), 0))
        for hh in range(FA_HEADS):
            q = q_ref[:, hh * HEAD_PAD:(hh + 1) * HEAD_PAD]
            k = k_ref[:, hh * HEAD_PAD:(hh + 1) * HEAD_PAD]
            v = v_ref[:, hh * V_DIM:(hh + 1) * V_DIM]
            s = lax.dot_general(q, k, (((1,), (1,)), ((), ())), preferred_element_type=F32)
            if masked:
                s = jnp.where(col_minus_row <= qi * FA_TQ - ki * FA_TK, s, NEG_INF)
            m_prev = m_sc[hh]
            m_new = jnp.maximum(m_prev, jnp.max(s, axis=-1, keepdims=True))
            alpha = jnp.exp2(m_prev - m_new)
            pr = jnp.exp2(s - jnp.concatenate([m_new] * (FA_TK // LANES), axis=1))
            l_sc[hh] = alpha * l_sc[hh] + jnp.sum(pr, axis=-1, keepdims=True)
            acc_sc[hh] = alpha[:, :V_DIM] * acc_sc[hh] + jnp.dot(pr.astype(BF16), v, preferred_element_type=F32)
            m_sc[hh] = m_new

    @pl.when(ki * FA_TK + FA_TK - 1 <= qi * FA_TQ)
    def _():
        step(False)

    @pl.when(ki * FA_TK + FA_TK - 1 > qi * FA_TQ)
    def _():
        step(True)

    @pl.when(ki == last_k)
    def _():
        for hh in range(FA_HEADS):
            o_ref[:, hh * V_DIM:(hh + 1) * V_DIM] = acc_sc[hh] / l_sc[hh][:, :V_DIM]


def _prompt_attention(q_all, k_all, v_all, seq):
    nq = seq // FA_TQ
    ratio = FA_TQ // FA_TK
    qt, kt = [], []
    for qi in range(nq):
        for ki in range((qi + 1) * ratio):
            qt.append(qi)
            kt.append(ki)
    qt = jnp.asarray(np.array(qt, np.int32))
    kt = jnp.asarray(np.array(kt, np.int32))
    n_pairs = int(qt.shape[0])
    grid_spec = pltpu.PrefetchScalarGridSpec(
        num_scalar_prefetch=2,
        grid=(N_HEADS // FA_HEADS, n_pairs),
        in_specs=[pl.BlockSpec((FA_TQ, FA_HEADS * HEAD_PAD), lambda hp, t, qt, kt: (qt[t], hp)),
                  pl.BlockSpec((FA_TK, FA_HEADS * HEAD_PAD), lambda hp, t, qt, kt: (kt[t], hp)),
                  pl.BlockSpec((FA_TK, FA_HEADS * V_DIM), lambda hp, t, qt, kt: (kt[t], hp))],
        out_specs=pl.BlockSpec((FA_TQ, FA_HEADS * V_DIM), lambda hp, t, qt, kt: (qt[t], hp)),
        scratch_shapes=[pltpu.VMEM((FA_HEADS, FA_TQ, LANES), F32),
                        pltpu.VMEM((FA_HEADS, FA_TQ, LANES), F32),
                        pltpu.VMEM((FA_HEADS, FA_TQ, V_DIM), F32)],
    )
    return pl.pallas_call(
        _fa_kernel,
        grid_spec=grid_spec,
        out_shape=jax.ShapeDtypeStruct((seq, ATTN_WIDTH), F32),
        compiler_params=_cparams(("parallel", "arbitrary"), VMEM_LIMIT),
        name="prompt_attention",
    )(qt, kt, q_all, k_all, v_all)


def _sattn_kernel(pt_ref, q_ref, wabs_ref, wkt_ref, wv_ref, ckvn_ref, kpen_ref, *rest):
    ckv_pages = rest[:PAGES_PER_STEP]
    kpe_pages = rest[PAGES_PER_STEP:2 * PAGES_PER_STEP]
    o_ref, m_sc, l_sc, acc_sc, qa_sc = rest[2 * PAGES_PER_STEP:]
    kb = pl.program_id(1)
    n_kb = pl.num_programs(1)
    rows = 4 * N_HEADS

    @pl.when(kb == 0)
    def _():
        m_sc[...] = jnp.full(m_sc.shape, NEG_INF, F32)
        l_sc[...] = jnp.zeros(l_sc.shape, F32)
        acc_sc[...] = jnp.zeros(acc_sc.shape, F32)
        q4 = q_ref[...].astype(F32)
        head_of_lane = lax.broadcasted_iota(I32, (N_HEADS, N_HEADS * HEAD_PAD), 1) // HEAD_PAD
        head_of_row = lax.broadcasted_iota(I32, (N_HEADS, N_HEADS * HEAD_PAD), 0)
        own = head_of_lane == head_of_row
        qbd = jnp.concatenate(
            [jnp.where(own, jnp.broadcast_to(q4[qq:qq + 1, :], own.shape), 0.0) for qq in range(4)], axis=0)
        qa_sc[...] = jnp.dot(qbd.astype(BF16), wabs_ref[...], preferred_element_type=F32).astype(BF16)

    def block(ckv_b, kpe_t, mask):
        nt = (((1,), (1,)), ((), ()))
        kn_t = lax.dot_general(wkt_ref[...], ckv_b, nt, preferred_element_type=F32)
        keys = kn_t.shape[1]
        ss = jnp.sum((kn_t * kn_t).reshape(N_HEADS, QK_NOPE, keys), axis=1)
        r8 = lax.rsqrt(ss * (1.0 / QK_NOPE) + EPS)
        qa = qa_sc[...]
        a = lax.dot_general(qa[:, :KV_LORA], ckv_b, nt, preferred_element_type=F32)
        b = jnp.dot(qa[:, KV_LORA:KV_LORA + QK_ROPE], kpe_t, preferred_element_type=F32)
        s = a * jnp.concatenate([r8] * 4, axis=0) + b
        if mask is not None:
            s = jnp.where(mask, s, NEG_INF)
        m_prev = m_sc[...]
        m_new = jnp.maximum(m_prev, jnp.max(s, axis=-1, keepdims=True))
        alpha = jnp.exp2(m_prev - m_new)
        pr = jnp.exp2(s - m_new)
        l_sc[...] = alpha * l_sc[...] + jnp.sum(pr, axis=-1, keepdims=True)
        acc_sc[...] = alpha * acc_sc[...] + jnp.dot(pr.astype(BF16), ckv_b, preferred_element_type=F32)
        m_sc[...] = m_new

    ckv_b = jnp.concatenate([r[...].astype(BF16) for r in ckv_pages], axis=0)
    kpe_t = jnp.concatenate([r[...].astype(BF16) for r in kpe_pages], axis=1)
    block(ckv_b, kpe_t, None)

    @pl.when(kb == n_kb - 1)
    def _():
        key = lax.broadcasted_iota(I32, (rows, LANES), 1)
        qry = lax.broadcasted_iota(I32, (rows, LANES), 0) // N_HEADS
        block(ckvn_ref[...].astype(BF16), kpen_ref[...].astype(BF16), key <= qry)
        lat = acc_sc[...] / l_sc[...]
        o_all = jnp.dot(lat.astype(BF16), wv_ref[...], preferred_element_type=F32)
        head_of_col = lax.broadcasted_iota(I32, (rows, ATTN_WIDTH), 1) // V_DIM
        head_of_row = lax.broadcasted_iota(I32, (rows, ATTN_WIDTH), 0) % N_HEADS
        o_own = jnp.where(head_of_col == head_of_row, o_all, 0.0)
        o_ref[...] = jnp.sum(o_own.reshape(4, N_HEADS, ATTN_WIDTH), axis=1)


def _sample_attention(page_table, q_s, ckv_new_pad, kpe_new_t, cache_ckv, cache_kpe_t, p):
    n_seq, n_pages = page_table.shape
    page = cache_ckv.shape[2]
    n_kb = n_pages // PAGES_PER_STEP
    hw = N_HEADS * HEAD_PAD

    def ckv_map(j):
        return lambda b, kb, pt: (0, pt[b * n_pages + kb * PAGES_PER_STEP + j], 0, 0)

    per_seq3 = lambda b, kb, pt: (b, 0, 0)
    const = lambda b, kb, pt: (0, 0)
    in_specs = [pl.BlockSpec((None, 4, hw), per_seq3),
                pl.BlockSpec((hw, 2 * LANES), const),
                pl.BlockSpec((N_HEADS * QK_NOPE, KV_LORA), const),
                pl.BlockSpec((KV_LORA, ATTN_WIDTH), const),
                pl.BlockSpec((None, LANES, KV_LORA), per_seq3),
                pl.BlockSpec((None, QK_ROPE, LANES), per_seq3)]
    in_specs += [pl.BlockSpec((None, None, page, KV_LORA), ckv_map(j)) for j in range(PAGES_PER_STEP)]
    in_specs += [pl.BlockSpec((None, None, QK_ROPE, page), ckv_map(j)) for j in range(PAGES_PER_STEP)]
    grid_spec = pltpu.PrefetchScalarGridSpec(
        num_scalar_prefetch=1,
        grid=(n_seq, n_kb),
        in_specs=in_specs,
        out_specs=pl.BlockSpec((None, 4, ATTN_WIDTH), per_seq3),
        scratch_shapes=[pltpu.VMEM((4 * N_HEADS, 1), F32),
                        pltpu.VMEM((4 * N_HEADS, 1), F32),
                        pltpu.VMEM((4 * N_HEADS, KV_LORA), F32),
                        pltpu.VMEM((4 * N_HEADS, 2 * LANES), BF16)],
    )
    return pl.pallas_call(
        _sattn_kernel,
        grid_spec=grid_spec,
        out_shape=jax.ShapeDtypeStruct((n_seq, 4, ATTN_WIDTH), F32),
        compiler_params=_cparams(("parallel", "arbitrary"), VMEM_LIMIT),
        name="decode_attention",
    )(page_table.reshape(-1), q_s, p["w_abs"], p["w_kt"], p["w_v"], ckv_new_pad, kpe_new_t,
      *([cache_ckv] * PAGES_PER_STEP), *([cache_kpe_t] * PAGES_PER_STEP))


CONV_HALO = 32
CONV_ROWS = 64


def _conv_kernel(halo_ref, u_ref, w_ref, b_ref, o_ref, ext_sc):
    i = pl.program_id(0)
    tm = u_ref.shape[0]
    ext_sc[pl.ds(0, CONV_HALO), :] = jnp.where(i == 0, 0.0, halo_ref[...])
    ext_sc[pl.ds(CONV_HALO, tm), :] = u_ref[...]
    first = CONV_HALO - (CONV_W - 1)
    for rc in range(tm // CONV_ROWS):
        acc = jnp.broadcast_to(b_ref[...], (CONV_ROWS, CONV_CH))
        for j in range(CONV_W):
            acc = acc + ext_sc[pl.ds(rc * CONV_ROWS + first + j, CONV_ROWS), :] * w_ref[j:j + 1, :]
        o_ref[pl.ds(rc * CONV_ROWS, CONV_ROWS), :] = acc


def _prompt_conv(u_all, w_dw, b_dw):
    n = u_all.shape[0]
    tm = TOK_TILE
    per = tm // CONV_HALO
    return pl.pallas_call(
        _conv_kernel,
        grid=(n // tm,),
        in_specs=[pl.BlockSpec((CONV_HALO, CONV_CH), lambda i: (jnp.maximum(i * per - 1, 0), 0)),
                  pl.BlockSpec((tm, CONV_CH), lambda i: (i, 0)),
                  pl.BlockSpec((CONV_W, CONV_CH), lambda i: (0, 0)),
                  pl.BlockSpec((1, CONV_CH), lambda i: (0, 0))],
        out_specs=pl.BlockSpec((tm, CONV_CH), lambda i: (i, 0)),
        out_shape=jax.ShapeDtypeStruct((n, CONV_CH), F32),
        scratch_shapes=[pltpu.VMEM((CONV_HALO + tm, CONV_CH), F32)],
        compiler_params=_cparams(("parallel",)),
        name="prompt_conv",
    )(u_all, u_all, w_dw, b_dw)


def _sconv_kernel(u_ref, wt_ref, b_ref, o_ref):
    u = u_ref[...]
    for t in range(o_ref.shape[0]):
        o_ref[t] = jnp.sum(u * wt_ref[t][None, :, :], axis=1) + b_ref[...]


def _sample_conv(u_ext, w_taps, b_dw):
    n_seq, ext, _ = u_ext.shape
    t_new = w_taps.shape[0]
    sb = 8
    return pl.pallas_call(
        _sconv_kernel,
        grid=(n_seq // sb,),
        in_specs=[pl.BlockSpec((sb, ext, CONV_CH), lambda i: (i, 0, 0)),
                  pl.BlockSpec((t_new, ext, CONV_CH), lambda i: (0, 0, 0)),
                  pl.BlockSpec((1, CONV_CH), lambda i: (0, 0))],
        out_specs=pl.BlockSpec((t_new, sb, CONV_CH), lambda i: (0, i, 0)),
        out_shape=jax.ShapeDtypeStruct((t_new, n_seq, CONV_CH), F32),
        compiler_params=_cparams(("parallel",)),
        name="decode_conv",
    )(u_ext, w_taps, b_dw)


def _lane_pack(cols, dtype):
    lane = lax.broadcasted_iota(I32, (cols[0].shape[0], LANES), 1)
    out = jnp.zeros((cols[0].shape[0], LANES), dtype)
    for j, c in enumerate(cols):
        out = jnp.where(lane == j, c.astype(dtype), out)
    return out


def _merge_kernel(n_prompt_tiles, xp_ref, xs_ref, ap_ref, as_ref, cp_ref, cs_ref, gate_ref, shf_ref, scf_ref,
                  gln_ref, bln_ref, ga_ref, gc_ref, wout_ref, gffn_ref, wr_ref, br_ref,
                  y_out, f_out, ti_out, tw_out):
    yc = _pick_rows(n_prompt_tiles, cp_ref, cs_ref)
    mu = jnp.mean(yc, axis=-1, keepdims=True)
    xc = yc - mu
    var = jnp.mean(xc * xc, axis=-1, keepdims=True)
    ln = xc * lax.rsqrt(var + EPS) * gln_ref[...] + bln_ref[...]
    conv = ln * jax.nn.sigmoid(ln)
    attn = _pick_rows(n_prompt_tiles, ap_ref, as_ref)
    a_n = attn * _rsqrt_mean(attn, ATTN_WIDTH) * ga_ref[...]
    c_n = conv * _rsqrt_mean(conv, CONV_CH) * gc_ref[...]
    m = (jnp.dot(a_n.astype(BF16), wout_ref[:ATTN_WIDTH, :], preferred_element_type=F32)
         + jnp.dot(c_n.astype(BF16), wout_ref[ATTN_WIDTH:, :], preferred_element_type=F32))
    y = _pick_rows(n_prompt_tiles, xp_ref, xs_ref) + gate_ref[...] * m
    y_out[...] = y
    f = y * _rsqrt_mean(y, D_MODEL) * gffn_ref[...]
    f = f * (1.0 + scf_ref[...]) + shf_ref[...]
    for j in range(SC_PIECES):
        f_out[j] = f[:, j * SC_ROW:(j + 1) * SC_ROW]
    logits = jnp.dot(f, wr_ref[...], precision=HIGHEST, preferred_element_type=F32) + br_ref[...]
    lane = lax.broadcasted_iota(I32, logits.shape, 1)
    vals, idxs = [], []
    for _ in range(TOP_K):
        mx = jnp.max(logits, axis=-1, keepdims=True)
        ix = jnp.min(jnp.where(logits == mx, lane, LANES), axis=-1, keepdims=True)
        vals.append(mx)
        idxs.append(ix)
        logits = jnp.where(lane == ix, NEG_INF * 4.0, logits)
    exps = [jnp.exp(v - vals[0]) for v in vals]
    tot = exps[0] + exps[1] + exps[2] + exps[3]
    ti_out[...] = _lane_pack(idxs, I32)
    tw_out[...] = _lane_pack([e / tot for e in exps], F32)


def _merge_router(x_p, x_s, attn_p, attn_s, conv_p, conv_s, gate, shf, scf, p):
    n_prompt_tiles = x_p.shape[0] // TOK_TILE
    n = x_p.shape[0] + x_s.shape[0]
    tm = TOK_TILE
    const = lambda i: (0, 0)
    row = lambda i: (i, 0)
    mod = _mod_index(n_prompt_tiles)
    from_p = _prompt_index(n_prompt_tiles)
    from_s = _decode_index(n_prompt_tiles)
    return pl.pallas_call(
        functools.partial(_merge_kernel, n_prompt_tiles),
        grid=(n // tm,),
        in_specs=[pl.BlockSpec((tm, D_MODEL), from_p),
                  pl.BlockSpec((tm, D_MODEL), from_s),
                  pl.BlockSpec((tm, ATTN_WIDTH), from_p),
                  pl.BlockSpec((tm, ATTN_WIDTH), from_s),
                  pl.BlockSpec((tm, CONV_CH), from_p),
                  pl.BlockSpec((tm, CONV_CH), from_s),
                  pl.BlockSpec((tm, D_MODEL), mod),
                  pl.BlockSpec((tm, D_MODEL), mod),
                  pl.BlockSpec((tm, D_MODEL), mod),
                  pl.BlockSpec((1, CONV_CH), const),
                  pl.BlockSpec((1, CONV_CH), const),
                  pl.BlockSpec((1, ATTN_WIDTH), const),
                  pl.BlockSpec((1, CONV_CH), const),
                  pl.BlockSpec((D_MODEL, D_MODEL), const),
                  pl.BlockSpec((1, D_MODEL), const),
                  pl.BlockSpec((D_MODEL, LANES), const),
                  pl.BlockSpec((1, LANES), const)],
        out_specs=[pl.BlockSpec((tm, D_MODEL), row),
                   pl.BlockSpec((SC_PIECES, tm, SC_ROW), lambda i: (0, i, 0)),
                   pl.BlockSpec((tm, LANES), row),
                   pl.BlockSpec((tm, LANES), row)],
        out_shape=[jax.ShapeDtypeStruct((n, D_MODEL), F32),
                   jax.ShapeDtypeStruct((SC_PIECES, n, SC_ROW), F32),
                   jax.ShapeDtypeStruct((n, LANES), I32),
                   jax.ShapeDtypeStruct((n, LANES), F32)],
        compiler_params=_cparams(("parallel",), VMEM_LIMIT),
        name="merge_router",
    )(x_p, x_s, attn_p, attn_s, conv_p, conv_s, gate, shf, scf, p["g_conv_ln"], p["b_conv_ln"], p["g_out_attn"],
      p["g_out_conv"], p["w_out"], p["g_norm_ffn"], p["w_router"], p["b_router"])


def _select_lane(table, idx_col, lane):
    return jnp.sum(jnp.where(lane == idx_col, table, 0.0), axis=-1, keepdims=True)


def _rank_kernel(ti_ref, rk_out, cnt_out, carry_sc):
    i = pl.program_id(0)

    @pl.when(i == 0)
    def _():
        carry_sc[...] = jnp.zeros(carry_sc.shape, F32)

    ti = ti_ref[...]
    tn = ti.shape[0]
    lane = lax.broadcasted_iota(I32, (tn, LANES), 1)
    sel = jnp.zeros((tn, LANES), F32)
    for k in range(TOP_K):
        sel = sel + (lane == ti[:, k:k + 1]).astype(F32)
    r_i = lax.broadcasted_iota(I32, (tn, tn), 0)
    c_i = lax.broadcasted_iota(I32, (tn, tn), 1)
    below = (c_i < r_i).astype(BF16)
    rank = carry_sc[...] + jnp.dot(below, sel.astype(BF16), preferred_element_type=F32)
    rk_out[...] = _lane_pack([_select_lane(rank, ti[:, k:k + 1], lane) for k in range(TOP_K)], F32)
    carry_sc[...] = carry_sc[...] + jnp.sum(sel, axis=0, keepdims=True)
    cnt_out[...] = jnp.broadcast_to(carry_sc[...], cnt_out.shape)


def _pos_kernel(cnt_ref, ti_ref, rk_ref, pos_out, meta_out):
    cnt = cnt_ref[...]
    padded = jnp.ceil(cnt * (1.0 / FFN_TILE)) * FFN_TILE
    r_i = lax.broadcasted_iota(I32, (LANES, LANES), 0)
    c_i = lax.broadcasted_iota(I32, (LANES, LANES), 1)
    before = (r_i < c_i).astype(F32)
    offs = jnp.dot(padded, before, precision=HIGHEST, preferred_element_type=F32)
    ends = offs + padded
    ti = ti_ref[...]
    tn = ti.shape[0]
    lane = lax.broadcasted_iota(I32, (tn, LANES), 1)
    off_row = offs[0:1, :]
    rk = rk_ref[...]
    pos = [_select_lane(jnp.broadcast_to(off_row, (tn, LANES)), ti[:, k:k + 1], lane) + rk[:, k:k + 1]
           for k in range(TOP_K)]
    pos_out[...] = _lane_pack(pos, F32).astype(I32)

    @pl.when(pl.program_id(0) == 0)
    def _():
        nt = meta_out.shape[0]
        start = (lax.broadcasted_iota(I32, (nt, LANES), 0) * FFN_TILE).astype(F32)
        elane = lax.broadcasted_iota(I32, (nt, LANES), 1)
        done = jnp.where((elane < N_EXPERTS) & (jnp.broadcast_to(ends[0:1, :], (nt, LANES)) <= start), 1.0, 0.0)
        expert = jnp.minimum(jnp.sum(done, axis=-1, keepdims=True), N_EXPERTS - 1.0)
        total = jnp.sum(jnp.where(elane < N_EXPERTS, jnp.broadcast_to(padded[0:1, :], (nt, LANES)), 0.0),
                        axis=-1, keepdims=True)
        meta_out[...] = _lane_pack([expert, total * (1.0 / FFN_TILE)], F32).astype(I32)


def _route(top_i, n_tiles_max):
    n = top_i.shape[0]
    tn = ROUTE_TILE
    rk, cnt = pl.pallas_call(
        _rank_kernel,
        grid=(n // tn,),
        in_specs=[pl.BlockSpec((tn, LANES), lambda i: (i, 0))],
        out_specs=[pl.BlockSpec((tn, LANES), lambda i: (i, 0)),
                   pl.BlockSpec((8, LANES), lambda i: (0, 0))],
        out_shape=[jax.ShapeDtypeStruct((n, LANES), F32),
                   jax.ShapeDtypeStruct((8, LANES), F32)],
        scratch_shapes=[pltpu.VMEM((1, LANES), F32)],
        compiler_params=_cparams(("arbitrary",)),
        name="route_rank",
    )(top_i)
    nt_pad = -(-n_tiles_max // 8) * 8
    pos, meta = pl.pallas_call(
        _pos_kernel,
        grid=(n // tn,),
        in_specs=[pl.BlockSpec((8, LANES), lambda i: (0, 0)),
                  pl.BlockSpec((tn, LANES), lambda i: (i, 0)),
                  pl.BlockSpec((tn, LANES), lambda i: (i, 0))],
        out_specs=[pl.BlockSpec((tn, LANES), lambda i: (i, 0)),
                   pl.BlockSpec((nt_pad, LANES), lambda i: (0, 0))],
        out_shape=[jax.ShapeDtypeStruct((n, LANES), I32),
                   jax.ShapeDtypeStruct((nt_pad, LANES), I32)],
        compiler_params=_cparams(("arbitrary",)),
        name="route_pos",
    )(cnt, top_i, rk)
    return pos, meta


def _sc_scatter_rows(x, idx, n_out):
    n_src = x.shape[0]
    n = idx.shape[0]
    n_src_blk = n_src // SC_WINDOW
    mesh = plsc.VectorSubcoreMesh(core_axis_name="c", subcore_axis_name="s")

    @pl.kernel(out_type=jax.ShapeDtypeStruct((n_out, SC_ROW), x.dtype), mesh=mesh)
    def k(x_hbm, i_hbm, o_hbm):
        def body(x_vmem, i_vmem):
            pltpu.sync_copy(x_vmem, o_hbm.at[i_vmem.at[0]])

        pltpu.emit_pipeline(
            body,
            grid=(n // SC_WINDOW,),
            in_specs=[pl.BlockSpec((SC_WINDOW, SC_ROW), index_map=lambda i: (i % n_src_blk, 0)),
                      pl.BlockSpec((1, SC_WINDOW), index_map=lambda i: (0, i))],
            out_specs=[],
            core_axis_name=("c", "s"),
            dimension_semantics=(pltpu.PARALLEL,),
        )(x_hbm, i_hbm)

    return k(x, idx.reshape(1, n))


def _sc_gather_rows(x, idx):
    n = idx.shape[0]
    mesh = plsc.VectorSubcoreMesh(core_axis_name="c", subcore_axis_name="s")

    @pl.kernel(out_type=jax.ShapeDtypeStruct((n, SC_ROW), x.dtype), mesh=mesh)
    def k(x_hbm, i_hbm, o_hbm):
        def body(i_vmem, o_vmem):
            pltpu.sync_copy(x_hbm.at[i_vmem.at[0]], o_vmem)

        pltpu.emit_pipeline(
            body,
            grid=(n // SC_WINDOW,),
            in_specs=[pl.BlockSpec((1, SC_WINDOW), index_map=lambda i: (0, i))],
            out_specs=[pl.BlockSpec((SC_WINDOW, SC_ROW), index_map=lambda i: (i, 0))],
            core_axis_name=("c", "s"),
            dimension_semantics=(pltpu.PARALLEL,),
        )(i_hbm, o_hbm)

    return k(x, idx.reshape(1, n))


def _ffn_kernel(te_ref, nv_ref, x_ref, wu_ref, bu_ref, wd_ref, bd_ref, o_ref, wu_sc, wd_sc):
    t = pl.program_id(0)
    valid = t < nv_ref[0]
    prev = te_ref[jnp.maximum(t - 1, 0)]
    fresh = jnp.logical_or(t == 0, te_ref[t] != prev)

    @pl.when(jnp.logical_and(valid, fresh))
    def _():
        wu_sc[...] = wu_ref[...].astype(BF16)
        wd_sc[...] = wd_ref[...].astype(BF16)

    @pl.when(valid)
    def _():
        x = jnp.concatenate([x_ref[j] for j in range(SC_PIECES)], axis=1)
        z = jnp.dot(x.astype(BF16), wu_sc[...], preferred_element_type=F32) + bu_ref[...]
        zg = jnp.minimum(z[:, :D_FF], SWIGLU_LIMIT)
        zl = jnp.clip(z[:, D_FF:], -SWIGLU_LIMIT, SWIGLU_LIMIT)
        act = zg * jax.nn.sigmoid(SWIGLU_ALPHA * zg) * (zl + 1.0)
        out = jnp.dot(act.astype(BF16), wd_sc[...], preferred_element_type=F32) + bd_ref[...]
        for j in range(SC_PIECES):
            o_ref[j] = out[:, j * SC_ROW:(j + 1) * SC_ROW]


def _expert_ffn(tile_expert, n_valid, x_sorted, w_up, b_up, w_down, b_down):
    n_slots = x_sorted.shape[1]
    n_tiles = n_slots // FFN_TILE
    xmap = lambda t, te, nv: (0, jnp.minimum(t, nv[0] - 1), 0)
    emap = lambda t, te, nv: (te[t], 0, 0)
    grid_spec = pltpu.PrefetchScalarGridSpec(
        num_scalar_prefetch=2,
        grid=(n_tiles,),
        in_specs=[pl.BlockSpec((SC_PIECES, FFN_TILE, SC_ROW), xmap),
                  pl.BlockSpec((None, D_MODEL, 2 * D_FF), emap),
                  pl.BlockSpec((None, 1, 2 * D_FF), emap),
                  pl.BlockSpec((None, D_FF, D_MODEL), emap),
                  pl.BlockSpec((None, 1, D_MODEL), emap)],
        out_specs=pl.BlockSpec((SC_PIECES, FFN_TILE, SC_ROW), xmap),
        scratch_shapes=[pltpu.VMEM((D_MODEL, 2 * D_FF), BF16),
                        pltpu.VMEM((D_FF, D_MODEL), BF16)],
    )
    return pl.pallas_call(
        _ffn_kernel,
        grid_spec=grid_spec,
        out_shape=jax.ShapeDtypeStruct((SC_PIECES, n_slots, SC_ROW), F32),
        compiler_params=_cparams(("arbitrary",), VMEM_LIMIT),
        name="expert_ffn",
    )(tile_expert, n_valid, x_sorted, w_up, b_up.reshape(N_EXPERTS, 1, 2 * D_FF), w_down,
      b_down.reshape(N_EXPERTS, 1, D_MODEL))


def _combine_kernel(n_prompt_tiles, y_ref, g_ref, tw_ref, gate_ref, op_ref, os_ref):
    tw = tw_ref[...]
    rows = lambda k: jnp.concatenate([g_ref[k, j] for j in range(SC_PIECES)], axis=1)
    moe = rows(0) * tw[:, 0:1]
    for k in range(1, TOP_K):
        moe = moe + rows(k) * tw[:, k:k + 1]
    out = y_ref[...] + gate_ref[...] * moe
    i = pl.program_id(0)

    @pl.when(i < n_prompt_tiles)
    def _():
        op_ref[...] = out

    @pl.when(i >= n_prompt_tiles)
    def _():
        os_ref[...] = out


def _combine(y_all, gathered, top_w, gate, n_p):
    n = y_all.shape[0]
    tm = TOK_TILE
    n_prompt_tiles = n_p // tm
    row = lambda i: (i, 0)
    return pl.pallas_call(
        functools.partial(_combine_kernel, n_prompt_tiles),
        grid=(n // tm,),
        in_specs=[pl.BlockSpec((tm, D_MODEL), row),
                  pl.BlockSpec((TOP_K, SC_PIECES, tm, SC_ROW), lambda i: (0, 0, i, 0)),
                  pl.BlockSpec((tm, LANES), row),
                  pl.BlockSpec((tm, D_MODEL), _mod_index(n_prompt_tiles))],
        out_specs=[pl.BlockSpec((tm, D_MODEL), _prompt_index(n_prompt_tiles)),
                   pl.BlockSpec((tm, D_MODEL), _decode_index(n_prompt_tiles))],
        out_shape=[jax.ShapeDtypeStruct((n_p, D_MODEL), F32),
                   jax.ShapeDtypeStruct((n - n_p, D_MODEL), F32)],
        compiler_params=_cparams(("arbitrary",), VMEM_LIMIT),
        name="moe_combine",
    )(y_all, gathered, top_w, gate)


def _head_tiles(nope, rope):
    pad = jnp.zeros(nope.shape[:-1] + (HEAD_PAD - QK_NOPE - QK_ROPE,), nope.dtype)
    t = jnp.concatenate([nope, rope, pad], axis=-1)
    return t.reshape(t.shape[:-2] + (N_HEADS * HEAD_PAD,))


def _prepare(pos, w_in, g_norm_mix, g_q_lat, w_q_up, g_q_nope, g_q_rope, g_kv_lat, g_k_rope, w_kv_up, g_k_nope,
             g_conv_ln, b_conv_ln, g_out_attn, g_out_conv, w_out, g_norm_ffn, w_router, b_router):
    z32 = jnp.zeros((HEAD_PAD - QK_NOPE - QK_ROPE,), F32)
    z64 = jnp.zeros((QK_NOPE,), F32)
    d = D_MODEL
    kpe_cols = w_in[:, Q_LORA + KV_LORA:Q_LORA + KV_LORA + QK_ROPE]
    kpe_tile = jnp.concatenate([jnp.zeros((d, QK_NOPE), F32), kpe_cols, jnp.zeros((d, z32.shape[0]), F32)], axis=1)
    w_in_r = jnp.concatenate([w_in[:, :Q_LORA + KV_LORA], kpe_tile, w_in[:, Q_LORA + KV_LORA + QK_ROPE:]], axis=1)
    wq = w_q_up.reshape(Q_LORA, N_HEADS, QK_NOPE + QK_ROPE)
    w_q_r = _head_tiles(wq[..., :QK_NOPE], wq[..., QK_NOPE:])
    wk = w_kv_up[..., :QK_NOPE]
    wv = w_kv_up[..., QK_NOPE:].reshape(KV_LORA, ATTN_WIDTH)
    w_k_r = _head_tiles(wk, jnp.zeros((KV_LORA, N_HEADS, QK_ROPE), F32))
    lane = np.arange(LANES)
    grp = np.where(lane < QK_NOPE, 0, np.where(lane < QK_NOPE + QK_ROPE, 1, 2))
    m_grp = ((grp[:, None] == grp[None, :]) & (grp[:, None] < 2)).astype(np.float32)
    m_grp = m_grp / np.where(grp < 1, QK_NOPE, QK_ROPE)[None, :]
    inv = ROPE_THETA ** (-jnp.arange(0, QK_ROPE, 2, dtype=F32) / QK_ROPE)
    ang = pos.astype(F32)[:, None] * inv[None, :]
    cs, sn = jnp.cos(ang), jnp.sin(ang)
    n = pos.shape[0]
    cos_t = jnp.concatenate([jnp.ones((n, QK_NOPE), F32), cs, cs, jnp.zeros((n, z32.shape[0]), F32)], axis=1)
    sin_t = jnp.concatenate([jnp.zeros((n, QK_NOPE), F32), -sn, sn, jnp.zeros((n, z32.shape[0]), F32)], axis=1)
    wk_g = wk * g_k_nope[None, None, :]
    absorb = jnp.zeros((N_HEADS, HEAD_PAD, 2 * LANES), F32)
    absorb = absorb.at[:, :QK_NOPE, :KV_LORA].set(jnp.transpose(wk_g, (1, 2, 0)))
    absorb = absorb.at[:, ROPE_LO:ROPE_LO + QK_ROPE, KV_LORA:KV_LORA + QK_ROPE].set(
        jnp.broadcast_to(jnp.eye(QK_ROPE, dtype=F32), (N_HEADS, QK_ROPE, QK_ROPE)))
    w_kt = jnp.transpose(wk, (1, 2, 0)).reshape(N_HEADS * QK_NOPE, KV_LORA)
    wr = jnp.concatenate([w_router, jnp.zeros((d, LANES - N_EXPERTS), F32)], axis=1)
    br = jnp.concatenate([b_router, jnp.full((LANES - N_EXPERTS,), NEG_INF, F32)])
    return {
        "w_in": w_in_r.astype(BF16), "g_norm_mix": g_norm_mix.reshape(1, d), "g_q_lat": g_q_lat.reshape(1, Q_LORA),
        "w_q_up": w_q_r.astype(BF16),
        "gain_q": jnp.concatenate([g_q_nope, g_q_rope, z32]).reshape(1, LANES),
        "m_grp": jnp.asarray(m_grp, BF16), "cos_t": cos_t, "sin_t": sin_t,
        "g_kv_lat": g_kv_lat.reshape(1, KV_LORA),
        "gain_k": jnp.concatenate([g_k_nope, z64]).reshape(1, LANES),
        "gain_kpe": jnp.concatenate([z64, g_k_rope, z32]).reshape(1, LANES),
        "w_k": w_k_r.astype(BF16), "w_v": wv.astype(BF16),
        "w_abs": absorb.reshape(N_HEADS * HEAD_PAD, 2 * LANES).astype(BF16), "w_kt": w_kt.astype(BF16),
        "g_conv_ln": g_conv_ln.reshape(1, CONV_CH), "b_conv_ln": b_conv_ln.reshape(1, CONV_CH),
        "g_out_attn": g_out_attn.reshape(1, ATTN_WIDTH), "g_out_conv": g_out_conv.reshape(1, CONV_CH),
        "w_out": w_out.astype(BF16), "g_norm_ffn": g_norm_ffn.reshape(1, d),
        "w_router": wr, "b_router": br.reshape(1, LANES),
    }


def _mod_table(mod_p, mod_s, t_new):
    return jnp.concatenate([jnp.broadcast_to(mod_p, (TOK_TILE, D_MODEL)), jnp.repeat(mod_s, t_new, axis=0)], axis=0)


def kernel(x_prompt, x_sample, cache_ckv, cache_kpe, state_conv, page_table, c_prompt, c_sample, w_ada, b_ada, g_norm_mix, g_norm_ffn, w_in, g_q_lat, w_q_up, g_q_nope, g_q_rope, g_kv_lat, g_k_rope, w_kv_up, g_k_nope, w_dw, b_dw, g_conv_ln, b_conv_ln, g_out_attn, g_out_conv, w_out, w_router, b_router, w_exp_up, b_exp_up, w_exp_down, b_exp_down):
    bsz, seq, d = x_prompt.shape
    n_seq, t_new = x_sample.shape[:2]
    depth = w_ada.shape[0]
    assert bsz == 1 and depth == 1 and t_new == 4 and d == D_MODEL
    n_p = bsz * seq
    n_s = n_seq * t_new
    n = n_p + n_s
    past = page_table.shape[1] * cache_ckv.shape[2]
    l = 0

    pos = jnp.concatenate([jnp.arange(seq, dtype=I32), jnp.tile(past + jnp.arange(t_new, dtype=I32), n_seq)])
    p = _prepare(pos, w_in[l], g_norm_mix[l], g_q_lat[l], w_q_up[l], g_q_nope[l], g_q_rope[l], g_kv_lat[l],
                 g_k_rope[l], w_kv_up[l], g_k_nope[l], g_conv_ln[l], b_conv_ln[l], g_out_attn[l], g_out_conv[l],
                 w_out[l], g_norm_ffn[l], w_router[l], b_router[l])

    n_c = 1 + n_seq
    c_all = jnp.concatenate([c_prompt, c_sample, jnp.zeros((-n_c % 8, d), F32)], axis=0)
    mod = _adaln(c_all, w_ada[l], b_ada[l])
    tabs = [_mod_table(mod[0:1, j * d:(j + 1) * d], mod[1:n_c, j * d:(j + 1) * d], t_new) for j in range(6)]
    sh_m, sc_m, gt_m, sh_f, sc_f, gt_f = tabs

    x_p = x_prompt.reshape(n_p, d)
    x_s = x_sample.reshape(n_s, d)
    q_all, ckv_all, kpe_all, u_all, k_all, v_all = _mixer_inputs(x_p, x_s, sh_m, sc_m, p)

    attn_p = _prompt_attention(q_all, k_all, v_all, seq)
    ckv_s = ckv_all[n_p:].reshape(n_seq, t_new, KV_LORA)
    kpe_s = kpe_all[n_p:].reshape(n_seq, t_new, QK_ROPE)
    ckv_new_pad = jnp.pad(ckv_s, ((0, 0), (0, LANES - t_new), (0, 0)))
    kpe_new_t = jnp.swapaxes(jnp.pad(kpe_s, ((0, 0), (0, LANES - t_new), (0, 0))), 1, 2)
    q_s = q_all[n_p:].reshape(n_seq, t_new, N_HEADS * HEAD_PAD)
    attn_s = _sample_attention(page_table, q_s, ckv_new_pad, kpe_new_t, cache_ckv,
                               jnp.swapaxes(cache_kpe, 2, 3), p)

    conv_p = _prompt_conv(u_all, w_dw[l], b_dw[l].reshape(1, CONV_CH))
    u_s = u_all[n_p:].reshape(n_seq, t_new, CONV_CH)
    u_ext_s = jnp.concatenate([state_conv[l], u_s], axis=1)
    ext = CONV_W - 1 + t_new
    ext_pad = -ext % 8
    w_taps = jnp.stack([jnp.pad(w_dw[l], ((t, t_new - 1 - t + ext_pad), (0, 0))) for t in range(t_new)])
    conv_s = _sample_conv(jnp.pad(u_ext_s, ((0, 0), (0, ext_pad), (0, 0))), w_taps, b_dw[l].reshape(1, CONV_CH))
    conv_s = jnp.transpose(conv_s, (1, 0, 2)).reshape(n_s, CONV_CH)

    y_all, f_all, top_i, top_w = _merge_router(x_p, x_s, attn_p, attn_s.reshape(n_s, ATTN_WIDTH), conv_p, conv_s,
                                               gt_m, sh_f, sc_f, p)

    n_tiles = -(-(n * TOP_K + N_EXPERTS * (FFN_TILE - 1)) // FFN_TILE)
    n_slots = n_tiles * FFN_TILE
    pos_tok, meta = _route(top_i, n_tiles)
    slot = jnp.transpose(pos_tok[:, :TOP_K])
    piece_idx = (slot[:, None, :] + (jnp.arange(SC_PIECES, dtype=I32) * n_slots)[None, :, None]).reshape(-1)
    x_sorted = _sc_scatter_rows(f_all.reshape(SC_PIECES * n, SC_ROW), piece_idx, SC_PIECES * n_slots)
    h_sorted = _expert_ffn(meta[:n_tiles, 0], meta[0:1, 1], x_sorted.reshape(SC_PIECES, n_slots, SC_ROW),
                           w_exp_up[l], b_exp_up[l], w_exp_down[l], b_exp_down[l])
    gathered = _sc_gather_rows(h_sorted.reshape(SC_PIECES * n_slots, SC_ROW), piece_idx)
    y_p, y_s = _combine(y_all, gathered.reshape(TOP_K, SC_PIECES, n, SC_ROW), top_w, gt_f, n_p)
    y_p = y_p.reshape(bsz, seq, d)
    y_s = y_s.reshape(n_seq, t_new, d)
    ckv_prompt = ckv_all[:n_p].reshape(1, bsz, seq, KV_LORA)
    kpe_prompt = kpe_all[:n_p].reshape(1, bsz, seq, QK_ROPE)
    conv_prompt = u_all[n_p - (CONV_W - 1):n_p].reshape(1, bsz, CONV_W - 1, CONV_CH)
    ckv_sample = ckv_s[None]
    kpe_sample = kpe_s[None]
    conv_sample = u_ext_s[:, t_new:][None]
    return (y_p, y_s, ckv_prompt, kpe_prompt, conv_prompt, ckv_sample, kpe_sample, conv_sample)
```

```python
import functools

import numpy as np
import jax
import jax.numpy as jnp
from jax import lax
from jax.experimental import pallas as pl
from jax.experimental.pallas import tpu as pltpu
from jax.experimental.pallas import tpu_sc as plsc

F32 = jnp.float32
BF16 = jnp.bfloat16
I32 = jnp.int32
HIGHEST = lax.Precision.HIGHEST

D_MODEL = 1024
N_HEADS = 8
QK_NOPE = 64
QK_ROPE = 32
V_DIM = 64
Q_LORA = 256
KV_LORA = 128
ATTN_WIDTH = N_HEADS * V_DIM
CONV_CH = D_MODEL - ATTN_WIDTH
CONV_W = 31
N_EXPERTS = 32
TOP_K = 4
D_FF = D_MODEL
SWIGLU_LIMIT = 7.0
SWIGLU_ALPHA = 1.702
EPS = 1e-6
NEG_INF = -1e30
ROPE_THETA = 10000.0
SM_SCALE = (QK_NOPE + QK_ROPE) ** -0.5
LOG2E = 1.4426950408889634
Q_SCALE = SM_SCALE * LOG2E

LANES = 128
HEAD_PAD = LANES
ROPE_LO = QK_NOPE
ROPE_HALF = QK_ROPE // 2

TOK_TILE = 256
FA_TQ = 1024
FA_TK = 1024
FA_HEADS = 4
PAGES_PER_STEP = 32
FFN_TILE = 256
ROUTE_TILE = 512
SC_ROW = 256
SC_PIECES = D_MODEL // SC_ROW
SC_WINDOW = 128
VMEM_LIMIT = 56 * 1024 * 1024


def _cparams(sem, vmem=None):
    return pltpu.CompilerParams(dimension_semantics=sem, vmem_limit_bytes=vmem)


def _rsqrt_mean(x, n):
    return lax.rsqrt(jnp.sum(x * x, axis=-1, keepdims=True) * (1.0 / n) + EPS)


def _ada_kernel(c_ref, w_ref, b_ref, o_ref):
    c = c_ref[...]
    s = c * jax.nn.sigmoid(c)
    o_ref[...] = jnp.dot(s, w_ref[...], precision=HIGHEST, preferred_element_type=F32) + b_ref[...]


def _adaln(c_all, w_ada, b_ada):
    rows = c_all.shape[0]
    n_out = w_ada.shape[1]
    return pl.pallas_call(
        _ada_kernel,
        grid=(n_out // D_MODEL,),
        in_specs=[pl.BlockSpec((rows, D_MODEL), lambda j: (0, 0)),
                  pl.BlockSpec((D_MODEL, D_MODEL), lambda j: (0, j)),
                  pl.BlockSpec((1, D_MODEL), lambda j: (0, j))],
        out_specs=pl.BlockSpec((rows, D_MODEL), lambda j: (0, j)),
        out_shape=jax.ShapeDtypeStruct((rows, n_out), F32),
        compiler_params=_cparams(("arbitrary",)),
        name="adaln",
    )(c_all, w_ada, b_ada.reshape(1, n_out))


def _group_norm_rope(x, m_grp, gain, cos_t, sin_t, first_half):
    ms = jnp.dot((x * x).astype(BF16), m_grp, preferred_element_type=F32)
    xn = x * lax.rsqrt(ms + EPS) * gain
    swapped = jnp.where(first_half, pltpu.roll(xn, LANES - ROPE_HALF, 1), pltpu.roll(xn, ROPE_HALF, 1))
    return xn * cos_t + swapped * sin_t


def _pick_rows(n_prompt_tiles, prompt_ref, decode_ref):
    return jnp.where(pl.program_id(0) < n_prompt_tiles, prompt_ref[...], decode_ref[...])


def _mix_kernel(n_prompt_tiles, xp_ref, xs_ref, sh_ref, sc_ref, gmix_ref, win_ref, gql_ref, wq_ref, gq_ref, m_ref,
                cos_ref, sin_ref, gkv_ref, gk_ref, gkpe_ref, wk_ref, wv_ref,
                q_out, ckv_out, kpe_out, u_out, k_out, v_out):
    x = _pick_rows(n_prompt_tiles, xp_ref, xs_ref)
    h = x * _rsqrt_mean(x, D_MODEL) * gmix_ref[...]
    h = h * (1.0 + sc_ref[...]) + sh_ref[...]
    proj = jnp.dot(h.astype(BF16), win_ref[...], preferred_element_type=F32)
    q_lat = proj[:, :Q_LORA]
    ckv_raw = proj[:, Q_LORA:Q_LORA + KV_LORA]
    kpe_blk = proj[:, Q_LORA + KV_LORA:Q_LORA + KV_LORA + LANES]
    glu_lo = Q_LORA + KV_LORA + LANES
    u_out[...] = proj[:, glu_lo:glu_lo + CONV_CH] * jax.nn.sigmoid(proj[:, glu_lo + CONV_CH:glu_lo + 2 * CONV_CH])

    m_grp = m_ref[...]
    cos_t = cos_ref[...]
    sin_t = sin_ref[...]
    lane = lax.broadcasted_iota(I32, (1, LANES), 1)
    first_half = lane < ROPE_LO + ROPE_HALF

    q_lat_n = q_lat * _rsqrt_mean(q_lat, Q_LORA) * gql_ref[...]
    q = jnp.dot(q_lat_n.astype(BF16), wq_ref[...], preferred_element_type=F32)
    gq = gq_ref[...]
    for hd in range(N_HEADS):
        qh = _group_norm_rope(q[:, hd * HEAD_PAD:(hd + 1) * HEAD_PAD], m_grp, gq, cos_t, sin_t, first_half)
        q_out[:, hd * HEAD_PAD:(hd + 1) * HEAD_PAD] = (qh * Q_SCALE).astype(BF16)

    ckv_n = ckv_raw * _rsqrt_mean(ckv_raw, KV_LORA) * gkv_ref[...]
    ckv_out[...] = ckv_n
    kpe_r = _group_norm_rope(kpe_blk, m_grp, gkpe_ref[...], cos_t, sin_t, first_half)
    kpe_out[...] = kpe_r[:, ROPE_LO:ROPE_LO + QK_ROPE]

    ckv_b = ckv_n.astype(BF16)
    kexp = jnp.dot(ckv_b, wk_ref[...], preferred_element_type=F32)
    gk = gk_ref[...]
    for hd in range(N_HEADS):
        kh = kexp[:, hd * HEAD_PAD:(hd + 1) * HEAD_PAD]
        ms = jnp.dot((kh * kh).astype(BF16), m_grp, preferred_element_type=F32)
        k_out[:, hd * HEAD_PAD:(hd + 1) * HEAD_PAD] = (kh * lax.rsqrt(ms + EPS) * gk + kpe_r).astype(BF16)
    v_out[...] = jnp.dot(ckv_b, wv_ref[...], preferred_element_type=F32).astype(BF16)


def _mod_index(n_prompt_tiles):
    return lambda i: (jnp.where(i < n_prompt_tiles, 0, i - n_prompt_tiles + 1), 0)


def _prompt_index(n_prompt_tiles):
    return lambda i: (jnp.minimum(i, n_prompt_tiles - 1), 0)


def _decode_index(n_prompt_tiles):
    return lambda i: (jnp.maximum(i - n_prompt_tiles, 0), 0)


def _mixer_inputs(x_p, x_s, sh, sc, p):
    n_prompt_tiles = x_p.shape[0] // TOK_TILE
    n = x_p.shape[0] + x_s.shape[0]
    tm = TOK_TILE
    const = lambda i: (0, 0)
    row = lambda i: (i, 0)
    mod = _mod_index(n_prompt_tiles)
    hw = N_HEADS * HEAD_PAD
    in_cols = p["w_in"].shape[1]
    return pl.pallas_call(
        functools.partial(_mix_kernel, n_prompt_tiles),
        grid=(n // tm,),
        in_specs=[pl.BlockSpec((tm, D_MODEL), _prompt_index(n_prompt_tiles)),
                  pl.BlockSpec((tm, D_MODEL), _decode_index(n_prompt_tiles)),
                  pl.BlockSpec((tm, D_MODEL), mod),
                  pl.BlockSpec((tm, D_MODEL), mod),
                  pl.BlockSpec((1, D_MODEL), const),
                  pl.BlockSpec((D_MODEL, in_cols), const),
                  pl.BlockSpec((1, Q_LORA), const),
                  pl.BlockSpec((Q_LORA, hw), const),
                  pl.BlockSpec((1, LANES), const),
                  pl.BlockSpec((LANES, LANES), const),
                  pl.BlockSpec((tm, LANES), row),
                  pl.BlockSpec((tm, LANES), row),
                  pl.BlockSpec((1, KV_LORA), const),
                  pl.BlockSpec((1, LANES), const),
                  pl.BlockSpec((1, LANES), const),
                  pl.BlockSpec((KV_LORA, hw), const),
                  pl.BlockSpec((KV_LORA, ATTN_WIDTH), const)],
        out_specs=[pl.BlockSpec((tm, hw), row),
                   pl.BlockSpec((tm, KV_LORA), row),
                   pl.BlockSpec((tm, QK_ROPE), row),
                   pl.BlockSpec((tm, CONV_CH), row),
                   pl.BlockSpec((tm, hw), row),
                   pl.BlockSpec((tm, ATTN_WIDTH), row)],
        out_shape=[jax.ShapeDtypeStruct((n, hw), BF16),
                   jax.ShapeDtypeStruct((n, KV_LORA), F32),
                   jax.ShapeDtypeStruct((n, QK_ROPE), F32),
                   jax.ShapeDtypeStruct((n, CONV_CH), F32),
                   jax.ShapeDtypeStruct((n, hw), BF16),
                   jax.ShapeDtypeStruct((n, ATTN_WIDTH), BF16)],
        compiler_params=_cparams(("parallel",), VMEM_LIMIT),
        name="mixer_inputs",
    )(x_p, x_s, sh, sc, p["g_norm_mix"], p["w_in"], p["g_q_lat"], p["w_q_up"], p["gain_q"], p["m_grp"],
      p["cos_t"], p["sin_t"], p["g_kv_lat"], p["gain_k"], p["gain_kpe"], p["w_k"], p["w_v"])


def _fa_kernel(qt_ref, kt_ref, q_ref, k_ref, v_ref, o_ref, m_sc, l_sc, acc_sc):
    t = pl.program_id(1)
    qi = qt_ref[t]
    ki = kt_ref[t]
    last_k = (qi + 1) * (FA_TQ // FA_TK) - 1

    @pl.when(ki == 0)
    def _():
        m_sc[...] = jnp.full(m_sc.shape, NEG_INF, F32)
        l_sc[...] = jnp.zeros(l_sc.shape, F32)
        acc_sc[...] = jnp.zeros(acc_sc.shape, F32)

    def step(masked):
        if masked:
            col_minus_row = (lax.broadcasted_iota(I32, (FA_TQ, FA_TK), 1)
                             - lax.broadcasted_iota(I32, (FA_TQ, FA_TK), 0))
        for hh in range(FA_HEADS):
            q = q_ref[:, hh * HEAD_PAD:(hh + 1) * HEAD_PAD]
            k = k_ref[:, hh * HEAD_PAD:(hh + 1) * HEAD_PAD]
            v = v_ref[:, hh * V_DIM:(hh + 1) * V_DIM]
            s = lax.dot_general(q, k, (((1,), (1,)), ((), ())), preferred_element_type=F32)
            if masked:
                s = jnp.where(col_minus_row <= qi * FA_TQ - ki * FA_TK, s, NEG_INF)
            m_prev = m_sc[hh]
            m_new = jnp.maximum(m_prev, jnp.max(s, axis=-1, keepdims=True))
            alpha = jnp.exp2(m_prev - m_new)
            pr = jnp.exp2(s - jnp.concatenate([m_new] * (FA_TK // LANES), axis=1))
            l_sc[hh] = alpha * l_sc[hh] + jnp.sum(pr, axis=-1, keepdims=True)
            acc_sc[hh] = alpha[:, :V_DIM] * acc_sc[hh] + jnp.dot(pr.astype(BF16), v, preferred_element_type=F32)
            m_sc[hh] = m_new

    @pl.when(ki * FA_TK + FA_TK - 1 <= qi * FA_TQ)
    def _():
        step(False)

    @pl.when(ki * FA_TK + FA_TK - 1 > qi * FA_TQ)
    def _():
        step(True)

    @pl.when(ki == last_k)
    def _():
        for hh in range(FA_HEADS):
            o_ref[:, hh * V_DIM:(hh + 1) * V_DIM] = acc_sc[hh] / l_sc[hh][:, :V_DIM]


def _prompt_attention(q_all, k_all, v_all, seq):
    nq = seq // FA_TQ
    ratio = FA_TQ // FA_TK
    qt, kt = [], []
    for qi in range(nq):
        for ki in range((qi + 1) * ratio):
            qt.append(qi)
            kt.append(ki)
    qt = jnp.asarray(np.array(qt, np.int32))
    kt = jnp.asarray(np.array(kt, np.int32))
    n_pairs = int(qt.shape[0])
    grid_spec = pltpu.PrefetchScalarGridSpec(
        num_scalar_prefetch=2,
        grid=(N_HEADS // FA_HEADS, n_pairs),
        in_specs=[pl.BlockSpec((FA_TQ, FA_HEADS * HEAD_PAD), lambda hp, t, qt, kt: (qt[t], hp)),
                  pl.BlockSpec((FA_TK, FA_HEADS * HEAD_PAD), lambda hp, t, qt, kt: (kt[t], hp)),
                  pl.BlockSpec((FA_TK, FA_HEADS * V_DIM), lambda hp, t, qt, kt: (kt[t], hp))],
        out_specs=pl.BlockSpec((FA_TQ, FA_HEADS * V_DIM), lambda hp, t, qt, kt: (qt[t], hp)),
        scratch_shapes=[pltpu.VMEM((FA_HEADS, FA_TQ, LANES), F32),
                        pltpu.VMEM((FA_HEADS, FA_TQ, LANES), F32),
                        pltpu.VMEM((FA_HEADS, FA_TQ, V_DIM), F32)],
    )
    return pl.pallas_call(
        _fa_kernel,
        grid_spec=grid_spec,
        out_shape=jax.ShapeDtypeStruct((seq, ATTN_WIDTH), F32),
        compiler_params=_cparams(("parallel", "arbitrary"), VMEM_LIMIT),
        name="prompt_attention",
    )(qt, kt, q_all, k_all, v_all)


def _sattn_kernel(pt_ref, q_ref, wabs_ref, wkt_ref, wv_ref, ckvn_ref, kpen_ref, ckv_hbm, kpe_hbm,
                  o_ref, m_sc, l_sc, acc_sc, qa_sc, wall_sc, ckv_buf, kpe_buf, sem):
    kb = pl.program_id(1)
    n_kb = pl.num_programs(1)
    step = pl.program_id(0) * n_kb + kb
    n_steps = pl.num_programs(0) * n_kb
    slot = step % 2
    rows = 4 * N_HEADS

    def start_fetch(step_i, slot_i):
        for j in range(PAGES_PER_STEP):
            page = pt_ref[step_i * PAGES_PER_STEP + j]
            pltpu.make_async_copy(ckv_hbm.at[0, page], ckv_buf.at[slot_i, j], sem.at[0, slot_i]).start()
            pltpu.make_async_copy(kpe_hbm.at[0, page], kpe_buf.at[slot_i, j], sem.at[1, slot_i]).start()

    @pl.when(step == 0)
    def _():
        start_fetch(step, slot)

    @pl.when(step + 1 < n_steps)
    def _():
        start_fetch(step + 1, 1 - slot)

    @pl.when(kb == 0)
    def _():
        m_sc[...] = jnp.full(m_sc.shape, NEG_INF, F32)
        l_sc[...] = jnp.zeros(l_sc.shape, F32)
        acc_sc[...] = jnp.zeros(acc_sc.shape, F32)
        q4 = q_ref[...].astype(F32)
        head_of_lane = lax.broadcasted_iota(I32, (N_HEADS, N_HEADS * HEAD_PAD), 1) // HEAD_PAD
        head_of_row = lax.broadcasted_iota(I32, (N_HEADS, N_HEADS * HEAD_PAD), 0)
        own = head_of_lane == head_of_row
        qbd = jnp.concatenate(
            [jnp.where(own, jnp.broadcast_to(q4[qq:qq + 1, :], own.shape), 0.0) for qq in range(4)], axis=0)
        qa = jnp.dot(qbd.astype(BF16), wabs_ref[...], preferred_element_type=F32).astype(BF16)
        qa_sc[...] = qa
        wall_sc[pl.ds(0, N_HEADS * QK_NOPE), :] = wkt_ref[...]
        wall_sc[pl.ds(N_HEADS * QK_NOPE, rows), :] = qa[:, :KV_LORA]

    def attend(state, ckv_b, kpe_t, mask):
        m_prev, l_prev, acc_prev = state
        nt = (((1,), (1,)), ((), ()))
        kn_all = lax.dot_general(wall_sc[...], ckv_b, nt, preferred_element_type=F32)
        keys = kn_all.shape[1]
        kn_t = kn_all[:N_HEADS * QK_NOPE]
        ss = jnp.sum((kn_t * kn_t).reshape(N_HEADS, QK_NOPE, keys), axis=1)
        r8 = lax.rsqrt(ss * (1.0 / QK_NOPE) + EPS)
        rope = jnp.dot(qa_sc[:, KV_LORA:KV_LORA + QK_ROPE], kpe_t, preferred_element_type=F32)
        s = kn_all[N_HEADS * QK_NOPE:] * jnp.concatenate([r8] * 4, axis=0) + rope
        if mask is not None:
            s = jnp.where(mask, s, NEG_INF)
        m_new = jnp.maximum(m_prev, jnp.max(s, axis=-1, keepdims=True))
        alpha = jnp.exp2(m_prev - m_new)
        pr = jnp.exp2(s - m_new)
        l_new = alpha * l_prev + jnp.sum(pr, axis=-1, keepdims=True)
        acc_new = alpha * acc_prev + jnp.dot(pr.astype(BF16), ckv_b, preferred_element_type=F32)
        return m_new, l_new, acc_new

    pltpu.make_async_copy(ckv_hbm.at[0, pl.ds(0, PAGES_PER_STEP)], ckv_buf.at[slot], sem.at[0, slot]).wait()
    pltpu.make_async_copy(kpe_hbm.at[0, pl.ds(0, PAGES_PER_STEP)], kpe_buf.at[slot], sem.at[1, slot]).wait()

    page = ckv_buf.shape[2]
    ckv_b = ckv_buf[slot].reshape(PAGES_PER_STEP * page, KV_LORA).astype(BF16)
    kpe_t = jnp.concatenate([kpe_buf[slot, j].astype(BF16) for j in range(PAGES_PER_STEP)], axis=1)
    state = attend((m_sc[...], l_sc[...], acc_sc[...]), ckv_b, kpe_t, None)
    m_sc[...], l_sc[...], acc_sc[...] = state

    @pl.when(kb == n_kb - 1)
    def _():
        key = lax.broadcasted_iota(I32, (rows, LANES), 1)
        qry = lax.broadcasted_iota(I32, (rows, LANES), 0) // N_HEADS
        _, l_fin, acc_fin = attend(state, ckvn_ref[...].astype(BF16), kpen_ref[...].astype(BF16), key <= qry)
        lat = acc_fin / l_fin
        o_all = jnp.dot(lat.astype(BF16), wv_ref[...], preferred_element_type=F32)
        head_of_col = lax.broadcasted_iota(I32, (rows, ATTN_WIDTH), 1) // V_DIM
        head_of_row = lax.broadcasted_iota(I32, (rows, ATTN_WIDTH), 0) % N_HEADS
        o_own = jnp.where(head_of_col == head_of_row, o_all, 0.0)
        o_ref[...] = jnp.sum(o_own.reshape(4, N_HEADS, ATTN_WIDTH), axis=1)


def _sample_attention(page_table, q_s, ckv_new_pad, kpe_new_t, cache_ckv, cache_kpe_t, p):
    n_seq, n_pages = page_table.shape
    page = cache_ckv.shape[2]
    n_kb = n_pages // PAGES_PER_STEP
    hw = N_HEADS * HEAD_PAD
    assert n_pages % PAGES_PER_STEP == 0
    per_seq3 = lambda b, kb, pt: (b, 0, 0)
    const = lambda b, kb, pt: (0, 0)
    in_specs = [pl.BlockSpec((None, 4, hw), per_seq3),
                pl.BlockSpec((hw, 2 * LANES), const),
                pl.BlockSpec((N_HEADS * QK_NOPE, KV_LORA), const),
                pl.BlockSpec((KV_LORA, ATTN_WIDTH), const),
                pl.BlockSpec((None, LANES, KV_LORA), per_seq3),
                pl.BlockSpec((None, QK_ROPE, LANES), per_seq3),
                pl.BlockSpec(memory_space=pl.ANY),
                pl.BlockSpec(memory_space=pl.ANY)]
    grid_spec = pltpu.PrefetchScalarGridSpec(
        num_scalar_prefetch=1,
        grid=(n_seq, n_kb),
        in_specs=in_specs,
        out_specs=pl.BlockSpec((None, 4, ATTN_WIDTH), per_seq3),
        scratch_shapes=[pltpu.VMEM((4 * N_HEADS, 1), F32),
                        pltpu.VMEM((4 * N_HEADS, 1), F32),
                        pltpu.VMEM((4 * N_HEADS, KV_LORA), F32),
                        pltpu.VMEM((4 * N_HEADS, 2 * LANES), BF16),
                        pltpu.VMEM((N_HEADS * QK_NOPE + 4 * N_HEADS, KV_LORA), BF16),
                        pltpu.VMEM((2, PAGES_PER_STEP, page, KV_LORA), F32),
                        pltpu.VMEM((2, PAGES_PER_STEP, QK_ROPE, page), F32),
                        pltpu.SemaphoreType.DMA((2, 2))],
    )
    return pl.pallas_call(
        _sattn_kernel,
        grid_spec=grid_spec,
        out_shape=jax.ShapeDtypeStruct((n_seq, 4, ATTN_WIDTH), F32),
        compiler_params=_cparams(("arbitrary", "arbitrary"), VMEM_LIMIT),
        name="decode_attention",
    )(page_table.reshape(-1), q_s, p["w_abs"], p["w_kt"], p["w_v"], ckv_new_pad, kpe_new_t, cache_ckv, cache_kpe_t)


CONV_HALO = 32
CONV_ROWS = 64


def _conv_kernel(halo_ref, u_ref, w_ref, b_ref, o_ref, ext_sc):
    i = pl.program_id(0)
    tm = u_ref.shape[0]
    ext_sc[pl.ds(0, CONV_HALO), :] = jnp.where(i == 0, 0.0, halo_ref[...])
    ext_sc[pl.ds(CONV_HALO, tm), :] = u_ref[...]
    first = CONV_HALO - (CONV_W - 1)
    for rc in range(tm // CONV_ROWS):
        acc = jnp.broadcast_to(b_ref[...], (CONV_ROWS, CONV_CH))
        for j in range(CONV_W):
            acc = acc + ext_sc[pl.ds(rc * CONV_ROWS + first + j, CONV_ROWS), :] * w_ref[j:j + 1, :]
        o_ref[pl.ds(rc * CONV_ROWS, CONV_ROWS), :] = acc


def _prompt_conv(u_all, w_dw, b_dw):
    n = u_all.shape[0]
    tm = TOK_TILE
    per = tm // CONV_HALO
    return pl.pallas_call(
        _conv_kernel,
        grid=(n // tm,),
        in_specs=[pl.BlockSpec((CONV_HALO, CONV_CH), lambda i: (jnp.maximum(i * per - 1, 0), 0)),
                  pl.BlockSpec((tm, CONV_CH), lambda i: (i, 0)),
                  pl.BlockSpec((CONV_W, CONV_CH), lambda i: (0, 0)),
                  pl.BlockSpec((1, CONV_CH), lambda i: (0, 0))],
        out_specs=pl.BlockSpec((tm, CONV_CH), lambda i: (i, 0)),
        out_shape=jax.ShapeDtypeStruct((n, CONV_CH), F32),
        scratch_shapes=[pltpu.VMEM((CONV_HALO + tm, CONV_CH), F32)],
        compiler_params=_cparams(("parallel",)),
        name="prompt_conv",
    )(u_all, u_all, w_dw, b_dw)


def _sconv_kernel(u_ref, wt_ref, b_ref, o_ref):
    u = u_ref[...]
    for t in range(o_ref.shape[0]):
        o_ref[t] = jnp.sum(u * wt_ref[t][None, :, :], axis=1) + b_ref[...]


def _sample_conv(u_ext, w_taps, b_dw):
    n_seq, ext, _ = u_ext.shape
    t_new = w_taps.shape[0]
    sb = 8
    return pl.pallas_call(
        _sconv_kernel,
        grid=(n_seq // sb,),
        in_specs=[pl.BlockSpec((sb, ext, CONV_CH), lambda i: (i, 0, 0)),
                  pl.BlockSpec((t_new, ext, CONV_CH), lambda i: (0, 0, 0)),
                  pl.BlockSpec((1, CONV_CH), lambda i: (0, 0))],
        out_specs=pl.BlockSpec((t_new, sb, CONV_CH), lambda i: (0, i, 0)),
        out_shape=jax.ShapeDtypeStruct((t_new, n_seq, CONV_CH), F32),
        compiler_params=_cparams(("parallel",)),
        name="decode_conv",
    )(u_ext, w_taps, b_dw)


def _lane_pack(cols, dtype):
    lane = lax.broadcasted_iota(I32, (cols[0].shape[0], LANES), 1)
    out = jnp.zeros((cols[0].shape[0], LANES), dtype)
    for j, c in enumerate(cols):
        out = jnp.where(lane == j, c.astype(dtype), out)
    return out


def _merge_kernel(n_prompt_tiles, xp_ref, xs_ref, ap_ref, as_ref, cp_ref, cs_ref, gate_ref, shf_ref, scf_ref,
                  gln_ref, bln_ref, ga_ref, gc_ref, wout_ref, gffn_ref, wr_ref, br_ref,
                  y_out, f_out, ti_out, tw_out):
    yc = _pick_rows(n_prompt_tiles, cp_ref, cs_ref)
    mu = jnp.mean(yc, axis=-1, keepdims=True)
    xc = yc - mu
    var = jnp.mean(xc * xc, axis=-1, keepdims=True)
    ln = xc * lax.rsqrt(var + EPS) * gln_ref[...] + bln_ref[...]
    conv = ln * jax.nn.sigmoid(ln)
    attn = _pick_rows(n_prompt_tiles, ap_ref, as_ref)
    a_n = attn * _rsqrt_mean(attn, ATTN_WIDTH) * ga_ref[...]
    c_n = conv * _rsqrt_mean(conv, CONV_CH) * gc_ref[...]
    m = (jnp.dot(a_n.astype(BF16), wout_ref[:ATTN_WIDTH, :], preferred_element_type=F32)
         + jnp.dot(c_n.astype(BF16), wout_ref[ATTN_WIDTH:, :], preferred_element_type=F32))
    y = _pick_rows(n_prompt_tiles, xp_ref, xs_ref) + gate_ref[...] * m
    y_out[...] = y
    f = y * _rsqrt_mean(y, D_MODEL) * gffn_ref[...]
    f = f * (1.0 + scf_ref[...]) + shf_ref[...]
    for j in range(SC_PIECES):
        f_out[j] = f[:, j * SC_ROW:(j + 1) * SC_ROW]
    logits = jnp.dot(f, wr_ref[...], precision=HIGHEST, preferred_element_type=F32) + br_ref[...]
    lane = lax.broadcasted_iota(I32, logits.shape, 1)
    vals, idxs = [], []
    for _ in range(TOP_K):
        mx = jnp.max(logits, axis=-1, keepdims=True)
        ix = jnp.min(jnp.where(logits == mx, lane, LANES), axis=-1, keepdims=True)
        vals.append(mx)
        idxs.append(ix)
        logits = jnp.where(lane == ix, NEG_INF * 4.0, logits)
    exps = [jnp.exp(v - vals[0]) for v in vals]
    tot = exps[0] + exps[1] + exps[2] + exps[3]
    ti_out[...] = _lane_pack(idxs, I32)
    tw_out[...] = _lane_pack([e / tot for e in exps], F32)


def _merge_router(x_p, x_s, attn_p, attn_s, conv_p, conv_s, gate, shf, scf, p):
    n_prompt_tiles = x_p.shape[0] // TOK_TILE
    n = x_p.shape[0] + x_s.shape[0]
    tm = TOK_TILE
    const = lambda i: (0, 0)
    row = lambda i: (i, 0)
    mod = _mod_index(n_prompt_tiles)
    from_p = _prompt_index(n_prompt_tiles)
    from_s = _decode_index(n_prompt_tiles)
    return pl.pallas_call(
        functools.partial(_merge_kernel, n_prompt_tiles),
        grid=(n // tm,),
        in_specs=[pl.BlockSpec((tm, D_MODEL), from_p),
                  pl.BlockSpec((tm, D_MODEL), from_s),
                  pl.BlockSpec((tm, ATTN_WIDTH), from_p),
                  pl.BlockSpec((tm, ATTN_WIDTH), from_s),
                  pl.BlockSpec((tm, CONV_CH), from_p),
                  pl.BlockSpec((tm, CONV_CH), from_s),
                  pl.BlockSpec((tm, D_MODEL), mod),
                  pl.BlockSpec((tm, D_MODEL), mod),
                  pl.BlockSpec((tm, D_MODEL), mod),
                  pl.BlockSpec((1, CONV_CH), const),
                  pl.BlockSpec((1, CONV_CH), const),
                  pl.BlockSpec((1, ATTN_WIDTH), const),
                  pl.BlockSpec((1, CONV_CH), const),
                  pl.BlockSpec((D_MODEL, D_MODEL), const),
                  pl.BlockSpec((1, D_MODEL), const),
                  pl.BlockSpec((D_MODEL, LANES), const),
                  pl.BlockSpec((1, LANES), const)],
        out_specs=[pl.BlockSpec((tm, D_MODEL), row),
                   pl.BlockSpec((SC_PIECES, tm, SC_ROW), lambda i: (0, i, 0)),
                   pl.BlockSpec((tm, LANES), row),
                   pl.BlockSpec((tm, LANES), row)],
        out_shape=[jax.ShapeDtypeStruct((n, D_MODEL), F32),
                   jax.ShapeDtypeStruct((SC_PIECES, n, SC_ROW), F32),
                   jax.ShapeDtypeStruct((n, LANES), I32),
                   jax.ShapeDtypeStruct((n, LANES), F32)],
        compiler_params=_cparams(("parallel",), VMEM_LIMIT),
        name="merge_router",
    )(x_p, x_s, attn_p, attn_s, conv_p, conv_s, gate, shf, scf, p["g_conv_ln"], p["b_conv_ln"], p["g_out_attn"],
      p["g_out_conv"], p["w_out"], p["g_norm_ffn"], p["w_router"], p["b_router"])


def _select_lane(table, idx_col, lane):
    return jnp.sum(jnp.where(lane == idx_col, table, 0.0), axis=-1, keepdims=True)


def _rank_kernel(ti_ref, rk_out, cnt_out, carry_sc):
    i = pl.program_id(0)

    @pl.when(i == 0)
    def _():
        carry_sc[...] = jnp.zeros(carry_sc.shape, F32)

    ti = ti_ref[...]
    tn = ti.shape[0]
    lane = lax.broadcasted_iota(I32, (tn, LANES), 1)
    sel = jnp.zeros((tn, LANES), F32)
    for k in range(TOP_K):
        sel = sel + (lane == ti[:, k:k + 1]).astype(F32)
    r_i = lax.broadcasted_iota(I32, (tn, tn), 0)
    c_i = lax.broadcasted_iota(I32, (tn, tn), 1)
    below = (c_i < r_i).astype(BF16)
    rank = carry_sc[...] + jnp.dot(below, sel.astype(BF16), preferred_element_type=F32)
    rk_out[...] = _lane_pack([_select_lane(rank, ti[:, k:k + 1], lane) for k in range(TOP_K)], F32)
    carry_sc[...] = carry_sc[...] + jnp.sum(sel, axis=0, keepdims=True)
    cnt_out[...] = jnp.broadcast_to(carry_sc[...], cnt_out.shape)


def _pos_kernel(cnt_ref, ti_ref, rk_ref, pos_out, meta_out):
    cnt = cnt_ref[...]
    padded = jnp.ceil(cnt * (1.0 / FFN_TILE)) * FFN_TILE
    r_i = lax.broadcasted_iota(I32, (LANES, LANES), 0)
    c_i = lax.broadcasted_iota(I32, (LANES, LANES), 1)
    before = (r_i < c_i).astype(F32)
    offs = jnp.dot(padded, before, precision=HIGHEST, preferred_element_type=F32)
    ends = offs + padded
    ti = ti_ref[...]
    tn = ti.shape[0]
    lane = lax.broadcasted_iota(I32, (tn, LANES), 1)
    off_row = offs[0:1, :]
    rk = rk_ref[...]
    pos = [_select_lane(jnp.broadcast_to(off_row, (tn, LANES)), ti[:, k:k + 1], lane) + rk[:, k:k + 1]
           for k in range(TOP_K)]
    pos_out[...] = _lane_pack(pos, F32).astype(I32)

    @pl.when(pl.program_id(0) == 0)
    def _():
        nt = meta_out.shape[0]
        start = (lax.broadcasted_iota(I32, (nt, LANES), 0) * FFN_TILE).astype(F32)
        elane = lax.broadcasted_iota(I32, (nt, LANES), 1)
        done = jnp.where((elane < N_EXPERTS) & (jnp.broadcast_to(ends[0:1, :], (nt, LANES)) <= start), 1.0, 0.0)
        expert = jnp.minimum(jnp.sum(done, axis=-1, keepdims=True), N_EXPERTS - 1.0)
        total = jnp.sum(jnp.where(elane < N_EXPERTS, jnp.broadcast_to(padded[0:1, :], (nt, LANES)), 0.0),
                        axis=-1, keepdims=True)
        meta_out[...] = _lane_pack([expert, total * (1.0 / FFN_TILE)], F32).astype(I32)


def _route(top_i, n_tiles_max):
    n = top_i.shape[0]
    tn = ROUTE_TILE
    rk, cnt = pl.pallas_call(
        _rank_kernel,
        grid=(n // tn,),
        in_specs=[pl.BlockSpec((tn, LANES), lambda i: (i, 0))],
        out_specs=[pl.BlockSpec((tn, LANES), lambda i: (i, 0)),
                   pl.BlockSpec((8, LANES), lambda i: (0, 0))],
        out_shape=[jax.ShapeDtypeStruct((n, LANES), F32),
                   jax.ShapeDtypeStruct((8, LANES), F32)],
        scratch_shapes=[pltpu.VMEM((1, LANES), F32)],
        compiler_params=_cparams(("arbitrary",)),
        name="route_rank",
    )(top_i)
    nt_pad = -(-n_tiles_max // 8) * 8
    pos, meta = pl.pallas_call(
        _pos_kernel,
        grid=(n // tn,),
        in_specs=[pl.BlockSpec((8, LANES), lambda i: (0, 0)),
                  pl.BlockSpec((tn, LANES), lambda i: (i, 0)),
                  pl.BlockSpec((tn, LANES), lambda i: (i, 0))],
        out_specs=[pl.BlockSpec((tn, LANES), lambda i: (i, 0)),
                   pl.BlockSpec((nt_pad, LANES), lambda i: (0, 0))],
        out_shape=[jax.ShapeDtypeStruct((n, LANES), I32),
                   jax.ShapeDtypeStruct((nt_pad, LANES), I32)],
        compiler_params=_cparams(("arbitrary",)),
        name="route_pos",
    )(cnt, top_i, rk)
    return pos, meta


def _sc_scatter_rows(x, idx, n_out):
    n_src = x.shape[0]
    n = idx.shape[0]
    n_src_blk = n_src // SC_WINDOW
    mesh = plsc.VectorSubcoreMesh(core_axis_name="c", subcore_axis_name="s")

    @pl.kernel(out_type=jax.ShapeDtypeStruct((n_out, SC_ROW), x.dtype), mesh=mesh)
    def k(x_hbm, i_hbm, o_hbm):
        def body(x_vmem, i_vmem):
            pltpu.sync_copy(x_vmem, o_hbm.at[i_vmem.at[0]])

        pltpu.emit_pipeline(
            body,
            grid=(n // SC_WINDOW,),
            in_specs=[pl.BlockSpec((SC_WINDOW, SC_ROW), index_map=lambda i: (i % n_src_blk, 0)),
                      pl.BlockSpec((1, SC_WINDOW), index_map=lambda i: (0, i))],
            out_specs=[],
            core_axis_name=("c", "s"),
            dimension_semantics=(pltpu.PARALLEL,),
        )(x_hbm, i_hbm)

    return k(x, idx.reshape(1, n))


def _sc_gather_rows(x, idx):
    n = idx.shape[0]
    mesh = plsc.VectorSubcoreMesh(core_axis_name="c", subcore_axis_name="s")

    @pl.kernel(out_type=jax.ShapeDtypeStruct((n, SC_ROW), x.dtype), mesh=mesh)
    def k(x_hbm, i_hbm, o_hbm):
        def body(i_vmem, o_vmem):
            pltpu.sync_copy(x_hbm.at[i_vmem.at[0]], o_vmem)

        pltpu.emit_pipeline(
            body,
            grid=(n // SC_WINDOW,),
            in_specs=[pl.BlockSpec((1, SC_WINDOW), index_map=lambda i: (0, i))],
            out_specs=[pl.BlockSpec((SC_WINDOW, SC_ROW), index_map=lambda i: (i, 0))],
            core_axis_name=("c", "s"),
            dimension_semantics=(pltpu.PARALLEL,),
        )(i_hbm, o_hbm)

    return k(x, idx.reshape(1, n))


def _ffn_kernel(te_ref, nv_ref, x_ref, wu_ref, bu_ref, wd_ref, bd_ref, o_ref, wu_sc, wd_sc):
    t = pl.program_id(0)
    valid = t < nv_ref[0]
    prev = te_ref[jnp.maximum(t - 1, 0)]
    fresh = jnp.logical_or(t == 0, te_ref[t] != prev)

    @pl.when(jnp.logical_and(valid, fresh))
    def _():
        wu_sc[...] = wu_ref[...].astype(BF16)
        wd_sc[...] = wd_ref[...].astype(BF16)

    @pl.when(valid)
    def _():
        x = jnp.concatenate([x_ref[j] for j in range(SC_PIECES)], axis=1)
        z = jnp.dot(x.astype(BF16), wu_sc[...], preferred_element_type=F32) + bu_ref[...]
        zg = jnp.minimum(z[:, :D_FF], SWIGLU_LIMIT)
        zl = jnp.clip(z[:, D_FF:], -SWIGLU_LIMIT, SWIGLU_LIMIT)
        act = zg * jax.nn.sigmoid(SWIGLU_ALPHA * zg) * (zl + 1.0)
        out = jnp.dot(act.astype(BF16), wd_sc[...], preferred_element_type=F32) + bd_ref[...]
        for j in range(SC_PIECES):
            o_ref[j] = out[:, j * SC_ROW:(j + 1) * SC_ROW]


def _expert_ffn(tile_expert, n_valid, x_sorted, w_up, b_up, w_down, b_down):
    n_slots = x_sorted.shape[1]
    n_tiles = n_slots // FFN_TILE
    xmap = lambda t, te, nv: (0, jnp.minimum(t, nv[0] - 1), 0)
    emap = lambda t, te, nv: (te[t], 0, 0)
    grid_spec = pltpu.PrefetchScalarGridSpec(
        num_scalar_prefetch=2,
        grid=(n_tiles,),
        in_specs=[pl.BlockSpec((SC_PIECES, FFN_TILE, SC_ROW), xmap),
                  pl.BlockSpec((None, D_MODEL, 2 * D_FF), emap),
                  pl.BlockSpec((None, 1, 2 * D_FF), emap),
                  pl.BlockSpec((None, D_FF, D_MODEL), emap),
                  pl.BlockSpec((None, 1, D_MODEL), emap)],
        out_specs=pl.BlockSpec((SC_PIECES, FFN_TILE, SC_ROW), xmap),
        scratch_shapes=[pltpu.VMEM((D_MODEL, 2 * D_FF), BF16),
                        pltpu.VMEM((D_FF, D_MODEL), BF16)],
    )
    return pl.pallas_call(
        _ffn_kernel,
        grid_spec=grid_spec,
        out_shape=jax.ShapeDtypeStruct((SC_PIECES, n_slots, SC_ROW), F32),
        compiler_params=_cparams(("arbitrary",), VMEM_LIMIT),
        name="expert_ffn",
    )(tile_expert, n_valid, x_sorted, w_up, b_up.reshape(N_EXPERTS, 1, 2 * D_FF), w_down,
      b_down.reshape(N_EXPERTS, 1, D_MODEL))


def _combine_kernel(n_prompt_tiles, y_ref, g_ref, tw_ref, gate_ref, op_ref, os_ref):
    tw = tw_ref[...]
    rows = lambda k: jnp.concatenate([g_ref[k, j] for j in range(SC_PIECES)], axis=1)
    moe = rows(0) * tw[:, 0:1]
    for k in range(1, TOP_K):
        moe = moe + rows(k) * tw[:, k:k + 1]
    out = y_ref[...] + gate_ref[...] * moe
    i = pl.program_id(0)

    @pl.when(i < n_prompt_tiles)
    def _():
        op_ref[...] = out

    @pl.when(i >= n_prompt_tiles)
    def _():
        os_ref[...] = out


def _combine(y_all, gathered, top_w, gate, n_p):
    n = y_all.shape[0]
    tm = TOK_TILE
    n_prompt_tiles = n_p // tm
    row = lambda i: (i, 0)
    return pl.pallas_call(
        functools.partial(_combine_kernel, n_prompt_tiles),
        grid=(n // tm,),
        in_specs=[pl.BlockSpec((tm, D_MODEL), row),
                  pl.BlockSpec((TOP_K, SC_PIECES, tm, SC_ROW), lambda i: (0, 0, i, 0)),
                  pl.BlockSpec((tm, LANES), row),
                  pl.BlockSpec((tm, D_MODEL), _mod_index(n_prompt_tiles))],
        out_specs=[pl.BlockSpec((tm, D_MODEL), _prompt_index(n_prompt_tiles)),
                   pl.BlockSpec((tm, D_MODEL), _decode_index(n_prompt_tiles))],
        out_shape=[jax.ShapeDtypeStruct((n_p, D_MODEL), F32),
                   jax.ShapeDtypeStruct((n - n_p, D_MODEL), F32)],
        compiler_params=_cparams(("arbitrary",), VMEM_LIMIT),
        name="moe_combine",
    )(y_all, gathered, top_w, gate)


def _head_tiles(nope, rope):
    pad = jnp.zeros(nope.shape[:-1] + (HEAD_PAD - QK_NOPE - QK_ROPE,), nope.dtype)
    t = jnp.concatenate([nope, rope, pad], axis=-1)
    return t.reshape(t.shape[:-2] + (N_HEADS * HEAD_PAD,))


def _prepare(pos, w_in, g_norm_mix, g_q_lat, w_q_up, g_q_nope, g_q_rope, g_kv_lat, g_k_rope, w_kv_up, g_k_nope,
             g_conv_ln, b_conv_ln, g_out_attn, g_out_conv, w_out, g_norm_ffn, w_router, b_router):
    z32 = jnp.zeros((HEAD_PAD - QK_NOPE - QK_ROPE,), F32)
    z64 = jnp.zeros((QK_NOPE,), F32)
    d = D_MODEL
    kpe_cols = w_in[:, Q_LORA + KV_LORA:Q_LORA + KV_LORA + QK_ROPE]
    kpe_tile = jnp.concatenate([jnp.zeros((d, QK_NOPE), F32), kpe_cols, jnp.zeros((d, z32.shape[0]), F32)], axis=1)
    w_in_r = jnp.concatenate([w_in[:, :Q_LORA + KV_LORA], kpe_tile, w_in[:, Q_LORA + KV_LORA + QK_ROPE:]], axis=1)
    wq = w_q_up.reshape(Q_LORA, N_HEADS, QK_NOPE + QK_ROPE)
    w_q_r = _head_tiles(wq[..., :QK_NOPE], wq[..., QK_NOPE:])
    wk = w_kv_up[..., :QK_NOPE]
    wv = w_kv_up[..., QK_NOPE:].reshape(KV_LORA, ATTN_WIDTH)
    w_k_r = _head_tiles(wk, jnp.zeros((KV_LORA, N_HEADS, QK_ROPE), F32))
    lane = np.arange(LANES)
    grp = np.where(lane < QK_NOPE, 0, np.where(lane < QK_NOPE + QK_ROPE, 1, 2))
    m_grp = ((grp[:, None] == grp[None, :]) & (grp[:, None] < 2)).astype(np.float32)
    m_grp = m_grp / np.where(grp < 1, QK_NOPE, QK_ROPE)[None, :]
    inv = ROPE_THETA ** (-jnp.arange(0, QK_ROPE, 2, dtype=F32) / QK_ROPE)
    ang = pos.astype(F32)[:, None] * inv[None, :]
    cs, sn = jnp.cos(ang), jnp.sin(ang)
    n = pos.shape[0]
    cos_t = jnp.concatenate([jnp.ones((n, QK_NOPE), F32), cs, cs, jnp.zeros((n, z32.shape[0]), F32)], axis=1)
    sin_t = jnp.concatenate([jnp.zeros((n, QK_NOPE), F32), -sn, sn, jnp.zeros((n, z32.shape[0]), F32)], axis=1)
    wk_g = wk * g_k_nope[None, None, :]
    absorb = jnp.zeros((N_HEADS, HEAD_PAD, 2 * LANES), F32)
    absorb = absorb.at[:, :QK_NOPE, :KV_LORA].set(jnp.transpose(wk_g, (1, 2, 0)))
    absorb = absorb.at[:, ROPE_LO:ROPE_LO + QK_ROPE, KV_LORA:KV_LORA + QK_ROPE].set(
        jnp.broadcast_to(jnp.eye(QK_ROPE, dtype=F32), (N_HEADS, QK_ROPE, QK_ROPE)))
    w_kt = jnp.transpose(wk, (1, 2, 0)).reshape(N_HEADS * QK_NOPE, KV_LORA)
    wr = jnp.concatenate([w_router, jnp.zeros((d, LANES - N_EXPERTS), F32)], axis=1)
    br = jnp.concatenate([b_router, jnp.full((LANES - N_EXPERTS,), NEG_INF, F32)])
    return {
        "w_in": w_in_r.astype(BF16), "g_norm_mix": g_norm_mix.reshape(1, d), "g_q_lat": g_q_lat.reshape(1, Q_LORA),
        "w_q_up": w_q_r.astype(BF16),
        "gain_q": jnp.concatenate([g_q_nope, g_q_rope, z32]).reshape(1, LANES),
        "m_grp": jnp.asarray(m_grp, BF16), "cos_t": cos_t, "sin_t": sin_t,
        "g_kv_lat": g_kv_lat.reshape(1, KV_LORA),
        "gain_k": jnp.concatenate([g_k_nope, z64]).reshape(1, LANES),
        "gain_kpe": jnp.concatenate([z64, g_k_rope, z32]).reshape(1, LANES),
        "w_k": w_k_r.astype(BF16), "w_v": wv.astype(BF16),
        "w_abs": absorb.reshape(N_HEADS * HEAD_PAD, 2 * LANES).astype(BF16), "w_kt": w_kt.astype(BF16),
        "g_conv_ln": g_conv_ln.reshape(1, CONV_CH), "b_conv_ln": b_conv_ln.reshape(1, CONV_CH),
        "g_out_attn": g_out_attn.reshape(1, ATTN_WIDTH), "g_out_conv": g_out_conv.reshape(1, CONV_CH),
        "w_out": w_out.astype(BF16), "g_norm_ffn": g_norm_ffn.reshape(1, d),
        "w_router": wr, "b_router": br.reshape(1, LANES),
    }


def _mod_table(mod_p, mod_s, t_new):
    return jnp.concatenate([jnp.broadcast_to(mod_p, (TOK_TILE, D_MODEL)), jnp.repeat(mod_s, t_new, axis=0)], axis=0)


def kernel(x_prompt, x_sample, cache_ckv, cache_kpe, state_conv, page_table, c_prompt, c_sample, w_ada, b_ada, g_norm_mix, g_norm_ffn, w_in, g_q_lat, w_q_up, g_q_nope, g_q_rope, g_kv_lat, g_k_rope, w_kv_up, g_k_nope, w_dw, b_dw, g_conv_ln, b_conv_ln, g_out_attn, g_out_conv, w_out, w_router, b_router, w_exp_up, b_exp_up, w_exp_down, b_exp_down):
    bsz, seq, d = x_prompt.shape
    n_seq, t_new = x_sample.shape[:2]
    depth = w_ada.shape[0]
    assert bsz == 1 and depth == 1 and t_new == 4 and d == D_MODEL
    n_p = bsz * seq
    n_s = n_seq * t_new
    n = n_p + n_s
    past = page_table.shape[1] * cache_ckv.shape[2]
    l = 0

    pos = jnp.concatenate([jnp.arange(seq, dtype=I32), jnp.tile(past + jnp.arange(t_new, dtype=I32), n_seq)])
    p = _prepare(pos, w_in[l], g_norm_mix[l], g_q_lat[l], w_q_up[l], g_q_nope[l], g_q_rope[l], g_kv_lat[l],
                 g_k_rope[l], w_kv_up[l], g_k_nope[l], g_conv_ln[l], b_conv_ln[l], g_out_attn[l], g_out_conv[l],
                 w_out[l], g_norm_ffn[l], w_router[l], b_router[l])

    n_c = 1 + n_seq
    c_all = jnp.concatenate([c_prompt, c_sample, jnp.zeros((-n_c % 8, d), F32)], axis=0)
    mod = _adaln(c_all, w_ada[l], b_ada[l])
    tabs = [_mod_table(mod[0:1, j * d:(j + 1) * d], mod[1:n_c, j * d:(j + 1) * d], t_new) for j in range(6)]
    sh_m, sc_m, gt_m, sh_f, sc_f, gt_f = tabs

    x_p = x_prompt.reshape(n_p, d)
    x_s = x_sample.reshape(n_s, d)
    q_all, ckv_all, kpe_all, u_all, k_all, v_all = _mixer_inputs(x_p, x_s, sh_m, sc_m, p)

    attn_p = _prompt_attention(q_all, k_all, v_all, seq)
    ckv_s = ckv_all[n_p:].reshape(n_seq, t_new, KV_LORA)
    kpe_s = kpe_all[n_p:].reshape(n_seq, t_new, QK_ROPE)
    ckv_new_pad = jnp.pad(ckv_s, ((0, 0), (0, LANES - t_new), (0, 0)))
    kpe_new_t = jnp.swapaxes(jnp.pad(kpe_s, ((0, 0), (0, LANES - t_new), (0, 0))), 1, 2)
    q_s = q_all[n_p:].reshape(n_seq, t_new, N_HEADS * HEAD_PAD)
    attn_s = _sample_attention(page_table, q_s, ckv_new_pad, kpe_new_t, cache_ckv,
                               jnp.swapaxes(cache_kpe, 2, 3), p)

    conv_p = _prompt_conv(u_all, w_dw[l], b_dw[l].reshape(1, CONV_CH))
    u_s = u_all[n_p:].reshape(n_seq, t_new, CONV_CH)
    u_ext_s = jnp.concatenate([state_conv[l], u_s], axis=1)
    ext = CONV_W - 1 + t_new
    ext_pad = -ext % 8
    w_taps = jnp.stack([jnp.pad(w_dw[l], ((t, t_new - 1 - t + ext_pad), (0, 0))) for t in range(t_new)])
    conv_s = _sample_conv(jnp.pad(u_ext_s, ((0, 0), (0, ext_pad), (0, 0))), w_taps, b_dw[l].reshape(1, CONV_CH))
    conv_s = jnp.transpose(conv_s, (1, 0, 2)).reshape(n_s, CONV_CH)

    y_all, f_all, top_i, top_w = _merge_router(x_p, x_s, attn_p, attn_s.reshape(n_s, ATTN_WIDTH), conv_p, conv_s,
                                               gt_m, sh_f, sc_f, p)

    n_tiles = -(-(n * TOP_K + N_EXPERTS * (FFN_TILE - 1)) // FFN_TILE)
    n_slots = n_tiles * FFN_TILE
    pos_tok, meta = _route(top_i, n_tiles)
    slot = jnp.transpose(pos_tok[:, :TOP_K])
    piece_idx = (slot[:, None, :] + (jnp.arange(SC_PIECES, dtype=I32) * n_slots)[None, :, None]).reshape(-1)
    x_sorted = _sc_scatter_rows(f_all.reshape(SC_PIECES * n, SC_ROW), piece_idx, SC_PIECES * n_slots)
    h_sorted = _expert_ffn(meta[:n_tiles, 0], meta[0:1, 1], x_sorted.reshape(SC_PIECES, n_slots, SC_ROW),
                           w_exp_up[l], b_exp_up[l], w_exp_down[l], b_exp_down[l])
    gathered = _sc_gather_rows(h_sorted.reshape(SC_PIECES * n_slots, SC_ROW), piece_idx)
    y_p, y_s = _combine(y_all, gathered.reshape(TOP_K, SC_PIECES, n, SC_ROW), top_w, gt_f, n_p)
    y_p = y_p.reshape(bsz, seq, d)
    y_s = y_s.reshape(n_seq, t_new, d)
    ckv_prompt = ckv_all[:n_p].reshape(1, bsz, seq, KV_LORA)
    kpe_prompt = kpe_all[:n_p].reshape(1, bsz, seq, QK_ROPE)
    conv_prompt = u_all[n_p - (CONV_W - 1):n_p].reshape(1, bsz, CONV_W - 1, CONV_CH)
    ckv_sample = ckv_s[None]
    kpe_sample = kpe_s[None]
    conv_sample = u_ext_s[:, t_new:][None]
    return (y_p, y_s, ckv_prompt, kpe_prompt, conv_prompt, ckv_sample, kpe_sample, conv_sample)
```

```python
import functools

import numpy as np
import jax
import jax.numpy as jnp
from jax import lax
from jax.experimental import pallas as pl
from jax.experimental.pallas import tpu as pltpu
from jax.experimental.pallas import tpu_sc as plsc

F32 = jnp.float32
BF16 = jnp.bfloat16
I32 = jnp.int32
HIGHEST = lax.Precision.HIGHEST

D_MODEL = 1024
N_HEADS = 8
QK_NOPE = 64
QK_ROPE = 32
V_DIM = 64
Q_LORA = 256
KV_LORA = 128
ATTN_WIDTH = N_HEADS * V_DIM
CONV_CH = D_MODEL - ATTN_WIDTH
CONV_W = 31
N_EXPERTS = 32
TOP_K = 4
D_FF = D_MODEL
SWIGLU_LIMIT = 7.0
SWIGLU_ALPHA = 1.702
EPS = 1e-6
NEG_INF = -1e30
ROPE_THETA = 10000.0
SM_SCALE = (QK_NOPE + QK_ROPE) ** -0.5
LOG2E = 1.4426950408889634
Q_SCALE = SM_SCALE * LOG2E

LANES = 128
HEAD_PAD = LANES
ROPE_LO = QK_NOPE
ROPE_HALF = QK_ROPE // 2

TOK_TILE = 256
FA_TQ = 1024
FA_TK = 1024
FA_HEADS = 4
PAGES_PER_STEP = 32
FFN_TILE = 512
ROUTE_TILE = 512
SC_ROW = 256
SC_PIECES = D_MODEL // SC_ROW
SC_WINDOW = 128
VMEM_LIMIT = 56 * 1024 * 1024


def _cparams(sem, vmem=None):
    return pltpu.CompilerParams(dimension_semantics=sem, vmem_limit_bytes=vmem)


def _rsqrt_mean(x, n):
    return lax.rsqrt(jnp.sum(x * x, axis=-1, keepdims=True) * (1.0 / n) + EPS)


def _ada_kernel(c_ref, w_ref, b_ref, o_ref):
    c = c_ref[...]
    s = c * jax.nn.sigmoid(c)
    o_ref[...] = jnp.dot(s, w_ref[...], precision=HIGHEST, preferred_element_type=F32) + b_ref[...]


def _adaln(c_all, w_ada, b_ada):
    rows = c_all.shape[0]
    n_out = w_ada.shape[1]
    return pl.pallas_call(
        _ada_kernel,
        grid=(n_out // D_MODEL,),
        in_specs=[pl.BlockSpec((rows, D_MODEL), lambda j: (0, 0)),
                  pl.BlockSpec((D_MODEL, D_MODEL), lambda j: (0, j)),
                  pl.BlockSpec((1, D_MODEL), lambda j: (0, j))],
        out_specs=pl.BlockSpec((rows, D_MODEL), lambda j: (0, j)),
        out_shape=jax.ShapeDtypeStruct((rows, n_out), F32),
        compiler_params=_cparams(("arbitrary",)),
        name="adaln",
    )(c_all, w_ada, b_ada.reshape(1, n_out))


def _group_norm_rope(x, m_grp, gain, cos_t, sin_t, first_half):
    ms = jnp.dot((x * x).astype(BF16), m_grp, preferred_element_type=F32)
    xn = x * lax.rsqrt(ms + EPS) * gain
    swapped = jnp.where(first_half, pltpu.roll(xn, LANES - ROPE_HALF, 1), pltpu.roll(xn, ROPE_HALF, 1))
    return xn * cos_t + swapped * sin_t


def _pick_rows(n_prompt_tiles, prompt_ref, decode_ref):
    return jnp.where(pl.program_id(0) < n_prompt_tiles, prompt_ref[...], decode_ref[...])


def _mix_kernel(n_prompt_tiles, xp_ref, xs_ref, sh_ref, sc_ref, gmix_ref, win_ref, gql_ref, wq_ref, gq_ref, m_ref,
                cos_ref, sin_ref, gkv_ref, gk_ref, gkpe_ref, wk_ref, wv_ref, vone_ref,
                q_out, ckv_out, kpe_out, u_out, k_out, v_out):
    x = _pick_rows(n_prompt_tiles, xp_ref, xs_ref)
    h = x * _rsqrt_mean(x, D_MODEL) * gmix_ref[...]
    h = h * (1.0 + sc_ref[...]) + sh_ref[...]
    proj = jnp.dot(h.astype(BF16), win_ref[...], preferred_element_type=F32)
    q_lat = proj[:, :Q_LORA]
    ckv_raw = proj[:, Q_LORA:Q_LORA + KV_LORA]
    kpe_blk = proj[:, Q_LORA + KV_LORA:Q_LORA + KV_LORA + LANES]
    glu_lo = Q_LORA + KV_LORA + LANES
    u_out[...] = proj[:, glu_lo:glu_lo + CONV_CH] * jax.nn.sigmoid(proj[:, glu_lo + CONV_CH:glu_lo + 2 * CONV_CH])

    m_grp = m_ref[...]
    cos_t = cos_ref[...]
    sin_t = sin_ref[...]
    lane = lax.broadcasted_iota(I32, (1, LANES), 1)
    first_half = lane < ROPE_LO + ROPE_HALF

    q_lat_n = q_lat * _rsqrt_mean(q_lat, Q_LORA) * gql_ref[...]
    q = jnp.dot(q_lat_n.astype(BF16), wq_ref[...], preferred_element_type=F32)
    gq = gq_ref[...]
    for hd in range(N_HEADS):
        qh = _group_norm_rope(q[:, hd * HEAD_PAD:(hd + 1) * HEAD_PAD], m_grp, gq, cos_t, sin_t, first_half)
        q_out[:, hd * HEAD_PAD:(hd + 1) * HEAD_PAD] = (qh * Q_SCALE).astype(BF16)

    ckv_n = ckv_raw * _rsqrt_mean(ckv_raw, KV_LORA) * gkv_ref[...]
    ckv_out[...] = ckv_n
    kpe_r = _group_norm_rope(kpe_blk, m_grp, gkpe_ref[...], cos_t, sin_t, first_half)
    kpe_out[...] = kpe_r[:, ROPE_LO:ROPE_LO + QK_ROPE]

    ckv_b = ckv_n.astype(BF16)
    kexp = jnp.dot(ckv_b, wk_ref[...], preferred_element_type=F32)
    gk = gk_ref[...]
    for hd in range(N_HEADS):
        kh = kexp[:, hd * HEAD_PAD:(hd + 1) * HEAD_PAD]
        ms = jnp.dot((kh * kh).astype(BF16), m_grp, preferred_element_type=F32)
        k_out[:, hd * HEAD_PAD:(hd + 1) * HEAD_PAD] = (kh * lax.rsqrt(ms + EPS) * gk + kpe_r).astype(BF16)
    v_out[...] = (jnp.dot(ckv_b, wv_ref[...], preferred_element_type=F32) + vone_ref[...]).astype(BF16)


def _mod_index(n_prompt_tiles):
    return lambda i: (jnp.where(i < n_prompt_tiles, 0, i - n_prompt_tiles + 1), 0)


def _prompt_index(n_prompt_tiles):
    return lambda i: (jnp.minimum(i, n_prompt_tiles - 1), 0)


def _decode_index(n_prompt_tiles):
    return lambda i: (jnp.maximum(i - n_prompt_tiles, 0), 0)


def _mixer_inputs(x_p, x_s, sh, sc, p):
    n_prompt_tiles = x_p.shape[0] // TOK_TILE
    n = x_p.shape[0] + x_s.shape[0]
    tm = TOK_TILE
    const = lambda i: (0, 0)
    row = lambda i: (i, 0)
    mod = _mod_index(n_prompt_tiles)
    hw = N_HEADS * HEAD_PAD
    in_cols = p["w_in"].shape[1]
    return pl.pallas_call(
        functools.partial(_mix_kernel, n_prompt_tiles),
        grid=(n // tm,),
        in_specs=[pl.BlockSpec((tm, D_MODEL), _prompt_index(n_prompt_tiles)),
                  pl.BlockSpec((tm, D_MODEL), _decode_index(n_prompt_tiles)),
                  pl.BlockSpec((tm, D_MODEL), mod),
                  pl.BlockSpec((tm, D_MODEL), mod),
                  pl.BlockSpec((1, D_MODEL), const),
                  pl.BlockSpec((D_MODEL, in_cols), const),
                  pl.BlockSpec((1, Q_LORA), const),
                  pl.BlockSpec((Q_LORA, hw), const),
                  pl.BlockSpec((1, LANES), const),
                  pl.BlockSpec((LANES, LANES), const),
                  pl.BlockSpec((tm, LANES), row),
                  pl.BlockSpec((tm, LANES), row),
                  pl.BlockSpec((1, KV_LORA), const),
                  pl.BlockSpec((1, LANES), const),
                  pl.BlockSpec((1, LANES), const),
                  pl.BlockSpec((KV_LORA, hw), const),
                  pl.BlockSpec((KV_LORA, hw), const),
                  pl.BlockSpec((1, hw), const)],
        out_specs=[pl.BlockSpec((tm, hw), row),
                   pl.BlockSpec((tm, KV_LORA), row),
                   pl.BlockSpec((tm, QK_ROPE), row),
                   pl.BlockSpec((tm, CONV_CH), row),
                   pl.BlockSpec((tm, hw), row),
                   pl.BlockSpec((tm, hw), row)],
        out_shape=[jax.ShapeDtypeStruct((n, hw), BF16),
                   jax.ShapeDtypeStruct((n, KV_LORA), F32),
                   jax.ShapeDtypeStruct((n, QK_ROPE), F32),
                   jax.ShapeDtypeStruct((n, CONV_CH), F32),
                   jax.ShapeDtypeStruct((n, hw), BF16),
                   jax.ShapeDtypeStruct((n, hw), BF16)],
        compiler_params=_cparams(("parallel",), VMEM_LIMIT),
        name="mixer_inputs",
    )(x_p, x_s, sh, sc, p["g_norm_mix"], p["w_in"], p["g_q_lat"], p["w_q_up"], p["gain_q"], p["m_grp"],
      p["cos_t"], p["sin_t"], p["g_kv_lat"], p["gain_k"], p["gain_kpe"], p["w_k"], p["w_v_tiles"], p["v_ones"])


def _fa_kernel(qt_ref, kt_ref, q_ref, k_ref, v_ref, o_ref, m_sc, acc_sc):
    t = pl.program_id(1)
    qi = qt_ref[t]
    ki = kt_ref[t]
    last_k = (qi + 1) * (FA_TQ // FA_TK) - 1

    @pl.when(ki == 0)
    def _():
        m_sc[...] = jnp.full(m_sc.shape, NEG_INF, F32)
        acc_sc[...] = jnp.zeros(acc_sc.shape, F32)

    def step(masked):
        if masked:
            col_minus_row = (lax.broadcasted_iota(I32, (FA_TQ, FA_TK), 1)
                             - lax.broadcasted_iota(I32, (FA_TQ, FA_TK), 0))
        for hh in range(FA_HEADS):
            q = q_ref[:, hh * HEAD_PAD:(hh + 1) * HEAD_PAD]
            k = k_ref[:, hh * HEAD_PAD:(hh + 1) * HEAD_PAD]
            v = v_ref[:, hh * HEAD_PAD:(hh + 1) * HEAD_PAD]
            s = lax.dot_general(q, k, (((1,), (1,)), ((), ())), preferred_element_type=F32)
            if masked:
                s = jnp.where(col_minus_row <= qi * FA_TQ - ki * FA_TK, s, NEG_INF)
            m_prev = m_sc[hh]
            m_new = jnp.maximum(m_prev, jnp.max(s, axis=-1, keepdims=True))
            alpha = jnp.exp2(m_prev - m_new)
            pr = jnp.exp2((s - jnp.concatenate([m_new] * (FA_TK // LANES), axis=1)).astype(BF16))
            acc_sc[hh] = alpha * acc_sc[hh] + jnp.dot(pr, v, preferred_element_type=F32)
            m_sc[hh] = m_new

    @pl.when(ki * FA_TK + FA_TK - 1 <= qi * FA_TQ)
    def _():
        step(False)

    @pl.when(ki * FA_TK + FA_TK - 1 > qi * FA_TQ)
    def _():
        step(True)

    @pl.when(ki == last_k)
    def _():
        for hh in range(FA_HEADS):
            acc = acc_sc[hh]
            o_ref[:, hh * V_DIM:(hh + 1) * V_DIM] = acc[:, :V_DIM] / acc[:, V_DIM:V_DIM + 1]


def _prompt_attention(q_all, k_all, v_all, seq):
    nq = seq // FA_TQ
    ratio = FA_TQ // FA_TK
    qt, kt = [], []
    for qi in range(nq):
        for ki in range((qi + 1) * ratio):
            qt.append(qi)
            kt.append(ki)
    qt = jnp.asarray(np.array(qt, np.int32))
    kt = jnp.asarray(np.array(kt, np.int32))
    n_pairs = int(qt.shape[0])
    grid_spec = pltpu.PrefetchScalarGridSpec(
        num_scalar_prefetch=2,
        grid=(N_HEADS // FA_HEADS, n_pairs),
        in_specs=[pl.BlockSpec((FA_TQ, FA_HEADS * HEAD_PAD), lambda hp, t, qt, kt: (qt[t], hp)),
                  pl.BlockSpec((FA_TK, FA_HEADS * HEAD_PAD), lambda hp, t, qt, kt: (kt[t], hp)),
                  pl.BlockSpec((FA_TK, FA_HEADS * HEAD_PAD), lambda hp, t, qt, kt: (kt[t], hp))],
        out_specs=pl.BlockSpec((FA_TQ, FA_HEADS * V_DIM), lambda hp, t, qt, kt: (qt[t], hp)),
        scratch_shapes=[pltpu.VMEM((FA_HEADS, FA_TQ, LANES), F32),
                        pltpu.VMEM((FA_HEADS, FA_TQ, HEAD_PAD), F32)],
    )
    return pl.pallas_call(
        _fa_kernel,
        grid_spec=grid_spec,
        out_shape=jax.ShapeDtypeStruct((seq, ATTN_WIDTH), F32),
        compiler_params=_cparams(("parallel", "arbitrary"), VMEM_LIMIT),
        name="prompt_attention",
    )(qt, kt, q_all, k_all, v_all)


def _sattn_kernel(pt_ref, q_ref, wabs_ref, wkt_ref, wv_ref, ckvn_ref, kpen_ref, ckv_hbm, kpe_hbm,
                  o_ref, m_sc, l_sc, acc_sc, qa_sc, wall_sc, ckv_buf, kpe_buf, sem):
    kb = pl.program_id(1)
    n_kb = pl.num_programs(1)
    step = pl.program_id(0) * n_kb + kb
    n_steps = pl.num_programs(0) * n_kb
    slot = step % 2
    rows = 4 * N_HEADS

    def start_fetch(step_i, slot_i):
        for j in range(PAGES_PER_STEP):
            page = pt_ref[step_i * PAGES_PER_STEP + j]
            pltpu.make_async_copy(ckv_hbm.at[0, page], ckv_buf.at[slot_i, j], sem.at[0, slot_i]).start()
            pltpu.make_async_copy(kpe_hbm.at[0, page], kpe_buf.at[slot_i, j], sem.at[1, slot_i]).start()

    @pl.when(step == 0)
    def _():
        start_fetch(step, slot)

    @pl.when(step + 1 < n_steps)
    def _():
        start_fetch(step + 1, 1 - slot)

    @pl.when(kb == 0)
    def _():
        m_sc[...] = jnp.full(m_sc.shape, NEG_INF, F32)
        l_sc[...] = jnp.zeros(l_sc.shape, F32)
        acc_sc[...] = jnp.zeros(acc_sc.shape, F32)
        q4 = q_ref[...].astype(F32)
        head_of_lane = lax.broadcasted_iota(I32, (N_HEADS, N_HEADS * HEAD_PAD), 1) // HEAD_PAD
        head_of_row = lax.broadcasted_iota(I32, (N_HEADS, N_HEADS * HEAD_PAD), 0)
        own = head_of_lane == head_of_row
        qbd = jnp.concatenate(
            [jnp.where(own, jnp.broadcast_to(q4[qq:qq + 1, :], own.shape), 0.0) for qq in range(4)], axis=0)
        qa = jnp.dot(qbd.astype(BF16), wabs_ref[...], preferred_element_type=F32).astype(BF16)
        qa_sc[...] = qa
        wall_sc[pl.ds(0, N_HEADS * QK_NOPE), :] = wkt_ref[...]
        wall_sc[pl.ds(N_HEADS * QK_NOPE, rows), :] = qa[:, :KV_LORA]

    def attend(state, ckv_b, kpe_t, mask):
        m_prev, l_prev, acc_prev = state
        nt = (((1,), (1,)), ((), ()))
        kn_all = lax.dot_general(wall_sc[...], ckv_b, nt, preferred_element_type=F32)
        keys = kn_all.shape[1]
        kn_t = kn_all[:N_HEADS * QK_NOPE]
        ss = jnp.sum((kn_t * kn_t).reshape(N_HEADS, QK_NOPE, keys), axis=1)
        r8 = lax.rsqrt(ss * (1.0 / QK_NOPE) + EPS)
        rope = jnp.dot(qa_sc[:, KV_LORA:KV_LORA + QK_ROPE], kpe_t, preferred_element_type=F32)
        s = kn_all[N_HEADS * QK_NOPE:] * jnp.concatenate([r8] * 4, axis=0) + rope
        if mask is not None:
            s = jnp.where(mask, s, NEG_INF)
        m_new = jnp.maximum(m_prev, jnp.max(s, axis=-1, keepdims=True))
        alpha = jnp.exp2(m_prev - m_new)
        pr = jnp.exp2(s - m_new)
        l_new = alpha * l_prev + jnp.sum(pr, axis=-1, keepdims=True)
        acc_new = alpha * acc_prev + jnp.dot(pr.astype(BF16), ckv_b, preferred_element_type=F32)
        return m_new, l_new, acc_new

    pltpu.make_async_copy(ckv_hbm.at[0, pl.ds(0, PAGES_PER_STEP)], ckv_buf.at[slot], sem.at[0, slot]).wait()
    pltpu.make_async_copy(kpe_hbm.at[0, pl.ds(0, PAGES_PER_STEP)], kpe_buf.at[slot], sem.at[1, slot]).wait()

    page = ckv_buf.shape[2]
    ckv_b = ckv_buf[slot].reshape(PAGES_PER_STEP * page, KV_LORA).astype(BF16)
    kpe_t = jnp.concatenate([kpe_buf[slot, j].astype(BF16) for j in range(PAGES_PER_STEP)], axis=1)
    state = attend((m_sc[...], l_sc[...], acc_sc[...]), ckv_b, kpe_t, None)
    m_sc[...], l_sc[...], acc_sc[...] = state

    @pl.when(kb == n_kb - 1)
    def _():
        key = lax.broadcasted_iota(I32, (rows, LANES), 1)
        qry = lax.broadcasted_iota(I32, (rows, LANES), 0) // N_HEADS
        _, l_fin, acc_fin = attend(state, ckvn_ref[...].astype(BF16), kpen_ref[...].astype(BF16), key <= qry)
        lat = acc_fin / l_fin
        o_all = jnp.dot(lat.astype(BF16), wv_ref[...], preferred_element_type=F32)
        head_of_col = lax.broadcasted_iota(I32, (rows, ATTN_WIDTH), 1) // V_DIM
        head_of_row = lax.broadcasted_iota(I32, (rows, ATTN_WIDTH), 0) % N_HEADS
        o_own = jnp.where(head_of_col == head_of_row, o_all, 0.0)
        o_ref[...] = jnp.sum(o_own.reshape(4, N_HEADS, ATTN_WIDTH), axis=1)


def _sample_attention(page_table, q_s, ckv_new_pad, kpe_new_t, cache_ckv, cache_kpe_t, p):
    n_seq, n_pages = page_table.shape
    page = cache_ckv.shape[2]
    n_kb = n_pages // PAGES_PER_STEP
    hw = N_HEADS * HEAD_PAD
    assert n_pages % PAGES_PER_STEP == 0
    per_seq3 = lambda b, kb, pt: (b, 0, 0)
    const = lambda b, kb, pt: (0, 0)
    in_specs = [pl.BlockSpec((None, 4, hw), per_seq3),
                pl.BlockSpec((hw, 2 * LANES), const),
                pl.BlockSpec((N_HEADS * QK_NOPE, KV_LORA), const),
                pl.BlockSpec((KV_LORA, ATTN_WIDTH), const),
                pl.BlockSpec((None, LANES, KV_LORA), per_seq3),
                pl.BlockSpec((None, QK_ROPE, LANES), per_seq3),
                pl.BlockSpec(memory_space=pl.ANY),
                pl.BlockSpec(memory_space=pl.ANY)]
    grid_spec = pltpu.PrefetchScalarGridSpec(
        num_scalar_prefetch=1,
        grid=(n_seq, n_kb),
        in_specs=in_specs,
        out_specs=pl.BlockSpec((None, 4, ATTN_WIDTH), per_seq3),
        scratch_shapes=[pltpu.VMEM((4 * N_HEADS, 1), F32),
                        pltpu.VMEM((4 * N_HEADS, 1), F32),
                        pltpu.VMEM((4 * N_HEADS, KV_LORA), F32),
                        pltpu.VMEM((4 * N_HEADS, 2 * LANES), BF16),
                        pltpu.VMEM((N_HEADS * QK_NOPE + 4 * N_HEADS, KV_LORA), BF16),
                        pltpu.VMEM((2, PAGES_PER_STEP, page, KV_LORA), F32),
                        pltpu.VMEM((2, PAGES_PER_STEP, QK_ROPE, page), F32),
                        pltpu.SemaphoreType.DMA((2, 2))],
    )
    return pl.pallas_call(
        _sattn_kernel,
        grid_spec=grid_spec,
        out_shape=jax.ShapeDtypeStruct((n_seq, 4, ATTN_WIDTH), F32),
        compiler_params=_cparams(("arbitrary", "arbitrary"), VMEM_LIMIT),
        name="decode_attention",
    )(page_table.reshape(-1), q_s, p["w_abs"], p["w_kt"], p["w_v"], ckv_new_pad, kpe_new_t, cache_ckv, cache_kpe_t)


CONV_HALO = 32
CONV_ROWS = 64


def _conv_kernel(halo_ref, u_ref, w_ref, b_ref, o_ref, ext_sc):
    i = pl.program_id(0)
    tm = u_ref.shape[0]
    ext_sc[pl.ds(0, CONV_HALO), :] = jnp.where(i == 0, 0.0, halo_ref[...])
    ext_sc[pl.ds(CONV_HALO, tm), :] = u_ref[...]
    first = CONV_HALO - (CONV_W - 1)
    for rc in range(tm // CONV_ROWS):
        acc = jnp.broadcast_to(b_ref[...], (CONV_ROWS, CONV_CH))
        for j in range(CONV_W):
            acc = acc + ext_sc[pl.ds(rc * CONV_ROWS + first + j, CONV_ROWS), :] * w_ref[j:j + 1, :]
        o_ref[pl.ds(rc * CONV_ROWS, CONV_ROWS), :] = acc


def _prompt_conv(u_all, w_dw, b_dw):
    n = u_all.shape[0]
    tm = TOK_TILE
    per = tm // CONV_HALO
    return pl.pallas_call(
        _conv_kernel,
        grid=(n // tm,),
        in_specs=[pl.BlockSpec((CONV_HALO, CONV_CH), lambda i: (jnp.maximum(i * per - 1, 0), 0)),
                  pl.BlockSpec((tm, CONV_CH), lambda i: (i, 0)),
                  pl.BlockSpec((CONV_W, CONV_CH), lambda i: (0, 0)),
                  pl.BlockSpec((1, CONV_CH), lambda i: (0, 0))],
        out_specs=pl.BlockSpec((tm, CONV_CH), lambda i: (i, 0)),
        out_shape=jax.ShapeDtypeStruct((n, CONV_CH), F32),
        scratch_shapes=[pltpu.VMEM((CONV_HALO + tm, CONV_CH), F32)],
        compiler_params=_cparams(("parallel",)),
        name="prompt_conv",
    )(u_all, u_all, w_dw, b_dw)


def _sconv_kernel(u_ref, wt_ref, b_ref, o_ref):
    u = u_ref[...]
    for t in range(o_ref.shape[0]):
        o_ref[t] = jnp.sum(u * wt_ref[t][None, :, :], axis=1) + b_ref[...]


def _sample_conv(u_ext, w_taps, b_dw):
    n_seq, ext, _ = u_ext.shape
    t_new = w_taps.shape[0]
    sb = 8
    return pl.pallas_call(
        _sconv_kernel,
        grid=(n_seq // sb,),
        in_specs=[pl.BlockSpec((sb, ext, CONV_CH), lambda i: (i, 0, 0)),
                  pl.BlockSpec((t_new, ext, CONV_CH), lambda i: (0, 0, 0)),
                  pl.BlockSpec((1, CONV_CH), lambda i: (0, 0))],
        out_specs=pl.BlockSpec((t_new, sb, CONV_CH), lambda i: (0, i, 0)),
        out_shape=jax.ShapeDtypeStruct((t_new, n_seq, CONV_CH), F32),
        compiler_params=_cparams(("parallel",)),
        name="decode_conv",
    )(u_ext, w_taps, b_dw)


def _lane_pack(cols, dtype):
    lane = lax.broadcasted_iota(I32, (cols[0].shape[0], LANES), 1)
    out = jnp.zeros((cols[0].shape[0], LANES), dtype)
    for j, c in enumerate(cols):
        out = jnp.where(lane == j, c.astype(dtype), out)
    return out


def _merge_kernel(n_prompt_tiles, xp_ref, xs_ref, ap_ref, as_ref, cp_ref, cs_ref, gate_ref, shf_ref, scf_ref,
                  gln_ref, bln_ref, ga_ref, gc_ref, wout_ref, gffn_ref, wrh_ref, wrl_ref, br_ref,
                  y_out, f_out, ti_out, tw_out):
    yc = _pick_rows(n_prompt_tiles, cp_ref, cs_ref)
    mu = jnp.mean(yc, axis=-1, keepdims=True)
    xc = yc - mu
    var = jnp.mean(xc * xc, axis=-1, keepdims=True)
    ln = xc * lax.rsqrt(var + EPS) * gln_ref[...] + bln_ref[...]
    conv = ln * jax.nn.sigmoid(ln)
    attn = _pick_rows(n_prompt_tiles, ap_ref, as_ref)
    a_n = attn * _rsqrt_mean(attn, ATTN_WIDTH) * ga_ref[...]
    c_n = conv * _rsqrt_mean(conv, CONV_CH) * gc_ref[...]
    m = (jnp.dot(a_n.astype(BF16), wout_ref[:ATTN_WIDTH, :], preferred_element_type=F32)
         + jnp.dot(c_n.astype(BF16), wout_ref[ATTN_WIDTH:, :], preferred_element_type=F32))
    y = _pick_rows(n_prompt_tiles, xp_ref, xs_ref) + gate_ref[...] * m
    y_out[...] = y
    f = y * _rsqrt_mean(y, D_MODEL) * gffn_ref[...]
    f = f * (1.0 + scf_ref[...]) + shf_ref[...]
    for j in range(SC_PIECES):
        f_out[j] = f[:, j * SC_ROW:(j + 1) * SC_ROW]
    f_hi = f.astype(BF16)
    f_lo = (f - f_hi.astype(F32)).astype(BF16)
    logits = (jnp.dot(f_hi, wrh_ref[...], preferred_element_type=F32)
              + jnp.dot(f_hi, wrl_ref[...], preferred_element_type=F32)
              + jnp.dot(f_lo, wrh_ref[...], preferred_element_type=F32)) + br_ref[...]
    lane = lax.broadcasted_iota(I32, logits.shape, 1)
    vals, idxs = [], []
    for _ in range(TOP_K):
        mx = jnp.max(logits, axis=-1, keepdims=True)
        ix = jnp.min(jnp.where(logits == mx, lane, LANES), axis=-1, keepdims=True)
        vals.append(mx)
        idxs.append(ix)
        logits = jnp.where(lane == ix, NEG_INF * 4.0, logits)
    exps = [jnp.exp(v - vals[0]) for v in vals]
    tot = exps[0] + exps[1] + exps[2] + exps[3]
    ti_out[...] = _lane_pack(idxs, I32)
    tw_out[...] = _lane_pack([e / tot for e in exps], F32)


def _merge_router(x_p, x_s, attn_p, attn_s, conv_p, conv_s, gate, shf, scf, p):
    n_prompt_tiles = x_p.shape[0] // TOK_TILE
    n = x_p.shape[0] + x_s.shape[0]
    tm = TOK_TILE
    const = lambda i: (0, 0)
    row = lambda i: (i, 0)
    mod = _mod_index(n_prompt_tiles)
    from_p = _prompt_index(n_prompt_tiles)
    from_s = _decode_index(n_prompt_tiles)
    return pl.pallas_call(
        functools.partial(_merge_kernel, n_prompt_tiles),
        grid=(n // tm,),
        in_specs=[pl.BlockSpec((tm, D_MODEL), from_p),
                  pl.BlockSpec((tm, D_MODEL), from_s),
                  pl.BlockSpec((tm, ATTN_WIDTH), from_p),
                  pl.BlockSpec((tm, ATTN_WIDTH), from_s),
                  pl.BlockSpec((tm, CONV_CH), from_p),
                  pl.BlockSpec((tm, CONV_CH), from_s),
                  pl.BlockSpec((tm, D_MODEL), mod),
                  pl.BlockSpec((tm, D_MODEL), mod),
                  pl.BlockSpec((tm, D_MODEL), mod),
                  pl.BlockSpec((1, CONV_CH), const),
                  pl.BlockSpec((1, CONV_CH), const),
                  pl.BlockSpec((1, ATTN_WIDTH), const),
                  pl.BlockSpec((1, CONV_CH), const),
                  pl.BlockSpec((D_MODEL, D_MODEL), const),
                  pl.BlockSpec((1, D_MODEL), const),
                  pl.BlockSpec((D_MODEL, LANES), const),
                  pl.BlockSpec((D_MODEL, LANES), const),
                  pl.BlockSpec((1, LANES), const)],
        out_specs=[pl.BlockSpec((tm, D_MODEL), row),
                   pl.BlockSpec((SC_PIECES, tm, SC_ROW), lambda i: (0, i, 0)),
                   pl.BlockSpec((tm, LANES), row),
                   pl.BlockSpec((tm, LANES), row)],
        out_shape=[jax.ShapeDtypeStruct((n, D_MODEL), F32),
                   jax.ShapeDtypeStruct((SC_PIECES, n, SC_ROW), F32),
                   jax.ShapeDtypeStruct((n, LANES), I32),
                   jax.ShapeDtypeStruct((n, LANES), F32)],
        compiler_params=_cparams(("parallel",), VMEM_LIMIT),
        name="merge_router",
    )(x_p, x_s, attn_p, attn_s, conv_p, conv_s, gate, shf, scf, p["g_conv_ln"], p["b_conv_ln"], p["g_out_attn"],
      p["g_out_conv"], p["w_out"], p["g_norm_ffn"], p["w_router_hi"], p["w_router_lo"], p["b_router"])


def _select_lane(table, idx_col, lane):
    return jnp.sum(jnp.where(lane == idx_col, table, 0.0), axis=-1, keepdims=True)


def _rank_kernel(ti_ref, rk_out, cnt_out, carry_sc):
    i = pl.program_id(0)

    @pl.when(i == 0)
    def _():
        carry_sc[...] = jnp.zeros(carry_sc.shape, F32)

    ti = ti_ref[...]
    tn = ti.shape[0]
    lane = lax.broadcasted_iota(I32, (tn, LANES), 1)
    sel = jnp.zeros((tn, LANES), F32)
    for k in range(TOP_K):
        sel = sel + (lane == ti[:, k:k + 1]).astype(F32)
    r_i = lax.broadcasted_iota(I32, (tn, tn), 0)
    c_i = lax.broadcasted_iota(I32, (tn, tn), 1)
    below = (c_i < r_i).astype(BF16)
    rank = carry_sc[...] + jnp.dot(below, sel.astype(BF16), preferred_element_type=F32)
    rk_out[...] = _lane_pack([_select_lane(rank, ti[:, k:k + 1], lane) for k in range(TOP_K)], F32)
    carry_sc[...] = carry_sc[...] + jnp.sum(sel, axis=0, keepdims=True)
    cnt_out[...] = jnp.broadcast_to(carry_sc[...], cnt_out.shape)


def _pos_kernel(cnt_ref, ti_ref, rk_ref, pos_out, meta_out):
    cnt = cnt_ref[...]
    padded = jnp.ceil(cnt * (1.0 / FFN_TILE)) * FFN_TILE
    r_i = lax.broadcasted_iota(I32, (LANES, LANES), 0)
    c_i = lax.broadcasted_iota(I32, (LANES, LANES), 1)
    before = (r_i < c_i).astype(F32)
    offs = jnp.dot(padded, before, precision=HIGHEST, preferred_element_type=F32)
    ends = offs + padded
    ti = ti_ref[...]
    tn = ti.shape[0]
    lane = lax.broadcasted_iota(I32, (tn, LANES), 1)
    off_row = offs[0:1, :]
    rk = rk_ref[...]
    pos = [_select_lane(jnp.broadcast_to(off_row, (tn, LANES)), ti[:, k:k + 1], lane) + rk[:, k:k + 1]
           for k in range(TOP_K)]
    pos_out[...] = _lane_pack(pos, F32).astype(I32)

    @pl.when(pl.program_id(0) == 0)
    def _():
        nt = meta_out.shape[0]
        start = (lax.broadcasted_iota(I32, (nt, LANES), 0) * FFN_TILE).astype(F32)
        elane = lax.broadcasted_iota(I32, (nt, LANES), 1)
        done = jnp.where((elane < N_EXPERTS) & (jnp.broadcast_to(ends[0:1, :], (nt, LANES)) <= start), 1.0, 0.0)
        expert = jnp.minimum(jnp.sum(done, axis=-1, keepdims=True), N_EXPERTS - 1.0)
        total = jnp.sum(jnp.where(elane < N_EXPERTS, jnp.broadcast_to(padded[0:1, :], (nt, LANES)), 0.0),
                        axis=-1, keepdims=True)
        meta_out[...] = _lane_pack([expert, total * (1.0 / FFN_TILE)], F32).astype(I32)


def _route(top_i, n_tiles_max):
    n = top_i.shape[0]
    tn = ROUTE_TILE
    rk, cnt = pl.pallas_call(
        _rank_kernel,
        grid=(n // tn,),
        in_specs=[pl.BlockSpec((tn, LANES), lambda i: (i, 0))],
        out_specs=[pl.BlockSpec((tn, LANES), lambda i: (i, 0)),
                   pl.BlockSpec((8, LANES), lambda i: (0, 0))],
        out_shape=[jax.ShapeDtypeStruct((n, LANES), F32),
                   jax.ShapeDtypeStruct((8, LANES), F32)],
        scratch_shapes=[pltpu.VMEM((1, LANES), F32)],
        compiler_params=_cparams(("arbitrary",)),
        name="route_rank",
    )(top_i)
    nt_pad = -(-n_tiles_max // 8) * 8
    pos, meta = pl.pallas_call(
        _pos_kernel,
        grid=(n // tn,),
        in_specs=[pl.BlockSpec((8, LANES), lambda i: (0, 0)),
                  pl.BlockSpec((tn, LANES), lambda i: (i, 0)),
                  pl.BlockSpec((tn, LANES), lambda i: (i, 0))],
        out_specs=[pl.BlockSpec((tn, LANES), lambda i: (i, 0)),
                   pl.BlockSpec((nt_pad, LANES), lambda i: (0, 0))],
        out_shape=[jax.ShapeDtypeStruct((n, LANES), I32),
                   jax.ShapeDtypeStruct((nt_pad, LANES), I32)],
        compiler_params=_cparams(("arbitrary",)),
        name="route_pos",
    )(cnt, top_i, rk)
    return pos, meta


def _sc_scatter_rows(x, idx, n_out):
    n_src = x.shape[0]
    n = idx.shape[0]
    n_src_blk = n_src // SC_WINDOW
    mesh = plsc.VectorSubcoreMesh(core_axis_name="c", subcore_axis_name="s")

    @pl.kernel(out_type=jax.ShapeDtypeStruct((n_out, SC_ROW), x.dtype), mesh=mesh)
    def k(x_hbm, i_hbm, o_hbm):
        def body(x_vmem, i_vmem):
            pltpu.sync_copy(x_vmem, o_hbm.at[i_vmem.at[0]])

        pltpu.emit_pipeline(
            body,
            grid=(n // SC_WINDOW,),
            in_specs=[pl.BlockSpec((SC_WINDOW, SC_ROW), index_map=lambda i: (i % n_src_blk, 0)),
                      pl.BlockSpec((1, SC_WINDOW), index_map=lambda i: (0, i))],
            out_specs=[],
            core_axis_name=("c", "s"),
            dimension_semantics=(pltpu.PARALLEL,),
        )(x_hbm, i_hbm)

    return k(x, idx.reshape(1, n))


def _sc_gather_rows(x, idx):
    n = idx.shape[0]
    mesh = plsc.VectorSubcoreMesh(core_axis_name="c", subcore_axis_name="s")

    @pl.kernel(out_type=jax.ShapeDtypeStruct((n, SC_ROW), x.dtype), mesh=mesh)
    def k(x_hbm, i_hbm, o_hbm):
        def body(i_vmem, o_vmem):
            pltpu.sync_copy(x_hbm.at[i_vmem.at[0]], o_vmem)

        pltpu.emit_pipeline(
            body,
            grid=(n // SC_WINDOW,),
            in_specs=[pl.BlockSpec((1, SC_WINDOW), index_map=lambda i: (0, i))],
            out_specs=[pl.BlockSpec((SC_WINDOW, SC_ROW), index_map=lambda i: (i, 0))],
            core_axis_name=("c", "s"),
            dimension_semantics=(pltpu.PARALLEL,),
        )(i_hbm, o_hbm)

    return k(x, idx.reshape(1, n))


def _ffn_kernel(te_ref, nv_ref, x_ref, wu_ref, bu_ref, wd_ref, bd_ref, o_ref, wu_sc, wd_sc):
    t = pl.program_id(0)
    valid = t < nv_ref[0]
    prev = te_ref[jnp.maximum(t - 1, 0)]
    fresh = jnp.logical_or(t == 0, te_ref[t] != prev)

    @pl.when(jnp.logical_and(valid, fresh))
    def _():
        wu_sc[...] = wu_ref[...].astype(BF16)
        wd_sc[...] = wd_ref[...].astype(BF16)

    @pl.when(valid)
    def _():
        x = jnp.concatenate([x_ref[j] for j in range(SC_PIECES)], axis=1)
        z = jnp.dot(x.astype(BF16), wu_sc[...], preferred_element_type=F32) + bu_ref[...]
        zg = jnp.minimum(z[:, :D_FF], SWIGLU_LIMIT)
        zl = jnp.clip(z[:, D_FF:], -SWIGLU_LIMIT, SWIGLU_LIMIT)
        act = zg * jax.nn.sigmoid(SWIGLU_ALPHA * zg) * (zl + 1.0)
        out = jnp.dot(act.astype(BF16), wd_sc[...], preferred_element_type=F32) + bd_ref[...]
        for j in range(SC_PIECES):
            o_ref[j] = out[:, j * SC_ROW:(j + 1) * SC_ROW]


def _expert_ffn(tile_expert, n_valid, x_sorted, w_up, b_up, w_down, b_down):
    n_slots = x_sorted.shape[1]
    n_tiles = n_slots // FFN_TILE
    xmap = lambda t, te, nv: (0, jnp.minimum(t, nv[0] - 1), 0)
    emap = lambda t, te, nv: (te[t], 0, 0)
    grid_spec = pltpu.PrefetchScalarGridSpec(
        num_scalar_prefetch=2,
        grid=(n_tiles,),
        in_specs=[pl.BlockSpec((SC_PIECES, FFN_TILE, SC_ROW), xmap),
                  pl.BlockSpec((None, D_MODEL, 2 * D_FF), emap),
                  pl.BlockSpec((None, 1, 2 * D_FF), emap),
                  pl.BlockSpec((None, D_FF, D_MODEL), emap),
                  pl.BlockSpec((None, 1, D_MODEL), emap)],
        out_specs=pl.BlockSpec((SC_PIECES, FFN_TILE, SC_ROW), xmap),
        scratch_shapes=[pltpu.VMEM((D_MODEL, 2 * D_FF), BF16),
                        pltpu.VMEM((D_FF, D_MODEL), BF16)],
    )
    return pl.pallas_call(
        _ffn_kernel,
        grid_spec=grid_spec,
        out_shape=jax.ShapeDtypeStruct((SC_PIECES, n_slots, SC_ROW), F32),
        compiler_params=_cparams(("arbitrary",), VMEM_LIMIT),
        name="expert_ffn",
    )(tile_expert, n_valid, x_sorted, w_up, b_up.reshape(N_EXPERTS, 1, 2 * D_FF), w_down,
      b_down.reshape(N_EXPERTS, 1, D_MODEL))


def _combine_kernel(n_prompt_tiles, y_ref, g_ref, tw_ref, gate_ref, op_ref, os_ref):
    tw = tw_ref[...]
    rows = lambda k: jnp.concatenate([g_ref[k, j] for j in range(SC_PIECES)], axis=1)
    moe = rows(0) * tw[:, 0:1]
    for k in range(1, TOP_K):
        moe = moe + rows(k) * tw[:, k:k + 1]
    out = y_ref[...] + gate_ref[...] * moe
    i = pl.program_id(0)

    @pl.when(i < n_prompt_tiles)
    def _():
        op_ref[...] = out

    @pl.when(i >= n_prompt_tiles)
    def _():
        os_ref[...] = out


def _combine(y_all, gathered, top_w, gate, n_p):
    n = y_all.shape[0]
    tm = TOK_TILE
    n_prompt_tiles = n_p // tm
    row = lambda i: (i, 0)
    return pl.pallas_call(
        functools.partial(_combine_kernel, n_prompt_tiles),
        grid=(n // tm,),
        in_specs=[pl.BlockSpec((tm, D_MODEL), row),
                  pl.BlockSpec((TOP_K, SC_PIECES, tm, SC_ROW), lambda i: (0, 0, i, 0)),
                  pl.BlockSpec((tm, LANES), row),
                  pl.BlockSpec((tm, D_MODEL), _mod_index(n_prompt_tiles))],
        out_specs=[pl.BlockSpec((tm, D_MODEL), _prompt_index(n_prompt_tiles)),
                   pl.BlockSpec((tm, D_MODEL), _decode_index(n_prompt_tiles))],
        out_shape=[jax.ShapeDtypeStruct((n_p, D_MODEL), F32),
                   jax.ShapeDtypeStruct((n - n_p, D_MODEL), F32)],
        compiler_params=_cparams(("arbitrary",), VMEM_LIMIT),
        name="moe_combine",
    )(y_all, gathered, top_w, gate)


def _head_tiles(nope, rope):
    pad = jnp.zeros(nope.shape[:-1] + (HEAD_PAD - QK_NOPE - QK_ROPE,), nope.dtype)
    t = jnp.concatenate([nope, rope, pad], axis=-1)
    return t.reshape(t.shape[:-2] + (N_HEADS * HEAD_PAD,))


def _prepare(pos, w_in, g_norm_mix, g_q_lat, w_q_up, g_q_nope, g_q_rope, g_kv_lat, g_k_rope, w_kv_up, g_k_nope,
             g_conv_ln, b_conv_ln, g_out_attn, g_out_conv, w_out, g_norm_ffn, w_router, b_router):
    z32 = jnp.zeros((HEAD_PAD - QK_NOPE - QK_ROPE,), F32)
    z64 = jnp.zeros((QK_NOPE,), F32)
    d = D_MODEL
    kpe_cols = w_in[:, Q_LORA + KV_LORA:Q_LORA + KV_LORA + QK_ROPE]
    kpe_tile = jnp.concatenate([jnp.zeros((d, QK_NOPE), F32), kpe_cols, jnp.zeros((d, z32.shape[0]), F32)], axis=1)
    w_in_r = jnp.concatenate([w_in[:, :Q_LORA + KV_LORA], kpe_tile, w_in[:, Q_LORA + KV_LORA + QK_ROPE:]], axis=1)
    wq = w_q_up.reshape(Q_LORA, N_HEADS, QK_NOPE + QK_ROPE)
    w_q_r = _head_tiles(wq[..., :QK_NOPE], wq[..., QK_NOPE:])
    wk = w_kv_up[..., :QK_NOPE]
    wv = w_kv_up[..., QK_NOPE:].reshape(KV_LORA, ATTN_WIDTH)
    w_k_r = _head_tiles(wk, jnp.zeros((KV_LORA, N_HEADS, QK_ROPE), F32))
    lane = np.arange(LANES)
    grp = np.where(lane < QK_NOPE, 0, np.where(lane < QK_NOPE + QK_ROPE, 1, 2))
    m_grp = ((grp[:, None] == grp[None, :]) & (grp[:, None] < 2)).astype(np.float32)
    m_grp = m_grp / np.where(grp < 1, QK_NOPE, QK_ROPE)[None, :]
    inv = ROPE_THETA ** (-jnp.arange(0, QK_ROPE, 2, dtype=F32) / QK_ROPE)
    ang = pos.astype(F32)[:, None] * inv[None, :]
    cs, sn = jnp.cos(ang), jnp.sin(ang)
    n = pos.shape[0]
    cos_t = jnp.concatenate([jnp.ones((n, QK_NOPE), F32), cs, cs, jnp.zeros((n, z32.shape[0]), F32)], axis=1)
    sin_t = jnp.concatenate([jnp.zeros((n, QK_NOPE), F32), -sn, sn, jnp.zeros((n, z32.shape[0]), F32)], axis=1)
    wk_g = wk * g_k_nope[None, None, :]
    absorb = jnp.zeros((N_HEADS, HEAD_PAD, 2 * LANES), F32)
    absorb = absorb.at[:, :QK_NOPE, :KV_LORA].set(jnp.transpose(wk_g, (1, 2, 0)))
    absorb = absorb.at[:, ROPE_LO:ROPE_LO + QK_ROPE, KV_LORA:KV_LORA + QK_ROPE].set(
        jnp.broadcast_to(jnp.eye(QK_ROPE, dtype=F32), (N_HEADS, QK_ROPE, QK_ROPE)))
    w_kt = jnp.transpose(wk, (1, 2, 0)).reshape(N_HEADS * QK_NOPE, KV_LORA)
    wr = jnp.concatenate([w_router, jnp.zeros((d, LANES - N_EXPERTS), F32)], axis=1)
    wr_hi = wr.astype(BF16)
    wr_lo = (wr - wr_hi.astype(F32)).astype(BF16)
    br = jnp.concatenate([b_router, jnp.full((LANES - N_EXPERTS,), NEG_INF, F32)])
    wv_h = w_kv_up[..., QK_NOPE:]
    w_v_tiles = jnp.concatenate([wv_h, jnp.zeros((KV_LORA, N_HEADS, HEAD_PAD - V_DIM), F32)], axis=-1)
    v_ones = (jnp.arange(N_HEADS * HEAD_PAD) % HEAD_PAD == V_DIM).astype(F32).reshape(1, N_HEADS * HEAD_PAD)
    return {
        "w_v_tiles": w_v_tiles.reshape(KV_LORA, N_HEADS * HEAD_PAD).astype(BF16), "v_ones": v_ones,
        "w_router_hi": wr_hi, "w_router_lo": wr_lo,
        "w_in": w_in_r.astype(BF16), "g_norm_mix": g_norm_mix.reshape(1, d), "g_q_lat": g_q_lat.reshape(1, Q_LORA),
        "w_q_up": w_q_r.astype(BF16),
        "gain_q": jnp.concatenate([g_q_nope, g_q_rope, z32]).reshape(1, LANES),
        "m_grp": jnp.asarray(m_grp, BF16), "cos_t": cos_t, "sin_t": sin_t,
        "g_kv_lat": g_kv_lat.reshape(1, KV_LORA),
        "gain_k": jnp.concatenate([g_k_nope, z64]).reshape(1, LANES),
        "gain_kpe": jnp.concatenate([z64, g_k_rope, z32]).reshape(1, LANES),
        "w_k": w_k_r.astype(BF16), "w_v": wv.astype(BF16),
        "w_abs": absorb.reshape(N_HEADS * HEAD_PAD, 2 * LANES).astype(BF16), "w_kt": w_kt.astype(BF16),
        "g_conv_ln": g_conv_ln.reshape(1, CONV_CH), "b_conv_ln": b_conv_ln.reshape(1, CONV_CH),
        "g_out_attn": g_out_attn.reshape(1, ATTN_WIDTH), "g_out_conv": g_out_conv.reshape(1, CONV_CH),
        "w_out": w_out.astype(BF16), "g_norm_ffn": g_norm_ffn.reshape(1, d),
        "b_router": br.reshape(1, LANES),
    }


def _mod_table(mod_p, mod_s, t_new):
    return jnp.concatenate([jnp.broadcast_to(mod_p, (TOK_TILE, D_MODEL)), jnp.repeat(mod_s, t_new, axis=0)], axis=0)


def kernel(x_prompt, x_sample, cache_ckv, cache_kpe, state_conv, page_table, c_prompt, c_sample, w_ada, b_ada, g_norm_mix, g_norm_ffn, w_in, g_q_lat, w_q_up, g_q_nope, g_q_rope, g_kv_lat, g_k_rope, w_kv_up, g_k_nope, w_dw, b_dw, g_conv_ln, b_conv_ln, g_out_attn, g_out_conv, w_out, w_router, b_router, w_exp_up, b_exp_up, w_exp_down, b_exp_down):
    bsz, seq, d = x_prompt.shape
    n_seq, t_new = x_sample.shape[:2]
    depth = w_ada.shape[0]
    assert bsz == 1 and depth == 1 and t_new == 4 and d == D_MODEL
    n_p = bsz * seq
    n_s = n_seq * t_new
    n = n_p + n_s
    past = page_table.shape[1] * cache_ckv.shape[2]
    l = 0

    pos = jnp.concatenate([jnp.arange(seq, dtype=I32), jnp.tile(past + jnp.arange(t_new, dtype=I32), n_seq)])
    p = _prepare(pos, w_in[l], g_norm_mix[l], g_q_lat[l], w_q_up[l], g_q_nope[l], g_q_rope[l], g_kv_lat[l],
                 g_k_rope[l], w_kv_up[l], g_k_nope[l], g_conv_ln[l], b_conv_ln[l], g_out_attn[l], g_out_conv[l],
                 w_out[l], g_norm_ffn[l], w_router[l], b_router[l])

    n_c = 1 + n_seq
    c_all = jnp.concatenate([c_prompt, c_sample, jnp.zeros((-n_c % 8, d), F32)], axis=0)
    mod = _adaln(c_all, w_ada[l], b_ada[l])
    tabs = [_mod_table(mod[0:1, j * d:(j + 1) * d], mod[1:n_c, j * d:(j + 1) * d], t_new) for j in range(6)]
    sh_m, sc_m, gt_m, sh_f, sc_f, gt_f = tabs

    x_p = x_prompt.reshape(n_p, d)
    x_s = x_sample.reshape(n_s, d)
    q_all, ckv_all, kpe_all, u_all, k_all, v_all = _mixer_inputs(x_p, x_s, sh_m, sc_m, p)

    attn_p = _prompt_attention(q_all, k_all, v_all, seq)
    ckv_s = ckv_all[n_p:].reshape(n_seq, t_new, KV_LORA)
    kpe_s = kpe_all[n_p:].reshape(n_seq, t_new, QK_ROPE)
    ckv_new_pad = jnp.pad(ckv_s, ((0, 0), (0, LANES - t_new), (0, 0)))
    kpe_new_t = jnp.swapaxes(jnp.pad(kpe_s, ((0, 0), (0, LANES - t_new), (0, 0))), 1, 2)
    q_s = q_all[n_p:].reshape(n_seq, t_new, N_HEADS * HEAD_PAD)
    attn_s = _sample_attention(page_table, q_s, ckv_new_pad, kpe_new_t, cache_ckv,
                               jnp.swapaxes(cache_kpe, 2, 3), p)

    conv_p = _prompt_conv(u_all, w_dw[l], b_dw[l].reshape(1, CONV_CH))
    u_s = u_all[n_p:].reshape(n_seq, t_new, CONV_CH)
    u_ext_s = jnp.concatenate([state_conv[l], u_s], axis=1)
    ext = CONV_W - 1 + t_new
    ext_pad = -ext % 8
    w_taps = jnp.stack([jnp.pad(w_dw[l], ((t, t_new - 1 - t + ext_pad), (0, 0))) for t in range(t_new)])
    conv_s = _sample_conv(jnp.pad(u_ext_s, ((0, 0), (0, ext_pad), (0, 0))), w_taps, b_dw[l].reshape(1, CONV_CH))
    conv_s = jnp.transpose(conv_s, (1, 0, 2)).reshape(n_s, CONV_CH)

    y_all, f_all, top_i, top_w = _merge_router(x_p, x_s, attn_p, attn_s.reshape(n_s, ATTN_WIDTH), conv_p, conv_s,
                                               gt_m, sh_f, sc_f, p)

    n_tiles = -(-(n * TOP_K + N_EXPERTS * (FFN_TILE - 1)) // FFN_TILE)
    n_slots = n_tiles * FFN_TILE
    pos_tok, meta = _route(top_i, n_tiles)
    slot = jnp.transpose(pos_tok[:, :TOP_K])
    piece_idx = (slot[:, None, :] + (jnp.arange(SC_PIECES, dtype=I32) * n_slots)[None, :, None]).reshape(-1)
    x_sorted = _sc_scatter_rows(f_all.reshape(SC_PIECES * n, SC_ROW), piece_idx, SC_PIECES * n_slots)
    h_sorted = _expert_ffn(meta[:n_tiles, 0], meta[0:1, 1], x_sorted.reshape(SC_PIECES, n_slots, SC_ROW),
                           w_exp_up[l], b_exp_up[l], w_exp_down[l], b_exp_down[l])
    gathered = _sc_gather_rows(h_sorted.reshape(SC_PIECES * n_slots, SC_ROW), piece_idx)
    y_p, y_s = _combine(y_all, gathered.reshape(TOP_K, SC_PIECES, n, SC_ROW), top_w, gt_f, n_p)
    y_p = y_p.reshape(bsz, seq, d)
    y_s = y_s.reshape(n_seq, t_new, d)
    ckv_prompt = ckv_all[:n_p].reshape(1, bsz, seq, KV_LORA)
    kpe_prompt = kpe_all[:n_p].reshape(1, bsz, seq, QK_ROPE)
    conv_prompt = u_all[n_p - (CONV_W - 1):n_p].reshape(1, bsz, CONV_W - 1, CONV_CH)
    ckv_sample = ckv_s[None]
    kpe_sample = kpe_s[None]
    conv_sample = u_ext_s[:, t_new:][None]
    return (y_p, y_s, ckv_prompt, kpe_prompt, conv_prompt, ckv_sample, kpe_sample, conv_sample)
```

```python
import functools

import numpy as np
import jax
import jax.numpy as jnp
from jax import lax
from jax.experimental import pallas as pl
from jax.experimental.pallas import tpu as pltpu
from jax.experimental.pallas import tpu_sc as plsc

F32 = jnp.float32
BF16 = jnp.bfloat16
I32 = jnp.int32
HIGHEST = lax.Precision.HIGHEST

D_MODEL = 1024
N_HEADS = 8
QK_NOPE = 64
QK_ROPE = 32
V_DIM = 64
Q_LORA = 256
KV_LORA = 128
ATTN_WIDTH = N_HEADS * V_DIM
CONV_CH = D_MODEL - ATTN_WIDTH
CONV_W = 31
N_EXPERTS = 32
TOP_K = 4
D_FF = D_MODEL
SWIGLU_LIMIT = 7.0
SWIGLU_ALPHA = 1.702
EPS = 1e-6
NEG_INF = -1e30
ROPE_THETA = 10000.0
SM_SCALE = (QK_NOPE + QK_ROPE) ** -0.5
LOG2E = 1.4426950408889634
Q_SCALE = SM_SCALE * LOG2E

LANES = 128
HEAD_PAD = LANES
ROPE_LO = QK_NOPE
ROPE_HALF = QK_ROPE // 2

TOK_TILE = 256
FA_TQ = 1024
FA_TK = 1024
FA_HEADS = 4
PAGES_PER_STEP = 32
FFN_TILE = 512
ROUTE_TILE = 512
SC_ROW = 256
PACKED_WIDTH = D_MODEL // 2
SC_PIECES = PACKED_WIDTH // SC_ROW
SC_WINDOW = 128
VMEM_LIMIT = 56 * 1024 * 1024


def _cparams(sem, vmem=None):
    return pltpu.CompilerParams(dimension_semantics=sem, vmem_limit_bytes=vmem)


def _rsqrt_mean(x, n):
    return lax.rsqrt(jnp.sum(x * x, axis=-1, keepdims=True) * (1.0 / n) + EPS)


def _pack_rows(x):
    half = x.shape[1] // 2
    hi = lax.bitcast_convert_type(x[:, :half].astype(BF16).astype(F32), I32)
    lo = lax.bitcast_convert_type(x[:, half:].astype(BF16).astype(F32), I32)
    return hi | lax.shift_right_logical(lo, jnp.full(lo.shape, 16, I32))


def _unpack_rows(w):
    hi = lax.bitcast_convert_type(w & jnp.int32(-65536), F32)
    lo = lax.bitcast_convert_type(lax.shift_left(w, jnp.full(w.shape, 16, I32)), F32)
    return jnp.concatenate([hi, lo], axis=1)


def _ada_kernel(c_ref, w_ref, b_ref, o_ref):
    c = c_ref[...]
    s = c * jax.nn.sigmoid(c)
    o_ref[...] = jnp.dot(s, w_ref[...], precision=HIGHEST, preferred_element_type=F32) + b_ref[...]


def _adaln(c_all, w_ada, b_ada):
    rows = c_all.shape[0]
    n_out = w_ada.shape[1]
    return pl.pallas_call(
        _ada_kernel,
        grid=(n_out // D_MODEL,),
        in_specs=[pl.BlockSpec((rows, D_MODEL), lambda j: (0, 0)),
                  pl.BlockSpec((D_MODEL, D_MODEL), lambda j: (0, j)),
                  pl.BlockSpec((1, D_MODEL), lambda j: (0, j))],
        out_specs=pl.BlockSpec((rows, D_MODEL), lambda j: (0, j)),
        out_shape=jax.ShapeDtypeStruct((rows, n_out), F32),
        compiler_params=_cparams(("arbitrary",)),
        name="adaln",
    )(c_all, w_ada, b_ada.reshape(1, n_out))


def _group_norm_rope(x, m_grp, gain, cos_t, sin_t, first_half):
    ms = jnp.dot((x * x).astype(BF16), m_grp, preferred_element_type=F32)
    xn = x * lax.rsqrt(ms + EPS) * gain
    swapped = jnp.where(first_half, pltpu.roll(xn, LANES - ROPE_HALF, 1), pltpu.roll(xn, ROPE_HALF, 1))
    return xn * cos_t + swapped * sin_t


def _pick_rows(n_prompt_tiles, prompt_ref, decode_ref):
    return jnp.where(pl.program_id(0) < n_prompt_tiles, prompt_ref[...], decode_ref[...])


def _mix_kernel(n_prompt_tiles, xp_ref, xs_ref, sh_ref, sc_ref, gmix_ref, win_ref, gql_ref, wq_ref, gq_ref, m_ref,
                cos_ref, sin_ref, gkv_ref, gk_ref, gkpe_ref, wk_ref, wv_ref, vone_ref,
                q_out, ckv_out, kpe_out, u_out, k_out, v_out):
    x = _pick_rows(n_prompt_tiles, xp_ref, xs_ref)
    h = x * _rsqrt_mean(x, D_MODEL) * gmix_ref[...]
    h = h * (1.0 + sc_ref[...]) + sh_ref[...]
    proj = jnp.dot(h.astype(BF16), win_ref[...], preferred_element_type=F32)
    q_lat = proj[:, :Q_LORA]
    ckv_raw = proj[:, Q_LORA:Q_LORA + KV_LORA]
    kpe_blk = proj[:, Q_LORA + KV_LORA:Q_LORA + KV_LORA + LANES]
    glu_lo = Q_LORA + KV_LORA + LANES
    u_out[...] = proj[:, glu_lo:glu_lo + CONV_CH] * jax.nn.sigmoid(proj[:, glu_lo + CONV_CH:glu_lo + 2 * CONV_CH])

    m_grp = m_ref[...]
    cos_t = cos_ref[...]
    sin_t = sin_ref[...]
    lane = lax.broadcasted_iota(I32, (1, LANES), 1)
    first_half = lane < ROPE_LO + ROPE_HALF

    q_lat_n = q_lat * _rsqrt_mean(q_lat, Q_LORA) * gql_ref[...]
    q = jnp.dot(q_lat_n.astype(BF16), wq_ref[...], preferred_element_type=F32)
    gq = gq_ref[...]
    for hd in range(N_HEADS):
        qh = _group_norm_rope(q[:, hd * HEAD_PAD:(hd + 1) * HEAD_PAD], m_grp, gq, cos_t, sin_t, first_half)
        q_out[:, hd * HEAD_PAD:(hd + 1) * HEAD_PAD] = (qh * Q_SCALE).astype(BF16)

    ckv_n = ckv_raw * _rsqrt_mean(ckv_raw, KV_LORA) * gkv_ref[...]
    ckv_out[...] = ckv_n
    kpe_r = _group_norm_rope(kpe_blk, m_grp, gkpe_ref[...], cos_t, sin_t, first_half)
    kpe_out[...] = kpe_r[:, ROPE_LO:ROPE_LO + QK_ROPE]

    ckv_b = ckv_n.astype(BF16)
    kexp = jnp.dot(ckv_b, wk_ref[...], preferred_element_type=F32)
    gk = gk_ref[...]
    for hd in range(N_HEADS):
        kh = kexp[:, hd * HEAD_PAD:(hd + 1) * HEAD_PAD]
        ms = jnp.dot((kh * kh).astype(BF16), m_grp, preferred_element_type=F32)
        k_out[:, hd * HEAD_PAD:(hd + 1) * HEAD_PAD] = (kh * lax.rsqrt(ms + EPS) * gk + kpe_r).astype(BF16)
    v_out[...] = (jnp.dot(ckv_b, wv_ref[...], preferred_element_type=F32) + vone_ref[...]).astype(BF16)


def _mod_index(n_prompt_tiles):
    return lambda i: (jnp.where(i < n_prompt_tiles, 0, i - n_prompt_tiles + 1), 0)


def _prompt_index(n_prompt_tiles):
    return lambda i: (jnp.minimum(i, n_prompt_tiles - 1), 0)


def _decode_index(n_prompt_tiles):
    return lambda i: (jnp.maximum(i - n_prompt_tiles, 0), 0)


def _mixer_inputs(x_p, x_s, sh, sc, p):
    n_prompt_tiles = x_p.shape[0] // TOK_TILE
    n = x_p.shape[0] + x_s.shape[0]
    tm = TOK_TILE
    const = lambda i: (0, 0)
    row = lambda i: (i, 0)
    mod = _mod_index(n_prompt_tiles)
    hw = N_HEADS * HEAD_PAD
    in_cols = p["w_in"].shape[1]
    return pl.pallas_call(
        functools.partial(_mix_kernel, n_prompt_tiles),
        grid=(n // tm,),
        in_specs=[pl.BlockSpec((tm, D_MODEL), _prompt_index(n_prompt_tiles)),
                  pl.BlockSpec((tm, D_MODEL), _decode_index(n_prompt_tiles)),
                  pl.BlockSpec((tm, D_MODEL), mod),
                  pl.BlockSpec((tm, D_MODEL), mod),
                  pl.BlockSpec((1, D_MODEL), const),
                  pl.BlockSpec((D_MODEL, in_cols), const),
                  pl.BlockSpec((1, Q_LORA), const),
                  pl.BlockSpec((Q_LORA, hw), const),
                  pl.BlockSpec((1, LANES), const),
                  pl.BlockSpec((LANES, LANES), const),
                  pl.BlockSpec((tm, LANES), row),
                  pl.BlockSpec((tm, LANES), row),
                  pl.BlockSpec((1, KV_LORA), const),
                  pl.BlockSpec((1, LANES), const),
                  pl.BlockSpec((1, LANES), const),
                  pl.BlockSpec((KV_LORA, hw), const),
                  pl.BlockSpec((KV_LORA, hw), const),
                  pl.BlockSpec((1, hw), const)],
        out_specs=[pl.BlockSpec((tm, hw), row),
                   pl.BlockSpec((tm, KV_LORA), row),
                   pl.BlockSpec((tm, QK_ROPE), row),
                   pl.BlockSpec((tm, CONV_CH), row),
                   pl.BlockSpec((tm, hw), row),
                   pl.BlockSpec((tm, hw), row)],
        out_shape=[jax.ShapeDtypeStruct((n, hw), BF16),
                   jax.ShapeDtypeStruct((n, KV_LORA), F32),
                   jax.ShapeDtypeStruct((n, QK_ROPE), F32),
                   jax.ShapeDtypeStruct((n, CONV_CH), F32),
                   jax.ShapeDtypeStruct((n, hw), BF16),
                   jax.ShapeDtypeStruct((n, hw), BF16)],
        compiler_params=_cparams(("parallel",), VMEM_LIMIT),
        name="mixer_inputs",
    )(x_p, x_s, sh, sc, p["g_norm_mix"], p["w_in"], p["g_q_lat"], p["w_q_up"], p["gain_q"], p["m_grp"],
      p["cos_t"], p["sin_t"], p["g_kv_lat"], p["gain_k"], p["gain_kpe"], p["w_k"], p["w_v_tiles"], p["v_ones"])


def _fa_kernel(qt_ref, kt_ref, q_ref, k_ref, v_ref, o_ref, m_sc, acc_sc):
    t = pl.program_id(1)
    qi = qt_ref[t]
    ki = kt_ref[t]
    last_k = (qi + 1) * (FA_TQ // FA_TK) - 1

    @pl.when(ki == 0)
    def _():
        m_sc[...] = jnp.full(m_sc.shape, NEG_INF, F32)
        acc_sc[...] = jnp.zeros(acc_sc.shape, F32)

    def step(masked):
        if masked:
            col_minus_row = (lax.broadcasted_iota(I32, (FA_TQ, FA_TK), 1)
                             - lax.broadcasted_iota(I32, (FA_TQ, FA_TK), 0))
        for hh in range(FA_HEADS):
            q = q_ref[:, hh * HEAD_PAD:(hh + 1) * HEAD_PAD]
            k = k_ref[:, hh * HEAD_PAD:(hh + 1) * HEAD_PAD]
            v = v_ref[:, hh * HEAD_PAD:(hh + 1) * HEAD_PAD]
            s = lax.dot_general(q, k, (((1,), (1,)), ((), ())), preferred_element_type=F32)
            if masked:
                s = jnp.where(col_minus_row <= qi * FA_TQ - ki * FA_TK, s, NEG_INF)
            m_prev = m_sc[hh]
            m_new = jnp.maximum(m_prev, jnp.max(s, axis=-1, keepdims=True))
            alpha = jnp.exp2(m_prev - m_new)
            pr = jnp.exp2((s - jnp.concatenate([m_new] * (FA_TK // LANES), axis=1)).astype(BF16))
            acc_sc[hh] = alpha * acc_sc[hh] + jnp.dot(pr, v, preferred_element_type=F32)
            m_sc[hh] = m_new

    @pl.when(ki * FA_TK + FA_TK - 1 <= qi * FA_TQ)
    def _():
        step(False)

    @pl.when(ki * FA_TK + FA_TK - 1 > qi * FA_TQ)
    def _():
        step(True)

    @pl.when(ki == last_k)
    def _():
        for hh in range(FA_HEADS):
            acc = acc_sc[hh]
            o_ref[:, hh * V_DIM:(hh + 1) * V_DIM] = acc[:, :V_DIM] / acc[:, V_DIM:V_DIM + 1]


def _prompt_attention(q_all, k_all, v_all, seq):
    nq = seq // FA_TQ
    ratio = FA_TQ // FA_TK
    qt, kt = [], []
    for qi in range(nq):
        for ki in range((qi + 1) * ratio):
            qt.append(qi)
            kt.append(ki)
    qt = jnp.asarray(np.array(qt, np.int32))
    kt = jnp.asarray(np.array(kt, np.int32))
    n_pairs = int(qt.shape[0])
    grid_spec = pltpu.PrefetchScalarGridSpec(
        num_scalar_prefetch=2,
        grid=(N_HEADS // FA_HEADS, n_pairs),
        in_specs=[pl.BlockSpec((FA_TQ, FA_HEADS * HEAD_PAD), lambda hp, t, qt, kt: (qt[t], hp)),
                  pl.BlockSpec((FA_TK, FA_HEADS * HEAD_PAD), lambda hp, t, qt, kt: (kt[t], hp)),
                  pl.BlockSpec((FA_TK, FA_HEADS * HEAD_PAD), lambda hp, t, qt, kt: (kt[t], hp))],
        out_specs=pl.BlockSpec((FA_TQ, FA_HEADS * V_DIM), lambda hp, t, qt, kt: (qt[t], hp)),
        scratch_shapes=[pltpu.VMEM((FA_HEADS, FA_TQ, LANES), F32),
                        pltpu.VMEM((FA_HEADS, FA_TQ, HEAD_PAD), F32)],
    )
    return pl.pallas_call(
        _fa_kernel,
        grid_spec=grid_spec,
        out_shape=jax.ShapeDtypeStruct((seq, ATTN_WIDTH), F32),
        compiler_params=_cparams(("parallel", "arbitrary"), VMEM_LIMIT),
        name="prompt_attention",
    )(qt, kt, q_all, k_all, v_all)


def _sattn_kernel(pt_ref, q_ref, wabs_ref, wkt_ref, wv_ref, ckvn_ref, kpen_ref, ckv_hbm, kpe_hbm,
                  o_ref, m_sc, l_sc, acc_sc, qa_sc, wall_sc, ckv_buf, kpe_buf, sem):
    kb = pl.program_id(1)
    n_kb = pl.num_programs(1)
    step = pl.program_id(0) * n_kb + kb
    n_steps = pl.num_programs(0) * n_kb
    slot = step % 2
    rows = 4 * N_HEADS

    def start_fetch(step_i, slot_i):
        for j in range(PAGES_PER_STEP):
            page = pt_ref[step_i * PAGES_PER_STEP + j]
            pltpu.make_async_copy(ckv_hbm.at[0, page], ckv_buf.at[slot_i, j], sem.at[0, slot_i]).start()
            pltpu.make_async_copy(kpe_hbm.at[0, page], kpe_buf.at[slot_i, j], sem.at[1, slot_i]).start()

    @pl.when(step == 0)
    def _():
        start_fetch(step, slot)

    @pl.when(step + 1 < n_steps)
    def _():
        start_fetch(step + 1, 1 - slot)

    @pl.when(kb == 0)
    def _():
        m_sc[...] = jnp.full(m_sc.shape, NEG_INF, F32)
        l_sc[...] = jnp.zeros(l_sc.shape, F32)
        acc_sc[...] = jnp.zeros(acc_sc.shape, F32)
        q4 = q_ref[...].astype(F32)
        head_of_lane = lax.broadcasted_iota(I32, (N_HEADS, N_HEADS * HEAD_PAD), 1) // HEAD_PAD
        head_of_row = lax.broadcasted_iota(I32, (N_HEADS, N_HEADS * HEAD_PAD), 0)
        own = head_of_lane == head_of_row
        qbd = jnp.concatenate(
            [jnp.where(own, jnp.broadcast_to(q4[qq:qq + 1, :], own.shape), 0.0) for qq in range(4)], axis=0)
        qa = jnp.dot(qbd.astype(BF16), wabs_ref[...], preferred_element_type=F32).astype(BF16)
        qa_sc[...] = qa
        wall_sc[pl.ds(0, N_HEADS * QK_NOPE), :] = wkt_ref[...]
        wall_sc[pl.ds(N_HEADS * QK_NOPE, rows), :] = qa[:, :KV_LORA]

    def attend(state, ckv_b, kpe_t, mask):
        m_prev, l_prev, acc_prev = state
        nt = (((1,), (1,)), ((), ()))
        kn_all = lax.dot_general(wall_sc[...], ckv_b, nt, preferred_element_type=F32)
        keys = kn_all.shape[1]
        kn_t = kn_all[:N_HEADS * QK_NOPE]
        ss = jnp.sum((kn_t * kn_t).reshape(N_HEADS, QK_NOPE, keys), axis=1)
        r8 = lax.rsqrt(ss * (1.0 / QK_NOPE) + EPS)
        rope = jnp.dot(qa_sc[:, KV_LORA:KV_LORA + QK_ROPE], kpe_t, preferred_element_type=F32)
        s = kn_all[N_HEADS * QK_NOPE:] * jnp.concatenate([r8] * 4, axis=0) + rope
        if mask is not None:
            s = jnp.where(mask, s, NEG_INF)
        m_new = jnp.maximum(m_prev, jnp.max(s, axis=-1, keepdims=True))
        alpha = jnp.exp2(m_prev - m_new)
        pr = jnp.exp2(s - m_new)
        l_new = alpha * l_prev + jnp.sum(pr, axis=-1, keepdims=True)
        acc_new = alpha * acc_prev + jnp.dot(pr.astype(BF16), ckv_b, preferred_element_type=F32)
        return m_new, l_new, acc_new

    pltpu.make_async_copy(ckv_hbm.at[0, pl.ds(0, PAGES_PER_STEP)], ckv_buf.at[slot], sem.at[0, slot]).wait()
    pltpu.make_async_copy(kpe_hbm.at[0, pl.ds(0, PAGES_PER_STEP)], kpe_buf.at[slot], sem.at[1, slot]).wait()

    page = ckv_buf.shape[2]
    ckv_b = ckv_buf[slot].reshape(PAGES_PER_STEP * page, KV_LORA).astype(BF16)
    kpe_t = jnp.concatenate([kpe_buf[slot, j].astype(BF16) for j in range(PAGES_PER_STEP)], axis=1)
    state = attend((m_sc[...], l_sc[...], acc_sc[...]), ckv_b, kpe_t, None)
    m_sc[...], l_sc[...], acc_sc[...] = state

    @pl.when(kb == n_kb - 1)
    def _():
        key = lax.broadcasted_iota(I32, (rows, LANES), 1)
        qry = lax.broadcasted_iota(I32, (rows, LANES), 0) // N_HEADS
        _, l_fin, acc_fin = attend(state, ckvn_ref[...].astype(BF16), kpen_ref[...].astype(BF16), key <= qry)
        lat = acc_fin / l_fin
        o_all = jnp.dot(lat.astype(BF16), wv_ref[...], preferred_element_type=F32)
        head_of_col = lax.broadcasted_iota(I32, (rows, ATTN_WIDTH), 1) // V_DIM
        head_of_row = lax.broadcasted_iota(I32, (rows, ATTN_WIDTH), 0) % N_HEADS
        o_own = jnp.where(head_of_col == head_of_row, o_all, 0.0)
        o_ref[...] = jnp.sum(o_own.reshape(4, N_HEADS, ATTN_WIDTH), axis=1)


def _sample_attention(page_table, q_s, ckv_new_pad, kpe_new_t, cache_ckv, cache_kpe_t, p):
    n_seq, n_pages = page_table.shape
    page = cache_ckv.shape[2]
    n_kb = n_pages // PAGES_PER_STEP
    hw = N_HEADS * HEAD_PAD
    assert n_pages % PAGES_PER_STEP == 0
    per_seq3 = lambda b, kb, pt: (b, 0, 0)
    const = lambda b, kb, pt: (0, 0)
    in_specs = [pl.BlockSpec((None, 4, hw), per_seq3),
                pl.BlockSpec((hw, 2 * LANES), const),
                pl.BlockSpec((N_HEADS * QK_NOPE, KV_LORA), const),
                pl.BlockSpec((KV_LORA, ATTN_WIDTH), const),
                pl.BlockSpec((None, LANES, KV_LORA), per_seq3),
                pl.BlockSpec((None, QK_ROPE, LANES), per_seq3),
                pl.BlockSpec(memory_space=pl.ANY),
                pl.BlockSpec(memory_space=pl.ANY)]
    grid_spec = pltpu.PrefetchScalarGridSpec(
        num_scalar_prefetch=1,
        grid=(n_seq, n_kb),
        in_specs=in_specs,
        out_specs=pl.BlockSpec((None, 4, ATTN_WIDTH), per_seq3),
        scratch_shapes=[pltpu.VMEM((4 * N_HEADS, 1), F32),
                        pltpu.VMEM((4 * N_HEADS, 1), F32),
                        pltpu.VMEM((4 * N_HEADS, KV_LORA), F32),
                        pltpu.VMEM((4 * N_HEADS, 2 * LANES), BF16),
                        pltpu.VMEM((N_HEADS * QK_NOPE + 4 * N_HEADS, KV_LORA), BF16),
                        pltpu.VMEM((2, PAGES_PER_STEP, page, KV_LORA), F32),
                        pltpu.VMEM((2, PAGES_PER_STEP, QK_ROPE, page), F32),
                        pltpu.SemaphoreType.DMA((2, 2))],
    )
    return pl.pallas_call(
        _sattn_kernel,
        grid_spec=grid_spec,
        out_shape=jax.ShapeDtypeStruct((n_seq, 4, ATTN_WIDTH), F32),
        compiler_params=_cparams(("arbitrary", "arbitrary"), VMEM_LIMIT),
        name="decode_attention",
    )(page_table.reshape(-1), q_s, p["w_abs"], p["w_kt"], p["w_v"], ckv_new_pad, kpe_new_t, cache_ckv, cache_kpe_t)


CONV_HALO = 32
CONV_ROWS = 64


def _conv_kernel(halo_ref, u_ref, w_ref, b_ref, o_ref, ext_sc):
    i = pl.program_id(0)
    tm = u_ref.shape[0]
    ext_sc[pl.ds(0, CONV_HALO), :] = jnp.where(i == 0, 0.0, halo_ref[...])
    ext_sc[pl.ds(CONV_HALO, tm), :] = u_ref[...]
    first = CONV_HALO - (CONV_W - 1)
    for rc in range(tm // CONV_ROWS):
        acc = jnp.broadcast_to(b_ref[...], (CONV_ROWS, CONV_CH))
        for j in range(CONV_W):
            acc = acc + ext_sc[pl.ds(rc * CONV_ROWS + first + j, CONV_ROWS), :] * w_ref[j:j + 1, :]
        o_ref[pl.ds(rc * CONV_ROWS, CONV_ROWS), :] = acc


def _prompt_conv(u_all, w_dw, b_dw):
    n = u_all.shape[0]
    tm = TOK_TILE
    per = tm // CONV_HALO
    return pl.pallas_call(
        _conv_kernel,
        grid=(n // tm,),
        in_specs=[pl.BlockSpec((CONV_HALO, CONV_CH), lambda i: (jnp.maximum(i * per - 1, 0), 0)),
                  pl.BlockSpec((tm, CONV_CH), lambda i: (i, 0)),
                  pl.BlockSpec((CONV_W, CONV_CH), lambda i: (0, 0)),
                  pl.BlockSpec((1, CONV_CH), lambda i: (0, 0))],
        out_specs=pl.BlockSpec((tm, CONV_CH), lambda i: (i, 0)),
        out_shape=jax.ShapeDtypeStruct((n, CONV_CH), F32),
        scratch_shapes=[pltpu.VMEM((CONV_HALO + tm, CONV_CH), F32)],
        compiler_params=_cparams(("parallel",)),
        name="prompt_conv",
    )(u_all, u_all, w_dw, b_dw)


def _sconv_kernel(u_ref, wt_ref, b_ref, o_ref):
    u = u_ref[...]
    for t in range(o_ref.shape[0]):
        o_ref[t] = jnp.sum(u * wt_ref[t][None, :, :], axis=1) + b_ref[...]


def _sample_conv(u_ext, w_taps, b_dw):
    n_seq, ext, _ = u_ext.shape
    t_new = w_taps.shape[0]
    sb = 8
    return pl.pallas_call(
        _sconv_kernel,
        grid=(n_seq // sb,),
        in_specs=[pl.BlockSpec((sb, ext, CONV_CH), lambda i: (i, 0, 0)),
                  pl.BlockSpec((t_new, ext, CONV_CH), lambda i: (0, 0, 0)),
                  pl.BlockSpec((1, CONV_CH), lambda i: (0, 0))],
        out_specs=pl.BlockSpec((t_new, sb, CONV_CH), lambda i: (0, i, 0)),
        out_shape=jax.ShapeDtypeStruct((t_new, n_seq, CONV_CH), F32),
        compiler_params=_cparams(("parallel",)),
        name="decode_conv",
    )(u_ext, w_taps, b_dw)


def _lane_pack(cols, dtype):
    lane = lax.broadcasted_iota(I32, (cols[0].shape[0], LANES), 1)
    out = jnp.zeros((cols[0].shape[0], LANES), dtype)
    for j, c in enumerate(cols):
        out = jnp.where(lane == j, c.astype(dtype), out)
    return out


def _merge_kernel(n_prompt_tiles, xp_ref, xs_ref, ap_ref, as_ref, cp_ref, cs_ref, gate_ref, shf_ref, scf_ref,
                  gln_ref, bln_ref, ga_ref, gc_ref, wout_ref, gffn_ref, wrh_ref, wrl_ref, br_ref,
                  y_out, f_out, ti_out, tw_out):
    yc = _pick_rows(n_prompt_tiles, cp_ref, cs_ref)
    mu = jnp.mean(yc, axis=-1, keepdims=True)
    xc = yc - mu
    var = jnp.mean(xc * xc, axis=-1, keepdims=True)
    ln = xc * lax.rsqrt(var + EPS) * gln_ref[...] + bln_ref[...]
    conv = ln * jax.nn.sigmoid(ln)
    attn = _pick_rows(n_prompt_tiles, ap_ref, as_ref)
    a_n = attn * _rsqrt_mean(attn, ATTN_WIDTH) * ga_ref[...]
    c_n = conv * _rsqrt_mean(conv, CONV_CH) * gc_ref[...]
    m = (jnp.dot(a_n.astype(BF16), wout_ref[:ATTN_WIDTH, :], preferred_element_type=F32)
         + jnp.dot(c_n.astype(BF16), wout_ref[ATTN_WIDTH:, :], preferred_element_type=F32))
    y = _pick_rows(n_prompt_tiles, xp_ref, xs_ref) + gate_ref[...] * m
    y_out[...] = y
    f = y * _rsqrt_mean(y, D_MODEL) * gffn_ref[...]
    f = f * (1.0 + scf_ref[...]) + shf_ref[...]
    f_words = _pack_rows(f)
    for j in range(SC_PIECES):
        f_out[j] = f_words[:, j * SC_ROW:(j + 1) * SC_ROW]
    f_hi = f.astype(BF16)
    f_lo = (f - f_hi.astype(F32)).astype(BF16)
    logits = (jnp.dot(f_hi, wrh_ref[...], preferred_element_type=F32)
              + jnp.dot(f_hi, wrl_ref[...], preferred_element_type=F32)
              + jnp.dot(f_lo, wrh_ref[...], preferred_element_type=F32)) + br_ref[...]
    lane = lax.broadcasted_iota(I32, logits.shape, 1)
    vals, idxs = [], []
    for _ in range(TOP_K):
        mx = jnp.max(logits, axis=-1, keepdims=True)
        ix = jnp.min(jnp.where(logits == mx, lane, LANES), axis=-1, keepdims=True)
        vals.append(mx)
        idxs.append(ix)
        logits = jnp.where(lane == ix, NEG_INF * 4.0, logits)
    exps = [jnp.exp(v - vals[0]) for v in vals]
    tot = exps[0] + exps[1] + exps[2] + exps[3]
    ti_out[...] = _lane_pack(idxs, I32)
    tw_out[...] = _lane_pack([e / tot for e in exps], F32)


def _merge_router(x_p, x_s, attn_p, attn_s, conv_p, conv_s, gate, shf, scf, p):
    n_prompt_tiles = x_p.shape[0] // TOK_TILE
    n = x_p.shape[0] + x_s.shape[0]
    tm = TOK_TILE
    const = lambda i: (0, 0)
    row = lambda i: (i, 0)
    mod = _mod_index(n_prompt_tiles)
    from_p = _prompt_index(n_prompt_tiles)
    from_s = _decode_index(n_prompt_tiles)
    return pl.pallas_call(
        functools.partial(_merge_kernel, n_prompt_tiles),
        grid=(n // tm,),
        in_specs=[pl.BlockSpec((tm, D_MODEL), from_p),
                  pl.BlockSpec((tm, D_MODEL), from_s),
                  pl.BlockSpec((tm, ATTN_WIDTH), from_p),
                  pl.BlockSpec((tm, ATTN_WIDTH), from_s),
                  pl.BlockSpec((tm, CONV_CH), from_p),
                  pl.BlockSpec((tm, CONV_CH), from_s),
                  pl.BlockSpec((tm, D_MODEL), mod),
                  pl.BlockSpec((tm, D_MODEL), mod),
                  pl.BlockSpec((tm, D_MODEL), mod),
                  pl.BlockSpec((1, CONV_CH), const),
                  pl.BlockSpec((1, CONV_CH), const),
                  pl.BlockSpec((1, ATTN_WIDTH), const),
                  pl.BlockSpec((1, CONV_CH), const),
                  pl.BlockSpec((D_MODEL, D_MODEL), const),
                  pl.BlockSpec((1, D_MODEL), const),
                  pl.BlockSpec((D_MODEL, LANES), const),
                  pl.BlockSpec((D_MODEL, LANES), const),
                  pl.BlockSpec((1, LANES), const)],
        out_specs=[pl.BlockSpec((tm, D_MODEL), row),
                   pl.BlockSpec((SC_PIECES, tm, SC_ROW), lambda i: (0, i, 0)),
                   pl.BlockSpec((tm, LANES), row),
                   pl.BlockSpec((tm, LANES), row)],
        out_shape=[jax.ShapeDtypeStruct((n, D_MODEL), F32),
                   jax.ShapeDtypeStruct((SC_PIECES, n, SC_ROW), I32),
                   jax.ShapeDtypeStruct((n, LANES), I32),
                   jax.ShapeDtypeStruct((n, LANES), F32)],
        compiler_params=_cparams(("parallel",), VMEM_LIMIT),
        name="merge_router",
    )(x_p, x_s, attn_p, attn_s, conv_p, conv_s, gate, shf, scf, p["g_conv_ln"], p["b_conv_ln"], p["g_out_attn"],
      p["g_out_conv"], p["w_out"], p["g_norm_ffn"], p["w_router_hi"], p["w_router_lo"], p["b_router"])


def _select_lane(table, idx_col, lane):
    return jnp.sum(jnp.where(lane == idx_col, table, 0.0), axis=-1, keepdims=True)


def _rank_kernel(ti_ref, rk_out, cnt_out, carry_sc):
    i = pl.program_id(0)

    @pl.when(i == 0)
    def _():
        carry_sc[...] = jnp.zeros(carry_sc.shape, F32)

    ti = ti_ref[...]
    tn = ti.shape[0]
    lane = lax.broadcasted_iota(I32, (tn, LANES), 1)
    sel = jnp.zeros((tn, LANES), F32)
    for k in range(TOP_K):
        sel = sel + (lane == ti[:, k:k + 1]).astype(F32)
    r_i = lax.broadcasted_iota(I32, (tn, tn), 0)
    c_i = lax.broadcasted_iota(I32, (tn, tn), 1)
    below = (c_i < r_i).astype(BF16)
    rank = carry_sc[...] + jnp.dot(below, sel.astype(BF16), preferred_element_type=F32)
    rk_out[...] = _lane_pack([_select_lane(rank, ti[:, k:k + 1], lane) for k in range(TOP_K)], F32)
    carry_sc[...] = carry_sc[...] + jnp.sum(sel, axis=0, keepdims=True)
    cnt_out[...] = jnp.broadcast_to(carry_sc[...], cnt_out.shape)


def _pos_kernel(cnt_ref, ti_ref, rk_ref, pos_out, meta_out):
    cnt = cnt_ref[...]
    padded = jnp.ceil(cnt * (1.0 / FFN_TILE)) * FFN_TILE
    r_i = lax.broadcasted_iota(I32, (LANES, LANES), 0)
    c_i = lax.broadcasted_iota(I32, (LANES, LANES), 1)
    before = (r_i < c_i).astype(F32)
    offs = jnp.dot(padded, before, precision=HIGHEST, preferred_element_type=F32)
    ends = offs + padded
    ti = ti_ref[...]
    tn = ti.shape[0]
    lane = lax.broadcasted_iota(I32, (tn, LANES), 1)
    off_row = offs[0:1, :]
    rk = rk_ref[...]
    pos = [_select_lane(jnp.broadcast_to(off_row, (tn, LANES)), ti[:, k:k + 1], lane) + rk[:, k:k + 1]
           for k in range(TOP_K)]
    pos_out[...] = _lane_pack(pos, F32).astype(I32)

    @pl.when(pl.program_id(0) == 0)
    def _():
        nt = meta_out.shape[0]
        start = (lax.broadcasted_iota(I32, (nt, LANES), 0) * FFN_TILE).astype(F32)
        elane = lax.broadcasted_iota(I32, (nt, LANES), 1)
        done = jnp.where((elane < N_EXPERTS) & (jnp.broadcast_to(ends[0:1, :], (nt, LANES)) <= start), 1.0, 0.0)
        expert = jnp.minimum(jnp.sum(done, axis=-1, keepdims=True), N_EXPERTS - 1.0)
        total = jnp.sum(jnp.where(elane < N_EXPERTS, jnp.broadcast_to(padded[0:1, :], (nt, LANES)), 0.0),
                        axis=-1, keepdims=True)
        meta_out[...] = _lane_pack([expert, total * (1.0 / FFN_TILE)], F32).astype(I32)


def _route(top_i, n_tiles_max):
    n = top_i.shape[0]
    tn = ROUTE_TILE
    rk, cnt = pl.pallas_call(
        _rank_kernel,
        grid=(n // tn,),
        in_specs=[pl.BlockSpec((tn, LANES), lambda i: (i, 0))],
        out_specs=[pl.BlockSpec((tn, LANES), lambda i: (i, 0)),
                   pl.BlockSpec((8, LANES), lambda i: (0, 0))],
        out_shape=[jax.ShapeDtypeStruct((n, LANES), F32),
                   jax.ShapeDtypeStruct((8, LANES), F32)],
        scratch_shapes=[pltpu.VMEM((1, LANES), F32)],
        compiler_params=_cparams(("arbitrary",)),
        name="route_rank",
    )(top_i)
    nt_pad = -(-n_tiles_max // 8) * 8
    pos, meta = pl.pallas_call(
        _pos_kernel,
        grid=(n // tn,),
        in_specs=[pl.BlockSpec((8, LANES), lambda i: (0, 0)),
                  pl.BlockSpec((tn, LANES), lambda i: (i, 0)),
                  pl.BlockSpec((tn, LANES), lambda i: (i, 0))],
        out_specs=[pl.BlockSpec((tn, LANES), lambda i: (i, 0)),
                   pl.BlockSpec((nt_pad, LANES), lambda i: (0, 0))],
        out_shape=[jax.ShapeDtypeStruct((n, LANES), I32),
                   jax.ShapeDtypeStruct((nt_pad, LANES), I32)],
        compiler_params=_cparams(("arbitrary",)),
        name="route_pos",
    )(cnt, top_i, rk)
    return pos, meta


def _sc_scatter_rows(x, idx, n_out):
    n_src = x.shape[0]
    n = idx.shape[0]
    n_src_blk = n_src // SC_WINDOW
    mesh = plsc.VectorSubcoreMesh(core_axis_name="c", subcore_axis_name="s")

    @pl.kernel(out_type=jax.ShapeDtypeStruct((n_out, SC_ROW), x.dtype), mesh=mesh)
    def k(x_hbm, i_hbm, o_hbm):
        def body(x_vmem, i_vmem):
            pltpu.sync_copy(x_vmem, o_hbm.at[i_vmem.at[0]])

        pltpu.emit_pipeline(
            body,
            grid=(n // SC_WINDOW,),
            in_specs=[pl.BlockSpec((SC_WINDOW, SC_ROW), index_map=lambda i: (i % n_src_blk, 0)),
                      pl.BlockSpec((1, SC_WINDOW), index_map=lambda i: (0, i))],
            out_specs=[],
            core_axis_name=("c", "s"),
            dimension_semantics=(pltpu.PARALLEL,),
        )(x_hbm, i_hbm)

    return k(x, idx.reshape(1, n))


def _sc_gather_rows(x, idx):
    n = idx.shape[0]
    mesh = plsc.VectorSubcoreMesh(core_axis_name="c", subcore_axis_name="s")

    @pl.kernel(out_type=jax.ShapeDtypeStruct((n, SC_ROW), x.dtype), mesh=mesh)
    def k(x_hbm, i_hbm, o_hbm):
        def body(i_vmem, o_vmem):
            pltpu.sync_copy(x_hbm.at[i_vmem.at[0]], o_vmem)

        pltpu.emit_pipeline(
            body,
            grid=(n // SC_WINDOW,),
            in_specs=[pl.BlockSpec((1, SC_WINDOW), index_map=lambda i: (0, i))],
            out_specs=[pl.BlockSpec((SC_WINDOW, SC_ROW), index_map=lambda i: (i, 0))],
            core_axis_name=("c", "s"),
            dimension_semantics=(pltpu.PARALLEL,),
        )(i_hbm, o_hbm)

    return k(x, idx.reshape(1, n))


def _ffn_kernel(te_ref, nv_ref, x_ref, wu_ref, bu_ref, wd_ref, bd_ref, o_ref, wu_sc, wd_sc):
    t = pl.program_id(0)
    valid = t < nv_ref[0]
    prev = te_ref[jnp.maximum(t - 1, 0)]
    fresh = jnp.logical_or(t == 0, te_ref[t] != prev)

    @pl.when(jnp.logical_and(valid, fresh))
    def _():
        wu_sc[...] = wu_ref[...].astype(BF16)
        wd_sc[...] = wd_ref[...].astype(BF16)

    @pl.when(valid)
    def _():
        x = _unpack_rows(jnp.concatenate([x_ref[j] for j in range(SC_PIECES)], axis=1))
        z = jnp.dot(x.astype(BF16), wu_sc[...], preferred_element_type=F32) + bu_ref[...]
        zg = jnp.minimum(z[:, :D_FF], SWIGLU_LIMIT)
        zl = jnp.clip(z[:, D_FF:], -SWIGLU_LIMIT, SWIGLU_LIMIT)
        act = zg * jax.nn.sigmoid(SWIGLU_ALPHA * zg) * (zl + 1.0)
        out = _pack_rows(jnp.dot(act.astype(BF16), wd_sc[...], preferred_element_type=F32) + bd_ref[...])
        for j in range(SC_PIECES):
            o_ref[j] = out[:, j * SC_ROW:(j + 1) * SC_ROW]


def _expert_ffn(tile_expert, n_valid, x_sorted, w_up, b_up, w_down, b_down):
    n_slots = x_sorted.shape[1]
    n_tiles = n_slots // FFN_TILE
    xmap = lambda t, te, nv: (0, jnp.minimum(t, nv[0] - 1), 0)
    emap = lambda t, te, nv: (te[t], 0, 0)
    grid_spec = pltpu.PrefetchScalarGridSpec(
        num_scalar_prefetch=2,
        grid=(n_tiles,),
        in_specs=[pl.BlockSpec((SC_PIECES, FFN_TILE, SC_ROW), xmap),
                  pl.BlockSpec((None, D_MODEL, 2 * D_FF), emap),
                  pl.BlockSpec((None, 1, 2 * D_FF), emap),
                  pl.BlockSpec((None, D_FF, D_MODEL), emap),
                  pl.BlockSpec((None, 1, D_MODEL), emap)],
        out_specs=pl.BlockSpec((SC_PIECES, FFN_TILE, SC_ROW), xmap),
        scratch_shapes=[pltpu.VMEM((D_MODEL, 2 * D_FF), BF16),
                        pltpu.VMEM((D_FF, D_MODEL), BF16)],
    )
    return pl.pallas_call(
        _ffn_kernel,
        grid_spec=grid_spec,
        out_shape=jax.ShapeDtypeStruct((SC_PIECES, n_slots, SC_ROW), I32),
        compiler_params=_cparams(("arbitrary",), VMEM_LIMIT),
        name="expert_ffn",
    )(tile_expert, n_valid, x_sorted, w_up, b_up.reshape(N_EXPERTS, 1, 2 * D_FF), w_down,
      b_down.reshape(N_EXPERTS, 1, D_MODEL))


def _combine_kernel(n_prompt_tiles, y_ref, g_ref, tw_ref, gate_ref, op_ref, os_ref):
    tw = tw_ref[...]
    rows = lambda k: _unpack_rows(jnp.concatenate([g_ref[k, j] for j in range(SC_PIECES)], axis=1))
    moe = rows(0) * tw[:, 0:1]
    for k in range(1, TOP_K):
        moe = moe + rows(k) * tw[:, k:k + 1]
    out = y_ref[...] + gate_ref[...] * moe
    i = pl.program_id(0)

    @pl.when(i < n_prompt_tiles)
    def _():
        op_ref[...] = out

    @pl.when(i >= n_prompt_tiles)
    def _():
        os_ref[...] = out


def _combine(y_all, gathered, top_w, gate, n_p):
    n = y_all.shape[0]
    tm = TOK_TILE
    n_prompt_tiles = n_p // tm
    row = lambda i: (i, 0)
    return pl.pallas_call(
        functools.partial(_combine_kernel, n_prompt_tiles),
        grid=(n // tm,),
        in_specs=[pl.BlockSpec((tm, D_MODEL), row),
                  pl.BlockSpec((TOP_K, SC_PIECES, tm, SC_ROW), lambda i: (0, 0, i, 0)),
                  pl.BlockSpec((tm, LANES), row),
                  pl.BlockSpec((tm, D_MODEL), _mod_index(n_prompt_tiles))],
        out_specs=[pl.BlockSpec((tm, D_MODEL), _prompt_index(n_prompt_tiles)),
                   pl.BlockSpec((tm, D_MODEL), _decode_index(n_prompt_tiles))],
        out_shape=[jax.ShapeDtypeStruct((n_p, D_MODEL), F32),
                   jax.ShapeDtypeStruct((n - n_p, D_MODEL), F32)],
        compiler_params=_cparams(("arbitrary",), VMEM_LIMIT),
        name="moe_combine",
    )(y_all, gathered, top_w, gate)


def _head_tiles(nope, rope):
    pad = jnp.zeros(nope.shape[:-1] + (HEAD_PAD - QK_NOPE - QK_ROPE,), nope.dtype)
    t = jnp.concatenate([nope, rope, pad], axis=-1)
    return t.reshape(t.shape[:-2] + (N_HEADS * HEAD_PAD,))


def _prepare(pos, w_in, g_norm_mix, g_q_lat, w_q_up, g_q_nope, g_q_rope, g_kv_lat, g_k_rope, w_kv_up, g_k_nope,
             g_conv_ln, b_conv_ln, g_out_attn, g_out_conv, w_out, g_norm_ffn, w_router, b_router):
    z32 = jnp.zeros((HEAD_PAD - QK_NOPE - QK_ROPE,), F32)
    z64 = jnp.zeros((QK_NOPE,), F32)
    d = D_MODEL
    kpe_cols = w_in[:, Q_LORA + KV_LORA:Q_LORA + KV_LORA + QK_ROPE]
    kpe_tile = jnp.concatenate([jnp.zeros((d, QK_NOPE), F32), kpe_cols, jnp.zeros((d, z32.shape[0]), F32)], axis=1)
    w_in_r = jnp.concatenate([w_in[:, :Q_LORA + KV_LORA], kpe_tile, w_in[:, Q_LORA + KV_LORA + QK_ROPE:]], axis=1)
    wq = w_q_up.reshape(Q_LORA, N_HEADS, QK_NOPE + QK_ROPE)
    w_q_r = _head_tiles(wq[..., :QK_NOPE], wq[..., QK_NOPE:])
    wk = w_kv_up[..., :QK_NOPE]
    wv = w_kv_up[..., QK_NOPE:].reshape(KV_LORA, ATTN_WIDTH)
    w_k_r = _head_tiles(wk, jnp.zeros((KV_LORA, N_HEADS, QK_ROPE), F32))
    lane = np.arange(LANES)
    grp = np.where(lane < QK_NOPE, 0, np.where(lane < QK_NOPE + QK_ROPE, 1, 2))
    m_grp = ((grp[:, None] == grp[None, :]) & (grp[:, None] < 2)).astype(np.float32)
    m_grp = m_grp / np.where(grp < 1, QK_NOPE, QK_ROPE)[None, :]
    inv = ROPE_THETA ** (-jnp.arange(0, QK_ROPE, 2, dtype=F32) / QK_ROPE)
    ang = pos.astype(F32)[:, None] * inv[None, :]
    cs, sn = jnp.cos(ang), jnp.sin(ang)
    n = pos.shape[0]
    cos_t = jnp.concatenate([jnp.ones((n, QK_NOPE), F32), cs, cs, jnp.zeros((n, z32.shape[0]), F32)], axis=1)
    sin_t = jnp.concatenate([jnp.zeros((n, QK_NOPE), F32), -sn, sn, jnp.zeros((n, z32.shape[0]), F32)], axis=1)
    wk_g = wk * g_k_nope[None, None, :]
    absorb = jnp.zeros((N_HEADS, HEAD_PAD, 2 * LANES), F32)
    absorb = absorb.at[:, :QK_NOPE, :KV_LORA].set(jnp.transpose(wk_g, (1, 2, 0)))
    absorb = absorb.at[:, ROPE_LO:ROPE_LO + QK_ROPE, KV_LORA:KV_LORA + QK_ROPE].set(
        jnp.broadcast_to(jnp.eye(QK_ROPE, dtype=F32), (N_HEADS, QK_ROPE, QK_ROPE)))
    w_kt = jnp.transpose(wk, (1, 2, 0)).reshape(N_HEADS * QK_NOPE, KV_LORA)
    wr = jnp.concatenate([w_router, jnp.zeros((d, LANES - N_EXPERTS), F32)], axis=1)
    wr_hi = wr.astype(BF16)
    wr_lo = (wr - wr_hi.astype(F32)).astype(BF16)
    br = jnp.concatenate([b_router, jnp.full((LANES - N_EXPERTS,), NEG_INF, F32)])
    wv_h = w_kv_up[..., QK_NOPE:]
    w_v_tiles = jnp.concatenate([wv_h, jnp.zeros((KV_LORA, N_HEADS, HEAD_PAD - V_DIM), F32)], axis=-1)
    v_ones = (jnp.arange(N_HEADS * HEAD_PAD) % HEAD_PAD == V_DIM).astype(F32).reshape(1, N_HEADS * HEAD_PAD)
    return {
        "w_v_tiles": w_v_tiles.reshape(KV_LORA, N_HEADS * HEAD_PAD).astype(BF16), "v_ones": v_ones,
        "w_router_hi": wr_hi, "w_router_lo": wr_lo,
        "w_in": w_in_r.astype(BF16), "g_norm_mix": g_norm_mix.reshape(1, d), "g_q_lat": g_q_lat.reshape(1, Q_LORA),
        "w_q_up": w_q_r.astype(BF16),
        "gain_q": jnp.concatenate([g_q_nope, g_q_rope, z32]).reshape(1, LANES),
        "m_grp": jnp.asarray(m_grp, BF16), "cos_t": cos_t, "sin_t": sin_t,
        "g_kv_lat": g_kv_lat.reshape(1, KV_LORA),
        "gain_k": jnp.concatenate([g_k_nope, z64]).reshape(1, LANES),
        "gain_kpe": jnp.concatenate([z64, g_k_rope, z32]).reshape(1, LANES),
        "w_k": w_k_r.astype(BF16), "w_v": wv.astype(BF16),
        "w_abs": absorb.reshape(N_HEADS * HEAD_PAD, 2 * LANES).astype(BF16), "w_kt": w_kt.astype(BF16),
        "g_conv_ln": g_conv_ln.reshape(1, CONV_CH), "b_conv_ln": b_conv_ln.reshape(1, CONV_CH),
        "g_out_attn": g_out_attn.reshape(1, ATTN_WIDTH), "g_out_conv": g_out_conv.reshape(1, CONV_CH),
        "w_out": w_out.astype(BF16), "g_norm_ffn": g_norm_ffn.reshape(1, d),
        "b_router": br.reshape(1, LANES),
    }


def _mod_table(mod_p, mod_s, t_new):
    return jnp.concatenate([jnp.broadcast_to(mod_p, (TOK_TILE, D_MODEL)), jnp.repeat(mod_s, t_new, axis=0)], axis=0)


def kernel(x_prompt, x_sample, cache_ckv, cache_kpe, state_conv, page_table, c_prompt, c_sample, w_ada, b_ada, g_norm_mix, g_norm_ffn, w_in, g_q_lat, w_q_up, g_q_nope, g_q_rope, g_kv_lat, g_k_rope, w_kv_up, g_k_nope, w_dw, b_dw, g_conv_ln, b_conv_ln, g_out_attn, g_out_conv, w_out, w_router, b_router, w_exp_up, b_exp_up, w_exp_down, b_exp_down):
    bsz, seq, d = x_prompt.shape
    n_seq, t_new = x_sample.shape[:2]
    depth = w_ada.shape[0]
    assert bsz == 1 and depth == 1 and t_new == 4 and d == D_MODEL
    n_p = bsz * seq
    n_s = n_seq * t_new
    n = n_p + n_s
    past = page_table.shape[1] * cache_ckv.shape[2]
    l = 0

    pos = jnp.concatenate([jnp.arange(seq, dtype=I32), jnp.tile(past + jnp.arange(t_new, dtype=I32), n_seq)])
    p = _prepare(pos, w_in[l], g_norm_mix[l], g_q_lat[l], w_q_up[l], g_q_nope[l], g_q_rope[l], g_kv_lat[l],
                 g_k_rope[l], w_kv_up[l], g_k_nope[l], g_conv_ln[l], b_conv_ln[l], g_out_attn[l], g_out_conv[l],
                 w_out[l], g_norm_ffn[l], w_router[l], b_router[l])

    n_c = 1 + n_seq
    c_all = jnp.concatenate([c_prompt, c_sample, jnp.zeros((-n_c % 8, d), F32)], axis=0)
    mod = _adaln(c_all, w_ada[l], b_ada[l])
    tabs = [_mod_table(mod[0:1, j * d:(j + 1) * d], mod[1:n_c, j * d:(j + 1) * d], t_new) for j in range(6)]
    sh_m, sc_m, gt_m, sh_f, sc_f, gt_f = tabs

    x_p = x_prompt.reshape(n_p, d)
    x_s = x_sample.reshape(n_s, d)
    q_all, ckv_all, kpe_all, u_all, k_all, v_all = _mixer_inputs(x_p, x_s, sh_m, sc_m, p)

    attn_p = _prompt_attention(q_all, k_all, v_all, seq)
    ckv_s = ckv_all[n_p:].reshape(n_seq, t_new, KV_LORA)
    kpe_s = kpe_all[n_p:].reshape(n_seq, t_new, QK_ROPE)
    ckv_new_pad = jnp.pad(ckv_s, ((0, 0), (0, LANES - t_new), (0, 0)))
    kpe_new_t = jnp.swapaxes(jnp.pad(kpe_s, ((0, 0), (0, LANES - t_new), (0, 0))), 1, 2)
    q_s = q_all[n_p:].reshape(n_seq, t_new, N_HEADS * HEAD_PAD)
    attn_s = _sample_attention(page_table, q_s, ckv_new_pad, kpe_new_t, cache_ckv,
                               jnp.swapaxes(cache_kpe, 2, 3), p)

    conv_p = _prompt_conv(u_all, w_dw[l], b_dw[l].reshape(1, CONV_CH))
    u_s = u_all[n_p:].reshape(n_seq, t_new, CONV_CH)
    u_ext_s = jnp.concatenate([state_conv[l], u_s], axis=1)
    ext = CONV_W - 1 + t_new
    ext_pad = -ext % 8
    w_taps = jnp.stack([jnp.pad(w_dw[l], ((t, t_new - 1 - t + ext_pad), (0, 0))) for t in range(t_new)])
    conv_s = _sample_conv(jnp.pad(u_ext_s, ((0, 0), (0, ext_pad), (0, 0))), w_taps, b_dw[l].reshape(1, CONV_CH))
    conv_s = jnp.transpose(conv_s, (1, 0, 2)).reshape(n_s, CONV_CH)

    y_all, f_all, top_i, top_w = _merge_router(x_p, x_s, attn_p, attn_s.reshape(n_s, ATTN_WIDTH), conv_p, conv_s,
                                               gt_m, sh_f, sc_f, p)

    n_tiles = -(-(n * TOP_K + N_EXPERTS * (FFN_TILE - 1)) // FFN_TILE)
    n_slots = n_tiles * FFN_TILE
    pos_tok, meta = _route(top_i, n_tiles)
    slot = jnp.transpose(pos_tok[:, :TOP_K])
    piece_idx = (slot[:, None, :] + (jnp.arange(SC_PIECES, dtype=I32) * n_slots)[None, :, None]).reshape(-1)
    x_sorted = _sc_scatter_rows(f_all.reshape(SC_PIECES * n, SC_ROW), piece_idx, SC_PIECES * n_slots)
    h_sorted = _expert_ffn(meta[:n_tiles, 0], meta[0:1, 1], x_sorted.reshape(SC_PIECES, n_slots, SC_ROW),
                           w_exp_up[l], b_exp_up[l], w_exp_down[l], b_exp_down[l])
    gathered = _sc_gather_rows(h_sorted.reshape(SC_PIECES * n_slots, SC_ROW), piece_idx)
    y_p, y_s = _combine(y_all, gathered.reshape(TOP_K, SC_PIECES, n, SC_ROW), top_w, gt_f, n_p)
    y_p = y_p.reshape(bsz, seq, d)
    y_s = y_s.reshape(n_seq, t_new, d)
    ckv_prompt = ckv_all[:n_p].reshape(1, bsz, seq, KV_LORA)
    kpe_prompt = kpe_all[:n_p].reshape(1, bsz, seq, QK_ROPE)
    conv_prompt = u_all[n_p - (CONV_W - 1):n_p].reshape(1, bsz, CONV_W - 1, CONV_CH)
    ckv_sample = ckv_s[None]
    kpe_sample = kpe_s[None]
    conv_sample = u_ext_s[:, t_new:][None]
    return (y_p, y_s, ckv_prompt, kpe_prompt, conv_prompt, ckv_sample, kpe_sample, conv_sample)
```

```python
import functools

import numpy as np
import jax
import jax.numpy as jnp
from jax import lax
from jax.experimental import pallas as pl
from jax.experimental.pallas import tpu as pltpu
from jax.experimental.pallas import tpu_sc as plsc

F32 = jnp.float32
BF16 = jnp.bfloat16
I32 = jnp.int32
HIGHEST = lax.Precision.HIGHEST

D_MODEL = 1024
N_HEADS = 8
QK_NOPE = 64
QK_ROPE = 32
V_DIM = 64
Q_LORA = 256
KV_LORA = 128
ATTN_WIDTH = N_HEADS * V_DIM
CONV_CH = D_MODEL - ATTN_WIDTH
CONV_W = 31
N_EXPERTS = 32
TOP_K = 4
D_FF = D_MODEL
SWIGLU_LIMIT = 7.0
SWIGLU_ALPHA = 1.702
EPS = 1e-6
NEG_INF = -1e30
ROPE_THETA = 10000.0
SM_SCALE = (QK_NOPE + QK_ROPE) ** -0.5
LOG2E = 1.4426950408889634
Q_SCALE = SM_SCALE * LOG2E

LANES = 128
SUBLANES = 8
HEAD_PAD = LANES
ROPE_LO = QK_NOPE
ROPE_HALF = QK_ROPE // 2

TOK_TILE = 256
FA_TQ = 1024
FA_TK = 1024
FA_HEADS = 8
PAGES_PER_STEP = 64
FFN_TILE = 512
ROUTE_TILE = 512
SC_ROW = 256
PACKED_WIDTH = D_MODEL // 2
SC_PIECES = PACKED_WIDTH // SC_ROW
SC_WINDOW = 128
VMEM_LIMIT = 56 * 1024 * 1024


def _cparams(sem, vmem=None):
    return pltpu.CompilerParams(dimension_semantics=sem, vmem_limit_bytes=vmem)


def _rsqrt_mean(x, n):
    return lax.rsqrt(jnp.sum(x * x, axis=-1, keepdims=True) * (1.0 / n) + EPS)


def _pack_rows(x):
    half = x.shape[1] // 2
    hi = lax.bitcast_convert_type(x[:, :half].astype(BF16).astype(F32), I32)
    lo = lax.bitcast_convert_type(x[:, half:].astype(BF16).astype(F32), I32)
    return hi | lax.shift_right_logical(lo, jnp.full(lo.shape, 16, I32))


def _unpack_rows(w):
    hi = lax.bitcast_convert_type(w & jnp.int32(-65536), F32)
    lo = lax.bitcast_convert_type(lax.shift_left(w, jnp.full(w.shape, 16, I32)), F32)
    return jnp.concatenate([hi, lo], axis=1)


def _ada_kernel(c_ref, w_ref, b_ref, o_ref):
    c = c_ref[...]
    s = c * jax.nn.sigmoid(c)
    o_ref[...] = jnp.dot(s, w_ref[...], precision=HIGHEST, preferred_element_type=F32) + b_ref[...]


def _adaln(c_all, w_ada, b_ada):
    rows = c_all.shape[0]
    n_out = w_ada.shape[1]
    return pl.pallas_call(
        _ada_kernel,
        grid=(n_out // D_MODEL,),
        in_specs=[pl.BlockSpec((rows, D_MODEL), lambda j: (0, 0)),
                  pl.BlockSpec((D_MODEL, D_MODEL), lambda j: (0, j)),
                  pl.BlockSpec((1, D_MODEL), lambda j: (0, j))],
        out_specs=pl.BlockSpec((rows, D_MODEL), lambda j: (0, j)),
        out_shape=jax.ShapeDtypeStruct((rows, n_out), F32),
        compiler_params=_cparams(("arbitrary",)),
        name="adaln",
    )(c_all, w_ada, b_ada.reshape(1, n_out))


def _group_norm_rope(x, m_grp, gain, cos_t, sin_t, first_half):
    ms = jnp.dot((x * x).astype(BF16), m_grp, preferred_element_type=F32)
    xn = x * lax.rsqrt(ms + EPS) * gain
    swapped = jnp.where(first_half, pltpu.roll(xn, LANES - ROPE_HALF, 1), pltpu.roll(xn, ROPE_HALF, 1))
    return xn * cos_t + swapped * sin_t


def _pick_rows(n_prompt_tiles, prompt_ref, decode_ref):
    return jnp.where(pl.program_id(0) < n_prompt_tiles, prompt_ref[...], decode_ref[...])


def _mix_kernel(n_prompt_tiles, xp_ref, xs_ref, sh_ref, sc_ref, gmix_ref, win_ref, gql_ref, wq_ref, gq_ref, m_ref,
                cos_ref, sin_ref, gkv_ref, gk_ref, gkpe_ref, wk_ref, wv_ref, vone_ref,
                q_out, ckv_out, kpe_out, u_out, k_out, v_out):
    x = _pick_rows(n_prompt_tiles, xp_ref, xs_ref)
    h = x * _rsqrt_mean(x, D_MODEL) * gmix_ref[...]
    h = h * (1.0 + sc_ref[...]) + sh_ref[...]
    proj = jnp.dot(h.astype(BF16), win_ref[...], preferred_element_type=F32)
    q_lat = proj[:, :Q_LORA]
    ckv_raw = proj[:, Q_LORA:Q_LORA + KV_LORA]
    kpe_blk = proj[:, Q_LORA + KV_LORA:Q_LORA + KV_LORA + LANES]
    glu_lo = Q_LORA + KV_LORA + LANES
    u_out[...] = proj[:, glu_lo:glu_lo + CONV_CH] * jax.nn.sigmoid(proj[:, glu_lo + CONV_CH:glu_lo + 2 * CONV_CH])

    m_grp = m_ref[...]
    cos_t = cos_ref[...]
    sin_t = sin_ref[...]
    lane = lax.broadcasted_iota(I32, (1, LANES), 1)
    first_half = lane < ROPE_LO + ROPE_HALF

    q_lat_n = q_lat * _rsqrt_mean(q_lat, Q_LORA) * gql_ref[...]
    q = jnp.dot(q_lat_n.astype(BF16), wq_ref[...], preferred_element_type=F32)
    gq = gq_ref[...]
    for hd in range(N_HEADS):
        qh = _group_norm_rope(q[:, hd * HEAD_PAD:(hd + 1) * HEAD_PAD], m_grp, gq, cos_t, sin_t, first_half)
        q_out[:, hd * HEAD_PAD:(hd + 1) * HEAD_PAD] = (qh * Q_SCALE).astype(BF16)

    ckv_n = ckv_raw * _rsqrt_mean(ckv_raw, KV_LORA) * gkv_ref[...]
    ckv_out[...] = ckv_n
    kpe_r = _group_norm_rope(kpe_blk, m_grp, gkpe_ref[...], cos_t, sin_t, first_half)
    kpe_out[...] = kpe_r[:, ROPE_LO:ROPE_LO + QK_ROPE]

    ckv_b = ckv_n.astype(BF16)
    kexp = jnp.dot(ckv_b, wk_ref[...], preferred_element_type=F32)
    gk = gk_ref[...]
    for hd in range(N_HEADS):
        kh = kexp[:, hd * HEAD_PAD:(hd + 1) * HEAD_PAD]
        ms = jnp.dot((kh * kh).astype(BF16), m_grp, preferred_element_type=F32)
        k_out[:, hd * HEAD_PAD:(hd + 1) * HEAD_PAD] = (kh * lax.rsqrt(ms + EPS) * gk + kpe_r).astype(BF16)
    v_out[...] = (jnp.dot(ckv_b, wv_ref[...], preferred_element_type=F32) + vone_ref[...]).astype(BF16)


def _mod_index(n_prompt_tiles):
    return lambda i: (jnp.where(i < n_prompt_tiles, 0, i - n_prompt_tiles + 1), 0)


def _prompt_index(n_prompt_tiles):
    return lambda i: (jnp.minimum(i, n_prompt_tiles - 1), 0)


def _decode_index(n_prompt_tiles):
    return lambda i: (jnp.maximum(i - n_prompt_tiles, 0), 0)


def _mixer_inputs(x_p, x_s, sh, sc, p):
    n_prompt_tiles = x_p.shape[0] // TOK_TILE
    n = x_p.shape[0] + x_s.shape[0]
    tm = TOK_TILE
    const = lambda i: (0, 0)
    row = lambda i: (i, 0)
    mod = _mod_index(n_prompt_tiles)
    hw = N_HEADS * HEAD_PAD
    in_cols = p["w_in"].shape[1]
    return pl.pallas_call(
        functools.partial(_mix_kernel, n_prompt_tiles),
        grid=(n // tm,),
        in_specs=[pl.BlockSpec((tm, D_MODEL), _prompt_index(n_prompt_tiles)),
                  pl.BlockSpec((tm, D_MODEL), _decode_index(n_prompt_tiles)),
                  pl.BlockSpec((tm, D_MODEL), mod),
                  pl.BlockSpec((tm, D_MODEL), mod),
                  pl.BlockSpec((1, D_MODEL), const),
                  pl.BlockSpec((D_MODEL, in_cols), const),
                  pl.BlockSpec((1, Q_LORA), const),
                  pl.BlockSpec((Q_LORA, hw), const),
                  pl.BlockSpec((1, LANES), const),
                  pl.BlockSpec((LANES, LANES), const),
                  pl.BlockSpec((tm, LANES), row),
                  pl.BlockSpec((tm, LANES), row),
                  pl.BlockSpec((1, KV_LORA), const),
                  pl.BlockSpec((1, LANES), const),
                  pl.BlockSpec((1, LANES), const),
                  pl.BlockSpec((KV_LORA, hw), const),
                  pl.BlockSpec((KV_LORA, hw), const),
                  pl.BlockSpec((1, hw), const)],
        out_specs=[pl.BlockSpec((tm, hw), row),
                   pl.BlockSpec((tm, KV_LORA), row),
                   pl.BlockSpec((tm, QK_ROPE), row),
                   pl.BlockSpec((tm, CONV_CH), row),
                   pl.BlockSpec((tm, hw), row),
                   pl.BlockSpec((tm, hw), row)],
        out_shape=[jax.ShapeDtypeStruct((n, hw), BF16),
                   jax.ShapeDtypeStruct((n, KV_LORA), F32),
                   jax.ShapeDtypeStruct((n, QK_ROPE), F32),
                   jax.ShapeDtypeStruct((n, CONV_CH), F32),
                   jax.ShapeDtypeStruct((n, hw), BF16),
                   jax.ShapeDtypeStruct((n, hw), BF16)],
        compiler_params=_cparams(("parallel",), VMEM_LIMIT),
        name="mixer_inputs",
    )(x_p, x_s, sh, sc, p["g_norm_mix"], p["w_in"], p["g_q_lat"], p["w_q_up"], p["gain_q"], p["m_grp"],
      p["cos_t"], p["sin_t"], p["g_kv_lat"], p["gain_k"], p["gain_kpe"], p["w_k"], p["w_v_tiles"], p["v_ones"])


def _fa_kernel(qt_ref, kt_ref, q_ref, k_ref, v_ref, o_ref, m_sc, acc_sc):
    t = pl.program_id(1)
    qi = qt_ref[t]
    ki = kt_ref[t]
    last_k = (qi + 1) * (FA_TQ // FA_TK) - 1

    @pl.when(ki == 0)
    def _():
        m_sc[...] = jnp.full(m_sc.shape, NEG_INF, F32)
        acc_sc[...] = jnp.zeros(acc_sc.shape, F32)

    def step(masked):
        if masked:
            col_minus_row = (lax.broadcasted_iota(I32, (FA_TQ, FA_TK), 1)
                             - lax.broadcasted_iota(I32, (FA_TQ, FA_TK), 0))
        for hh in range(FA_HEADS):
            q = q_ref[:, hh * HEAD_PAD:(hh + 1) * HEAD_PAD]
            k = k_ref[:, hh * HEAD_PAD:(hh + 1) * HEAD_PAD]
            v = v_ref[:, hh * HEAD_PAD:(hh + 1) * HEAD_PAD]
            s = lax.dot_general(q, k, (((1,), (1,)), ((), ())), preferred_element_type=F32)
            if masked:
                s = jnp.where(col_minus_row <= qi * FA_TQ - ki * FA_TK, s, NEG_INF)
            m_prev = m_sc[hh]
            m_new = jnp.maximum(m_prev, jnp.max(s, axis=-1, keepdims=True))
            alpha = jnp.exp2(m_prev - m_new)
            pr = jnp.exp2((s - jnp.concatenate([m_new] * (FA_TK // LANES), axis=1)).astype(BF16))
            acc_sc[hh] = alpha * acc_sc[hh] + jnp.dot(pr, v, preferred_element_type=F32)
            m_sc[hh] = m_new

    @pl.when(ki * FA_TK + FA_TK - 1 <= qi * FA_TQ)
    def _():
        step(False)

    @pl.when(ki * FA_TK + FA_TK - 1 > qi * FA_TQ)
    def _():
        step(True)

    @pl.when(ki == last_k)
    def _():
        for hh in range(FA_HEADS):
            acc = acc_sc[hh]
            o_ref[:, hh * V_DIM:(hh + 1) * V_DIM] = acc[:, :V_DIM] / acc[:, V_DIM:V_DIM + 1]


def _prompt_attention(q_all, k_all, v_all, seq):
    nq = seq // FA_TQ
    ratio = FA_TQ // FA_TK
    qt, kt = [], []
    for qi in range(nq):
        for ki in range((qi + 1) * ratio):
            qt.append(qi)
            kt.append(ki)
    qt = jnp.asarray(np.array(qt, np.int32))
    kt = jnp.asarray(np.array(kt, np.int32))
    n_pairs = int(qt.shape[0])
    grid_spec = pltpu.PrefetchScalarGridSpec(
        num_scalar_prefetch=2,
        grid=(N_HEADS // FA_HEADS, n_pairs),
        in_specs=[pl.BlockSpec((FA_TQ, FA_HEADS * HEAD_PAD), lambda hp, t, qt, kt: (qt[t], hp)),
                  pl.BlockSpec((FA_TK, FA_HEADS * HEAD_PAD), lambda hp, t, qt, kt: (kt[t], hp)),
                  pl.BlockSpec((FA_TK, FA_HEADS * HEAD_PAD), lambda hp, t, qt, kt: (kt[t], hp))],
        out_specs=pl.BlockSpec((FA_TQ, FA_HEADS * V_DIM), lambda hp, t, qt, kt: (qt[t], hp)),
        scratch_shapes=[pltpu.VMEM((FA_HEADS, FA_TQ, LANES), F32),
                        pltpu.VMEM((FA_HEADS, FA_TQ, HEAD_PAD), F32)],
    )
    return pl.pallas_call(
        _fa_kernel,
        grid_spec=grid_spec,
        out_shape=jax.ShapeDtypeStruct((seq, ATTN_WIDTH), F32),
        compiler_params=_cparams(("parallel", "arbitrary"), VMEM_LIMIT),
        name="prompt_attention",
    )(qt, kt, q_all, k_all, v_all)


def _sattn_kernel(pt_ref, q_ref, wabs_ref, wkt_ref, wv_ref, ckvn_ref, kpen_ref, ckv_hbm, kpe_hbm,
                  o_ref, m_sc, l_sc, acc_sc, qa_sc, wall_sc, ckv_buf, kpe_buf, sem):
    kb = pl.program_id(1)
    n_kb = pl.num_programs(1)
    step = pl.program_id(0) * n_kb + kb
    n_steps = pl.num_programs(0) * n_kb
    slot = step % 2
    rows = 4 * N_HEADS

    def start_fetch(step_i, slot_i):
        for j in range(PAGES_PER_STEP):
            page = pt_ref[step_i * PAGES_PER_STEP + j]
            pltpu.make_async_copy(ckv_hbm.at[0, page], ckv_buf.at[slot_i, j], sem.at[0, slot_i]).start()
            pltpu.make_async_copy(kpe_hbm.at[0, page], kpe_buf.at[slot_i, j], sem.at[1, slot_i]).start()

    @pl.when(step == 0)
    def _():
        start_fetch(step, slot)

    @pl.when(step + 1 < n_steps)
    def _():
        start_fetch(step + 1, 1 - slot)

    @pl.when(kb == 0)
    def _():
        m_sc[...] = jnp.full(m_sc.shape, NEG_INF, F32)
        l_sc[...] = jnp.zeros(l_sc.shape, F32)
        acc_sc[...] = jnp.zeros(acc_sc.shape, F32)
        q4 = q_ref[...].astype(F32)
        head_of_lane = lax.broadcasted_iota(I32, (N_HEADS, N_HEADS * HEAD_PAD), 1) // HEAD_PAD
        head_of_row = lax.broadcasted_iota(I32, (N_HEADS, N_HEADS * HEAD_PAD), 0)
        own = head_of_lane == head_of_row
        qbd = jnp.concatenate(
            [jnp.where(own, jnp.broadcast_to(q4[qq:qq + 1, :], own.shape), 0.0) for qq in range(4)], axis=0)
        qa = jnp.dot(qbd.astype(BF16), wabs_ref[...], preferred_element_type=F32).astype(BF16)
        qa_sc[...] = qa
        wall_sc[pl.ds(0, N_HEADS * QK_NOPE), :] = wkt_ref[...]
        wall_sc[pl.ds(N_HEADS * QK_NOPE, rows), :] = qa[:, :KV_LORA]

    def attend(state, ckv_b, kpe_t, mask):
        m_prev, l_prev, acc_prev = state
        nt = (((1,), (1,)), ((), ()))
        kn_all = lax.dot_general(wall_sc[...], ckv_b, nt, preferred_element_type=F32)
        keys = kn_all.shape[1]
        kn_t = kn_all[:N_HEADS * QK_NOPE]
        ss = jnp.sum((kn_t * kn_t).reshape(N_HEADS, QK_NOPE, keys), axis=1)
        r8 = lax.rsqrt(ss * (1.0 / QK_NOPE) + EPS)
        rope = jnp.dot(qa_sc[:, KV_LORA:KV_LORA + QK_ROPE], kpe_t, preferred_element_type=F32)
        s = kn_all[N_HEADS * QK_NOPE:] * jnp.concatenate([r8] * 4, axis=0) + rope
        if mask is not None:
            s = jnp.where(mask, s, NEG_INF)
        m_new = jnp.maximum(m_prev, jnp.max(s, axis=-1, keepdims=True))
        alpha = jnp.exp2(m_prev - m_new)
        pr = jnp.exp2(s - m_new)
        l_new = alpha * l_prev + jnp.sum(pr, axis=-1, keepdims=True)
        acc_new = alpha * acc_prev + jnp.dot(pr.astype(BF16), ckv_b, preferred_element_type=F32)
        return m_new, l_new, acc_new

    pltpu.make_async_copy(ckv_hbm.at[0, pl.ds(0, PAGES_PER_STEP)], ckv_buf.at[slot], sem.at[0, slot]).wait()
    pltpu.make_async_copy(kpe_hbm.at[0, pl.ds(0, PAGES_PER_STEP)], kpe_buf.at[slot], sem.at[1, slot]).wait()

    page = ckv_buf.shape[2]
    ckv_b = ckv_buf[slot].reshape(PAGES_PER_STEP * page, KV_LORA).astype(BF16)
    kpe_t = jnp.concatenate([kpe_buf[slot, j].astype(BF16) for j in range(PAGES_PER_STEP)], axis=1)
    state = attend((m_sc[...], l_sc[...], acc_sc[...]), ckv_b, kpe_t, None)
    m_sc[...], l_sc[...], acc_sc[...] = state

    @pl.when(kb == n_kb - 1)
    def _():
        key = lax.broadcasted_iota(I32, (rows, LANES), 1)
        qry = lax.broadcasted_iota(I32, (rows, LANES), 0) // N_HEADS
        _, l_fin, acc_fin = attend(state, ckvn_ref[...].astype(BF16), kpen_ref[...].astype(BF16), key <= qry)
        lat = acc_fin / l_fin
        o_all = jnp.dot(lat.astype(BF16), wv_ref[...], preferred_element_type=F32)
        head_of_col = lax.broadcasted_iota(I32, (rows, ATTN_WIDTH), 1) // V_DIM
        head_of_row = lax.broadcasted_iota(I32, (rows, ATTN_WIDTH), 0) % N_HEADS
        o_own = jnp.where(head_of_col == head_of_row, o_all, 0.0)
        o_ref[...] = jnp.sum(o_own.reshape(4, N_HEADS, ATTN_WIDTH), axis=1)


def _sample_attention(page_table, q_s, ckv_new_pad, kpe_new_t, cache_ckv, cache_kpe_t, p):
    n_seq, n_pages = page_table.shape
    page = cache_ckv.shape[2]
    n_kb = n_pages // PAGES_PER_STEP
    hw = N_HEADS * HEAD_PAD
    assert n_pages % PAGES_PER_STEP == 0
    per_seq3 = lambda b, kb, pt: (b, 0, 0)
    const = lambda b, kb, pt: (0, 0)
    in_specs = [pl.BlockSpec((None, 4, hw), per_seq3),
                pl.BlockSpec((hw, 2 * LANES), const),
                pl.BlockSpec((N_HEADS * QK_NOPE, KV_LORA), const),
                pl.BlockSpec((KV_LORA, ATTN_WIDTH), const),
                pl.BlockSpec((None, LANES, KV_LORA), per_seq3),
                pl.BlockSpec((None, QK_ROPE, LANES), per_seq3),
                pl.BlockSpec(memory_space=pl.ANY),
                pl.BlockSpec(memory_space=pl.ANY)]
    grid_spec = pltpu.PrefetchScalarGridSpec(
        num_scalar_prefetch=1,
        grid=(n_seq, n_kb),
        in_specs=in_specs,
        out_specs=pl.BlockSpec((None, 4, ATTN_WIDTH), per_seq3),
        scratch_shapes=[pltpu.VMEM((4 * N_HEADS, 1), F32),
                        pltpu.VMEM((4 * N_HEADS, 1), F32),
                        pltpu.VMEM((4 * N_HEADS, KV_LORA), F32),
                        pltpu.VMEM((4 * N_HEADS, 2 * LANES), BF16),
                        pltpu.VMEM((N_HEADS * QK_NOPE + 4 * N_HEADS, KV_LORA), BF16),
                        pltpu.VMEM((2, PAGES_PER_STEP, page, KV_LORA), F32),
                        pltpu.VMEM((2, PAGES_PER_STEP, QK_ROPE, page), F32),
                        pltpu.SemaphoreType.DMA((2, 2))],
    )
    return pl.pallas_call(
        _sattn_kernel,
        grid_spec=grid_spec,
        out_shape=jax.ShapeDtypeStruct((n_seq, 4, ATTN_WIDTH), F32),
        compiler_params=_cparams(("arbitrary", "arbitrary"), VMEM_LIMIT),
        name="decode_attention",
    )(page_table.reshape(-1), q_s, p["w_abs"], p["w_kt"], p["w_v"], ckv_new_pad, kpe_new_t, cache_ckv, cache_kpe_t)


CONV_HALO = 32
CONV_ROWS = 64


def _conv_kernel(halo_ref, u_ref, w_ref, b_ref, o_ref, ext_sc, sh_sc):
    i = pl.program_id(0)
    tm = u_ref.shape[0]
    ext_sc[pl.ds(0, CONV_HALO), :] = jnp.where(i == 0, 0.0, halo_ref[...])
    ext_sc[pl.ds(CONV_HALO, tm), :] = u_ref[...]
    first = CONV_HALO - (CONV_W - 1)
    span = tm + CONV_HALO - SUBLANES
    for sft in range(1, SUBLANES):
        sh_sc[sft - 1] = ext_sc[pl.ds(sft, span), :]
    for rc in range(tm // CONV_ROWS):
        acc = jnp.broadcast_to(b_ref[...], (CONV_ROWS, CONV_CH))
        for j in range(CONV_W):
            a8, sft = divmod(first + j, SUBLANES)
            src = ext_sc if sft == 0 else sh_sc.at[sft - 1]
            acc = acc + src[pl.ds(rc * CONV_ROWS + SUBLANES * a8, CONV_ROWS), :] * w_ref[j:j + 1, :]
        o_ref[pl.ds(rc * CONV_ROWS, CONV_ROWS), :] = acc


def _prompt_conv(u_all, w_dw, b_dw):
    n = u_all.shape[0]
    tm = TOK_TILE
    per = tm // CONV_HALO
    return pl.pallas_call(
        _conv_kernel,
        grid=(n // tm,),
        in_specs=[pl.BlockSpec((CONV_HALO, CONV_CH), lambda i: (jnp.maximum(i * per - 1, 0), 0)),
                  pl.BlockSpec((tm, CONV_CH), lambda i: (i, 0)),
                  pl.BlockSpec((CONV_W, CONV_CH), lambda i: (0, 0)),
                  pl.BlockSpec((1, CONV_CH), lambda i: (0, 0))],
        out_specs=pl.BlockSpec((tm, CONV_CH), lambda i: (i, 0)),
        out_shape=jax.ShapeDtypeStruct((n, CONV_CH), F32),
        scratch_shapes=[pltpu.VMEM((CONV_HALO + tm, CONV_CH), F32),
                        pltpu.VMEM((SUBLANES - 1, CONV_HALO + tm - SUBLANES, CONV_CH), F32)],
        compiler_params=_cparams(("parallel",)),
        name="prompt_conv",
    )(u_all, u_all, w_dw, b_dw)


def _sconv_kernel(u_ref, wt_ref, b_ref, o_ref):
    u = u_ref[...]
    for t in range(o_ref.shape[0]):
        o_ref[t] = jnp.sum(u * wt_ref[t][None, :, :], axis=1) + b_ref[...]


def _sample_conv(u_ext, w_taps, b_dw):
    n_seq, ext, _ = u_ext.shape
    t_new = w_taps.shape[0]
    sb = 8
    return pl.pallas_call(
        _sconv_kernel,
        grid=(n_seq // sb,),
        in_specs=[pl.BlockSpec((sb, ext, CONV_CH), lambda i: (i, 0, 0)),
                  pl.BlockSpec((t_new, ext, CONV_CH), lambda i: (0, 0, 0)),
                  pl.BlockSpec((1, CONV_CH), lambda i: (0, 0))],
        out_specs=pl.BlockSpec((t_new, sb, CONV_CH), lambda i: (0, i, 0)),
        out_shape=jax.ShapeDtypeStruct((t_new, n_seq, CONV_CH), F32),
        compiler_params=_cparams(("parallel",)),
        name="decode_conv",
    )(u_ext, w_taps, b_dw)


def _lane_pack(cols, dtype):
    lane = lax.broadcasted_iota(I32, (cols[0].shape[0], LANES), 1)
    out = jnp.zeros((cols[0].shape[0], LANES), dtype)
    for j, c in enumerate(cols):
        out = jnp.where(lane == j, c.astype(dtype), out)
    return out


def _merge_kernel(n_prompt_tiles, xp_ref, xs_ref, ap_ref, as_ref, cp_ref, cs_ref, gate_ref, shf_ref, scf_ref,
                  gln_ref, bln_ref, ga_ref, gc_ref, wout_ref, gffn_ref, wrh_ref, wrl_ref, br_ref,
                  y_out, f_out, ti_out, tw_out):
    yc = _pick_rows(n_prompt_tiles, cp_ref, cs_ref)
    mu = jnp.mean(yc, axis=-1, keepdims=True)
    xc = yc - mu
    var = jnp.mean(xc * xc, axis=-1, keepdims=True)
    ln = xc * lax.rsqrt(var + EPS) * gln_ref[...] + bln_ref[...]
    conv = ln * jax.nn.sigmoid(ln)
    attn = _pick_rows(n_prompt_tiles, ap_ref, as_ref)
    a_n = attn * _rsqrt_mean(attn, ATTN_WIDTH) * ga_ref[...]
    c_n = conv * _rsqrt_mean(conv, CONV_CH) * gc_ref[...]
    m = (jnp.dot(a_n.astype(BF16), wout_ref[:ATTN_WIDTH, :], preferred_element_type=F32)
         + jnp.dot(c_n.astype(BF16), wout_ref[ATTN_WIDTH:, :], preferred_element_type=F32))
    y = _pick_rows(n_prompt_tiles, xp_ref, xs_ref) + gate_ref[...] * m
    y_out[...] = y
    f = y * _rsqrt_mean(y, D_MODEL) * gffn_ref[...]
    f = f * (1.0 + scf_ref[...]) + shf_ref[...]
    f_words = _pack_rows(f)
    for j in range(SC_PIECES):
        f_out[j] = f_words[:, j * SC_ROW:(j + 1) * SC_ROW]
    f_hi = f.astype(BF16)
    f_lo = (f - f_hi.astype(F32)).astype(BF16)
    logits = (jnp.dot(f_hi, wrh_ref[...], preferred_element_type=F32)
              + jnp.dot(f_hi, wrl_ref[...], preferred_element_type=F32)
              + jnp.dot(f_lo, wrh_ref[...], preferred_element_type=F32)) + br_ref[...]
    lane = lax.broadcasted_iota(I32, logits.shape, 1)
    vals, idxs = [], []
    for _ in range(TOP_K):
        mx = jnp.max(logits, axis=-1, keepdims=True)
        ix = jnp.min(jnp.where(logits == mx, lane, LANES), axis=-1, keepdims=True)
        vals.append(mx)
        idxs.append(ix)
        logits = jnp.where(lane == ix, NEG_INF * 4.0, logits)
    exps = [jnp.exp(v - vals[0]) for v in vals]
    tot = exps[0] + exps[1] + exps[2] + exps[3]
    ti_out[...] = _lane_pack(idxs, I32)
    tw_out[...] = _lane_pack([e / tot for e in exps], F32)


def _merge_router(x_p, x_s, attn_p, attn_s, conv_p, conv_s, gate, shf, scf, p):
    n_prompt_tiles = x_p.shape[0] // TOK_TILE
    n = x_p.shape[0] + x_s.shape[0]
    tm = TOK_TILE
    const = lambda i: (0, 0)
    row = lambda i: (i, 0)
    mod = _mod_index(n_prompt_tiles)
    from_p = _prompt_index(n_prompt_tiles)
    from_s = _decode_index(n_prompt_tiles)
    return pl.pallas_call(
        functools.partial(_merge_kernel, n_prompt_tiles),
        grid=(n // tm,),
        in_specs=[pl.BlockSpec((tm, D_MODEL), from_p),
                  pl.BlockSpec((tm, D_MODEL), from_s),
                  pl.BlockSpec((tm, ATTN_WIDTH), from_p),
                  pl.BlockSpec((tm, ATTN_WIDTH), from_s),
                  pl.BlockSpec((tm, CONV_CH), from_p),
                  pl.BlockSpec((tm, CONV_CH), from_s),
                  pl.BlockSpec((tm, D_MODEL), mod),
                  pl.BlockSpec((tm, D_MODEL), mod),
                  pl.BlockSpec((tm, D_MODEL), mod),
                  pl.BlockSpec((1, CONV_CH), const),
                  pl.BlockSpec((1, CONV_CH), const),
                  pl.BlockSpec((1, ATTN_WIDTH), const),
                  pl.BlockSpec((1, CONV_CH), const),
                  pl.BlockSpec((D_MODEL, D_MODEL), const),
                  pl.BlockSpec((1, D_MODEL), const),
                  pl.BlockSpec((D_MODEL, LANES), const),
                  pl.BlockSpec((D_MODEL, LANES), const),
                  pl.BlockSpec((1, LANES), const)],
        out_specs=[pl.BlockSpec((tm, D_MODEL), row),
                   pl.BlockSpec((SC_PIECES, tm, SC_ROW), lambda i: (0, i, 0)),
                   pl.BlockSpec((tm, LANES), row),
                   pl.BlockSpec((tm, LANES), row)],
        out_shape=[jax.ShapeDtypeStruct((n, D_MODEL), F32),
                   jax.ShapeDtypeStruct((SC_PIECES, n, SC_ROW), I32),
                   jax.ShapeDtypeStruct((n, LANES), I32),
                   jax.ShapeDtypeStruct((n, LANES), F32)],
        compiler_params=_cparams(("parallel",), VMEM_LIMIT),
        name="merge_router",
    )(x_p, x_s, attn_p, attn_s, conv_p, conv_s, gate, shf, scf, p["g_conv_ln"], p["b_conv_ln"], p["g_out_attn"],
      p["g_out_conv"], p["w_out"], p["g_norm_ffn"], p["w_router_hi"], p["w_router_lo"], p["b_router"])


def _select_lane(table, idx_col, lane):
    return jnp.sum(jnp.where(lane == idx_col, table, 0.0), axis=-1, keepdims=True)


def _rank_kernel(ti_ref, rk_out, cnt_out, carry_sc):
    i = pl.program_id(0)

    @pl.when(i == 0)
    def _():
        carry_sc[...] = jnp.zeros(carry_sc.shape, F32)

    ti = ti_ref[...]
    tn = ti.shape[0]
    lane = lax.broadcasted_iota(I32, (tn, LANES), 1)
    sel = jnp.zeros((tn, LANES), F32)
    for k in range(TOP_K):
        sel = sel + (lane == ti[:, k:k + 1]).astype(F32)
    r_i = lax.broadcasted_iota(I32, (tn, tn), 0)
    c_i = lax.broadcasted_iota(I32, (tn, tn), 1)
    below = (c_i < r_i).astype(BF16)
    rank = carry_sc[...] + jnp.dot(below, sel.astype(BF16), preferred_element_type=F32)
    rk_out[...] = _lane_pack([_select_lane(rank, ti[:, k:k + 1], lane) for k in range(TOP_K)], F32)
    carry_sc[...] = carry_sc[...] + jnp.sum(sel, axis=0, keepdims=True)
    cnt_out[...] = jnp.broadcast_to(carry_sc[...], cnt_out.shape)


def _pos_kernel(cnt_ref, ti_ref, rk_ref, pos_out, meta_out):
    cnt = cnt_ref[...]
    padded = jnp.ceil(cnt * (1.0 / FFN_TILE)) * FFN_TILE
    r_i = lax.broadcasted_iota(I32, (LANES, LANES), 0)
    c_i = lax.broadcasted_iota(I32, (LANES, LANES), 1)
    before = (r_i < c_i).astype(F32)
    offs = jnp.dot(padded, before, precision=HIGHEST, preferred_element_type=F32)
    ends = offs + padded
    ti = ti_ref[...]
    tn = ti.shape[0]
    lane = lax.broadcasted_iota(I32, (tn, LANES), 1)
    off_row = offs[0:1, :]
    rk = rk_ref[...]
    pos = [_select_lane(jnp.broadcast_to(off_row, (tn, LANES)), ti[:, k:k + 1], lane) + rk[:, k:k + 1]
           for k in range(TOP_K)]
    pos_out[...] = _lane_pack(pos, F32).astype(I32)

    @pl.when(pl.program_id(0) == 0)
    def _():
        nt = meta_out.shape[0]
        start = (lax.broadcasted_iota(I32, (nt, LANES), 0) * FFN_TILE).astype(F32)
        elane = lax.broadcasted_iota(I32, (nt, LANES), 1)
        done = jnp.where((elane < N_EXPERTS) & (jnp.broadcast_to(ends[0:1, :], (nt, LANES)) <= start), 1.0, 0.0)
        expert = jnp.minimum(jnp.sum(done, axis=-1, keepdims=True), N_EXPERTS - 1.0)
        total = jnp.sum(jnp.where(elane < N_EXPERTS, jnp.broadcast_to(padded[0:1, :], (nt, LANES)), 0.0),
                        axis=-1, keepdims=True)
        meta_out[...] = _lane_pack([expert, total * (1.0 / FFN_TILE)], F32).astype(I32)


def _route(top_i, n_tiles_max):
    n = top_i.shape[0]
    tn = ROUTE_TILE
    rk, cnt = pl.pallas_call(
        _rank_kernel,
        grid=(n // tn,),
        in_specs=[pl.BlockSpec((tn, LANES), lambda i: (i, 0))],
        out_specs=[pl.BlockSpec((tn, LANES), lambda i: (i, 0)),
                   pl.BlockSpec((8, LANES), lambda i: (0, 0))],
        out_shape=[jax.ShapeDtypeStruct((n, LANES), F32),
                   jax.ShapeDtypeStruct((8, LANES), F32)],
        scratch_shapes=[pltpu.VMEM((1, LANES), F32)],
        compiler_params=_cparams(("arbitrary",)),
        name="route_rank",
    )(top_i)
    nt_pad = -(-n_tiles_max // 8) * 8
    pos, meta = pl.pallas_call(
        _pos_kernel,
        grid=(n // tn,),
        in_specs=[pl.BlockSpec((8, LANES), lambda i: (0, 0)),
                  pl.BlockSpec((tn, LANES), lambda i: (i, 0)),
                  pl.BlockSpec((tn, LANES), lambda i: (i, 0))],
        out_specs=[pl.BlockSpec((tn, LANES), lambda i: (i, 0)),
                   pl.BlockSpec((nt_pad, LANES), lambda i: (0, 0))],
        out_shape=[jax.ShapeDtypeStruct((n, LANES), I32),
                   jax.ShapeDtypeStruct((nt_pad, LANES), I32)],
        compiler_params=_cparams(("arbitrary",)),
        name="route_pos",
    )(cnt, top_i, rk)
    return pos, meta


def _sc_scatter_rows(x, idx, n_out):
    n_src = x.shape[0]
    n = idx.shape[0]
    n_src_blk = n_src // SC_WINDOW
    mesh = plsc.VectorSubcoreMesh(core_axis_name="c", subcore_axis_name="s")

    @pl.kernel(out_type=jax.ShapeDtypeStruct((n_out, SC_ROW), x.dtype), mesh=mesh)
    def k(x_hbm, i_hbm, o_hbm):
        def body(x_vmem, i_vmem):
            pltpu.sync_copy(x_vmem, o_hbm.at[i_vmem.at[0]])

        pltpu.emit_pipeline(
            body,
            grid=(n // SC_WINDOW,),
            in_specs=[pl.BlockSpec((SC_WINDOW, SC_ROW), index_map=lambda i: (i % n_src_blk, 0)),
                      pl.BlockSpec((1, SC_WINDOW), index_map=lambda i: (0, i))],
            out_specs=[],
            core_axis_name=("c", "s"),
            dimension_semantics=(pltpu.PARALLEL,),
        )(x_hbm, i_hbm)

    return k(x, idx.reshape(1, n))


def _sc_gather_rows(x, idx):
    n = idx.shape[0]
    mesh = plsc.VectorSubcoreMesh(core_axis_name="c", subcore_axis_name="s")

    @pl.kernel(out_type=jax.ShapeDtypeStruct((n, SC_ROW), x.dtype), mesh=mesh)
    def k(x_hbm, i_hbm, o_hbm):
        def body(i_vmem, o_vmem):
            pltpu.sync_copy(x_hbm.at[i_vmem.at[0]], o_vmem)

        pltpu.emit_pipeline(
            body,
            grid=(n // SC_WINDOW,),
            in_specs=[pl.BlockSpec((1, SC_WINDOW), index_map=lambda i: (0, i))],
            out_specs=[pl.BlockSpec((SC_WINDOW, SC_ROW), index_map=lambda i: (i, 0))],
            core_axis_name=("c", "s"),
            dimension_semantics=(pltpu.PARALLEL,),
        )(i_hbm, o_hbm)

    return k(x, idx.reshape(1, n))


def _ffn_kernel(te_ref, nv_ref, x_ref, wu_ref, bu_ref, wd_ref, bd_ref, o_ref, wu_sc, wd_sc):
    t = pl.program_id(0)
    valid = t < nv_ref[0]
    prev = te_ref[jnp.maximum(t - 1, 0)]
    fresh = jnp.logical_or(t == 0, te_ref[t] != prev)

    @pl.when(jnp.logical_and(valid, fresh))
    def _():
        wu_sc[...] = wu_ref[...].astype(BF16)
        wd_sc[...] = wd_ref[...].astype(BF16)

    @pl.when(valid)
    def _():
        x = _unpack_rows(jnp.concatenate([x_ref[j] for j in range(SC_PIECES)], axis=1))
        z = jnp.dot(x.astype(BF16), wu_sc[...], preferred_element_type=F32) + bu_ref[...]
        zg = jnp.minimum(z[:, :D_FF], SWIGLU_LIMIT)
        zl = jnp.clip(z[:, D_FF:], -SWIGLU_LIMIT, SWIGLU_LIMIT)
        act = zg * jax.nn.sigmoid(SWIGLU_ALPHA * zg) * (zl + 1.0)
        out = _pack_rows(jnp.dot(act.astype(BF16), wd_sc[...], preferred_element_type=F32) + bd_ref[...])
        for j in range(SC_PIECES):
            o_ref[j] = out[:, j * SC_ROW:(j + 1) * SC_ROW]


def _expert_ffn(tile_expert, n_valid, x_sorted, w_up, b_up, w_down, b_down):
    n_slots = x_sorted.shape[1]
    n_tiles = n_slots // FFN_TILE
    xmap = lambda t, te, nv: (0, jnp.minimum(t, nv[0] - 1), 0)
    emap = lambda t, te, nv: (te[t], 0, 0)
    grid_spec = pltpu.PrefetchScalarGridSpec(
        num_scalar_prefetch=2,
        grid=(n_tiles,),
        in_specs=[pl.BlockSpec((SC_PIECES, FFN_TILE, SC_ROW), xmap),
                  pl.BlockSpec((None, D_MODEL, 2 * D_FF), emap),
                  pl.BlockSpec((None, 1, 2 * D_FF), emap),
                  pl.BlockSpec((None, D_FF, D_MODEL), emap),
                  pl.BlockSpec((None, 1, D_MODEL), emap)],
        out_specs=pl.BlockSpec((SC_PIECES, FFN_TILE, SC_ROW), xmap),
        scratch_shapes=[pltpu.VMEM((D_MODEL, 2 * D_FF), BF16),
                        pltpu.VMEM((D_FF, D_MODEL), BF16)],
    )
    return pl.pallas_call(
        _ffn_kernel,
        grid_spec=grid_spec,
        out_shape=jax.ShapeDtypeStruct((SC_PIECES, n_slots, SC_ROW), I32),
        compiler_params=_cparams(("arbitrary",), VMEM_LIMIT),
        name="expert_ffn",
    )(tile_expert, n_valid, x_sorted, w_up, b_up.reshape(N_EXPERTS, 1, 2 * D_FF), w_down,
      b_down.reshape(N_EXPERTS, 1, D_MODEL))


def _combine_kernel(n_prompt_tiles, y_ref, g_ref, tw_ref, gate_ref, op_ref, os_ref):
    tw = tw_ref[...]
    rows = lambda k: _unpack_rows(jnp.concatenate([g_ref[k, j] for j in range(SC_PIECES)], axis=1))
    moe = rows(0) * tw[:, 0:1]
    for k in range(1, TOP_K):
        moe = moe + rows(k) * tw[:, k:k + 1]
    out = y_ref[...] + gate_ref[...] * moe
    i = pl.program_id(0)

    @pl.when(i < n_prompt_tiles)
    def _():
        op_ref[...] = out

    @pl.when(i >= n_prompt_tiles)
    def _():
        os_ref[...] = out


def _combine(y_all, gathered, top_w, gate, n_p):
    n = y_all.shape[0]
    tm = TOK_TILE
    n_prompt_tiles = n_p // tm
    row = lambda i: (i, 0)
    return pl.pallas_call(
        functools.partial(_combine_kernel, n_prompt_tiles),
        grid=(n // tm,),
        in_specs=[pl.BlockSpec((tm, D_MODEL), row),
                  pl.BlockSpec((TOP_K, SC_PIECES, tm, SC_ROW), lambda i: (0, 0, i, 0)),
                  pl.BlockSpec((tm, LANES), row),
                  pl.BlockSpec((tm, D_MODEL), _mod_index(n_prompt_tiles))],
        out_specs=[pl.BlockSpec((tm, D_MODEL), _prompt_index(n_prompt_tiles)),
                   pl.BlockSpec((tm, D_MODEL), _decode_index(n_prompt_tiles))],
        out_shape=[jax.ShapeDtypeStruct((n_p, D_MODEL), F32),
                   jax.ShapeDtypeStruct((n - n_p, D_MODEL), F32)],
        compiler_params=_cparams(("arbitrary",), VMEM_LIMIT),
        name="moe_combine",
    )(y_all, gathered, top_w, gate)


def _head_tiles(nope, rope):
    pad = jnp.zeros(nope.shape[:-1] + (HEAD_PAD - QK_NOPE - QK_ROPE,), nope.dtype)
    t = jnp.concatenate([nope, rope, pad], axis=-1)
    return t.reshape(t.shape[:-2] + (N_HEADS * HEAD_PAD,))


def _prepare(pos, w_in, g_norm_mix, g_q_lat, w_q_up, g_q_nope, g_q_rope, g_kv_lat, g_k_rope, w_kv_up, g_k_nope,
             g_conv_ln, b_conv_ln, g_out_attn, g_out_conv, w_out, g_norm_ffn, w_router, b_router):
    z32 = jnp.zeros((HEAD_PAD - QK_NOPE - QK_ROPE,), F32)
    z64 = jnp.zeros((QK_NOPE,), F32)
    d = D_MODEL
    kpe_cols = w_in[:, Q_LORA + KV_LORA:Q_LORA + KV_LORA + QK_ROPE]
    kpe_tile = jnp.concatenate([jnp.zeros((d, QK_NOPE), F32), kpe_cols, jnp.zeros((d, z32.shape[0]), F32)], axis=1)
    w_in_r = jnp.concatenate([w_in[:, :Q_LORA + KV_LORA], kpe_tile, w_in[:, Q_LORA + KV_LORA + QK_ROPE:]], axis=1)
    wq = w_q_up.reshape(Q_LORA, N_HEADS, QK_NOPE + QK_ROPE)
    w_q_r = _head_tiles(wq[..., :QK_NOPE], wq[..., QK_NOPE:])
    wk = w_kv_up[..., :QK_NOPE]
    wv = w_kv_up[..., QK_NOPE:].reshape(KV_LORA, ATTN_WIDTH)
    w_k_r = _head_tiles(wk, jnp.zeros((KV_LORA, N_HEADS, QK_ROPE), F32))
    lane = np.arange(LANES)
    grp = np.where(lane < QK_NOPE, 0, np.where(lane < QK_NOPE + QK_ROPE, 1, 2))
    m_grp = ((grp[:, None] == grp[None, :]) & (grp[:, None] < 2)).astype(np.float32)
    m_grp = m_grp / np.where(grp < 1, QK_NOPE, QK_ROPE)[None, :]
    inv = ROPE_THETA ** (-jnp.arange(0, QK_ROPE, 2, dtype=F32) / QK_ROPE)
    ang = pos.astype(F32)[:, None] * inv[None, :]
    cs, sn = jnp.cos(ang), jnp.sin(ang)
    n = pos.shape[0]
    cos_t = jnp.concatenate([jnp.ones((n, QK_NOPE), F32), cs, cs, jnp.zeros((n, z32.shape[0]), F32)], axis=1)
    sin_t = jnp.concatenate([jnp.zeros((n, QK_NOPE), F32), -sn, sn, jnp.zeros((n, z32.shape[0]), F32)], axis=1)
    wk_g = wk * g_k_nope[None, None, :]
    absorb = jnp.zeros((N_HEADS, HEAD_PAD, 2 * LANES), F32)
    absorb = absorb.at[:, :QK_NOPE, :KV_LORA].set(jnp.transpose(wk_g, (1, 2, 0)))
    absorb = absorb.at[:, ROPE_LO:ROPE_LO + QK_ROPE, KV_LORA:KV_LORA + QK_ROPE].set(
        jnp.broadcast_to(jnp.eye(QK_ROPE, dtype=F32), (N_HEADS, QK_ROPE, QK_ROPE)))
    w_kt = jnp.transpose(wk, (1, 2, 0)).reshape(N_HEADS * QK_NOPE, KV_LORA)
    wr = jnp.concatenate([w_router, jnp.zeros((d, LANES - N_EXPERTS), F32)], axis=1)
    wr_hi = wr.astype(BF16)
    wr_lo = (wr - wr_hi.astype(F32)).astype(BF16)
    br = jnp.concatenate([b_router, jnp.full((LANES - N_EXPERTS,), NEG_INF, F32)])
    wv_h = w_kv_up[..., QK_NOPE:]
    w_v_tiles = jnp.concatenate([wv_h, jnp.zeros((KV_LORA, N_HEADS, HEAD_PAD - V_DIM), F32)], axis=-1)
    v_ones = (jnp.arange(N_HEADS * HEAD_PAD) % HEAD_PAD == V_DIM).astype(F32).reshape(1, N_HEADS * HEAD_PAD)
    return {
        "w_v_tiles": w_v_tiles.reshape(KV_LORA, N_HEADS * HEAD_PAD).astype(BF16), "v_ones": v_ones,
        "w_router_hi": wr_hi, "w_router_lo": wr_lo,
        "w_in": w_in_r.astype(BF16), "g_norm_mix": g_norm_mix.reshape(1, d), "g_q_lat": g_q_lat.reshape(1, Q_LORA),
        "w_q_up": w_q_r.astype(BF16),
        "gain_q": jnp.concatenate([g_q_nope, g_q_rope, z32]).reshape(1, LANES),
        "m_grp": jnp.asarray(m_grp, BF16), "cos_t": cos_t, "sin_t": sin_t,
        "g_kv_lat": g_kv_lat.reshape(1, KV_LORA),
        "gain_k": jnp.concatenate([g_k_nope, z64]).reshape(1, LANES),
        "gain_kpe": jnp.concatenate([z64, g_k_rope, z32]).reshape(1, LANES),
        "w_k": w_k_r.astype(BF16), "w_v": wv.astype(BF16),
        "w_abs": absorb.reshape(N_HEADS * HEAD_PAD, 2 * LANES).astype(BF16), "w_kt": w_kt.astype(BF16),
        "g_conv_ln": g_conv_ln.reshape(1, CONV_CH), "b_conv_ln": b_conv_ln.reshape(1, CONV_CH),
        "g_out_attn": g_out_attn.reshape(1, ATTN_WIDTH), "g_out_conv": g_out_conv.reshape(1, CONV_CH),
        "w_out": w_out.astype(BF16), "g_norm_ffn": g_norm_ffn.reshape(1, d),
        "b_router": br.reshape(1, LANES),
    }


def _mod_table(mod_p, mod_s, t_new):
    return jnp.concatenate([jnp.broadcast_to(mod_p, (TOK_TILE, D_MODEL)), jnp.repeat(mod_s, t_new, axis=0)], axis=0)


def kernel(x_prompt, x_sample, cache_ckv, cache_kpe, state_conv, page_table, c_prompt, c_sample, w_ada, b_ada, g_norm_mix, g_norm_ffn, w_in, g_q_lat, w_q_up, g_q_nope, g_q_rope, g_kv_lat, g_k_rope, w_kv_up, g_k_nope, w_dw, b_dw, g_conv_ln, b_conv_ln, g_out_attn, g_out_conv, w_out, w_router, b_router, w_exp_up, b_exp_up, w_exp_down, b_exp_down):
    bsz, seq, d = x_prompt.shape
    n_seq, t_new = x_sample.shape[:2]
    depth = w_ada.shape[0]
    assert bsz == 1 and depth == 1 and t_new == 4 and d == D_MODEL
    n_p = bsz * seq
    n_s = n_seq * t_new
    n = n_p + n_s
    past = page_table.shape[1] * cache_ckv.shape[2]
    l = 0

    pos = jnp.concatenate([jnp.arange(seq, dtype=I32), jnp.tile(past + jnp.arange(t_new, dtype=I32), n_seq)])
    p = _prepare(pos, w_in[l], g_norm_mix[l], g_q_lat[l], w_q_up[l], g_q_nope[l], g_q_rope[l], g_kv_lat[l],
                 g_k_rope[l], w_kv_up[l], g_k_nope[l], g_conv_ln[l], b_conv_ln[l], g_out_attn[l], g_out_conv[l],
                 w_out[l], g_norm_ffn[l], w_router[l], b_router[l])

    n_c = 1 + n_seq
    c_all = jnp.concatenate([c_prompt, c_sample, jnp.zeros((-n_c % 8, d), F32)], axis=0)
    mod = _adaln(c_all, w_ada[l], b_ada[l])
    tabs = [_mod_table(mod[0:1, j * d:(j + 1) * d], mod[1:n_c, j * d:(j + 1) * d], t_new) for j in range(6)]
    sh_m, sc_m, gt_m, sh_f, sc_f, gt_f = tabs

    x_p = x_prompt.reshape(n_p, d)
    x_s = x_sample.reshape(n_s, d)
    q_all, ckv_all, kpe_all, u_all, k_all, v_all = _mixer_inputs(x_p, x_s, sh_m, sc_m, p)

    attn_p = _prompt_attention(q_all, k_all, v_all, seq)
    ckv_s = ckv_all[n_p:].reshape(n_seq, t_new, KV_LORA)
    kpe_s = kpe_all[n_p:].reshape(n_seq, t_new, QK_ROPE)
    ckv_new_pad = jnp.pad(ckv_s, ((0, 0), (0, LANES - t_new), (0, 0)))
    kpe_new_t = jnp.swapaxes(jnp.pad(kpe_s, ((0, 0), (0, LANES - t_new), (0, 0))), 1, 2)
    q_s = q_all[n_p:].reshape(n_seq, t_new, N_HEADS * HEAD_PAD)
    attn_s = _sample_attention(page_table, q_s, ckv_new_pad, kpe_new_t, cache_ckv,
                               jnp.swapaxes(cache_kpe, 2, 3), p)

    conv_p = _prompt_conv(u_all, w_dw[l], b_dw[l].reshape(1, CONV_CH))
    u_s = u_all[n_p:].reshape(n_seq, t_new, CONV_CH)
    u_ext_s = jnp.concatenate([state_conv[l], u_s], axis=1)
    ext = CONV_W - 1 + t_new
    ext_pad = -ext % 8
    w_taps = jnp.stack([jnp.pad(w_dw[l], ((t, t_new - 1 - t + ext_pad), (0, 0))) for t in range(t_new)])
    conv_s = _sample_conv(jnp.pad(u_ext_s, ((0, 0), (0, ext_pad), (0, 0))), w_taps, b_dw[l].reshape(1, CONV_CH))
    conv_s = jnp.transpose(conv_s, (1, 0, 2)).reshape(n_s, CONV_CH)

    y_all, f_all, top_i, top_w = _merge_router(x_p, x_s, attn_p, attn_s.reshape(n_s, ATTN_WIDTH), conv_p, conv_s,
                                               gt_m, sh_f, sc_f, p)

    n_tiles = -(-(n * TOP_K + N_EXPERTS * (FFN_TILE - 1)) // FFN_TILE)
    n_slots = n_tiles * FFN_TILE
    pos_tok, meta = _route(top_i, n_tiles)
    slot = jnp.transpose(pos_tok[:, :TOP_K])
    piece_idx = (slot[:, None, :] + (jnp.arange(SC_PIECES, dtype=I32) * n_slots)[None, :, None]).reshape(-1)
    x_sorted = _sc_scatter_rows(f_all.reshape(SC_PIECES * n, SC_ROW), piece_idx, SC_PIECES * n_slots)
    h_sorted = _expert_ffn(meta[:n_tiles, 0], meta[0:1, 1], x_sorted.reshape(SC_PIECES, n_slots, SC_ROW),
                           w_exp_up[l], b_exp_up[l], w_exp_down[l], b_exp_down[l])
    gathered = _sc_gather_rows(h_sorted.reshape(SC_PIECES * n_slots, SC_ROW), piece_idx)
    y_p, y_s = _combine(y_all, gathered.reshape(TOP_K, SC_PIECES, n, SC_ROW), top_w, gt_f, n_p)
    y_p = y_p.reshape(bsz, seq, d)
    y_s = y_s.reshape(n_seq, t_new, d)
    ckv_prompt = ckv_all[:n_p].reshape(1, bsz, seq, KV_LORA)
    kpe_prompt = kpe_all[:n_p].reshape(1, bsz, seq, QK_ROPE)
    conv_prompt = u_all[n_p - (CONV_W - 1):n_p].reshape(1, bsz, CONV_W - 1, CONV_CH)
    ckv_sample = ckv_s[None]
    kpe_sample = kpe_s[None]
    conv_sample = u_ext_s[:, t_new:][None]
    return (y_p, y_s, ckv_prompt, kpe_prompt, conv_prompt, ckv_sample, kpe_sample, conv_sample)
```

```python
import functools

import numpy as np
import jax
import jax.numpy as jnp
from jax import lax
from jax.experimental import pallas as pl
from jax.experimental.pallas import tpu as pltpu
from jax.experimental.pallas import tpu_sc as plsc

F32 = jnp.float32
BF16 = jnp.bfloat16
I32 = jnp.int32
HIGHEST = lax.Precision.HIGHEST

D_MODEL = 1024
N_HEADS = 8
QK_NOPE = 64
QK_ROPE = 32
V_DIM = 64
Q_LORA = 256
KV_LORA = 128
ATTN_WIDTH = N_HEADS * V_DIM
CONV_CH = D_MODEL - ATTN_WIDTH
CONV_W = 31
N_EXPERTS = 32
TOP_K = 4
D_FF = D_MODEL
SWIGLU_LIMIT = 7.0
SWIGLU_ALPHA = 1.702
EPS = 1e-6
NEG_INF = -1e30
ROPE_THETA = 10000.0
SM_SCALE = (QK_NOPE + QK_ROPE) ** -0.5
LOG2E = 1.4426950408889634
Q_SCALE = SM_SCALE * LOG2E

LANES = 128
SUBLANES = 8
HEAD_PAD = LANES
ROPE_LO = QK_NOPE
ROPE_HALF = QK_ROPE // 2

TOK_TILE = 256
FA_TQ = 1024
FA_TK = FA_TQ
FA_HEADS = 8
PAGES_PER_STEP = 64
FFN_TILE = 512
ROUTE_TILE = 512
SC_ROW = 256
PACKED_WIDTH = D_MODEL // 2
SC_PIECES = PACKED_WIDTH // SC_ROW
SC_WINDOW = 128
VMEM_LIMIT = 56 * 1024 * 1024


def _cparams(sem, vmem=None):
    return pltpu.CompilerParams(dimension_semantics=sem, vmem_limit_bytes=vmem)


def _rsqrt_mean(x, n):
    return lax.rsqrt(jnp.sum(x * x, axis=-1, keepdims=True) * (1.0 / n) + EPS)


def _pack_rows(x):
    half = x.shape[1] // 2
    hi = lax.bitcast_convert_type(x[:, :half].astype(BF16).astype(F32), I32)
    lo = lax.bitcast_convert_type(x[:, half:].astype(BF16).astype(F32), I32)
    return hi | lax.shift_right_logical(lo, jnp.full(lo.shape, 16, I32))


def _unpack_rows(w):
    hi = lax.bitcast_convert_type(w & jnp.int32(-65536), F32)
    lo = lax.bitcast_convert_type(lax.shift_left(w, jnp.full(w.shape, 16, I32)), F32)
    return jnp.concatenate([hi, lo], axis=1)


def _ada_kernel(c_ref, w_ref, b_ref, o_ref):
    c = c_ref[...]
    s = c * jax.nn.sigmoid(c)
    o_ref[...] = jnp.dot(s, w_ref[...], precision=HIGHEST, preferred_element_type=F32) + b_ref[...]


def _adaln(c_all, w_ada, b_ada):
    rows = c_all.shape[0]
    n_out = w_ada.shape[1]
    return pl.pallas_call(
        _ada_kernel,
        grid=(n_out // D_MODEL,),
        in_specs=[pl.BlockSpec((rows, D_MODEL), lambda j: (0, 0)),
                  pl.BlockSpec((D_MODEL, D_MODEL), lambda j: (0, j)),
                  pl.BlockSpec((1, D_MODEL), lambda j: (0, j))],
        out_specs=pl.BlockSpec((rows, D_MODEL), lambda j: (0, j)),
        out_shape=jax.ShapeDtypeStruct((rows, n_out), F32),
        compiler_params=_cparams(("arbitrary",)),
        name="adaln",
    )(c_all, w_ada, b_ada.reshape(1, n_out))


def _group_norm_rope(x, m_grp, gain, cos_t, sin_t, first_half):
    ms = jnp.dot((x * x).astype(BF16), m_grp, preferred_element_type=F32)
    xn = x * lax.rsqrt(ms + EPS) * gain
    swapped = jnp.where(first_half, pltpu.roll(xn, LANES - ROPE_HALF, 1), pltpu.roll(xn, ROPE_HALF, 1))
    return xn * cos_t + swapped * sin_t


def _pick_rows(n_prompt_tiles, prompt_ref, decode_ref):
    return jnp.where(pl.program_id(0) < n_prompt_tiles, prompt_ref[...], decode_ref[...])


def _mix_kernel(n_prompt_tiles, xp_ref, xs_ref, sh_ref, sc_ref, gmix_ref, win_ref, gql_ref, wq_ref, gq_ref, m_ref,
                cos_ref, sin_ref, gkv_ref, gk_ref, gkpe_ref, wk_ref, wv_ref, vone_ref,
                q_out, ckv_out, kpe_out, u_out, k_out, v_out):
    x = _pick_rows(n_prompt_tiles, xp_ref, xs_ref)
    h = x * _rsqrt_mean(x, D_MODEL) * gmix_ref[...]
    h = h * (1.0 + sc_ref[...]) + sh_ref[...]
    proj = jnp.dot(h.astype(BF16), win_ref[...], preferred_element_type=F32)
    q_lat = proj[:, :Q_LORA]
    ckv_raw = proj[:, Q_LORA:Q_LORA + KV_LORA]
    kpe_blk = proj[:, Q_LORA + KV_LORA:Q_LORA + KV_LORA + LANES]
    glu_lo = Q_LORA + KV_LORA + LANES
    u_out[...] = proj[:, glu_lo:glu_lo + CONV_CH] * jax.nn.sigmoid(proj[:, glu_lo + CONV_CH:glu_lo + 2 * CONV_CH])

    m_grp = m_ref[...]
    cos_t = cos_ref[...]
    sin_t = sin_ref[...]
    lane = lax.broadcasted_iota(I32, (1, LANES), 1)
    first_half = lane < ROPE_LO + ROPE_HALF

    q_lat_n = q_lat * _rsqrt_mean(q_lat, Q_LORA) * gql_ref[...]
    q = jnp.dot(q_lat_n.astype(BF16), wq_ref[...], preferred_element_type=F32)
    gq = gq_ref[...]
    for hd in range(N_HEADS):
        qh = _group_norm_rope(q[:, hd * HEAD_PAD:(hd + 1) * HEAD_PAD], m_grp, gq, cos_t, sin_t, first_half)
        q_out[:, hd * HEAD_PAD:(hd + 1) * HEAD_PAD] = (qh * Q_SCALE).astype(BF16)

    ckv_n = ckv_raw * _rsqrt_mean(ckv_raw, KV_LORA) * gkv_ref[...]
    ckv_out[...] = ckv_n
    kpe_r = _group_norm_rope(kpe_blk, m_grp, gkpe_ref[...], cos_t, sin_t, first_half)
    kpe_out[...] = kpe_r[:, ROPE_LO:ROPE_LO + QK_ROPE]

    ckv_b = ckv_n.astype(BF16)
    kexp = jnp.dot(ckv_b, wk_ref[...], preferred_element_type=F32)
    gk = gk_ref[...]
    for hd in range(N_HEADS):
        kh = kexp[:, hd * HEAD_PAD:(hd + 1) * HEAD_PAD]
        ms = jnp.dot((kh * kh).astype(BF16), m_grp, preferred_element_type=F32)
        k_out[:, hd * HEAD_PAD:(hd + 1) * HEAD_PAD] = (kh * lax.rsqrt(ms + EPS) * gk + kpe_r).astype(BF16)
    v_out[...] = (jnp.dot(ckv_b, wv_ref[...], preferred_element_type=F32) + vone_ref[...]).astype(BF16)


def _mod_index(n_prompt_tiles):
    return lambda i: (jnp.where(i < n_prompt_tiles, 0, i - n_prompt_tiles + 1), 0)


def _prompt_index(n_prompt_tiles):
    return lambda i: (jnp.minimum(i, n_prompt_tiles - 1), 0)


def _decode_index(n_prompt_tiles):
    return lambda i: (jnp.maximum(i - n_prompt_tiles, 0), 0)


def _mixer_inputs(x_p, x_s, sh, sc, p):
    n_prompt_tiles = x_p.shape[0] // TOK_TILE
    n = x_p.shape[0] + x_s.shape[0]
    tm = TOK_TILE
    const = lambda i: (0, 0)
    row = lambda i: (i, 0)
    mod = _mod_index(n_prompt_tiles)
    hw = N_HEADS * HEAD_PAD
    in_cols = p["w_in"].shape[1]
    return pl.pallas_call(
        functools.partial(_mix_kernel, n_prompt_tiles),
        grid=(n // tm,),
        in_specs=[pl.BlockSpec((tm, D_MODEL), _prompt_index(n_prompt_tiles)),
                  pl.BlockSpec((tm, D_MODEL), _decode_index(n_prompt_tiles)),
                  pl.BlockSpec((tm, D_MODEL), mod),
                  pl.BlockSpec((tm, D_MODEL), mod),
                  pl.BlockSpec((1, D_MODEL), const),
                  pl.BlockSpec((D_MODEL, in_cols), const),
                  pl.BlockSpec((1, Q_LORA), const),
                  pl.BlockSpec((Q_LORA, hw), const),
                  pl.BlockSpec((1, LANES), const),
                  pl.BlockSpec((LANES, LANES), const),
                  pl.BlockSpec((tm, LANES), row),
                  pl.BlockSpec((tm, LANES), row),
                  pl.BlockSpec((1, KV_LORA), const),
                  pl.BlockSpec((1, LANES), const),
                  pl.BlockSpec((1, LANES), const),
                  pl.BlockSpec((KV_LORA, hw), const),
                  pl.BlockSpec((KV_LORA, hw), const),
                  pl.BlockSpec((1, hw), const)],
        out_specs=[pl.BlockSpec((tm, hw), row),
                   pl.BlockSpec((tm, KV_LORA), row),
                   pl.BlockSpec((tm, QK_ROPE), row),
                   pl.BlockSpec((tm, CONV_CH), row),
                   pl.BlockSpec((tm, hw), row),
                   pl.BlockSpec((tm, hw), row)],
        out_shape=[jax.ShapeDtypeStruct((n, hw), BF16),
                   jax.ShapeDtypeStruct((n, KV_LORA), F32),
                   jax.ShapeDtypeStruct((n, QK_ROPE), F32),
                   jax.ShapeDtypeStruct((n, CONV_CH), F32),
                   jax.ShapeDtypeStruct((n, hw), BF16),
                   jax.ShapeDtypeStruct((n, hw), BF16)],
        compiler_params=_cparams(("parallel",), VMEM_LIMIT),
        name="mixer_inputs",
    )(x_p, x_s, sh, sc, p["g_norm_mix"], p["w_in"], p["g_q_lat"], p["w_q_up"], p["gain_q"], p["m_grp"],
      p["cos_t"], p["sin_t"], p["g_kv_lat"], p["gain_k"], p["gain_kpe"], p["w_k"], p["w_v_tiles"], p["v_ones"])


def _fa_kernel(qt_ref, kt_ref, q_ref, k_ref, v_ref, o_ref, m_sc, acc_sc):
    t = pl.program_id(1)
    qi = qt_ref[t]
    ki = kt_ref[t]
    last_k = (qi + 1) * (FA_TQ // FA_TK) - 1

    @pl.when(ki == 0)
    def _():
        m_sc[...] = jnp.full(m_sc.shape, NEG_INF, F32)
        acc_sc[...] = jnp.zeros(acc_sc.shape, F32)

    def update(hh, r0, nr, k0, nk, masked):
        q = q_ref[pl.ds(r0, nr), hh * HEAD_PAD:(hh + 1) * HEAD_PAD]
        k = k_ref[pl.ds(k0, nk), hh * HEAD_PAD:(hh + 1) * HEAD_PAD]
        v = v_ref[pl.ds(k0, nk), hh * HEAD_PAD:(hh + 1) * HEAD_PAD]
        s = lax.dot_general(q, k, (((1,), (1,)), ((), ())), preferred_element_type=F32)
        if masked:
            col_minus_row = lax.broadcasted_iota(I32, (nr, nk), 1) - lax.broadcasted_iota(I32, (nr, nk), 0)
            s = jnp.where(col_minus_row <= (qi * FA_TQ + r0) - (ki * FA_TK + k0), s, NEG_INF)
        m_prev = m_sc[hh, pl.ds(r0, nr), :]
        m_new = jnp.maximum(m_prev, jnp.max(s, axis=-1, keepdims=True))
        alpha = jnp.exp2(m_prev - m_new)
        pr = jnp.exp2((s - jnp.concatenate([m_new] * (nk // LANES), axis=1)).astype(BF16))
        acc_sc[hh, pl.ds(r0, nr), :] = (alpha * acc_sc[hh, pl.ds(r0, nr), :]
                                        + jnp.dot(pr, v, preferred_element_type=F32))
        m_sc[hh, pl.ds(r0, nr), :] = m_new

    @pl.when(ki < qi)
    def _():
        for hh in range(FA_HEADS):
            update(hh, 0, FA_TQ, 0, FA_TK, False)

    @pl.when(ki == qi)
    def _():
        half = FA_TQ // 2
        for hh in range(FA_HEADS):
            update(hh, 0, half, 0, half, True)
            update(hh, half, half, 0, FA_TK, True)

    @pl.when(ki == last_k)
    def _():
        for hh in range(FA_HEADS):
            acc = acc_sc[hh]
            o_ref[:, hh * V_DIM:(hh + 1) * V_DIM] = acc[:, :V_DIM] / acc[:, V_DIM:V_DIM + 1]


def _prompt_attention(q_all, k_all, v_all, seq):
    nq = seq // FA_TQ
    ratio = FA_TQ // FA_TK
    qt, kt = [], []
    for qi in range(nq):
        for ki in range((qi + 1) * ratio):
            qt.append(qi)
            kt.append(ki)
    qt = jnp.asarray(np.array(qt, np.int32))
    kt = jnp.asarray(np.array(kt, np.int32))
    n_pairs = int(qt.shape[0])
    grid_spec = pltpu.PrefetchScalarGridSpec(
        num_scalar_prefetch=2,
        grid=(N_HEADS // FA_HEADS, n_pairs),
        in_specs=[pl.BlockSpec((FA_TQ, FA_HEADS * HEAD_PAD), lambda hp, t, qt, kt: (qt[t], hp)),
                  pl.BlockSpec((FA_TK, FA_HEADS * HEAD_PAD), lambda hp, t, qt, kt: (kt[t], hp)),
                  pl.BlockSpec((FA_TK, FA_HEADS * HEAD_PAD), lambda hp, t, qt, kt: (kt[t], hp))],
        out_specs=pl.BlockSpec((FA_TQ, FA_HEADS * V_DIM), lambda hp, t, qt, kt: (qt[t], hp)),
        scratch_shapes=[pltpu.VMEM((FA_HEADS, FA_TQ, LANES), F32),
                        pltpu.VMEM((FA_HEADS, FA_TQ, HEAD_PAD), F32)],
    )
    return pl.pallas_call(
        _fa_kernel,
        grid_spec=grid_spec,
        out_shape=jax.ShapeDtypeStruct((seq, ATTN_WIDTH), F32),
        compiler_params=_cparams(("parallel", "arbitrary"), VMEM_LIMIT),
        name="prompt_attention",
    )(qt, kt, q_all, k_all, v_all)


def _sattn_kernel(pt_ref, q_ref, wabs_ref, wkt_ref, wv_ref, ckvn_ref, kpen_ref, ckv_hbm, kpe_hbm,
                  o_ref, m_sc, l_sc, acc_sc, qa_sc, wall_sc, ckv_buf, kpe_buf, sem):
    kb = pl.program_id(1)
    n_kb = pl.num_programs(1)
    step = pl.program_id(0) * n_kb + kb
    n_steps = pl.num_programs(0) * n_kb
    slot = step % 2
    rows = 4 * N_HEADS

    def start_fetch(step_i, slot_i):
        for j in range(PAGES_PER_STEP):
            page = pt_ref[step_i * PAGES_PER_STEP + j]
            pltpu.make_async_copy(ckv_hbm.at[0, page], ckv_buf.at[slot_i, j], sem.at[0, slot_i]).start()
            pltpu.make_async_copy(kpe_hbm.at[0, page], kpe_buf.at[slot_i, j], sem.at[1, slot_i]).start()

    @pl.when(step == 0)
    def _():
        start_fetch(step, slot)

    @pl.when(step + 1 < n_steps)
    def _():
        start_fetch(step + 1, 1 - slot)

    @pl.when(kb == 0)
    def _():
        m_sc[...] = jnp.full(m_sc.shape, NEG_INF, F32)
        l_sc[...] = jnp.zeros(l_sc.shape, F32)
        acc_sc[...] = jnp.zeros(acc_sc.shape, F32)
        q4 = q_ref[...].astype(F32)
        head_of_lane = lax.broadcasted_iota(I32, (N_HEADS, N_HEADS * HEAD_PAD), 1) // HEAD_PAD
        head_of_row = lax.broadcasted_iota(I32, (N_HEADS, N_HEADS * HEAD_PAD), 0)
        own = head_of_lane == head_of_row
        qbd = jnp.concatenate(
            [jnp.where(own, jnp.broadcast_to(q4[qq:qq + 1, :], own.shape), 0.0) for qq in range(4)], axis=0)
        qa = jnp.dot(qbd.astype(BF16), wabs_ref[...], preferred_element_type=F32).astype(BF16)
        qa_sc[...] = qa
        wall_sc[pl.ds(0, N_HEADS * QK_NOPE), :] = wkt_ref[...]
        wall_sc[pl.ds(N_HEADS * QK_NOPE, rows), :] = qa[:, :KV_LORA]

    def attend(state, ckv_b, kpe_t, mask):
        m_prev, l_prev, acc_prev = state
        nt = (((1,), (1,)), ((), ()))
        kn_all = lax.dot_general(wall_sc[...], ckv_b, nt, preferred_element_type=F32)
        keys = kn_all.shape[1]
        kn_t = kn_all[:N_HEADS * QK_NOPE]
        ss = jnp.sum((kn_t * kn_t).reshape(N_HEADS, QK_NOPE, keys), axis=1)
        r8 = lax.rsqrt(ss * (1.0 / QK_NOPE) + EPS)
        rope = jnp.dot(qa_sc[:, KV_LORA:KV_LORA + QK_ROPE], kpe_t, preferred_element_type=F32)
        s = kn_all[N_HEADS * QK_NOPE:] * jnp.concatenate([r8] * 4, axis=0) + rope
        if mask is not None:
            s = jnp.where(mask, s, NEG_INF)
        m_new = jnp.maximum(m_prev, jnp.max(s, axis=-1, keepdims=True))
        alpha = jnp.exp2(m_prev - m_new)
        pr = jnp.exp2(s - m_new)
        l_new = alpha * l_prev + jnp.sum(pr, axis=-1, keepdims=True)
        acc_new = alpha * acc_prev + jnp.dot(pr.astype(BF16), ckv_b, preferred_element_type=F32)
        return m_new, l_new, acc_new

    pltpu.make_async_copy(ckv_hbm.at[0, pl.ds(0, PAGES_PER_STEP)], ckv_buf.at[slot], sem.at[0, slot]).wait()
    pltpu.make_async_copy(kpe_hbm.at[0, pl.ds(0, PAGES_PER_STEP)], kpe_buf.at[slot], sem.at[1, slot]).wait()

    page = ckv_buf.shape[2]
    ckv_b = ckv_buf[slot].reshape(PAGES_PER_STEP * page, KV_LORA).astype(BF16)
    kpe_t = jnp.concatenate([kpe_buf[slot, j].astype(BF16) for j in range(PAGES_PER_STEP)], axis=1)
    state = attend((m_sc[...], l_sc[...], acc_sc[...]), ckv_b, kpe_t, None)
    m_sc[...], l_sc[...], acc_sc[...] = state

    @pl.when(kb == n_kb - 1)
    def _():
        key = lax.broadcasted_iota(I32, (rows, LANES), 1)
        qry = lax.broadcasted_iota(I32, (rows, LANES), 0) // N_HEADS
        _, l_fin, acc_fin = attend(state, ckvn_ref[...].astype(BF16), kpen_ref[...].astype(BF16), key <= qry)
        lat = acc_fin / l_fin
        o_all = jnp.dot(lat.astype(BF16), wv_ref[...], preferred_element_type=F32)
        head_of_col = lax.broadcasted_iota(I32, (rows, ATTN_WIDTH), 1) // V_DIM
        head_of_row = lax.broadcasted_iota(I32, (rows, ATTN_WIDTH), 0) % N_HEADS
        o_own = jnp.where(head_of_col == head_of_row, o_all, 0.0)
        o_ref[...] = jnp.sum(o_own.reshape(4, N_HEADS, ATTN_WIDTH), axis=1)


def _sample_attention(page_table, q_s, ckv_new_pad, kpe_new_t, cache_ckv, cache_kpe_t, p):
    n_seq, n_pages = page_table.shape
    page = cache_ckv.shape[2]
    n_kb = n_pages // PAGES_PER_STEP
    hw = N_HEADS * HEAD_PAD
    assert n_pages % PAGES_PER_STEP == 0
    per_seq3 = lambda b, kb, pt: (b, 0, 0)
    const = lambda b, kb, pt: (0, 0)
    in_specs = [pl.BlockSpec((None, 4, hw), per_seq3),
                pl.BlockSpec((hw, 2 * LANES), const),
                pl.BlockSpec((N_HEADS * QK_NOPE, KV_LORA), const),
                pl.BlockSpec((KV_LORA, ATTN_WIDTH), const),
                pl.BlockSpec((None, LANES, KV_LORA), per_seq3),
                pl.BlockSpec((None, QK_ROPE, LANES), per_seq3),
                pl.BlockSpec(memory_space=pl.ANY),
                pl.BlockSpec(memory_space=pl.ANY)]
    grid_spec = pltpu.PrefetchScalarGridSpec(
        num_scalar_prefetch=1,
        grid=(n_seq, n_kb),
        in_specs=in_specs,
        out_specs=pl.BlockSpec((None, 4, ATTN_WIDTH), per_seq3),
        scratch_shapes=[pltpu.VMEM((4 * N_HEADS, 1), F32),
                        pltpu.VMEM((4 * N_HEADS, 1), F32),
                        pltpu.VMEM((4 * N_HEADS, KV_LORA), F32),
                        pltpu.VMEM((4 * N_HEADS, 2 * LANES), BF16),
                        pltpu.VMEM((N_HEADS * QK_NOPE + 4 * N_HEADS, KV_LORA), BF16),
                        pltpu.VMEM((2, PAGES_PER_STEP, page, KV_LORA), F32),
                        pltpu.VMEM((2, PAGES_PER_STEP, QK_ROPE, page), F32),
                        pltpu.SemaphoreType.DMA((2, 2))],
    )
    return pl.pallas_call(
        _sattn_kernel,
        grid_spec=grid_spec,
        out_shape=jax.ShapeDtypeStruct((n_seq, 4, ATTN_WIDTH), F32),
        compiler_params=_cparams(("arbitrary", "arbitrary"), VMEM_LIMIT),
        name="decode_attention",
    )(page_table.reshape(-1), q_s, p["w_abs"], p["w_kt"], p["w_v"], ckv_new_pad, kpe_new_t, cache_ckv, cache_kpe_t)


CONV_HALO = 32
CONV_ROWS = 64


def _conv_kernel(halo_ref, u_ref, w_ref, b_ref, o_ref, ext_sc, sh_sc):
    i = pl.program_id(0)
    tm = u_ref.shape[0]
    ext_sc[pl.ds(0, CONV_HALO), :] = jnp.where(i == 0, 0.0, halo_ref[...])
    ext_sc[pl.ds(CONV_HALO, tm), :] = u_ref[...]
    first = CONV_HALO - (CONV_W - 1)
    span = tm + CONV_HALO - SUBLANES
    for sft in range(1, SUBLANES):
        sh_sc[sft - 1] = ext_sc[pl.ds(sft, span), :]
    for rc in range(tm // CONV_ROWS):
        acc = jnp.broadcast_to(b_ref[...], (CONV_ROWS, CONV_CH))
        for j in range(CONV_W):
            a8, sft = divmod(first + j, SUBLANES)
            src = ext_sc if sft == 0 else sh_sc.at[sft - 1]
            acc = acc + src[pl.ds(rc * CONV_ROWS + SUBLANES * a8, CONV_ROWS), :] * w_ref[j:j + 1, :]
        o_ref[pl.ds(rc * CONV_ROWS, CONV_ROWS), :] = acc


def _prompt_conv(u_all, w_dw, b_dw):
    n = u_all.shape[0]
    tm = TOK_TILE
    per = tm // CONV_HALO
    return pl.pallas_call(
        _conv_kernel,
        grid=(n // tm,),
        in_specs=[pl.BlockSpec((CONV_HALO, CONV_CH), lambda i: (jnp.maximum(i * per - 1, 0), 0)),
                  pl.BlockSpec((tm, CONV_CH), lambda i: (i, 0)),
                  pl.BlockSpec((CONV_W, CONV_CH), lambda i: (0, 0)),
                  pl.BlockSpec((1, CONV_CH), lambda i: (0, 0))],
        out_specs=pl.BlockSpec((tm, CONV_CH), lambda i: (i, 0)),
        out_shape=jax.ShapeDtypeStruct((n, CONV_CH), F32),
        scratch_shapes=[pltpu.VMEM((CONV_HALO + tm, CONV_CH), F32),
                        pltpu.VMEM((SUBLANES - 1, CONV_HALO + tm - SUBLANES, CONV_CH), F32)],
        compiler_params=_cparams(("parallel",)),
        name="prompt_conv",
    )(u_all, u_all, w_dw, b_dw)


def _sconv_kernel(u_ref, wt_ref, b_ref, o_ref):
    u = u_ref[...]
    for t in range(o_ref.shape[0]):
        o_ref[t] = jnp.sum(u * wt_ref[t][None, :, :], axis=1) + b_ref[...]


def _sample_conv(u_ext, w_taps, b_dw):
    n_seq, ext, _ = u_ext.shape
    t_new = w_taps.shape[0]
    sb = 8
    return pl.pallas_call(
        _sconv_kernel,
        grid=(n_seq // sb,),
        in_specs=[pl.BlockSpec((sb, ext, CONV_CH), lambda i: (i, 0, 0)),
                  pl.BlockSpec((t_new, ext, CONV_CH), lambda i: (0, 0, 0)),
                  pl.BlockSpec((1, CONV_CH), lambda i: (0, 0))],
        out_specs=pl.BlockSpec((t_new, sb, CONV_CH), lambda i: (0, i, 0)),
        out_shape=jax.ShapeDtypeStruct((t_new, n_seq, CONV_CH), F32),
        compiler_params=_cparams(("parallel",)),
        name="decode_conv",
    )(u_ext, w_taps, b_dw)


def _lane_pack(cols, dtype):
    lane = lax.broadcasted_iota(I32, (cols[0].shape[0], LANES), 1)
    out = jnp.zeros((cols[0].shape[0], LANES), dtype)
    for j, c in enumerate(cols):
        out = jnp.where(lane == j, c.astype(dtype), out)
    return out


def _merge_kernel(n_prompt_tiles, xp_ref, xs_ref, ap_ref, as_ref, cp_ref, cs_ref, gate_ref, shf_ref, scf_ref,
                  gln_ref, bln_ref, ga_ref, gc_ref, wout_ref, gffn_ref, wrh_ref, wrl_ref, br_ref,
                  y_out, f_out, ti_out, tw_out):
    yc = _pick_rows(n_prompt_tiles, cp_ref, cs_ref)
    mu = jnp.mean(yc, axis=-1, keepdims=True)
    xc = yc - mu
    var = jnp.mean(xc * xc, axis=-1, keepdims=True)
    ln = xc * lax.rsqrt(var + EPS) * gln_ref[...] + bln_ref[...]
    conv = ln * jax.nn.sigmoid(ln)
    attn = _pick_rows(n_prompt_tiles, ap_ref, as_ref)
    a_n = attn * _rsqrt_mean(attn, ATTN_WIDTH) * ga_ref[...]
    c_n = conv * _rsqrt_mean(conv, CONV_CH) * gc_ref[...]
    m = (jnp.dot(a_n.astype(BF16), wout_ref[:ATTN_WIDTH, :], preferred_element_type=F32)
         + jnp.dot(c_n.astype(BF16), wout_ref[ATTN_WIDTH:, :], preferred_element_type=F32))
    y = _pick_rows(n_prompt_tiles, xp_ref, xs_ref) + gate_ref[...] * m
    y_out[...] = y
    f = y * _rsqrt_mean(y, D_MODEL) * gffn_ref[...]
    f = f * (1.0 + scf_ref[...]) + shf_ref[...]
    f_words = _pack_rows(f)
    for j in range(SC_PIECES):
        f_out[j] = f_words[:, j * SC_ROW:(j + 1) * SC_ROW]
    f_hi = f.astype(BF16)
    f_lo = (f - f_hi.astype(F32)).astype(BF16)
    logits = (jnp.dot(f_hi, wrh_ref[...], preferred_element_type=F32)
              + jnp.dot(f_hi, wrl_ref[...], preferred_element_type=F32)
              + jnp.dot(f_lo, wrh_ref[...], preferred_element_type=F32)) + br_ref[...]
    lane = lax.broadcasted_iota(I32, logits.shape, 1)
    vals, idxs = [], []
    for _ in range(TOP_K):
        mx = jnp.max(logits, axis=-1, keepdims=True)
        ix = jnp.min(jnp.where(logits == mx, lane, LANES), axis=-1, keepdims=True)
        vals.append(mx)
        idxs.append(ix)
        logits = jnp.where(lane == ix, NEG_INF * 4.0, logits)
    exps = [jnp.exp(v - vals[0]) for v in vals]
    tot = exps[0] + exps[1] + exps[2] + exps[3]
    ti_out[...] = _lane_pack(idxs, I32)
    tw_out[...] = _lane_pack([e / tot for e in exps], F32)


def _merge_router(x_p, x_s, attn_p, attn_s, conv_p, conv_s, gate, shf, scf, p):
    n_prompt_tiles = x_p.shape[0] // TOK_TILE
    n = x_p.shape[0] + x_s.shape[0]
    tm = TOK_TILE
    const = lambda i: (0, 0)
    row = lambda i: (i, 0)
    mod = _mod_index(n_prompt_tiles)
    from_p = _prompt_index(n_prompt_tiles)
    from_s = _decode_index(n_prompt_tiles)
    return pl.pallas_call(
        functools.partial(_merge_kernel, n_prompt_tiles),
        grid=(n // tm,),
        in_specs=[pl.BlockSpec((tm, D_MODEL), from_p),
                  pl.BlockSpec((tm, D_MODEL), from_s),
                  pl.BlockSpec((tm, ATTN_WIDTH), from_p),
                  pl.BlockSpec((tm, ATTN_WIDTH), from_s),
                  pl.BlockSpec((tm, CONV_CH), from_p),
                  pl.BlockSpec((tm, CONV_CH), from_s),
                  pl.BlockSpec((tm, D_MODEL), mod),
                  pl.BlockSpec((tm, D_MODEL), mod),
                  pl.BlockSpec((tm, D_MODEL), mod),
                  pl.BlockSpec((1, CONV_CH), const),
                  pl.BlockSpec((1, CONV_CH), const),
                  pl.BlockSpec((1, ATTN_WIDTH), const),
                  pl.BlockSpec((1, CONV_CH), const),
                  pl.BlockSpec((D_MODEL, D_MODEL), const),
                  pl.BlockSpec((1, D_MODEL), const),
                  pl.BlockSpec((D_MODEL, LANES), const),
                  pl.BlockSpec((D_MODEL, LANES), const),
                  pl.BlockSpec((1, LANES), const)],
        out_specs=[pl.BlockSpec((tm, D_MODEL), row),
                   pl.BlockSpec((SC_PIECES, tm, SC_ROW), lambda i: (0, i, 0)),
                   pl.BlockSpec((tm, LANES), row),
                   pl.BlockSpec((tm, LANES), row)],
        out_shape=[jax.ShapeDtypeStruct((n, D_MODEL), F32),
                   jax.ShapeDtypeStruct((SC_PIECES, n, SC_ROW), I32),
                   jax.ShapeDtypeStruct((n, LANES), I32),
                   jax.ShapeDtypeStruct((n, LANES), F32)],
        compiler_params=_cparams(("parallel",), VMEM_LIMIT),
        name="merge_router",
    )(x_p, x_s, attn_p, attn_s, conv_p, conv_s, gate, shf, scf, p["g_conv_ln"], p["b_conv_ln"], p["g_out_attn"],
      p["g_out_conv"], p["w_out"], p["g_norm_ffn"], p["w_router_hi"], p["w_router_lo"], p["b_router"])


def _select_lane(table, idx_col, lane):
    return jnp.sum(jnp.where(lane == idx_col, table, 0.0), axis=-1, keepdims=True)


def _rank_kernel(ti_ref, rk_out, cnt_out, carry_sc):
    i = pl.program_id(0)

    @pl.when(i == 0)
    def _():
        carry_sc[...] = jnp.zeros(carry_sc.shape, F32)

    ti = ti_ref[...]
    tn = ti.shape[0]
    lane = lax.broadcasted_iota(I32, (tn, LANES), 1)
    sel = jnp.zeros((tn, LANES), F32)
    for k in range(TOP_K):
        sel = sel + (lane == ti[:, k:k + 1]).astype(F32)
    r_i = lax.broadcasted_iota(I32, (tn, tn), 0)
    c_i = lax.broadcasted_iota(I32, (tn, tn), 1)
    below = (c_i < r_i).astype(BF16)
    rank = carry_sc[...] + jnp.dot(below, sel.astype(BF16), preferred_element_type=F32)
    rk_out[...] = _lane_pack([_select_lane(rank, ti[:, k:k + 1], lane) for k in range(TOP_K)], F32)
    carry_sc[...] = carry_sc[...] + jnp.sum(sel, axis=0, keepdims=True)
    cnt_out[...] = jnp.broadcast_to(carry_sc[...], cnt_out.shape)


def _pos_kernel(cnt_ref, ti_ref, rk_ref, pos_out, meta_out):
    cnt = cnt_ref[...]
    padded = jnp.ceil(cnt * (1.0 / FFN_TILE)) * FFN_TILE
    r_i = lax.broadcasted_iota(I32, (LANES, LANES), 0)
    c_i = lax.broadcasted_iota(I32, (LANES, LANES), 1)
    before = (r_i < c_i).astype(F32)
    offs = jnp.dot(padded, before, precision=HIGHEST, preferred_element_type=F32)
    ends = offs + padded
    ti = ti_ref[...]
    tn = ti.shape[0]
    lane = lax.broadcasted_iota(I32, (tn, LANES), 1)
    off_row = offs[0:1, :]
    rk = rk_ref[...]
    pos = [_select_lane(jnp.broadcast_to(off_row, (tn, LANES)), ti[:, k:k + 1], lane) + rk[:, k:k + 1]
           for k in range(TOP_K)]
    pos_out[...] = _lane_pack(pos, F32).astype(I32)

    @pl.when(pl.program_id(0) == 0)
    def _():
        nt = meta_out.shape[0]
        start = (lax.broadcasted_iota(I32, (nt, LANES), 0) * FFN_TILE).astype(F32)
        elane = lax.broadcasted_iota(I32, (nt, LANES), 1)
        done = jnp.where((elane < N_EXPERTS) & (jnp.broadcast_to(ends[0:1, :], (nt, LANES)) <= start), 1.0, 0.0)
        expert = jnp.minimum(jnp.sum(done, axis=-1, keepdims=True), N_EXPERTS - 1.0)
        total = jnp.sum(jnp.where(elane < N_EXPERTS, jnp.broadcast_to(padded[0:1, :], (nt, LANES)), 0.0),
                        axis=-1, keepdims=True)
        meta_out[...] = _lane_pack([expert, total * (1.0 / FFN_TILE)], F32).astype(I32)


def _route(top_i, n_tiles_max):
    n = top_i.shape[0]
    tn = ROUTE_TILE
    rk, cnt = pl.pallas_call(
        _rank_kernel,
        grid=(n // tn,),
        in_specs=[pl.BlockSpec((tn, LANES), lambda i: (i, 0))],
        out_specs=[pl.BlockSpec((tn, LANES), lambda i: (i, 0)),
                   pl.BlockSpec((8, LANES), lambda i: (0, 0))],
        out_shape=[jax.ShapeDtypeStruct((n, LANES), F32),
                   jax.ShapeDtypeStruct((8, LANES), F32)],
        scratch_shapes=[pltpu.VMEM((1, LANES), F32)],
        compiler_params=_cparams(("arbitrary",)),
        name="route_rank",
    )(top_i)
    nt_pad = -(-n_tiles_max // 8) * 8
    pos, meta = pl.pallas_call(
        _pos_kernel,
        grid=(n // tn,),
        in_specs=[pl.BlockSpec((8, LANES), lambda i: (0, 0)),
                  pl.BlockSpec((tn, LANES), lambda i: (i, 0)),
                  pl.BlockSpec((tn, LANES), lambda i: (i, 0))],
        out_specs=[pl.BlockSpec((tn, LANES), lambda i: (i, 0)),
                   pl.BlockSpec((nt_pad, LANES), lambda i: (0, 0))],
        out_shape=[jax.ShapeDtypeStruct((n, LANES), I32),
                   jax.ShapeDtypeStruct((nt_pad, LANES), I32)],
        compiler_params=_cparams(("arbitrary",)),
        name="route_pos",
    )(cnt, top_i, rk)
    return pos, meta


def _sc_scatter_rows(x, idx, n_out):
    n_src = x.shape[0]
    n = idx.shape[0]
    n_src_blk = n_src // SC_WINDOW
    mesh = plsc.VectorSubcoreMesh(core_axis_name="c", subcore_axis_name="s")

    @pl.kernel(out_type=jax.ShapeDtypeStruct((n_out, SC_ROW), x.dtype), mesh=mesh)
    def k(x_hbm, i_hbm, o_hbm):
        def body(x_vmem, i_vmem):
            pltpu.sync_copy(x_vmem, o_hbm.at[i_vmem.at[0]])

        pltpu.emit_pipeline(
            body,
            grid=(n // SC_WINDOW,),
            in_specs=[pl.BlockSpec((SC_WINDOW, SC_ROW), index_map=lambda i: (i % n_src_blk, 0)),
                      pl.BlockSpec((1, SC_WINDOW), index_map=lambda i: (0, i))],
            out_specs=[],
            core_axis_name=("c", "s"),
            dimension_semantics=(pltpu.PARALLEL,),
        )(x_hbm, i_hbm)

    return k(x, idx.reshape(1, n))


def _sc_gather_rows(x, idx):
    n = idx.shape[0]
    mesh = plsc.VectorSubcoreMesh(core_axis_name="c", subcore_axis_name="s")

    @pl.kernel(out_type=jax.ShapeDtypeStruct((n, SC_ROW), x.dtype), mesh=mesh)
    def k(x_hbm, i_hbm, o_hbm):
        def body(i_vmem, o_vmem):
            pltpu.sync_copy(x_hbm.at[i_vmem.at[0]], o_vmem)

        pltpu.emit_pipeline(
            body,
            grid=(n // SC_WINDOW,),
            in_specs=[pl.BlockSpec((1, SC_WINDOW), index_map=lambda i: (0, i))],
            out_specs=[pl.BlockSpec((SC_WINDOW, SC_ROW), index_map=lambda i: (i, 0))],
            core_axis_name=("c", "s"),
            dimension_semantics=(pltpu.PARALLEL,),
        )(i_hbm, o_hbm)

    return k(x, idx.reshape(1, n))


def _ffn_kernel(te_ref, nv_ref, x_ref, wu_ref, bu_ref, wd_ref, bd_ref, o_ref, wu_sc, wd_sc):
    t = pl.program_id(0)
    valid = t < nv_ref[0]
    prev = te_ref[jnp.maximum(t - 1, 0)]
    fresh = jnp.logical_or(t == 0, te_ref[t] != prev)

    @pl.when(jnp.logical_and(valid, fresh))
    def _():
        wu_sc[...] = wu_ref[...].astype(BF16)
        wd_sc[...] = wd_ref[...].astype(BF16)

    @pl.when(valid)
    def _():
        x = _unpack_rows(jnp.concatenate([x_ref[j] for j in range(SC_PIECES)], axis=1))
        z = jnp.dot(x.astype(BF16), wu_sc[...], preferred_element_type=F32) + bu_ref[...]
        zg = jnp.minimum(z[:, :D_FF], SWIGLU_LIMIT)
        zl = jnp.clip(z[:, D_FF:], -SWIGLU_LIMIT, SWIGLU_LIMIT)
        act = zg * jax.nn.sigmoid(SWIGLU_ALPHA * zg) * (zl + 1.0)
        out = _pack_rows(jnp.dot(act.astype(BF16), wd_sc[...], preferred_element_type=F32) + bd_ref[...])
        for j in range(SC_PIECES):
            o_ref[j] = out[:, j * SC_ROW:(j + 1) * SC_ROW]


def _expert_ffn(tile_expert, n_valid, x_sorted, w_up, b_up, w_down, b_down):
    n_slots = x_sorted.shape[1]
    n_tiles = n_slots // FFN_TILE
    xmap = lambda t, te, nv: (0, jnp.minimum(t, nv[0] - 1), 0)
    emap = lambda t, te, nv: (te[t], 0, 0)
    grid_spec = pltpu.PrefetchScalarGridSpec(
        num_scalar_prefetch=2,
        grid=(n_tiles,),
        in_specs=[pl.BlockSpec((SC_PIECES, FFN_TILE, SC_ROW), xmap),
                  pl.BlockSpec((None, D_MODEL, 2 * D_FF), emap),
                  pl.BlockSpec((None, 1, 2 * D_FF), emap),
                  pl.BlockSpec((None, D_FF, D_MODEL), emap),
                  pl.BlockSpec((None, 1, D_MODEL), emap)],
        out_specs=pl.BlockSpec((SC_PIECES, FFN_TILE, SC_ROW), xmap),
        scratch_shapes=[pltpu.VMEM((D_MODEL, 2 * D_FF), BF16),
                        pltpu.VMEM((D_FF, D_MODEL), BF16)],
    )
    return pl.pallas_call(
        _ffn_kernel,
        grid_spec=grid_spec,
        out_shape=jax.ShapeDtypeStruct((SC_PIECES, n_slots, SC_ROW), I32),
        compiler_params=_cparams(("arbitrary",), VMEM_LIMIT),
        name="expert_ffn",
    )(tile_expert, n_valid, x_sorted, w_up, b_up.reshape(N_EXPERTS, 1, 2 * D_FF), w_down,
      b_down.reshape(N_EXPERTS, 1, D_MODEL))


def _combine_kernel(n_prompt_tiles, y_ref, g_ref, tw_ref, gate_ref, op_ref, os_ref):
    tw = tw_ref[...]
    rows = lambda k: _unpack_rows(jnp.concatenate([g_ref[k, j] for j in range(SC_PIECES)], axis=1))
    moe = rows(0) * tw[:, 0:1]
    for k in range(1, TOP_K):
        moe = moe + rows(k) * tw[:, k:k + 1]
    out = y_ref[...] + gate_ref[...] * moe
    i = pl.program_id(0)

    @pl.when(i < n_prompt_tiles)
    def _():
        op_ref[...] = out

    @pl.when(i >= n_prompt_tiles)
    def _():
        os_ref[...] = out


def _combine(y_all, gathered, top_w, gate, n_p):
    n = y_all.shape[0]
    tm = TOK_TILE
    n_prompt_tiles = n_p // tm
    row = lambda i: (i, 0)
    return pl.pallas_call(
        functools.partial(_combine_kernel, n_prompt_tiles),
        grid=(n // tm,),
        in_specs=[pl.BlockSpec((tm, D_MODEL), row),
                  pl.BlockSpec((TOP_K, SC_PIECES, tm, SC_ROW), lambda i: (0, 0, i, 0)),
                  pl.BlockSpec((tm, LANES), row),
                  pl.BlockSpec((tm, D_MODEL), _mod_index(n_prompt_tiles))],
        out_specs=[pl.BlockSpec((tm, D_MODEL), _prompt_index(n_prompt_tiles)),
                   pl.BlockSpec((tm, D_MODEL), _decode_index(n_prompt_tiles))],
        out_shape=[jax.ShapeDtypeStruct((n_p, D_MODEL), F32),
                   jax.ShapeDtypeStruct((n - n_p, D_MODEL), F32)],
        compiler_params=_cparams(("arbitrary",), VMEM_LIMIT),
        name="moe_combine",
    )(y_all, gathered, top_w, gate)


def _head_tiles(nope, rope):
    pad = jnp.zeros(nope.shape[:-1] + (HEAD_PAD - QK_NOPE - QK_ROPE,), nope.dtype)
    t = jnp.concatenate([nope, rope, pad], axis=-1)
    return t.reshape(t.shape[:-2] + (N_HEADS * HEAD_PAD,))


def _prepare(pos, w_in, g_norm_mix, g_q_lat, w_q_up, g_q_nope, g_q_rope, g_kv_lat, g_k_rope, w_kv_up, g_k_nope,
             g_conv_ln, b_conv_ln, g_out_attn, g_out_conv, w_out, g_norm_ffn, w_router, b_router):
    z32 = jnp.zeros((HEAD_PAD - QK_NOPE - QK_ROPE,), F32)
    z64 = jnp.zeros((QK_NOPE,), F32)
    d = D_MODEL
    kpe_cols = w_in[:, Q_LORA + KV_LORA:Q_LORA + KV_LORA + QK_ROPE]
    kpe_tile = jnp.concatenate([jnp.zeros((d, QK_NOPE), F32), kpe_cols, jnp.zeros((d, z32.shape[0]), F32)], axis=1)
    w_in_r = jnp.concatenate([w_in[:, :Q_LORA + KV_LORA], kpe_tile, w_in[:, Q_LORA + KV_LORA + QK_ROPE:]], axis=1)
    wq = w_q_up.reshape(Q_LORA, N_HEADS, QK_NOPE + QK_ROPE)
    w_q_r = _head_tiles(wq[..., :QK_NOPE], wq[..., QK_NOPE:])
    wk = w_kv_up[..., :QK_NOPE]
    wv = w_kv_up[..., QK_NOPE:].reshape(KV_LORA, ATTN_WIDTH)
    w_k_r = _head_tiles(wk, jnp.zeros((KV_LORA, N_HEADS, QK_ROPE), F32))
    lane = np.arange(LANES)
    grp = np.where(lane < QK_NOPE, 0, np.where(lane < QK_NOPE + QK_ROPE, 1, 2))
    m_grp = ((grp[:, None] == grp[None, :]) & (grp[:, None] < 2)).astype(np.float32)
    m_grp = m_grp / np.where(grp < 1, QK_NOPE, QK_ROPE)[None, :]
    inv = ROPE_THETA ** (-jnp.arange(0, QK_ROPE, 2, dtype=F32) / QK_ROPE)
    ang = pos.astype(F32)[:, None] * inv[None, :]
    cs, sn = jnp.cos(ang), jnp.sin(ang)
    n = pos.shape[0]
    cos_t = jnp.concatenate([jnp.ones((n, QK_NOPE), F32), cs, cs, jnp.zeros((n, z32.shape[0]), F32)], axis=1)
    sin_t = jnp.concatenate([jnp.zeros((n, QK_NOPE), F32), -sn, sn, jnp.zeros((n, z32.shape[0]), F32)], axis=1)
    wk_g = wk * g_k_nope[None, None, :]
    absorb = jnp.zeros((N_HEADS, HEAD_PAD, 2 * LANES), F32)
    absorb = absorb.at[:, :QK_NOPE, :KV_LORA].set(jnp.transpose(wk_g, (1, 2, 0)))
    absorb = absorb.at[:, ROPE_LO:ROPE_LO + QK_ROPE, KV_LORA:KV_LORA + QK_ROPE].set(
        jnp.broadcast_to(jnp.eye(QK_ROPE, dtype=F32), (N_HEADS, QK_ROPE, QK_ROPE)))
    w_kt = jnp.transpose(wk, (1, 2, 0)).reshape(N_HEADS * QK_NOPE, KV_LORA)
    wr = jnp.concatenate([w_router, jnp.zeros((d, LANES - N_EXPERTS), F32)], axis=1)
    wr_hi = wr.astype(BF16)
    wr_lo = (wr - wr_hi.astype(F32)).astype(BF16)
    br = jnp.concatenate([b_router, jnp.full((LANES - N_EXPERTS,), NEG_INF, F32)])
    wv_h = w_kv_up[..., QK_NOPE:]
    w_v_tiles = jnp.concatenate([wv_h, jnp.zeros((KV_LORA, N_HEADS, HEAD_PAD - V_DIM), F32)], axis=-1)
    v_ones = (jnp.arange(N_HEADS * HEAD_PAD) % HEAD_PAD == V_DIM).astype(F32).reshape(1, N_HEADS * HEAD_PAD)
    return {
        "w_v_tiles": w_v_tiles.reshape(KV_LORA, N_HEADS * HEAD_PAD).astype(BF16), "v_ones": v_ones,
        "w_router_hi": wr_hi, "w_router_lo": wr_lo,
        "w_in": w_in_r.astype(BF16), "g_norm_mix": g_norm_mix.reshape(1, d), "g_q_lat": g_q_lat.reshape(1, Q_LORA),
        "w_q_up": w_q_r.astype(BF16),
        "gain_q": jnp.concatenate([g_q_nope, g_q_rope, z32]).reshape(1, LANES),
        "m_grp": jnp.asarray(m_grp, BF16), "cos_t": cos_t, "sin_t": sin_t,
        "g_kv_lat": g_kv_lat.reshape(1, KV_LORA),
        "gain_k": jnp.concatenate([g_k_nope, z64]).reshape(1, LANES),
        "gain_kpe": jnp.concatenate([z64, g_k_rope, z32]).reshape(1, LANES),
        "w_k": w_k_r.astype(BF16), "w_v": wv.astype(BF16),
        "w_abs": absorb.reshape(N_HEADS * HEAD_PAD, 2 * LANES).astype(BF16), "w_kt": w_kt.astype(BF16),
        "g_conv_ln": g_conv_ln.reshape(1, CONV_CH), "b_conv_ln": b_conv_ln.reshape(1, CONV_CH),
        "g_out_attn": g_out_attn.reshape(1, ATTN_WIDTH), "g_out_conv": g_out_conv.reshape(1, CONV_CH),
        "w_out": w_out.astype(BF16), "g_norm_ffn": g_norm_ffn.reshape(1, d),
        "b_router": br.reshape(1, LANES),
    }


def _mod_table(mod_p, mod_s, t_new):
    return jnp.concatenate([jnp.broadcast_to(mod_p, (TOK_TILE, D_MODEL)), jnp.repeat(mod_s, t_new, axis=0)], axis=0)


def kernel(x_prompt, x_sample, cache_ckv, cache_kpe, state_conv, page_table, c_prompt, c_sample, w_ada, b_ada, g_norm_mix, g_norm_ffn, w_in, g_q_lat, w_q_up, g_q_nope, g_q_rope, g_kv_lat, g_k_rope, w_kv_up, g_k_nope, w_dw, b_dw, g_conv_ln, b_conv_ln, g_out_attn, g_out_conv, w_out, w_router, b_router, w_exp_up, b_exp_up, w_exp_down, b_exp_down):
    bsz, seq, d = x_prompt.shape
    n_seq, t_new = x_sample.shape[:2]
    depth = w_ada.shape[0]
    assert bsz == 1 and depth == 1 and t_new == 4 and d == D_MODEL
    n_p = bsz * seq
    n_s = n_seq * t_new
    n = n_p + n_s
    past = page_table.shape[1] * cache_ckv.shape[2]
    l = 0

    pos = jnp.concatenate([jnp.arange(seq, dtype=I32), jnp.tile(past + jnp.arange(t_new, dtype=I32), n_seq)])
    p = _prepare(pos, w_in[l], g_norm_mix[l], g_q_lat[l], w_q_up[l], g_q_nope[l], g_q_rope[l], g_kv_lat[l],
                 g_k_rope[l], w_kv_up[l], g_k_nope[l], g_conv_ln[l], b_conv_ln[l], g_out_attn[l], g_out_conv[l],
                 w_out[l], g_norm_ffn[l], w_router[l], b_router[l])

    n_c = 1 + n_seq
    c_all = jnp.concatenate([c_prompt, c_sample, jnp.zeros((-n_c % 8, d), F32)], axis=0)
    mod = _adaln(c_all, w_ada[l], b_ada[l])
    tabs = [_mod_table(mod[0:1, j * d:(j + 1) * d], mod[1:n_c, j * d:(j + 1) * d], t_new) for j in range(6)]
    sh_m, sc_m, gt_m, sh_f, sc_f, gt_f = tabs

    x_p = x_prompt.reshape(n_p, d)
    x_s = x_sample.reshape(n_s, d)
    q_all, ckv_all, kpe_all, u_all, k_all, v_all = _mixer_inputs(x_p, x_s, sh_m, sc_m, p)

    attn_p = _prompt_attention(q_all, k_all, v_all, seq)
    ckv_s = ckv_all[n_p:].reshape(n_seq, t_new, KV_LORA)
    kpe_s = kpe_all[n_p:].reshape(n_seq, t_new, QK_ROPE)
    ckv_new_pad = jnp.pad(ckv_s, ((0, 0), (0, LANES - t_new), (0, 0)))
    kpe_new_t = jnp.swapaxes(jnp.pad(kpe_s, ((0, 0), (0, LANES - t_new), (0, 0))), 1, 2)
    q_s = q_all[n_p:].reshape(n_seq, t_new, N_HEADS * HEAD_PAD)
    attn_s = _sample_attention(page_table, q_s, ckv_new_pad, kpe_new_t, cache_ckv,
                               jnp.swapaxes(cache_kpe, 2, 3), p)

    conv_p = _prompt_conv(u_all, w_dw[l], b_dw[l].reshape(1, CONV_CH))
    u_s = u_all[n_p:].reshape(n_seq, t_new, CONV_CH)
    u_ext_s = jnp.concatenate([state_conv[l], u_s], axis=1)
    ext = CONV_W - 1 + t_new
    ext_pad = -ext % 8
    w_taps = jnp.stack([jnp.pad(w_dw[l], ((t, t_new - 1 - t + ext_pad), (0, 0))) for t in range(t_new)])
    conv_s = _sample_conv(jnp.pad(u_ext_s, ((0, 0), (0, ext_pad), (0, 0))), w_taps, b_dw[l].reshape(1, CONV_CH))
    conv_s = jnp.transpose(conv_s, (1, 0, 2)).reshape(n_s, CONV_CH)

    y_all, f_all, top_i, top_w = _merge_router(x_p, x_s, attn_p, attn_s.reshape(n_s, ATTN_WIDTH), conv_p, conv_s,
                                               gt_m, sh_f, sc_f, p)

    n_tiles = -(-(n * TOP_K + N_EXPERTS * (FFN_TILE - 1)) // FFN_TILE)
    n_slots = n_tiles * FFN_TILE
    pos_tok, meta = _route(top_i, n_tiles)
    slot = jnp.transpose(pos_tok[:, :TOP_K])
    piece_idx = (slot[:, None, :] + (jnp.arange(SC_PIECES, dtype=I32) * n_slots)[None, :, None]).reshape(-1)
    x_sorted = _sc_scatter_rows(f_all.reshape(SC_PIECES * n, SC_ROW), piece_idx, SC_PIECES * n_slots)
    h_sorted = _expert_ffn(meta[:n_tiles, 0], meta[0:1, 1], x_sorted.reshape(SC_PIECES, n_slots, SC_ROW),
                           w_exp_up[l], b_exp_up[l], w_exp_down[l], b_exp_down[l])
    gathered = _sc_gather_rows(h_sorted.reshape(SC_PIECES * n_slots, SC_ROW), piece_idx)
    y_p, y_s = _combine(y_all, gathered.reshape(TOP_K, SC_PIECES, n, SC_ROW), top_w, gt_f, n_p)
    y_p = y_p.reshape(bsz, seq, d)
    y_s = y_s.reshape(n_seq, t_new, d)
    ckv_prompt = ckv_all[:n_p].reshape(1, bsz, seq, KV_LORA)
    kpe_prompt = kpe_all[:n_p].reshape(1, bsz, seq, QK_ROPE)
    conv_prompt = u_all[n_p - (CONV_W - 1):n_p].reshape(1, bsz, CONV_W - 1, CONV_CH)
    ckv_sample = ckv_s[None]
    kpe_sample = kpe_s[None]
    conv_sample = u_ext_s[:, t_new:][None]
    return (y_p, y_s, ckv_prompt, kpe_prompt, conv_prompt, ckv_sample, kpe_sample, conv_sample)
```

```python
import functools

import numpy as np
import jax
import jax.numpy as jnp
from jax import lax
from jax.experimental import pallas as pl
from jax.experimental.pallas import tpu as pltpu
from jax.experimental.pallas import tpu_sc as plsc

F32 = jnp.float32
BF16 = jnp.bfloat16
I32 = jnp.int32
HIGHEST = lax.Precision.HIGHEST

D_MODEL = 1024
N_HEADS = 8
QK_NOPE = 64
QK_ROPE = 32
V_DIM = 64
Q_LORA = 256
KV_LORA = 128
ATTN_WIDTH = N_HEADS * V_DIM
CONV_CH = D_MODEL - ATTN_WIDTH
CONV_W = 31
N_EXPERTS = 32
TOP_K = 4
D_FF = D_MODEL
SWIGLU_LIMIT = 7.0
SWIGLU_ALPHA = 1.702
EPS = 1e-6
NEG_INF = -1e30
ROPE_THETA = 10000.0
SM_SCALE = (QK_NOPE + QK_ROPE) ** -0.5
LOG2E = 1.4426950408889634
Q_SCALE = SM_SCALE * LOG2E

LANES = 128
SUBLANES = 8
HEAD_PAD = LANES
ROPE_LO = QK_NOPE
ROPE_HALF = QK_ROPE // 2

TOK_TILE = 512
FA_TQ = 1024
FA_TK = FA_TQ
FA_HEADS = 8
PAGES_PER_STEP = 64
FFN_TILE = 512
ROUTE_TILE = 512
SC_ROW = 256
PACKED_WIDTH = D_MODEL // 2
SC_PIECES = PACKED_WIDTH // SC_ROW
SC_WINDOW = 128
VMEM_LIMIT = 56 * 1024 * 1024


def _cparams(sem, vmem=None):
    return pltpu.CompilerParams(dimension_semantics=sem, vmem_limit_bytes=vmem)


def _rsqrt_mean(x, n):
    return lax.rsqrt(jnp.sum(x * x, axis=-1, keepdims=True) * (1.0 / n) + EPS)


def _pack_rows(x):
    half = x.shape[1] // 2
    hi = lax.bitcast_convert_type(x[:, :half].astype(BF16).astype(F32), I32)
    lo = lax.bitcast_convert_type(x[:, half:].astype(BF16).astype(F32), I32)
    return hi | lax.shift_right_logical(lo, jnp.full(lo.shape, 16, I32))


def _unpack_rows(w):
    hi = lax.bitcast_convert_type(w & jnp.int32(-65536), F32)
    lo = lax.bitcast_convert_type(lax.shift_left(w, jnp.full(w.shape, 16, I32)), F32)
    return jnp.concatenate([hi, lo], axis=1)


def _ada_kernel(c_ref, w_ref, b_ref, o_ref):
    c = c_ref[...]
    s = c * jax.nn.sigmoid(c)
    o_ref[...] = jnp.dot(s, w_ref[...], precision=HIGHEST, preferred_element_type=F32) + b_ref[...]


def _adaln(c_all, w_ada, b_ada):
    rows = c_all.shape[0]
    n_out = w_ada.shape[1]
    return pl.pallas_call(
        _ada_kernel,
        grid=(n_out // D_MODEL,),
        in_specs=[pl.BlockSpec((rows, D_MODEL), lambda j: (0, 0)),
                  pl.BlockSpec((D_MODEL, D_MODEL), lambda j: (0, j)),
                  pl.BlockSpec((1, D_MODEL), lambda j: (0, j))],
        out_specs=pl.BlockSpec((rows, D_MODEL), lambda j: (0, j)),
        out_shape=jax.ShapeDtypeStruct((rows, n_out), F32),
        compiler_params=_cparams(("arbitrary",)),
        name="adaln",
    )(c_all, w_ada, b_ada.reshape(1, n_out))


def _group_norm_rope(x, m_grp, gain, cos_t, sin_t, first_half):
    ms = jnp.dot((x * x).astype(BF16), m_grp, preferred_element_type=F32)
    xn = x * lax.rsqrt(ms + EPS) * gain
    swapped = jnp.where(first_half, pltpu.roll(xn, LANES - ROPE_HALF, 1), pltpu.roll(xn, ROPE_HALF, 1))
    return xn * cos_t + swapped * sin_t


def _pick_rows(n_prompt_tiles, prompt_ref, decode_ref):
    return jnp.where(pl.program_id(0) < n_prompt_tiles, prompt_ref[...], decode_ref[...])


def _mix_kernel(n_prompt_tiles, xp_ref, xs_ref, sh_ref, sc_ref, gmix_ref, win_ref, gql_ref, wq_ref, gq_ref, m_ref,
                cos_ref, sin_ref, gkv_ref, gk_ref, gkpe_ref, wk_ref, wv_ref, vone_ref,
                q_out, ckv_out, kpe_out, u_out, k_out, v_out):
    x = _pick_rows(n_prompt_tiles, xp_ref, xs_ref)
    h = x * _rsqrt_mean(x, D_MODEL) * gmix_ref[...]
    h = h * (1.0 + sc_ref[...]) + sh_ref[...]
    proj = jnp.dot(h.astype(BF16), win_ref[...], preferred_element_type=F32)
    q_lat = proj[:, :Q_LORA]
    ckv_raw = proj[:, Q_LORA:Q_LORA + KV_LORA]
    kpe_blk = proj[:, Q_LORA + KV_LORA:Q_LORA + KV_LORA + LANES]
    glu_lo = Q_LORA + KV_LORA + LANES
    u_out[...] = proj[:, glu_lo:glu_lo + CONV_CH] * jax.nn.sigmoid(proj[:, glu_lo + CONV_CH:glu_lo + 2 * CONV_CH])

    m_grp = m_ref[...]
    cos_t = cos_ref[...]
    sin_t = sin_ref[...]
    lane = lax.broadcasted_iota(I32, (1, LANES), 1)
    first_half = lane < ROPE_LO + ROPE_HALF

    q_lat_n = q_lat * _rsqrt_mean(q_lat, Q_LORA) * gql_ref[...]
    q = jnp.dot(q_lat_n.astype(BF16), wq_ref[...], preferred_element_type=F32)
    gq = gq_ref[...]
    for hd in range(N_HEADS):
        qh = _group_norm_rope(q[:, hd * HEAD_PAD:(hd + 1) * HEAD_PAD], m_grp, gq, cos_t, sin_t, first_half)
        q_out[:, hd * HEAD_PAD:(hd + 1) * HEAD_PAD] = (qh * Q_SCALE).astype(BF16)

    ckv_n = ckv_raw * _rsqrt_mean(ckv_raw, KV_LORA) * gkv_ref[...]
    ckv_out[...] = ckv_n
    kpe_r = _group_norm_rope(kpe_blk, m_grp, gkpe_ref[...], cos_t, sin_t, first_half)
    kpe_out[...] = kpe_r[:, ROPE_LO:ROPE_LO + QK_ROPE]

    ckv_b = ckv_n.astype(BF16)
    kexp = jnp.dot(ckv_b, wk_ref[...], preferred_element_type=F32)
    gk = gk_ref[...]
    for hd in range(N_HEADS):
        kh = kexp[:, hd * HEAD_PAD:(hd + 1) * HEAD_PAD]
        ms = jnp.dot((kh * kh).astype(BF16), m_grp, preferred_element_type=F32)
        k_out[:, hd * HEAD_PAD:(hd + 1) * HEAD_PAD] = (kh * lax.rsqrt(ms + EPS) * gk + kpe_r).astype(BF16)
    v_out[...] = (jnp.dot(ckv_b, wv_ref[...], preferred_element_type=F32) + vone_ref[...]).astype(BF16)


def _mod_index(n_prompt_tiles):
    return lambda i: (jnp.where(i < n_prompt_tiles, 0, i - n_prompt_tiles + 1), 0)


def _prompt_index(n_prompt_tiles):
    return lambda i: (jnp.minimum(i, n_prompt_tiles - 1), 0)


def _decode_index(n_prompt_tiles):
    return lambda i: (jnp.maximum(i - n_prompt_tiles, 0), 0)


def _mixer_inputs(x_p, x_s, sh, sc, p):
    n_prompt_tiles = x_p.shape[0] // TOK_TILE
    n = x_p.shape[0] + x_s.shape[0]
    tm = TOK_TILE
    const = lambda i: (0, 0)
    row = lambda i: (i, 0)
    mod = _mod_index(n_prompt_tiles)
    hw = N_HEADS * HEAD_PAD
    in_cols = p["w_in"].shape[1]
    return pl.pallas_call(
        functools.partial(_mix_kernel, n_prompt_tiles),
        grid=(n // tm,),
        in_specs=[pl.BlockSpec((tm, D_MODEL), _prompt_index(n_prompt_tiles)),
                  pl.BlockSpec((tm, D_MODEL), _decode_index(n_prompt_tiles)),
                  pl.BlockSpec((tm, D_MODEL), mod),
                  pl.BlockSpec((tm, D_MODEL), mod),
                  pl.BlockSpec((1, D_MODEL), const),
                  pl.BlockSpec((D_MODEL, in_cols), const),
                  pl.BlockSpec((1, Q_LORA), const),
                  pl.BlockSpec((Q_LORA, hw), const),
                  pl.BlockSpec((1, LANES), const),
                  pl.BlockSpec((LANES, LANES), const),
                  pl.BlockSpec((tm, LANES), row),
                  pl.BlockSpec((tm, LANES), row),
                  pl.BlockSpec((1, KV_LORA), const),
                  pl.BlockSpec((1, LANES), const),
                  pl.BlockSpec((1, LANES), const),
                  pl.BlockSpec((KV_LORA, hw), const),
                  pl.BlockSpec((KV_LORA, hw), const),
                  pl.BlockSpec((1, hw), const)],
        out_specs=[pl.BlockSpec((tm, hw), row),
                   pl.BlockSpec((tm, KV_LORA), row),
                   pl.BlockSpec((tm, QK_ROPE), row),
                   pl.BlockSpec((tm, CONV_CH), row),
                   pl.BlockSpec((tm, hw), row),
                   pl.BlockSpec((tm, hw), row)],
        out_shape=[jax.ShapeDtypeStruct((n, hw), BF16),
                   jax.ShapeDtypeStruct((n, KV_LORA), F32),
                   jax.ShapeDtypeStruct((n, QK_ROPE), F32),
                   jax.ShapeDtypeStruct((n, CONV_CH), F32),
                   jax.ShapeDtypeStruct((n, hw), BF16),
                   jax.ShapeDtypeStruct((n, hw), BF16)],
        compiler_params=_cparams(("parallel",), VMEM_LIMIT),
        name="mixer_inputs",
    )(x_p, x_s, sh, sc, p["g_norm_mix"], p["w_in"], p["g_q_lat"], p["w_q_up"], p["gain_q"], p["m_grp"],
      p["cos_t"], p["sin_t"], p["g_kv_lat"], p["gain_k"], p["gain_kpe"], p["w_k"], p["w_v_tiles"], p["v_ones"])


def _fa_kernel(qt_ref, kt_ref, q_ref, k_ref, v_ref, o_ref, m_sc, acc_sc):
    t = pl.program_id(1)
    qi = qt_ref[t]
    ki = kt_ref[t]
    last_k = (qi + 1) * (FA_TQ // FA_TK) - 1

    @pl.when(ki == 0)
    def _():
        m_sc[...] = jnp.full(m_sc.shape, NEG_INF, F32)
        acc_sc[...] = jnp.zeros(acc_sc.shape, F32)

    def update(hh, r0, nr, k0, nk, masked):
        q = q_ref[pl.ds(r0, nr), hh * HEAD_PAD:(hh + 1) * HEAD_PAD]
        k = k_ref[pl.ds(k0, nk), hh * HEAD_PAD:(hh + 1) * HEAD_PAD]
        v = v_ref[pl.ds(k0, nk), hh * HEAD_PAD:(hh + 1) * HEAD_PAD]
        s = lax.dot_general(q, k, (((1,), (1,)), ((), ())), preferred_element_type=F32)
        if masked:
            col_minus_row = lax.broadcasted_iota(I32, (nr, nk), 1) - lax.broadcasted_iota(I32, (nr, nk), 0)
            s = jnp.where(col_minus_row <= (qi * FA_TQ + r0) - (ki * FA_TK + k0), s, NEG_INF)
        m_prev = m_sc[hh, pl.ds(r0, nr), :]
        m_new = jnp.maximum(m_prev, jnp.max(s, axis=-1, keepdims=True))
        alpha = jnp.exp2(m_prev - m_new)
        pr = jnp.exp2((s - jnp.concatenate([m_new] * (nk // LANES), axis=1)).astype(BF16))
        acc_sc[hh, pl.ds(r0, nr), :] = (alpha * acc_sc[hh, pl.ds(r0, nr), :]
                                        + jnp.dot(pr, v, preferred_element_type=F32))
        m_sc[hh, pl.ds(r0, nr), :] = m_new

    @pl.when(ki < qi)
    def _():
        for hh in range(FA_HEADS):
            update(hh, 0, FA_TQ, 0, FA_TK, False)

    @pl.when(ki == qi)
    def _():
        half = FA_TQ // 2
        for hh in range(FA_HEADS):
            update(hh, 0, half, 0, half, True)
            update(hh, half, half, 0, FA_TK, True)

    @pl.when(ki == last_k)
    def _():
        for hh in range(FA_HEADS):
            acc = acc_sc[hh]
            o_ref[:, hh * V_DIM:(hh + 1) * V_DIM] = acc[:, :V_DIM] / acc[:, V_DIM:V_DIM + 1]


def _prompt_attention(q_all, k_all, v_all, seq):
    nq = seq // FA_TQ
    ratio = FA_TQ // FA_TK
    qt, kt = [], []
    for qi in range(nq):
        for ki in range((qi + 1) * ratio):
            qt.append(qi)
            kt.append(ki)
    qt = jnp.asarray(np.array(qt, np.int32))
    kt = jnp.asarray(np.array(kt, np.int32))
    n_pairs = int(qt.shape[0])
    grid_spec = pltpu.PrefetchScalarGridSpec(
        num_scalar_prefetch=2,
        grid=(N_HEADS // FA_HEADS, n_pairs),
        in_specs=[pl.BlockSpec((FA_TQ, FA_HEADS * HEAD_PAD), lambda hp, t, qt, kt: (qt[t], hp)),
                  pl.BlockSpec((FA_TK, FA_HEADS * HEAD_PAD), lambda hp, t, qt, kt: (kt[t], hp)),
                  pl.BlockSpec((FA_TK, FA_HEADS * HEAD_PAD), lambda hp, t, qt, kt: (kt[t], hp))],
        out_specs=pl.BlockSpec((FA_TQ, FA_HEADS * V_DIM), lambda hp, t, qt, kt: (qt[t], hp)),
        scratch_shapes=[pltpu.VMEM((FA_HEADS, FA_TQ, LANES), F32),
                        pltpu.VMEM((FA_HEADS, FA_TQ, HEAD_PAD), F32)],
    )
    return pl.pallas_call(
        _fa_kernel,
        grid_spec=grid_spec,
        out_shape=jax.ShapeDtypeStruct((seq, ATTN_WIDTH), F32),
        compiler_params=_cparams(("parallel", "arbitrary"), VMEM_LIMIT),
        name="prompt_attention",
    )(qt, kt, q_all, k_all, v_all)


def _sattn_kernel(pt_ref, q_ref, wabs_ref, wkt_ref, wv_ref, ckvn_ref, kpen_ref, ckv_hbm, kpe_hbm,
                  o_ref, m_sc, l_sc, acc_sc, qa_sc, wall_sc, ckv_buf, kpe_buf, sem):
    kb = pl.program_id(1)
    n_kb = pl.num_programs(1)
    step = pl.program_id(0) * n_kb + kb
    n_steps = pl.num_programs(0) * n_kb
    slot = step % 2
    rows = 4 * N_HEADS

    def start_fetch(step_i, slot_i):
        for j in range(PAGES_PER_STEP):
            page = pt_ref[step_i * PAGES_PER_STEP + j]
            pltpu.make_async_copy(ckv_hbm.at[0, page], ckv_buf.at[slot_i, j], sem.at[0, slot_i]).start()
            pltpu.make_async_copy(kpe_hbm.at[0, page], kpe_buf.at[slot_i, j], sem.at[1, slot_i]).start()

    @pl.when(step == 0)
    def _():
        start_fetch(step, slot)

    @pl.when(step + 1 < n_steps)
    def _():
        start_fetch(step + 1, 1 - slot)

    @pl.when(kb == 0)
    def _():
        m_sc[...] = jnp.full(m_sc.shape, NEG_INF, F32)
        l_sc[...] = jnp.zeros(l_sc.shape, F32)
        acc_sc[...] = jnp.zeros(acc_sc.shape, F32)
        q4 = q_ref[...].astype(F32)
        head_of_lane = lax.broadcasted_iota(I32, (N_HEADS, N_HEADS * HEAD_PAD), 1) // HEAD_PAD
        head_of_row = lax.broadcasted_iota(I32, (N_HEADS, N_HEADS * HEAD_PAD), 0)
        own = head_of_lane == head_of_row
        qbd = jnp.concatenate(
            [jnp.where(own, jnp.broadcast_to(q4[qq:qq + 1, :], own.shape), 0.0) for qq in range(4)], axis=0)
        qa = jnp.dot(qbd.astype(BF16), wabs_ref[...], preferred_element_type=F32).astype(BF16)
        qa_sc[...] = qa
        wall_sc[pl.ds(0, N_HEADS * QK_NOPE), :] = wkt_ref[...]
        wall_sc[pl.ds(N_HEADS * QK_NOPE, rows), :] = qa[:, :KV_LORA]

    def attend(state, ckv_b, kpe_t, mask):
        m_prev, l_prev, acc_prev = state
        nt = (((1,), (1,)), ((), ()))
        kn_all = lax.dot_general(wall_sc[...], ckv_b, nt, preferred_element_type=F32)
        keys = kn_all.shape[1]
        kn_t = kn_all[:N_HEADS * QK_NOPE]
        ss = jnp.sum((kn_t * kn_t).reshape(N_HEADS, QK_NOPE, keys), axis=1)
        r8 = lax.rsqrt(ss * (1.0 / QK_NOPE) + EPS)
        rope = jnp.dot(qa_sc[:, KV_LORA:KV_LORA + QK_ROPE], kpe_t, preferred_element_type=F32)
        s = kn_all[N_HEADS * QK_NOPE:] * jnp.concatenate([r8] * 4, axis=0) + rope
        if mask is not None:
            s = jnp.where(mask, s, NEG_INF)
        m_new = jnp.maximum(m_prev, jnp.max(s, axis=-1, keepdims=True))
        alpha = jnp.exp2(m_prev - m_new)
        pr = jnp.exp2(s - m_new)
        l_new = alpha * l_prev + jnp.sum(pr, axis=-1, keepdims=True)
        acc_new = alpha * acc_prev + jnp.dot(pr.astype(BF16), ckv_b, preferred_element_type=F32)
        return m_new, l_new, acc_new

    pltpu.make_async_copy(ckv_hbm.at[0, pl.ds(0, PAGES_PER_STEP)], ckv_buf.at[slot], sem.at[0, slot]).wait()
    pltpu.make_async_copy(kpe_hbm.at[0, pl.ds(0, PAGES_PER_STEP)], kpe_buf.at[slot], sem.at[1, slot]).wait()

    page = ckv_buf.shape[2]
    ckv_b = ckv_buf[slot].reshape(PAGES_PER_STEP * page, KV_LORA).astype(BF16)
    kpe_t = jnp.concatenate([kpe_buf[slot, j].astype(BF16) for j in range(PAGES_PER_STEP)], axis=1)
    state = attend((m_sc[...], l_sc[...], acc_sc[...]), ckv_b, kpe_t, None)
    m_sc[...], l_sc[...], acc_sc[...] = state

    @pl.when(kb == n_kb - 1)
    def _():
        key = lax.broadcasted_iota(I32, (rows, LANES), 1)
        qry = lax.broadcasted_iota(I32, (rows, LANES), 0) // N_HEADS
        _, l_fin, acc_fin = attend(state, ckvn_ref[...].astype(BF16), kpen_ref[...].astype(BF16), key <= qry)
        lat = acc_fin / l_fin
        o_all = jnp.dot(lat.astype(BF16), wv_ref[...], preferred_element_type=F32)
        head_of_col = lax.broadcasted_iota(I32, (rows, ATTN_WIDTH), 1) // V_DIM
        head_of_row = lax.broadcasted_iota(I32, (rows, ATTN_WIDTH), 0) % N_HEADS
        o_own = jnp.where(head_of_col == head_of_row, o_all, 0.0)
        o_ref[...] = jnp.sum(o_own.reshape(4, N_HEADS, ATTN_WIDTH), axis=1)


def _sample_attention(page_table, q_s, ckv_new_pad, kpe_new_t, cache_ckv, cache_kpe_t, p):
    n_seq, n_pages = page_table.shape
    page = cache_ckv.shape[2]
    n_kb = n_pages // PAGES_PER_STEP
    hw = N_HEADS * HEAD_PAD
    assert n_pages % PAGES_PER_STEP == 0
    per_seq3 = lambda b, kb, pt: (b, 0, 0)
    const = lambda b, kb, pt: (0, 0)
    in_specs = [pl.BlockSpec((None, 4, hw), per_seq3),
                pl.BlockSpec((hw, 2 * LANES), const),
                pl.BlockSpec((N_HEADS * QK_NOPE, KV_LORA), const),
                pl.BlockSpec((KV_LORA, ATTN_WIDTH), const),
                pl.BlockSpec((None, LANES, KV_LORA), per_seq3),
                pl.BlockSpec((None, QK_ROPE, LANES), per_seq3),
                pl.BlockSpec(memory_space=pl.ANY),
                pl.BlockSpec(memory_space=pl.ANY)]
    grid_spec = pltpu.PrefetchScalarGridSpec(
        num_scalar_prefetch=1,
        grid=(n_seq, n_kb),
        in_specs=in_specs,
        out_specs=pl.BlockSpec((None, 4, ATTN_WIDTH), per_seq3),
        scratch_shapes=[pltpu.VMEM((4 * N_HEADS, 1), F32),
                        pltpu.VMEM((4 * N_HEADS, 1), F32),
                        pltpu.VMEM((4 * N_HEADS, KV_LORA), F32),
                        pltpu.VMEM((4 * N_HEADS, 2 * LANES), BF16),
                        pltpu.VMEM((N_HEADS * QK_NOPE + 4 * N_HEADS, KV_LORA), BF16),
                        pltpu.VMEM((2, PAGES_PER_STEP, page, KV_LORA), F32),
                        pltpu.VMEM((2, PAGES_PER_STEP, QK_ROPE, page), F32),
                        pltpu.SemaphoreType.DMA((2, 2))],
    )
    return pl.pallas_call(
        _sattn_kernel,
        grid_spec=grid_spec,
        out_shape=jax.ShapeDtypeStruct((n_seq, 4, ATTN_WIDTH), F32),
        compiler_params=_cparams(("arbitrary", "arbitrary"), VMEM_LIMIT),
        name="decode_attention",
    )(page_table.reshape(-1), q_s, p["w_abs"], p["w_kt"], p["w_v"], ckv_new_pad, kpe_new_t, cache_ckv, cache_kpe_t)


CONV_HALO = 32
CONV_ROWS = 64


def _conv_kernel(halo_ref, u_ref, w_ref, b_ref, o_ref, ext_sc, sh_sc):
    i = pl.program_id(0)
    tm = u_ref.shape[0]
    ext_sc[pl.ds(0, CONV_HALO), :] = jnp.where(i == 0, 0.0, halo_ref[...])
    ext_sc[pl.ds(CONV_HALO, tm), :] = u_ref[...]
    first = CONV_HALO - (CONV_W - 1)
    span = tm + CONV_HALO - SUBLANES
    for sft in range(1, SUBLANES):
        sh_sc[sft - 1] = ext_sc[pl.ds(sft, span), :]
    for rc in range(tm // CONV_ROWS):
        acc = jnp.broadcast_to(b_ref[...], (CONV_ROWS, CONV_CH))
        for j in range(CONV_W):
            a8, sft = divmod(first + j, SUBLANES)
            src = ext_sc if sft == 0 else sh_sc.at[sft - 1]
            acc = acc + src[pl.ds(rc * CONV_ROWS + SUBLANES * a8, CONV_ROWS), :] * w_ref[j:j + 1, :]
        o_ref[pl.ds(rc * CONV_ROWS, CONV_ROWS), :] = acc


def _prompt_conv(u_all, w_dw, b_dw):
    n = u_all.shape[0]
    tm = TOK_TILE
    per = tm // CONV_HALO
    return pl.pallas_call(
        _conv_kernel,
        grid=(n // tm,),
        in_specs=[pl.BlockSpec((CONV_HALO, CONV_CH), lambda i: (jnp.maximum(i * per - 1, 0), 0)),
                  pl.BlockSpec((tm, CONV_CH), lambda i: (i, 0)),
                  pl.BlockSpec((CONV_W, CONV_CH), lambda i: (0, 0)),
                  pl.BlockSpec((1, CONV_CH), lambda i: (0, 0))],
        out_specs=pl.BlockSpec((tm, CONV_CH), lambda i: (i, 0)),
        out_shape=jax.ShapeDtypeStruct((n, CONV_CH), F32),
        scratch_shapes=[pltpu.VMEM((CONV_HALO + tm, CONV_CH), F32),
                        pltpu.VMEM((SUBLANES - 1, CONV_HALO + tm - SUBLANES, CONV_CH), F32)],
        compiler_params=_cparams(("parallel",)),
        name="prompt_conv",
    )(u_all, u_all, w_dw, b_dw)


def _sconv_kernel(u_ref, wt_ref, b_ref, o_ref):
    u = u_ref[...]
    for t in range(o_ref.shape[0]):
        o_ref[t] = jnp.sum(u * wt_ref[t][None, :, :], axis=1) + b_ref[...]


def _sample_conv(u_ext, w_taps, b_dw):
    n_seq, ext, _ = u_ext.shape
    t_new = w_taps.shape[0]
    sb = 8
    return pl.pallas_call(
        _sconv_kernel,
        grid=(n_seq // sb,),
        in_specs=[pl.BlockSpec((sb, ext, CONV_CH), lambda i: (i, 0, 0)),
                  pl.BlockSpec((t_new, ext, CONV_CH), lambda i: (0, 0, 0)),
                  pl.BlockSpec((1, CONV_CH), lambda i: (0, 0))],
        out_specs=pl.BlockSpec((t_new, sb, CONV_CH), lambda i: (0, i, 0)),
        out_shape=jax.ShapeDtypeStruct((t_new, n_seq, CONV_CH), F32),
        compiler_params=_cparams(("parallel",)),
        name="decode_conv",
    )(u_ext, w_taps, b_dw)


def _lane_pack(cols, dtype):
    lane = lax.broadcasted_iota(I32, (cols[0].shape[0], LANES), 1)
    out = jnp.zeros((cols[0].shape[0], LANES), dtype)
    for j, c in enumerate(cols):
        out = jnp.where(lane == j, c.astype(dtype), out)
    return out


def _merge_kernel(n_prompt_tiles, xp_ref, xs_ref, ap_ref, as_ref, cp_ref, cs_ref, gate_ref, shf_ref, scf_ref,
                  gln_ref, bln_ref, ga_ref, gc_ref, wout_ref, gffn_ref, wrh_ref, wrl_ref, br_ref,
                  y_out, f_out, ti_out, tw_out):
    yc = _pick_rows(n_prompt_tiles, cp_ref, cs_ref)
    mu = jnp.mean(yc, axis=-1, keepdims=True)
    xc = yc - mu
    var = jnp.mean(xc * xc, axis=-1, keepdims=True)
    ln = xc * lax.rsqrt(var + EPS) * gln_ref[...] + bln_ref[...]
    conv = ln * jax.nn.sigmoid(ln)
    attn = _pick_rows(n_prompt_tiles, ap_ref, as_ref)
    a_n = attn * _rsqrt_mean(attn, ATTN_WIDTH) * ga_ref[...]
    c_n = conv * _rsqrt_mean(conv, CONV_CH) * gc_ref[...]
    m = (jnp.dot(a_n.astype(BF16), wout_ref[:ATTN_WIDTH, :], preferred_element_type=F32)
         + jnp.dot(c_n.astype(BF16), wout_ref[ATTN_WIDTH:, :], preferred_element_type=F32))
    y = _pick_rows(n_prompt_tiles, xp_ref, xs_ref) + gate_ref[...] * m
    y_out[...] = y
    f = y * _rsqrt_mean(y, D_MODEL) * gffn_ref[...]
    f = f * (1.0 + scf_ref[...]) + shf_ref[...]
    f_words = _pack_rows(f)
    for j in range(SC_PIECES):
        f_out[j] = f_words[:, j * SC_ROW:(j + 1) * SC_ROW]
    f_hi = f.astype(BF16)
    f_lo = (f - f_hi.astype(F32)).astype(BF16)
    logits = (jnp.dot(f_hi, wrh_ref[...], preferred_element_type=F32)
              + jnp.dot(f_hi, wrl_ref[...], preferred_element_type=F32)
              + jnp.dot(f_lo, wrh_ref[...], preferred_element_type=F32)) + br_ref[...]
    lane = lax.broadcasted_iota(I32, logits.shape, 1)
    vals, idxs = [], []
    for _ in range(TOP_K):
        mx = jnp.max(logits, axis=-1, keepdims=True)
        ix = jnp.min(jnp.where(logits == mx, lane, LANES), axis=-1, keepdims=True)
        vals.append(mx)
        idxs.append(ix)
        logits = jnp.where(lane == ix, NEG_INF * 4.0, logits)
    exps = [jnp.exp(v - vals[0]) for v in vals]
    tot = exps[0] + exps[1] + exps[2] + exps[3]
    ti_out[...] = _lane_pack(idxs, I32)
    tw_out[...] = _lane_pack([e / tot for e in exps], F32)


def _merge_router(x_p, x_s, attn_p, attn_s, conv_p, conv_s, gate, shf, scf, p):
    n_prompt_tiles = x_p.shape[0] // TOK_TILE
    n = x_p.shape[0] + x_s.shape[0]
    tm = TOK_TILE
    const = lambda i: (0, 0)
    row = lambda i: (i, 0)
    mod = _mod_index(n_prompt_tiles)
    from_p = _prompt_index(n_prompt_tiles)
    from_s = _decode_index(n_prompt_tiles)
    return pl.pallas_call(
        functools.partial(_merge_kernel, n_prompt_tiles),
        grid=(n // tm,),
        in_specs=[pl.BlockSpec((tm, D_MODEL), from_p),
                  pl.BlockSpec((tm, D_MODEL), from_s),
                  pl.BlockSpec((tm, ATTN_WIDTH), from_p),
                  pl.BlockSpec((tm, ATTN_WIDTH), from_s),
                  pl.BlockSpec((tm, CONV_CH), from_p),
                  pl.BlockSpec((tm, CONV_CH), from_s),
                  pl.BlockSpec((tm, D_MODEL), mod),
                  pl.BlockSpec((tm, D_MODEL), mod),
                  pl.BlockSpec((tm, D_MODEL), mod),
                  pl.BlockSpec((1, CONV_CH), const),
                  pl.BlockSpec((1, CONV_CH), const),
                  pl.BlockSpec((1, ATTN_WIDTH), const),
                  pl.BlockSpec((1, CONV_CH), const),
                  pl.BlockSpec((D_MODEL, D_MODEL), const),
                  pl.BlockSpec((1, D_MODEL), const),
                  pl.BlockSpec((D_MODEL, LANES), const),
                  pl.BlockSpec((D_MODEL, LANES), const),
                  pl.BlockSpec((1, LANES), const)],
        out_specs=[pl.BlockSpec((tm, D_MODEL), row),
                   pl.BlockSpec((SC_PIECES, tm, SC_ROW), lambda i: (0, i, 0)),
                   pl.BlockSpec((tm, LANES), row),
                   pl.BlockSpec((tm, LANES), row)],
        out_shape=[jax.ShapeDtypeStruct((n, D_MODEL), F32),
                   jax.ShapeDtypeStruct((SC_PIECES, n, SC_ROW), I32),
                   jax.ShapeDtypeStruct((n, LANES), I32),
                   jax.ShapeDtypeStruct((n, LANES), F32)],
        compiler_params=_cparams(("parallel",), VMEM_LIMIT),
        name="merge_router",
    )(x_p, x_s, attn_p, attn_s, conv_p, conv_s, gate, shf, scf, p["g_conv_ln"], p["b_conv_ln"], p["g_out_attn"],
      p["g_out_conv"], p["w_out"], p["g_norm_ffn"], p["w_router_hi"], p["w_router_lo"], p["b_router"])


def _select_lane(table, idx_col, lane):
    return jnp.sum(jnp.where(lane == idx_col, table, 0.0), axis=-1, keepdims=True)


def _rank_kernel(ti_ref, rk_out, cnt_out, carry_sc):
    i = pl.program_id(0)

    @pl.when(i == 0)
    def _():
        carry_sc[...] = jnp.zeros(carry_sc.shape, F32)

    ti = ti_ref[...]
    tn = ti.shape[0]
    lane = lax.broadcasted_iota(I32, (tn, LANES), 1)
    sel = jnp.zeros((tn, LANES), F32)
    for k in range(TOP_K):
        sel = sel + (lane == ti[:, k:k + 1]).astype(F32)
    r_i = lax.broadcasted_iota(I32, (tn, tn), 0)
    c_i = lax.broadcasted_iota(I32, (tn, tn), 1)
    below = (c_i < r_i).astype(BF16)
    rank = carry_sc[...] + jnp.dot(below, sel.astype(BF16), preferred_element_type=F32)
    rk_out[...] = _lane_pack([_select_lane(rank, ti[:, k:k + 1], lane) for k in range(TOP_K)], F32)
    carry_sc[...] = carry_sc[...] + jnp.sum(sel, axis=0, keepdims=True)
    cnt_out[...] = jnp.broadcast_to(carry_sc[...], cnt_out.shape)


def _pos_kernel(cnt_ref, ti_ref, rk_ref, pos_out, meta_out):
    cnt = cnt_ref[...]
    padded = jnp.ceil(cnt * (1.0 / FFN_TILE)) * FFN_TILE
    r_i = lax.broadcasted_iota(I32, (LANES, LANES), 0)
    c_i = lax.broadcasted_iota(I32, (LANES, LANES), 1)
    before = (r_i < c_i).astype(F32)
    offs = jnp.dot(padded, before, precision=HIGHEST, preferred_element_type=F32)
    ends = offs + padded
    ti = ti_ref[...]
    tn = ti.shape[0]
    lane = lax.broadcasted_iota(I32, (tn, LANES), 1)
    off_row = offs[0:1, :]
    rk = rk_ref[...]
    pos = [_select_lane(jnp.broadcast_to(off_row, (tn, LANES)), ti[:, k:k + 1], lane) + rk[:, k:k + 1]
           for k in range(TOP_K)]
    pos_out[...] = _lane_pack(pos, F32).astype(I32)

    @pl.when(pl.program_id(0) == 0)
    def _():
        nt = meta_out.shape[0]
        start = (lax.broadcasted_iota(I32, (nt, LANES), 0) * FFN_TILE).astype(F32)
        elane = lax.broadcasted_iota(I32, (nt, LANES), 1)
        done = jnp.where((elane < N_EXPERTS) & (jnp.broadcast_to(ends[0:1, :], (nt, LANES)) <= start), 1.0, 0.0)
        expert = jnp.minimum(jnp.sum(done, axis=-1, keepdims=True), N_EXPERTS - 1.0)
        total = jnp.sum(jnp.where(elane < N_EXPERTS, jnp.broadcast_to(padded[0:1, :], (nt, LANES)), 0.0),
                        axis=-1, keepdims=True)
        meta_out[...] = _lane_pack([expert, total * (1.0 / FFN_TILE)], F32).astype(I32)


def _route(top_i, n_tiles_max):
    n = top_i.shape[0]
    tn = ROUTE_TILE
    rk, cnt = pl.pallas_call(
        _rank_kernel,
        grid=(n // tn,),
        in_specs=[pl.BlockSpec((tn, LANES), lambda i: (i, 0))],
        out_specs=[pl.BlockSpec((tn, LANES), lambda i: (i, 0)),
                   pl.BlockSpec((8, LANES), lambda i: (0, 0))],
        out_shape=[jax.ShapeDtypeStruct((n, LANES), F32),
                   jax.ShapeDtypeStruct((8, LANES), F32)],
        scratch_shapes=[pltpu.VMEM((1, LANES), F32)],
        compiler_params=_cparams(("arbitrary",)),
        name="route_rank",
    )(top_i)
    nt_pad = -(-n_tiles_max // 8) * 8
    pos, meta = pl.pallas_call(
        _pos_kernel,
        grid=(n // tn,),
        in_specs=[pl.BlockSpec((8, LANES), lambda i: (0, 0)),
                  pl.BlockSpec((tn, LANES), lambda i: (i, 0)),
                  pl.BlockSpec((tn, LANES), lambda i: (i, 0))],
        out_specs=[pl.BlockSpec((tn, LANES), lambda i: (i, 0)),
                   pl.BlockSpec((nt_pad, LANES), lambda i: (0, 0))],
        out_shape=[jax.ShapeDtypeStruct((n, LANES), I32),
                   jax.ShapeDtypeStruct((nt_pad, LANES), I32)],
        compiler_params=_cparams(("arbitrary",)),
        name="route_pos",
    )(cnt, top_i, rk)
    return pos, meta


def _sc_scatter_rows(x, idx, n_out):
    n_src = x.shape[0]
    n = idx.shape[0]
    n_src_blk = n_src // SC_WINDOW
    mesh = plsc.VectorSubcoreMesh(core_axis_name="c", subcore_axis_name="s")

    @pl.kernel(out_type=jax.ShapeDtypeStruct((n_out, SC_ROW), x.dtype), mesh=mesh)
    def k(x_hbm, i_hbm, o_hbm):
        def body(x_vmem, i_vmem):
            pltpu.sync_copy(x_vmem, o_hbm.at[i_vmem.at[0]])

        pltpu.emit_pipeline(
            body,
            grid=(n // SC_WINDOW,),
            in_specs=[pl.BlockSpec((SC_WINDOW, SC_ROW), index_map=lambda i: (i % n_src_blk, 0)),
                      pl.BlockSpec((1, SC_WINDOW), index_map=lambda i: (0, i))],
            out_specs=[],
            core_axis_name=("c", "s"),
            dimension_semantics=(pltpu.PARALLEL,),
        )(x_hbm, i_hbm)

    return k(x, idx.reshape(1, n))


def _sc_gather_rows(x, idx):
    n = idx.shape[0]
    mesh = plsc.VectorSubcoreMesh(core_axis_name="c", subcore_axis_name="s")

    @pl.kernel(out_type=jax.ShapeDtypeStruct((n, SC_ROW), x.dtype), mesh=mesh)
    def k(x_hbm, i_hbm, o_hbm):
        def body(i_vmem, o_vmem):
            pltpu.sync_copy(x_hbm.at[i_vmem.at[0]], o_vmem)

        pltpu.emit_pipeline(
            body,
            grid=(n // SC_WINDOW,),
            in_specs=[pl.BlockSpec((1, SC_WINDOW), index_map=lambda i: (0, i))],
            out_specs=[pl.BlockSpec((SC_WINDOW, SC_ROW), index_map=lambda i: (i, 0))],
            core_axis_name=("c", "s"),
            dimension_semantics=(pltpu.PARALLEL,),
        )(i_hbm, o_hbm)

    return k(x, idx.reshape(1, n))


def _ffn_kernel(te_ref, nv_ref, x_ref, wu_ref, bu_ref, wd_ref, bd_ref, o_ref, wu_sc, wd_sc):
    t = pl.program_id(0)
    valid = t < nv_ref[0]
    prev = te_ref[jnp.maximum(t - 1, 0)]
    fresh = jnp.logical_or(t == 0, te_ref[t] != prev)

    @pl.when(jnp.logical_and(valid, fresh))
    def _():
        wu_sc[...] = wu_ref[...].astype(BF16)
        wd_sc[...] = wd_ref[...].astype(BF16)

    @pl.when(valid)
    def _():
        x = _unpack_rows(jnp.concatenate([x_ref[j] for j in range(SC_PIECES)], axis=1))
        z = jnp.dot(x.astype(BF16), wu_sc[...], preferred_element_type=F32) + bu_ref[...]
        zg = jnp.minimum(z[:, :D_FF], SWIGLU_LIMIT)
        zl = jnp.clip(z[:, D_FF:], -SWIGLU_LIMIT, SWIGLU_LIMIT)
        act = zg * jax.nn.sigmoid(SWIGLU_ALPHA * zg) * (zl + 1.0)
        out = _pack_rows(jnp.dot(act.astype(BF16), wd_sc[...], preferred_element_type=F32) + bd_ref[...])
        for j in range(SC_PIECES):
            o_ref[j] = out[:, j * SC_ROW:(j + 1) * SC_ROW]


def _expert_ffn(tile_expert, n_valid, x_sorted, w_up, b_up, w_down, b_down):
    n_slots = x_sorted.shape[1]
    n_tiles = n_slots // FFN_TILE
    xmap = lambda t, te, nv: (0, jnp.minimum(t, nv[0] - 1), 0)
    emap = lambda t, te, nv: (te[t], 0, 0)
    grid_spec = pltpu.PrefetchScalarGridSpec(
        num_scalar_prefetch=2,
        grid=(n_tiles,),
        in_specs=[pl.BlockSpec((SC_PIECES, FFN_TILE, SC_ROW), xmap),
                  pl.BlockSpec((None, D_MODEL, 2 * D_FF), emap),
                  pl.BlockSpec((None, 1, 2 * D_FF), emap),
                  pl.BlockSpec((None, D_FF, D_MODEL), emap),
                  pl.BlockSpec((None, 1, D_MODEL), emap)],
        out_specs=pl.BlockSpec((SC_PIECES, FFN_TILE, SC_ROW), xmap),
        scratch_shapes=[pltpu.VMEM((D_MODEL, 2 * D_FF), BF16),
                        pltpu.VMEM((D_FF, D_MODEL), BF16)],
    )
    return pl.pallas_call(
        _ffn_kernel,
        grid_spec=grid_spec,
        out_shape=jax.ShapeDtypeStruct((SC_PIECES, n_slots, SC_ROW), I32),
        compiler_params=_cparams(("arbitrary",), VMEM_LIMIT),
        name="expert_ffn",
    )(tile_expert, n_valid, x_sorted, w_up, b_up.reshape(N_EXPERTS, 1, 2 * D_FF), w_down,
      b_down.reshape(N_EXPERTS, 1, D_MODEL))


def _combine_kernel(n_prompt_tiles, y_ref, g_ref, tw_ref, gate_ref, op_ref, os_ref):
    tw = tw_ref[...]
    rows = lambda k: _unpack_rows(jnp.concatenate([g_ref[k, j] for j in range(SC_PIECES)], axis=1))
    moe = rows(0) * tw[:, 0:1]
    for k in range(1, TOP_K):
        moe = moe + rows(k) * tw[:, k:k + 1]
    out = y_ref[...] + gate_ref[...] * moe
    i = pl.program_id(0)

    @pl.when(i < n_prompt_tiles)
    def _():
        op_ref[...] = out

    @pl.when(i >= n_prompt_tiles)
    def _():
        os_ref[...] = out


def _combine(y_all, gathered, top_w, gate, n_p):
    n = y_all.shape[0]
    tm = TOK_TILE
    n_prompt_tiles = n_p // tm
    row = lambda i: (i, 0)
    return pl.pallas_call(
        functools.partial(_combine_kernel, n_prompt_tiles),
        grid=(n // tm,),
        in_specs=[pl.BlockSpec((tm, D_MODEL), row),
                  pl.BlockSpec((TOP_K, SC_PIECES, tm, SC_ROW), lambda i: (0, 0, i, 0)),
                  pl.BlockSpec((tm, LANES), row),
                  pl.BlockSpec((tm, D_MODEL), _mod_index(n_prompt_tiles))],
        out_specs=[pl.BlockSpec((tm, D_MODEL), _prompt_index(n_prompt_tiles)),
                   pl.BlockSpec((tm, D_MODEL), _decode_index(n_prompt_tiles))],
        out_shape=[jax.ShapeDtypeStruct((n_p, D_MODEL), F32),
                   jax.ShapeDtypeStruct((n - n_p, D_MODEL), F32)],
        compiler_params=_cparams(("arbitrary",), VMEM_LIMIT),
        name="moe_combine",
    )(y_all, gathered, top_w, gate)


def _head_tiles(nope, rope):
    pad = jnp.zeros(nope.shape[:-1] + (HEAD_PAD - QK_NOPE - QK_ROPE,), nope.dtype)
    t = jnp.concatenate([nope, rope, pad], axis=-1)
    return t.reshape(t.shape[:-2] + (N_HEADS * HEAD_PAD,))


def _prepare(pos, w_in, g_norm_mix, g_q_lat, w_q_up, g_q_nope, g_q_rope, g_kv_lat, g_k_rope, w_kv_up, g_k_nope,
             g_conv_ln, b_conv_ln, g_out_attn, g_out_conv, w_out, g_norm_ffn, w_router, b_router):
    z32 = jnp.zeros((HEAD_PAD - QK_NOPE - QK_ROPE,), F32)
    z64 = jnp.zeros((QK_NOPE,), F32)
    d = D_MODEL
    kpe_cols = w_in[:, Q_LORA + KV_LORA:Q_LORA + KV_LORA + QK_ROPE]
    kpe_tile = jnp.concatenate([jnp.zeros((d, QK_NOPE), F32), kpe_cols, jnp.zeros((d, z32.shape[0]), F32)], axis=1)
    w_in_r = jnp.concatenate([w_in[:, :Q_LORA + KV_LORA], kpe_tile, w_in[:, Q_LORA + KV_LORA + QK_ROPE:]], axis=1)
    wq = w_q_up.reshape(Q_LORA, N_HEADS, QK_NOPE + QK_ROPE)
    w_q_r = _head_tiles(wq[..., :QK_NOPE], wq[..., QK_NOPE:])
    wk = w_kv_up[..., :QK_NOPE]
    wv = w_kv_up[..., QK_NOPE:].reshape(KV_LORA, ATTN_WIDTH)
    w_k_r = _head_tiles(wk, jnp.zeros((KV_LORA, N_HEADS, QK_ROPE), F32))
    lane = np.arange(LANES)
    grp = np.where(lane < QK_NOPE, 0, np.where(lane < QK_NOPE + QK_ROPE, 1, 2))
    m_grp = ((grp[:, None] == grp[None, :]) & (grp[:, None] < 2)).astype(np.float32)
    m_grp = m_grp / np.where(grp < 1, QK_NOPE, QK_ROPE)[None, :]
    inv = ROPE_THETA ** (-jnp.arange(0, QK_ROPE, 2, dtype=F32) / QK_ROPE)
    ang = pos.astype(F32)[:, None] * inv[None, :]
    cs, sn = jnp.cos(ang), jnp.sin(ang)
    n = pos.shape[0]
    cos_t = jnp.concatenate([jnp.ones((n, QK_NOPE), F32), cs, cs, jnp.zeros((n, z32.shape[0]), F32)], axis=1)
    sin_t = jnp.concatenate([jnp.zeros((n, QK_NOPE), F32), -sn, sn, jnp.zeros((n, z32.shape[0]), F32)], axis=1)
    wk_g = wk * g_k_nope[None, None, :]
    absorb = jnp.zeros((N_HEADS, HEAD_PAD, 2 * LANES), F32)
    absorb = absorb.at[:, :QK_NOPE, :KV_LORA].set(jnp.transpose(wk_g, (1, 2, 0)))
    absorb = absorb.at[:, ROPE_LO:ROPE_LO + QK_ROPE, KV_LORA:KV_LORA + QK_ROPE].set(
        jnp.broadcast_to(jnp.eye(QK_ROPE, dtype=F32), (N_HEADS, QK_ROPE, QK_ROPE)))
    w_kt = jnp.transpose(wk, (1, 2, 0)).reshape(N_HEADS * QK_NOPE, KV_LORA)
    wr = jnp.concatenate([w_router, jnp.zeros((d, LANES - N_EXPERTS), F32)], axis=1)
    wr_hi = wr.astype(BF16)
    wr_lo = (wr - wr_hi.astype(F32)).astype(BF16)
    br = jnp.concatenate([b_router, jnp.full((LANES - N_EXPERTS,), NEG_INF, F32)])
    wv_h = w_kv_up[..., QK_NOPE:]
    w_v_tiles = jnp.concatenate([wv_h, jnp.zeros((KV_LORA, N_HEADS, HEAD_PAD - V_DIM), F32)], axis=-1)
    v_ones = (jnp.arange(N_HEADS * HEAD_PAD) % HEAD_PAD == V_DIM).astype(F32).reshape(1, N_HEADS * HEAD_PAD)
    return {
        "w_v_tiles": w_v_tiles.reshape(KV_LORA, N_HEADS * HEAD_PAD).astype(BF16), "v_ones": v_ones,
        "w_router_hi": wr_hi, "w_router_lo": wr_lo,
        "w_in": w_in_r.astype(BF16), "g_norm_mix": g_norm_mix.reshape(1, d), "g_q_lat": g_q_lat.reshape(1, Q_LORA),
        "w_q_up": w_q_r.astype(BF16),
        "gain_q": jnp.concatenate([g_q_nope, g_q_rope, z32]).reshape(1, LANES),
        "m_grp": jnp.asarray(m_grp, BF16), "cos_t": cos_t, "sin_t": sin_t,
        "g_kv_lat": g_kv_lat.reshape(1, KV_LORA),
        "gain_k": jnp.concatenate([g_k_nope, z64]).reshape(1, LANES),
        "gain_kpe": jnp.concatenate([z64, g_k_rope, z32]).reshape(1, LANES),
        "w_k": w_k_r.astype(BF16), "w_v": wv.astype(BF16),
        "w_abs": absorb.reshape(N_HEADS * HEAD_PAD, 2 * LANES).astype(BF16), "w_kt": w_kt.astype(BF16),
        "g_conv_ln": g_conv_ln.reshape(1, CONV_CH), "b_conv_ln": b_conv_ln.reshape(1, CONV_CH),
        "g_out_attn": g_out_attn.reshape(1, ATTN_WIDTH), "g_out_conv": g_out_conv.reshape(1, CONV_CH),
        "w_out": w_out.astype(BF16), "g_norm_ffn": g_norm_ffn.reshape(1, d),
        "b_router": br.reshape(1, LANES),
    }


def _mod_table(mod_p, mod_s, t_new):
    return jnp.concatenate([jnp.broadcast_to(mod_p, (TOK_TILE, D_MODEL)), jnp.repeat(mod_s, t_new, axis=0)], axis=0)


def kernel(x_prompt, x_sample, cache_ckv, cache_kpe, state_conv, page_table, c_prompt, c_sample, w_ada, b_ada, g_norm_mix, g_norm_ffn, w_in, g_q_lat, w_q_up, g_q_nope, g_q_rope, g_kv_lat, g_k_rope, w_kv_up, g_k_nope, w_dw, b_dw, g_conv_ln, b_conv_ln, g_out_attn, g_out_conv, w_out, w_router, b_router, w_exp_up, b_exp_up, w_exp_down, b_exp_down):
    bsz, seq, d = x_prompt.shape
    n_seq, t_new = x_sample.shape[:2]
    depth = w_ada.shape[0]
    assert bsz == 1 and depth == 1 and t_new == 4 and d == D_MODEL
    n_p = bsz * seq
    n_s = n_seq * t_new
    n = n_p + n_s
    past = page_table.shape[1] * cache_ckv.shape[2]
    l = 0

    pos = jnp.concatenate([jnp.arange(seq, dtype=I32), jnp.tile(past + jnp.arange(t_new, dtype=I32), n_seq)])
    p = _prepare(pos, w_in[l], g_norm_mix[l], g_q_lat[l], w_q_up[l], g_q_nope[l], g_q_rope[l], g_kv_lat[l],
                 g_k_rope[l], w_kv_up[l], g_k_nope[l], g_conv_ln[l], b_conv_ln[l], g_out_attn[l], g_out_conv[l],
                 w_out[l], g_norm_ffn[l], w_router[l], b_router[l])

    n_c = 1 + n_seq
    c_all = jnp.concatenate([c_prompt, c_sample, jnp.zeros((-n_c % 8, d), F32)], axis=0)
    mod = _adaln(c_all, w_ada[l], b_ada[l])
    tabs = [_mod_table(mod[0:1, j * d:(j + 1) * d], mod[1:n_c, j * d:(j + 1) * d], t_new) for j in range(6)]
    sh_m, sc_m, gt_m, sh_f, sc_f, gt_f = tabs

    x_p = x_prompt.reshape(n_p, d)
    x_s = x_sample.reshape(n_s, d)
    q_all, ckv_all, kpe_all, u_all, k_all, v_all = _mixer_inputs(x_p, x_s, sh_m, sc_m, p)

    attn_p = _prompt_attention(q_all, k_all, v_all, seq)
    ckv_s = ckv_all[n_p:].reshape(n_seq, t_new, KV_LORA)
    kpe_s = kpe_all[n_p:].reshape(n_seq, t_new, QK_ROPE)
    ckv_new_pad = jnp.pad(ckv_s, ((0, 0), (0, LANES - t_new), (0, 0)))
    kpe_new_t = jnp.swapaxes(jnp.pad(kpe_s, ((0, 0), (0, LANES - t_new), (0, 0))), 1, 2)
    q_s = q_all[n_p:].reshape(n_seq, t_new, N_HEADS * HEAD_PAD)
    attn_s = _sample_attention(page_table, q_s, ckv_new_pad, kpe_new_t, cache_ckv,
                               jnp.swapaxes(cache_kpe, 2, 3), p)

    conv_p = _prompt_conv(u_all, w_dw[l], b_dw[l].reshape(1, CONV_CH))
    u_s = u_all[n_p:].reshape(n_seq, t_new, CONV_CH)
    u_ext_s = jnp.concatenate([state_conv[l], u_s], axis=1)
    ext = CONV_W - 1 + t_new
    ext_pad = -ext % 8
    w_taps = jnp.stack([jnp.pad(w_dw[l], ((t, t_new - 1 - t + ext_pad), (0, 0))) for t in range(t_new)])
    conv_s = _sample_conv(jnp.pad(u_ext_s, ((0, 0), (0, ext_pad), (0, 0))), w_taps, b_dw[l].reshape(1, CONV_CH))
    conv_s = jnp.transpose(conv_s, (1, 0, 2)).reshape(n_s, CONV_CH)

    y_all, f_all, top_i, top_w = _merge_router(x_p, x_s, attn_p, attn_s.reshape(n_s, ATTN_WIDTH), conv_p, conv_s,
                                               gt_m, sh_f, sc_f, p)

    n_tiles = -(-(n * TOP_K + N_EXPERTS * (FFN_TILE - 1)) // FFN_TILE)
    n_slots = n_tiles * FFN_TILE
    pos_tok, meta = _route(top_i, n_tiles)
    slot = jnp.transpose(pos_tok[:, :TOP_K])
    piece_idx = (slot[:, None, :] + (jnp.arange(SC_PIECES, dtype=I32) * n_slots)[None, :, None]).reshape(-1)
    x_sorted = _sc_scatter_rows(f_all.reshape(SC_PIECES * n, SC_ROW), piece_idx, SC_PIECES * n_slots)
    h_sorted = _expert_ffn(meta[:n_tiles, 0], meta[0:1, 1], x_sorted.reshape(SC_PIECES, n_slots, SC_ROW),
                           w_exp_up[l], b_exp_up[l], w_exp_down[l], b_exp_down[l])
    gathered = _sc_gather_rows(h_sorted.reshape(SC_PIECES * n_slots, SC_ROW), piece_idx)
    y_p, y_s = _combine(y_all, gathered.reshape(TOP_K, SC_PIECES, n, SC_ROW), top_w, gt_f, n_p)
    y_p = y_p.reshape(bsz, seq, d)
    y_s = y_s.reshape(n_seq, t_new, d)
    ckv_prompt = ckv_all[:n_p].reshape(1, bsz, seq, KV_LORA)
    kpe_prompt = kpe_all[:n_p].reshape(1, bsz, seq, QK_ROPE)
    conv_prompt = u_all[n_p - (CONV_W - 1):n_p].reshape(1, bsz, CONV_W - 1, CONV_CH)
    ckv_sample = ckv_s[None]
    kpe_sample = kpe_s[None]
    conv_sample = u_ext_s[:, t_new:][None]
    return (y_p, y_s, ckv_prompt, kpe_prompt, conv_prompt, ckv_sample, kpe_sample, conv_sample)
```

```python
import functools

import numpy as np
import jax
import jax.numpy as jnp
from jax import lax
from jax.experimental import pallas as pl
from jax.experimental.pallas import tpu as pltpu
from jax.experimental.pallas import tpu_sc as plsc

F32 = jnp.float32
BF16 = jnp.bfloat16
I32 = jnp.int32
HIGHEST = lax.Precision.HIGHEST

D_MODEL = 1024
N_HEADS = 8
QK_NOPE = 64
QK_ROPE = 32
V_DIM = 64
Q_LORA = 256
KV_LORA = 128
ATTN_WIDTH = N_HEADS * V_DIM
CONV_CH = D_MODEL - ATTN_WIDTH
CONV_W = 31
N_EXPERTS = 32
TOP_K = 4
D_FF = D_MODEL
SWIGLU_LIMIT = 7.0
SWIGLU_ALPHA = 1.702
EPS = 1e-6
NEG_INF = -1e30
ROPE_THETA = 10000.0
SM_SCALE = (QK_NOPE + QK_ROPE) ** -0.5
LOG2E = 1.4426950408889634
Q_SCALE = SM_SCALE * LOG2E

LANES = 128
SUBLANES = 8
HEAD_PAD = LANES
ROPE_LO = QK_NOPE
ROPE_HALF = QK_ROPE // 2

TOK_TILE = 512
FA_TQ = 1024
FA_TK = FA_TQ
FA_HEADS = 8
PAGES_PER_STEP = 64
FFN_TILE = 512
ROUTE_TILE = 512
SC_ROW = 256
PACKED_WIDTH = D_MODEL // 2
SC_PIECES = PACKED_WIDTH // SC_ROW
SC_WINDOW = 128
VMEM_LIMIT = 56 * 1024 * 1024


def _cparams(sem, vmem=None):
    return pltpu.CompilerParams(dimension_semantics=sem, vmem_limit_bytes=vmem)


def _rsqrt_mean(x, n):
    return lax.rsqrt(jnp.sum(x * x, axis=-1, keepdims=True) * (1.0 / n) + EPS)


def _pack_rows(x):
    half = x.shape[1] // 2
    hi = lax.bitcast_convert_type(x[:, :half].astype(BF16).astype(F32), I32)
    lo = lax.bitcast_convert_type(x[:, half:].astype(BF16).astype(F32), I32)
    return hi | lax.shift_right_logical(lo, jnp.full(lo.shape, 16, I32))


def _unpack_rows(w):
    hi = lax.bitcast_convert_type(w & jnp.int32(-65536), F32)
    lo = lax.bitcast_convert_type(lax.shift_left(w, jnp.full(w.shape, 16, I32)), F32)
    return jnp.concatenate([hi, lo], axis=1)


def _ada_kernel(c_ref, w_ref, b_ref, o_ref):
    c = c_ref[...]
    s = c * jax.nn.sigmoid(c)
    o_ref[...] = jnp.dot(s, w_ref[...], precision=HIGHEST, preferred_element_type=F32) + b_ref[...]


def _adaln(c_all, w_ada, b_ada):
    rows = c_all.shape[0]
    n_out = w_ada.shape[1]
    return pl.pallas_call(
        _ada_kernel,
        grid=(n_out // D_MODEL,),
        in_specs=[pl.BlockSpec((rows, D_MODEL), lambda j: (0, 0)),
                  pl.BlockSpec((D_MODEL, D_MODEL), lambda j: (0, j)),
                  pl.BlockSpec((1, D_MODEL), lambda j: (0, j))],
        out_specs=pl.BlockSpec((rows, D_MODEL), lambda j: (0, j)),
        out_shape=jax.ShapeDtypeStruct((rows, n_out), F32),
        compiler_params=_cparams(("arbitrary",)),
        name="adaln",
    )(c_all, w_ada, b_ada.reshape(1, n_out))


def _group_norm_rope(x, m_grp, gain, cos_t, sin_t, first_half):
    ms = jnp.dot((x * x).astype(BF16), m_grp, preferred_element_type=F32)
    xn = x * lax.rsqrt(ms + EPS) * gain
    swapped = jnp.where(first_half, pltpu.roll(xn, LANES - ROPE_HALF, 1), pltpu.roll(xn, ROPE_HALF, 1))
    return xn * cos_t + swapped * sin_t


def _pick_rows(n_prompt_tiles, prompt_ref, decode_ref):
    return jnp.where(pl.program_id(0) < n_prompt_tiles, prompt_ref[...], decode_ref[...])


def _mix_kernel(n_prompt_tiles, xp_ref, xs_ref, sh_ref, sc_ref, gmix_ref, win_ref, gql_ref, wq_ref, gq_ref, m_ref,
                cos_ref, sin_ref, gkv_ref, gk_ref, gkpe_ref, wk_ref, wv_ref, vone_ref,
                q_out, ckvp_out, ckvs_out, kpetp_out, kpets_out, u_out, k_out, v_out):
    x = _pick_rows(n_prompt_tiles, xp_ref, xs_ref)
    h = x * _rsqrt_mean(x, D_MODEL) * gmix_ref[...]
    h = h * (1.0 + sc_ref[...]) + sh_ref[...]
    proj = jnp.dot(h.astype(BF16), win_ref[...], preferred_element_type=F32)
    q_lat = proj[:, :Q_LORA]
    ckv_raw = proj[:, Q_LORA:Q_LORA + KV_LORA]
    kpe_blk = proj[:, Q_LORA + KV_LORA:Q_LORA + KV_LORA + LANES]
    glu_lo = Q_LORA + KV_LORA + LANES
    u_out[...] = proj[:, glu_lo:glu_lo + CONV_CH] * jax.nn.sigmoid(proj[:, glu_lo + CONV_CH:glu_lo + 2 * CONV_CH])

    m_grp = m_ref[...]
    cos_t = cos_ref[...]
    sin_t = sin_ref[...]
    lane = lax.broadcasted_iota(I32, (1, LANES), 1)
    first_half = lane < ROPE_LO + ROPE_HALF

    q_lat_n = q_lat * _rsqrt_mean(q_lat, Q_LORA) * gql_ref[...]
    q = jnp.dot(q_lat_n.astype(BF16), wq_ref[...], preferred_element_type=F32)
    gq = gq_ref[...]
    for hd in range(N_HEADS):
        qh = _group_norm_rope(q[:, hd * HEAD_PAD:(hd + 1) * HEAD_PAD], m_grp, gq, cos_t, sin_t, first_half)
        q_out[:, hd * HEAD_PAD:(hd + 1) * HEAD_PAD] = (qh * Q_SCALE).astype(BF16)

    ckv_n = ckv_raw * _rsqrt_mean(ckv_raw, KV_LORA) * gkv_ref[...]
    kpe_r = _group_norm_rope(kpe_blk, m_grp, gkpe_ref[...], cos_t, sin_t, first_half)
    kpe_t = jnp.transpose(kpe_r)[ROPE_LO:ROPE_LO + QK_ROPE, :]
    is_prompt = pl.program_id(0) < n_prompt_tiles

    @pl.when(is_prompt)
    def _():
        ckvp_out[...] = ckv_n
        kpetp_out[...] = kpe_t

    @pl.when(jnp.logical_not(is_prompt))
    def _():
        ckvs_out[...] = ckv_n
        kpets_out[...] = kpe_t

    ckv_b = ckv_n.astype(BF16)
    kexp = jnp.dot(ckv_b, wk_ref[...], preferred_element_type=F32)
    gk = gk_ref[...]
    for hd in range(N_HEADS):
        kh = kexp[:, hd * HEAD_PAD:(hd + 1) * HEAD_PAD]
        ms = jnp.dot((kh * kh).astype(BF16), m_grp, preferred_element_type=F32)
        k_out[:, hd * HEAD_PAD:(hd + 1) * HEAD_PAD] = (kh * lax.rsqrt(ms + EPS) * gk + kpe_r).astype(BF16)
    v_out[...] = (jnp.dot(ckv_b, wv_ref[...], preferred_element_type=F32) + vone_ref[...]).astype(BF16)


def _mod_index(n_prompt_tiles):
    return lambda i: (jnp.where(i < n_prompt_tiles, 0, i - n_prompt_tiles + 1), 0)


def _prompt_index(n_prompt_tiles):
    return lambda i: (jnp.minimum(i, n_prompt_tiles - 1), 0)


def _decode_index(n_prompt_tiles):
    return lambda i: (jnp.maximum(i - n_prompt_tiles, 0), 0)


def _mixer_inputs(x_p, x_s, sh, sc, p):
    n_prompt_tiles = x_p.shape[0] // TOK_TILE
    n = x_p.shape[0] + x_s.shape[0]
    tm = TOK_TILE
    const = lambda i: (0, 0)
    row = lambda i: (i, 0)
    mod = _mod_index(n_prompt_tiles)
    hw = N_HEADS * HEAD_PAD
    in_cols = p["w_in"].shape[1]
    return pl.pallas_call(
        functools.partial(_mix_kernel, n_prompt_tiles),
        grid=(n // tm,),
        in_specs=[pl.BlockSpec((tm, D_MODEL), _prompt_index(n_prompt_tiles)),
                  pl.BlockSpec((tm, D_MODEL), _decode_index(n_prompt_tiles)),
                  pl.BlockSpec((tm, D_MODEL), mod),
                  pl.BlockSpec((tm, D_MODEL), mod),
                  pl.BlockSpec((1, D_MODEL), const),
                  pl.BlockSpec((D_MODEL, in_cols), const),
                  pl.BlockSpec((1, Q_LORA), const),
                  pl.BlockSpec((Q_LORA, hw), const),
                  pl.BlockSpec((1, LANES), const),
                  pl.BlockSpec((LANES, LANES), const),
                  pl.BlockSpec((tm, LANES), row),
                  pl.BlockSpec((tm, LANES), row),
                  pl.BlockSpec((1, KV_LORA), const),
                  pl.BlockSpec((1, LANES), const),
                  pl.BlockSpec((1, LANES), const),
                  pl.BlockSpec((KV_LORA, hw), const),
                  pl.BlockSpec((KV_LORA, hw), const),
                  pl.BlockSpec((1, hw), const)],
        out_specs=[pl.BlockSpec((tm, hw), row),
                   pl.BlockSpec((tm, KV_LORA), _prompt_index(n_prompt_tiles)),
                   pl.BlockSpec((tm, KV_LORA), _decode_index(n_prompt_tiles)),
                   pl.BlockSpec((QK_ROPE, tm), lambda i: (0, jnp.minimum(i, n_prompt_tiles - 1))),
                   pl.BlockSpec((QK_ROPE, tm), lambda i: (0, jnp.maximum(i - n_prompt_tiles, 0))),
                   pl.BlockSpec((tm, CONV_CH), row),
                   pl.BlockSpec((tm, hw), row),
                   pl.BlockSpec((tm, hw), row)],
        out_shape=[jax.ShapeDtypeStruct((n, hw), BF16),
                   jax.ShapeDtypeStruct((x_p.shape[0], KV_LORA), F32),
                   jax.ShapeDtypeStruct((x_s.shape[0], KV_LORA), F32),
                   jax.ShapeDtypeStruct((QK_ROPE, x_p.shape[0]), F32),
                   jax.ShapeDtypeStruct((QK_ROPE, x_s.shape[0]), F32),
                   jax.ShapeDtypeStruct((n, CONV_CH), F32),
                   jax.ShapeDtypeStruct((n, hw), BF16),
                   jax.ShapeDtypeStruct((n, hw), BF16)],
        compiler_params=_cparams(("arbitrary",), VMEM_LIMIT),
        name="mixer_inputs",
    )(x_p, x_s, sh, sc, p["g_norm_mix"], p["w_in"], p["g_q_lat"], p["w_q_up"], p["gain_q"], p["m_grp"],
      p["cos_t"], p["sin_t"], p["g_kv_lat"], p["gain_k"], p["gain_kpe"], p["w_k"], p["w_v_tiles"], p["v_ones"])


def _fa_kernel(qt_ref, kt_ref, q_ref, k_ref, v_ref, o_ref, m_sc, acc_sc):
    t = pl.program_id(1)
    qi = qt_ref[t]
    ki = kt_ref[t]
    last_k = (qi + 1) * (FA_TQ // FA_TK) - 1

    @pl.when(ki == 0)
    def _():
        m_sc[...] = jnp.full(m_sc.shape, NEG_INF, F32)
        acc_sc[...] = jnp.zeros(acc_sc.shape, F32)

    def update(hh, r0, nr, k0, nk, masked):
        q = q_ref[pl.ds(r0, nr), hh * HEAD_PAD:(hh + 1) * HEAD_PAD]
        k = k_ref[pl.ds(k0, nk), hh * HEAD_PAD:(hh + 1) * HEAD_PAD]
        v = v_ref[pl.ds(k0, nk), hh * HEAD_PAD:(hh + 1) * HEAD_PAD]
        s = lax.dot_general(q, k, (((1,), (1,)), ((), ())), preferred_element_type=F32)
        if masked:
            col_minus_row = lax.broadcasted_iota(I32, (nr, nk), 1) - lax.broadcasted_iota(I32, (nr, nk), 0)
            s = jnp.where(col_minus_row <= (qi * FA_TQ + r0) - (ki * FA_TK + k0), s, NEG_INF)
        m_prev = m_sc[hh, pl.ds(r0, nr), :]
        m_new = jnp.maximum(m_prev, jnp.max(s, axis=-1, keepdims=True))
        alpha = jnp.exp2(m_prev - m_new)
        pr = jnp.exp2((s - jnp.concatenate([m_new] * (nk // LANES), axis=1)).astype(BF16))
        acc_sc[hh, pl.ds(r0, nr), :] = (alpha * acc_sc[hh, pl.ds(r0, nr), :]
                                        + jnp.dot(pr, v, preferred_element_type=F32))
        m_sc[hh, pl.ds(r0, nr), :] = m_new

    @pl.when(ki < qi)
    def _():
        for hh in range(FA_HEADS):
            update(hh, 0, FA_TQ, 0, FA_TK, False)

    @pl.when(ki == qi)
    def _():
        half = FA_TQ // 2
        for hh in range(FA_HEADS):
            update(hh, 0, half, 0, half, True)
            update(hh, half, half, 0, FA_TK, True)

    @pl.when(ki == last_k)
    def _():
        for hh in range(FA_HEADS):
            acc = acc_sc[hh]
            o_ref[:, hh * V_DIM:(hh + 1) * V_DIM] = acc[:, :V_DIM] / acc[:, V_DIM:V_DIM + 1]


def _prompt_attention(q_all, k_all, v_all, seq):
    nq = seq // FA_TQ
    ratio = FA_TQ // FA_TK
    qt, kt = [], []
    for qi in range(nq):
        for ki in range((qi + 1) * ratio):
            qt.append(qi)
            kt.append(ki)
    qt = jnp.asarray(np.array(qt, np.int32))
    kt = jnp.asarray(np.array(kt, np.int32))
    n_pairs = int(qt.shape[0])
    grid_spec = pltpu.PrefetchScalarGridSpec(
        num_scalar_prefetch=2,
        grid=(N_HEADS // FA_HEADS, n_pairs),
        in_specs=[pl.BlockSpec((FA_TQ, FA_HEADS * HEAD_PAD), lambda hp, t, qt, kt: (qt[t], hp)),
                  pl.BlockSpec((FA_TK, FA_HEADS * HEAD_PAD), lambda hp, t, qt, kt: (kt[t], hp)),
                  pl.BlockSpec((FA_TK, FA_HEADS * HEAD_PAD), lambda hp, t, qt, kt: (kt[t], hp))],
        out_specs=pl.BlockSpec((FA_TQ, FA_HEADS * V_DIM), lambda hp, t, qt, kt: (qt[t], hp)),
        scratch_shapes=[pltpu.VMEM((FA_HEADS, FA_TQ, LANES), F32),
                        pltpu.VMEM((FA_HEADS, FA_TQ, HEAD_PAD), F32)],
    )
    return pl.pallas_call(
        _fa_kernel,
        grid_spec=grid_spec,
        out_shape=jax.ShapeDtypeStruct((seq, ATTN_WIDTH), F32),
        compiler_params=_cparams(("parallel", "arbitrary"), VMEM_LIMIT),
        name="prompt_attention",
    )(qt, kt, q_all, k_all, v_all)


def _sattn_kernel(pt_ref, q_ref, wabs_ref, wkt_ref, wv_ref, ckvn_ref, kpen_ref, ckv_hbm, kpe_hbm,
                  o_ref, m_sc, l_sc, acc_sc, qa_sc, wall_sc, ckv_buf, kpe_buf, sem):
    kb = pl.program_id(1)
    n_kb = pl.num_programs(1)
    step = pl.program_id(0) * n_kb + kb
    n_steps = pl.num_programs(0) * n_kb
    slot = step % 2
    rows = 4 * N_HEADS

    def start_fetch(step_i, slot_i):
        for j in range(PAGES_PER_STEP):
            page = pt_ref[step_i * PAGES_PER_STEP + j]
            pltpu.make_async_copy(ckv_hbm.at[0, page], ckv_buf.at[slot_i, j], sem.at[0, slot_i]).start()
            pltpu.make_async_copy(kpe_hbm.at[0, page], kpe_buf.at[slot_i, j], sem.at[1, slot_i]).start()

    @pl.when(step == 0)
    def _():
        start_fetch(step, slot)

    @pl.when(step + 1 < n_steps)
    def _():
        start_fetch(step + 1, 1 - slot)

    @pl.when(kb == 0)
    def _():
        m_sc[...] = jnp.full(m_sc.shape, NEG_INF, F32)
        l_sc[...] = jnp.zeros(l_sc.shape, F32)
        acc_sc[...] = jnp.zeros(acc_sc.shape, F32)
        q4 = q_ref[...].astype(F32)
        head_of_lane = lax.broadcasted_iota(I32, (N_HEADS, N_HEADS * HEAD_PAD), 1) // HEAD_PAD
        head_of_row = lax.broadcasted_iota(I32, (N_HEADS, N_HEADS * HEAD_PAD), 0)
        own = head_of_lane == head_of_row
        qbd = jnp.concatenate(
            [jnp.where(own, jnp.broadcast_to(q4[qq:qq + 1, :], own.shape), 0.0) for qq in range(4)], axis=0)
        qa = jnp.dot(qbd.astype(BF16), wabs_ref[...], preferred_element_type=F32).astype(BF16)
        qa_sc[...] = qa
        wall_sc[pl.ds(0, N_HEADS * QK_NOPE), :] = wkt_ref[...]
        wall_sc[pl.ds(N_HEADS * QK_NOPE, rows), :] = qa[:, :KV_LORA]

    def attend(state, ckv_b, kpe_t, mask):
        m_prev, l_prev, acc_prev = state
        nt = (((1,), (1,)), ((), ()))
        kn_all = lax.dot_general(wall_sc[...], ckv_b, nt, preferred_element_type=F32)
        keys = kn_all.shape[1]
        kn_t = kn_all[:N_HEADS * QK_NOPE]
        ss = jnp.sum((kn_t * kn_t).reshape(N_HEADS, QK_NOPE, keys), axis=1)
        r8 = lax.rsqrt(ss * (1.0 / QK_NOPE) + EPS)
        rope = jnp.dot(qa_sc[:, KV_LORA:KV_LORA + QK_ROPE], kpe_t, preferred_element_type=F32)
        s = kn_all[N_HEADS * QK_NOPE:] * jnp.concatenate([r8] * 4, axis=0) + rope
        if mask is not None:
            s = jnp.where(mask, s, NEG_INF)
        m_new = jnp.maximum(m_prev, jnp.max(s, axis=-1, keepdims=True))
        alpha = jnp.exp2(m_prev - m_new)
        pr = jnp.exp2(s - m_new)
        l_new = alpha * l_prev + jnp.sum(pr, axis=-1, keepdims=True)
        acc_new = alpha * acc_prev + jnp.dot(pr.astype(BF16), ckv_b, preferred_element_type=F32)
        return m_new, l_new, acc_new

    pltpu.make_async_copy(ckv_hbm.at[0, pl.ds(0, PAGES_PER_STEP)], ckv_buf.at[slot], sem.at[0, slot]).wait()
    pltpu.make_async_copy(kpe_hbm.at[0, pl.ds(0, PAGES_PER_STEP)], kpe_buf.at[slot], sem.at[1, slot]).wait()

    page = ckv_buf.shape[2]
    ckv_b = ckv_buf[slot].reshape(PAGES_PER_STEP * page, KV_LORA).astype(BF16)
    kpe_t = jnp.concatenate([kpe_buf[slot, j].astype(BF16) for j in range(PAGES_PER_STEP)], axis=1)
    state = attend((m_sc[...], l_sc[...], acc_sc[...]), ckv_b, kpe_t, None)
    m_sc[...], l_sc[...], acc_sc[...] = state

    @pl.when(kb == n_kb - 1)
    def _():
        key = lax.broadcasted_iota(I32, (rows, LANES), 1)
        qry = lax.broadcasted_iota(I32, (rows, LANES), 0) // N_HEADS
        _, l_fin, acc_fin = attend(state, ckvn_ref[...].astype(BF16), kpen_ref[...].astype(BF16), key <= qry)
        lat = acc_fin / l_fin
        o_all = jnp.dot(lat.astype(BF16), wv_ref[...], preferred_element_type=F32)
        head_of_col = lax.broadcasted_iota(I32, (rows, ATTN_WIDTH), 1) // V_DIM
        head_of_row = lax.broadcasted_iota(I32, (rows, ATTN_WIDTH), 0) % N_HEADS
        o_own = jnp.where(head_of_col == head_of_row, o_all, 0.0)
        o_ref[...] = jnp.sum(o_own.reshape(4, N_HEADS, ATTN_WIDTH), axis=1)


def _sample_attention(page_table, q_s, ckv_new_pad, kpe_new_t, cache_ckv, cache_kpe_t, p):
    n_seq, n_pages = page_table.shape
    page = cache_ckv.shape[2]
    n_kb = n_pages // PAGES_PER_STEP
    hw = N_HEADS * HEAD_PAD
    assert n_pages % PAGES_PER_STEP == 0
    per_seq3 = lambda b, kb, pt: (b, 0, 0)
    const = lambda b, kb, pt: (0, 0)
    in_specs = [pl.BlockSpec((None, 4, hw), per_seq3),
                pl.BlockSpec((hw, 2 * LANES), const),
                pl.BlockSpec((N_HEADS * QK_NOPE, KV_LORA), const),
                pl.BlockSpec((KV_LORA, ATTN_WIDTH), const),
                pl.BlockSpec((None, LANES, KV_LORA), per_seq3),
                pl.BlockSpec((None, QK_ROPE, LANES), per_seq3),
                pl.BlockSpec(memory_space=pl.ANY),
                pl.BlockSpec(memory_space=pl.ANY)]
    grid_spec = pltpu.PrefetchScalarGridSpec(
        num_scalar_prefetch=1,
        grid=(n_seq, n_kb),
        in_specs=in_specs,
        out_specs=pl.BlockSpec((None, 4, ATTN_WIDTH), per_seq3),
        scratch_shapes=[pltpu.VMEM((4 * N_HEADS, 1), F32),
                        pltpu.VMEM((4 * N_HEADS, 1), F32),
                        pltpu.VMEM((4 * N_HEADS, KV_LORA), F32),
                        pltpu.VMEM((4 * N_HEADS, 2 * LANES), BF16),
                        pltpu.VMEM((N_HEADS * QK_NOPE + 4 * N_HEADS, KV_LORA), BF16),
                        pltpu.VMEM((2, PAGES_PER_STEP, page, KV_LORA), F32),
                        pltpu.VMEM((2, PAGES_PER_STEP, QK_ROPE, page), F32),
                        pltpu.SemaphoreType.DMA((2, 2))],
    )
    return pl.pallas_call(
        _sattn_kernel,
        grid_spec=grid_spec,
        out_shape=jax.ShapeDtypeStruct((n_seq, 4, ATTN_WIDTH), F32),
        compiler_params=_cparams(("arbitrary", "arbitrary"), VMEM_LIMIT),
        name="decode_attention",
    )(page_table.reshape(-1), q_s, p["w_abs"], p["w_kt"], p["w_v"], ckv_new_pad, kpe_new_t, cache_ckv, cache_kpe_t)


CONV_HALO = 32
CONV_ROWS = 64


def _conv_kernel(halo_ref, u_ref, w_ref, b_ref, o_ref, ext_sc, sh_sc):
    i = pl.program_id(0)
    tm = u_ref.shape[0]
    ext_sc[pl.ds(0, CONV_HALO), :] = jnp.where(i == 0, 0.0, halo_ref[...])
    ext_sc[pl.ds(CONV_HALO, tm), :] = u_ref[...]
    first = CONV_HALO - (CONV_W - 1)
    span = tm + CONV_HALO - SUBLANES
    for sft in range(1, SUBLANES):
        sh_sc[sft - 1] = ext_sc[pl.ds(sft, span), :]
    for rc in range(tm // CONV_ROWS):
        acc = jnp.broadcast_to(b_ref[...], (CONV_ROWS, CONV_CH))
        for j in range(CONV_W):
            a8, sft = divmod(first + j, SUBLANES)
            src = ext_sc if sft == 0 else sh_sc.at[sft - 1]
            acc = acc + src[pl.ds(rc * CONV_ROWS + SUBLANES * a8, CONV_ROWS), :] * w_ref[j:j + 1, :]
        o_ref[pl.ds(rc * CONV_ROWS, CONV_ROWS), :] = acc


def _prompt_conv(u_all, w_dw, b_dw):
    n = u_all.shape[0]
    tm = TOK_TILE
    per = tm // CONV_HALO
    return pl.pallas_call(
        _conv_kernel,
        grid=(n // tm,),
        in_specs=[pl.BlockSpec((CONV_HALO, CONV_CH), lambda i: (jnp.maximum(i * per - 1, 0), 0)),
                  pl.BlockSpec((tm, CONV_CH), lambda i: (i, 0)),
                  pl.BlockSpec((CONV_W, CONV_CH), lambda i: (0, 0)),
                  pl.BlockSpec((1, CONV_CH), lambda i: (0, 0))],
        out_specs=pl.BlockSpec((tm, CONV_CH), lambda i: (i, 0)),
        out_shape=jax.ShapeDtypeStruct((n, CONV_CH), F32),
        scratch_shapes=[pltpu.VMEM((CONV_HALO + tm, CONV_CH), F32),
                        pltpu.VMEM((SUBLANES - 1, CONV_HALO + tm - SUBLANES, CONV_CH), F32)],
        compiler_params=_cparams(("parallel",)),
        name="prompt_conv",
    )(u_all, u_all, w_dw, b_dw)


def _sconv_kernel(u_ref, wt_ref, b_ref, o_ref):
    u = u_ref[...]
    for t in range(o_ref.shape[0]):
        o_ref[t] = jnp.sum(u * wt_ref[t][None, :, :], axis=1) + b_ref[...]


def _sample_conv(u_ext, w_taps, b_dw):
    n_seq, ext, _ = u_ext.shape
    t_new = w_taps.shape[0]
    sb = 8
    return pl.pallas_call(
        _sconv_kernel,
        grid=(n_seq // sb,),
        in_specs=[pl.BlockSpec((sb, ext, CONV_CH), lambda i: (i, 0, 0)),
                  pl.BlockSpec((t_new, ext, CONV_CH), lambda i: (0, 0, 0)),
                  pl.BlockSpec((1, CONV_CH), lambda i: (0, 0))],
        out_specs=pl.BlockSpec((t_new, sb, CONV_CH), lambda i: (0, i, 0)),
        out_shape=jax.ShapeDtypeStruct((t_new, n_seq, CONV_CH), F32),
        compiler_params=_cparams(("parallel",)),
        name="decode_conv",
    )(u_ext, w_taps, b_dw)


def _lane_pack(cols, dtype):
    lane = lax.broadcasted_iota(I32, (cols[0].shape[0], LANES), 1)
    out = jnp.zeros((cols[0].shape[0], LANES), dtype)
    for j, c in enumerate(cols):
        out = jnp.where(lane == j, c.astype(dtype), out)
    return out


def _merge_kernel(n_prompt_tiles, xp_ref, xs_ref, ap_ref, as_ref, cp_ref, cs_ref, gate_ref, shf_ref, scf_ref,
                  gln_ref, bln_ref, ga_ref, gc_ref, wout_ref, gffn_ref, wrh_ref, wrl_ref, br_ref,
                  y_out, f_out, ti_out, tw_out):
    yc = _pick_rows(n_prompt_tiles, cp_ref, cs_ref)
    mu = jnp.mean(yc, axis=-1, keepdims=True)
    xc = yc - mu
    var = jnp.mean(xc * xc, axis=-1, keepdims=True)
    ln = xc * lax.rsqrt(var + EPS) * gln_ref[...] + bln_ref[...]
    conv = ln * jax.nn.sigmoid(ln)
    attn = _pick_rows(n_prompt_tiles, ap_ref, as_ref)
    a_n = attn * _rsqrt_mean(attn, ATTN_WIDTH) * ga_ref[...]
    c_n = conv * _rsqrt_mean(conv, CONV_CH) * gc_ref[...]
    m = (jnp.dot(a_n.astype(BF16), wout_ref[:ATTN_WIDTH, :], preferred_element_type=F32)
         + jnp.dot(c_n.astype(BF16), wout_ref[ATTN_WIDTH:, :], preferred_element_type=F32))
    y = _pick_rows(n_prompt_tiles, xp_ref, xs_ref) + gate_ref[...] * m
    y_out[...] = y
    f = y * _rsqrt_mean(y, D_MODEL) * gffn_ref[...]
    f = f * (1.0 + scf_ref[...]) + shf_ref[...]
    f_words = _pack_rows(f)
    for j in range(SC_PIECES):
        f_out[j] = f_words[:, j * SC_ROW:(j + 1) * SC_ROW]
    f_hi = f.astype(BF16)
    f_lo = (f - f_hi.astype(F32)).astype(BF16)
    logits = (jnp.dot(f_hi, wrh_ref[...], preferred_element_type=F32)
              + jnp.dot(f_hi, wrl_ref[...], preferred_element_type=F32)
              + jnp.dot(f_lo, wrh_ref[...], preferred_element_type=F32)) + br_ref[...]
    lane = lax.broadcasted_iota(I32, logits.shape, 1)
    vals, idxs = [], []
    for _ in range(TOP_K):
        mx = jnp.max(logits, axis=-1, keepdims=True)
        ix = jnp.min(jnp.where(logits == mx, lane, LANES), axis=-1, keepdims=True)
        vals.append(mx)
        idxs.append(ix)
        logits = jnp.where(lane == ix, NEG_INF * 4.0, logits)
    exps = [jnp.exp(v - vals[0]) for v in vals]
    tot = exps[0] + exps[1] + exps[2] + exps[3]
    ti_out[...] = _lane_pack(idxs, I32)
    tw_out[...] = _lane_pack([e / tot for e in exps], F32)


def _merge_router(x_p, x_s, attn_p, attn_s, conv_p, conv_s, gate, shf, scf, p):
    n_prompt_tiles = x_p.shape[0] // TOK_TILE
    n = x_p.shape[0] + x_s.shape[0]
    tm = TOK_TILE
    const = lambda i: (0, 0)
    row = lambda i: (i, 0)
    mod = _mod_index(n_prompt_tiles)
    from_p = _prompt_index(n_prompt_tiles)
    from_s = _decode_index(n_prompt_tiles)
    return pl.pallas_call(
        functools.partial(_merge_kernel, n_prompt_tiles),
        grid=(n // tm,),
        in_specs=[pl.BlockSpec((tm, D_MODEL), from_p),
                  pl.BlockSpec((tm, D_MODEL), from_s),
                  pl.BlockSpec((tm, ATTN_WIDTH), from_p),
                  pl.BlockSpec((tm, ATTN_WIDTH), from_s),
                  pl.BlockSpec((tm, CONV_CH), from_p),
                  pl.BlockSpec((tm, CONV_CH), from_s),
                  pl.BlockSpec((tm, D_MODEL), mod),
                  pl.BlockSpec((tm, D_MODEL), mod),
                  pl.BlockSpec((tm, D_MODEL), mod),
                  pl.BlockSpec((1, CONV_CH), const),
                  pl.BlockSpec((1, CONV_CH), const),
                  pl.BlockSpec((1, ATTN_WIDTH), const),
                  pl.BlockSpec((1, CONV_CH), const),
                  pl.BlockSpec((D_MODEL, D_MODEL), const),
                  pl.BlockSpec((1, D_MODEL), const),
                  pl.BlockSpec((D_MODEL, LANES), const),
                  pl.BlockSpec((D_MODEL, LANES), const),
                  pl.BlockSpec((1, LANES), const)],
        out_specs=[pl.BlockSpec((tm, D_MODEL), row),
                   pl.BlockSpec((SC_PIECES, tm, SC_ROW), lambda i: (0, i, 0)),
                   pl.BlockSpec((tm, LANES), row),
                   pl.BlockSpec((tm, LANES), row)],
        out_shape=[jax.ShapeDtypeStruct((n, D_MODEL), F32),
                   jax.ShapeDtypeStruct((SC_PIECES, n, SC_ROW), I32),
                   jax.ShapeDtypeStruct((n, LANES), I32),
                   jax.ShapeDtypeStruct((n, LANES), F32)],
        compiler_params=_cparams(("parallel",), VMEM_LIMIT),
        name="merge_router",
    )(x_p, x_s, attn_p, attn_s, conv_p, conv_s, gate, shf, scf, p["g_conv_ln"], p["b_conv_ln"], p["g_out_attn"],
      p["g_out_conv"], p["w_out"], p["g_norm_ffn"], p["w_router_hi"], p["w_router_lo"], p["b_router"])


def _select_lane(table, idx_col, lane):
    return jnp.sum(jnp.where(lane == idx_col, table, 0.0), axis=-1, keepdims=True)


def _rank_kernel(ti_ref, rk_out, cnt_out, carry_sc):
    i = pl.program_id(0)

    @pl.when(i == 0)
    def _():
        carry_sc[...] = jnp.zeros(carry_sc.shape, F32)

    ti = ti_ref[...]
    tn = ti.shape[0]
    lane = lax.broadcasted_iota(I32, (tn, LANES), 1)
    sel = jnp.zeros((tn, LANES), F32)
    for k in range(TOP_K):
        sel = sel + (lane == ti[:, k:k + 1]).astype(F32)
    r_i = lax.broadcasted_iota(I32, (tn, tn), 0)
    c_i = lax.broadcasted_iota(I32, (tn, tn), 1)
    below = (c_i < r_i).astype(BF16)
    rank = carry_sc[...] + jnp.dot(below, sel.astype(BF16), preferred_element_type=F32)
    rk_out[...] = _lane_pack([_select_lane(rank, ti[:, k:k + 1], lane) for k in range(TOP_K)], F32)
    carry_sc[...] = carry_sc[...] + jnp.sum(sel, axis=0, keepdims=True)
    cnt_out[...] = jnp.broadcast_to(carry_sc[...], cnt_out.shape)


def _pos_kernel(cnt_ref, ti_ref, rk_ref, pos_out, meta_out):
    cnt = cnt_ref[...]
    padded = jnp.ceil(cnt * (1.0 / FFN_TILE)) * FFN_TILE
    r_i = lax.broadcasted_iota(I32, (LANES, LANES), 0)
    c_i = lax.broadcasted_iota(I32, (LANES, LANES), 1)
    before = (r_i < c_i).astype(F32)
    offs = jnp.dot(padded, before, precision=HIGHEST, preferred_element_type=F32)
    ends = offs + padded
    ti = ti_ref[...]
    tn = ti.shape[0]
    lane = lax.broadcasted_iota(I32, (tn, LANES), 1)
    off_row = offs[0:1, :]
    rk = rk_ref[...]
    pos = [_select_lane(jnp.broadcast_to(off_row, (tn, LANES)), ti[:, k:k + 1], lane) + rk[:, k:k + 1]
           for k in range(TOP_K)]
    pos_out[...] = _lane_pack(pos, F32).astype(I32)

    @pl.when(pl.program_id(0) == 0)
    def _():
        nt = meta_out.shape[0]
        start = (lax.broadcasted_iota(I32, (nt, LANES), 0) * FFN_TILE).astype(F32)
        elane = lax.broadcasted_iota(I32, (nt, LANES), 1)
        done = jnp.where((elane < N_EXPERTS) & (jnp.broadcast_to(ends[0:1, :], (nt, LANES)) <= start), 1.0, 0.0)
        expert = jnp.minimum(jnp.sum(done, axis=-1, keepdims=True), N_EXPERTS - 1.0)
        total = jnp.sum(jnp.where(elane < N_EXPERTS, jnp.broadcast_to(padded[0:1, :], (nt, LANES)), 0.0),
                        axis=-1, keepdims=True)
        meta_out[...] = _lane_pack([expert, total * (1.0 / FFN_TILE)], F32).astype(I32)


def _route(top_i, n_tiles_max):
    n = top_i.shape[0]
    tn = ROUTE_TILE
    rk, cnt = pl.pallas_call(
        _rank_kernel,
        grid=(n // tn,),
        in_specs=[pl.BlockSpec((tn, LANES), lambda i: (i, 0))],
        out_specs=[pl.BlockSpec((tn, LANES), lambda i: (i, 0)),
                   pl.BlockSpec((8, LANES), lambda i: (0, 0))],
        out_shape=[jax.ShapeDtypeStruct((n, LANES), F32),
                   jax.ShapeDtypeStruct((8, LANES), F32)],
        scratch_shapes=[pltpu.VMEM((1, LANES), F32)],
        compiler_params=_cparams(("arbitrary",)),
        name="route_rank",
    )(top_i)
    nt_pad = -(-n_tiles_max // 8) * 8
    pos, meta = pl.pallas_call(
        _pos_kernel,
        grid=(n // tn,),
        in_specs=[pl.BlockSpec((8, LANES), lambda i: (0, 0)),
                  pl.BlockSpec((tn, LANES), lambda i: (i, 0)),
                  pl.BlockSpec((tn, LANES), lambda i: (i, 0))],
        out_specs=[pl.BlockSpec((tn, LANES), lambda i: (i, 0)),
                   pl.BlockSpec((nt_pad, LANES), lambda i: (0, 0))],
        out_shape=[jax.ShapeDtypeStruct((n, LANES), I32),
                   jax.ShapeDtypeStruct((nt_pad, LANES), I32)],
        compiler_params=_cparams(("arbitrary",)),
        name="route_pos",
    )(cnt, top_i, rk)
    return pos, meta


def _sc_scatter_rows(x, idx, n_out):
    n_src = x.shape[0]
    n = idx.shape[0]
    n_src_blk = n_src // SC_WINDOW
    mesh = plsc.VectorSubcoreMesh(core_axis_name="c", subcore_axis_name="s")

    @pl.kernel(out_type=jax.ShapeDtypeStruct((n_out, SC_ROW), x.dtype), mesh=mesh)
    def k(x_hbm, i_hbm, o_hbm):
        def body(x_vmem, i_vmem):
            pltpu.sync_copy(x_vmem, o_hbm.at[i_vmem.at[0]])

        pltpu.emit_pipeline(
            body,
            grid=(n // SC_WINDOW,),
            in_specs=[pl.BlockSpec((SC_WINDOW, SC_ROW), index_map=lambda i: (i % n_src_blk, 0)),
                      pl.BlockSpec((1, SC_WINDOW), index_map=lambda i: (0, i))],
            out_specs=[],
            core_axis_name=("c", "s"),
            dimension_semantics=(pltpu.PARALLEL,),
        )(x_hbm, i_hbm)

    return k(x, idx.reshape(1, n))


def _sc_gather_rows(x, idx):
    n = idx.shape[0]
    mesh = plsc.VectorSubcoreMesh(core_axis_name="c", subcore_axis_name="s")

    @pl.kernel(out_type=jax.ShapeDtypeStruct((n, SC_ROW), x.dtype), mesh=mesh)
    def k(x_hbm, i_hbm, o_hbm):
        def body(i_vmem, o_vmem):
            pltpu.sync_copy(x_hbm.at[i_vmem.at[0]], o_vmem)

        pltpu.emit_pipeline(
            body,
            grid=(n // SC_WINDOW,),
            in_specs=[pl.BlockSpec((1, SC_WINDOW), index_map=lambda i: (0, i))],
            out_specs=[pl.BlockSpec((SC_WINDOW, SC_ROW), index_map=lambda i: (i, 0))],
            core_axis_name=("c", "s"),
            dimension_semantics=(pltpu.PARALLEL,),
        )(i_hbm, o_hbm)

    return k(x, idx.reshape(1, n))


def _ffn_kernel(te_ref, nv_ref, x_ref, wu_ref, bu_ref, wd_ref, bd_ref, o_ref, wu_sc, wd_sc):
    t = pl.program_id(0)
    valid = t < nv_ref[0]
    prev = te_ref[jnp.maximum(t - 1, 0)]
    fresh = jnp.logical_or(t == 0, te_ref[t] != prev)

    @pl.when(jnp.logical_and(valid, fresh))
    def _():
        wu_sc[...] = wu_ref[...].astype(BF16)
        wd_sc[...] = wd_ref[...].astype(BF16)

    @pl.when(valid)
    def _():
        x = _unpack_rows(jnp.concatenate([x_ref[j] for j in range(SC_PIECES)], axis=1))
        z = jnp.dot(x.astype(BF16), wu_sc[...], preferred_element_type=F32) + bu_ref[...]
        zg = jnp.minimum(z[:, :D_FF], SWIGLU_LIMIT)
        zl = jnp.clip(z[:, D_FF:], -SWIGLU_LIMIT, SWIGLU_LIMIT)
        act = zg * jax.nn.sigmoid(SWIGLU_ALPHA * zg) * (zl + 1.0)
        out = _pack_rows(jnp.dot(act.astype(BF16), wd_sc[...], preferred_element_type=F32) + bd_ref[...])
        for j in range(SC_PIECES):
            o_ref[j] = out[:, j * SC_ROW:(j + 1) * SC_ROW]


def _expert_ffn(tile_expert, n_valid, x_sorted, w_up, b_up, w_down, b_down):
    n_slots = x_sorted.shape[1]
    n_tiles = n_slots // FFN_TILE
    xmap = lambda t, te, nv: (0, jnp.minimum(t, nv[0] - 1), 0)
    emap = lambda t, te, nv: (te[t], 0, 0)
    grid_spec = pltpu.PrefetchScalarGridSpec(
        num_scalar_prefetch=2,
        grid=(n_tiles,),
        in_specs=[pl.BlockSpec((SC_PIECES, FFN_TILE, SC_ROW), xmap),
                  pl.BlockSpec((None, D_MODEL, 2 * D_FF), emap),
                  pl.BlockSpec((None, 1, 2 * D_FF), emap),
                  pl.BlockSpec((None, D_FF, D_MODEL), emap),
                  pl.BlockSpec((None, 1, D_MODEL), emap)],
        out_specs=pl.BlockSpec((SC_PIECES, FFN_TILE, SC_ROW), xmap),
        scratch_shapes=[pltpu.VMEM((D_MODEL, 2 * D_FF), BF16),
                        pltpu.VMEM((D_FF, D_MODEL), BF16)],
    )
    return pl.pallas_call(
        _ffn_kernel,
        grid_spec=grid_spec,
        out_shape=jax.ShapeDtypeStruct((SC_PIECES, n_slots, SC_ROW), I32),
        compiler_params=_cparams(("arbitrary",), VMEM_LIMIT),
        name="expert_ffn",
    )(tile_expert, n_valid, x_sorted, w_up, b_up.reshape(N_EXPERTS, 1, 2 * D_FF), w_down,
      b_down.reshape(N_EXPERTS, 1, D_MODEL))


def _combine_kernel(n_prompt_tiles, y_ref, g_ref, tw_ref, gate_ref, op_ref, os_ref):
    tw = tw_ref[...]
    rows = lambda k: _unpack_rows(jnp.concatenate([g_ref[k, j] for j in range(SC_PIECES)], axis=1))
    moe = rows(0) * tw[:, 0:1]
    for k in range(1, TOP_K):
        moe = moe + rows(k) * tw[:, k:k + 1]
    out = y_ref[...] + gate_ref[...] * moe
    i = pl.program_id(0)

    @pl.when(i < n_prompt_tiles)
    def _():
        op_ref[...] = out

    @pl.when(i >= n_prompt_tiles)
    def _():
        os_ref[...] = out


def _combine(y_all, gathered, top_w, gate, n_p):
    n = y_all.shape[0]
    tm = TOK_TILE
    n_prompt_tiles = n_p // tm
    row = lambda i: (i, 0)
    return pl.pallas_call(
        functools.partial(_combine_kernel, n_prompt_tiles),
        grid=(n // tm,),
        in_specs=[pl.BlockSpec((tm, D_MODEL), row),
                  pl.BlockSpec((TOP_K, SC_PIECES, tm, SC_ROW), lambda i: (0, 0, i, 0)),
                  pl.BlockSpec((tm, LANES), row),
                  pl.BlockSpec((tm, D_MODEL), _mod_index(n_prompt_tiles))],
        out_specs=[pl.BlockSpec((tm, D_MODEL), _prompt_index(n_prompt_tiles)),
                   pl.BlockSpec((tm, D_MODEL), _decode_index(n_prompt_tiles))],
        out_shape=[jax.ShapeDtypeStruct((n_p, D_MODEL), F32),
                   jax.ShapeDtypeStruct((n - n_p, D_MODEL), F32)],
        compiler_params=_cparams(("arbitrary",), VMEM_LIMIT),
        name="moe_combine",
    )(y_all, gathered, top_w, gate)


def _head_tiles(nope, rope):
    pad = jnp.zeros(nope.shape[:-1] + (HEAD_PAD - QK_NOPE - QK_ROPE,), nope.dtype)
    t = jnp.concatenate([nope, rope, pad], axis=-1)
    return t.reshape(t.shape[:-2] + (N_HEADS * HEAD_PAD,))


def _prepare(pos, w_in, g_norm_mix, g_q_lat, w_q_up, g_q_nope, g_q_rope, g_kv_lat, g_k_rope, w_kv_up, g_k_nope,
             g_conv_ln, b_conv_ln, g_out_attn, g_out_conv, w_out, g_norm_ffn, w_router, b_router):
    z32 = jnp.zeros((HEAD_PAD - QK_NOPE - QK_ROPE,), F32)
    z64 = jnp.zeros((QK_NOPE,), F32)
    d = D_MODEL
    kpe_cols = w_in[:, Q_LORA + KV_LORA:Q_LORA + KV_LORA + QK_ROPE]
    kpe_tile = jnp.concatenate([jnp.zeros((d, QK_NOPE), F32), kpe_cols, jnp.zeros((d, z32.shape[0]), F32)], axis=1)
    w_in_r = jnp.concatenate([w_in[:, :Q_LORA + KV_LORA], kpe_tile, w_in[:, Q_LORA + KV_LORA + QK_ROPE:]], axis=1)
    wq = w_q_up.reshape(Q_LORA, N_HEADS, QK_NOPE + QK_ROPE)
    w_q_r = _head_tiles(wq[..., :QK_NOPE], wq[..., QK_NOPE:])
    wk = w_kv_up[..., :QK_NOPE]
    wv = w_kv_up[..., QK_NOPE:].reshape(KV_LORA, ATTN_WIDTH)
    w_k_r = _head_tiles(wk, jnp.zeros((KV_LORA, N_HEADS, QK_ROPE), F32))
    lane = np.arange(LANES)
    grp = np.where(lane < QK_NOPE, 0, np.where(lane < QK_NOPE + QK_ROPE, 1, 2))
    m_grp = ((grp[:, None] == grp[None, :]) & (grp[:, None] < 2)).astype(np.float32)
    m_grp = m_grp / np.where(grp < 1, QK_NOPE, QK_ROPE)[None, :]
    inv = np.float32(ROPE_THETA) ** (-np.arange(0, QK_ROPE, 2, dtype=np.float32) / np.float32(QK_ROPE))
    ang = pos.astype(np.float32)[:, None] * inv[None, :].astype(np.float32)
    cs, sn = jnp.asarray(np.cos(ang), F32), jnp.asarray(np.sin(ang), F32)
    n = pos.shape[0]
    cos_t = jnp.concatenate([jnp.ones((n, QK_NOPE), F32), cs, cs, jnp.zeros((n, z32.shape[0]), F32)], axis=1)
    sin_t = jnp.concatenate([jnp.zeros((n, QK_NOPE), F32), -sn, sn, jnp.zeros((n, z32.shape[0]), F32)], axis=1)
    wk_g = wk * g_k_nope[None, None, :]
    absorb = jnp.zeros((N_HEADS, HEAD_PAD, 2 * LANES), F32)
    absorb = absorb.at[:, :QK_NOPE, :KV_LORA].set(jnp.transpose(wk_g, (1, 2, 0)))
    absorb = absorb.at[:, ROPE_LO:ROPE_LO + QK_ROPE, KV_LORA:KV_LORA + QK_ROPE].set(
        jnp.broadcast_to(jnp.eye(QK_ROPE, dtype=F32), (N_HEADS, QK_ROPE, QK_ROPE)))
    w_kt = jnp.transpose(wk, (1, 2, 0)).reshape(N_HEADS * QK_NOPE, KV_LORA)
    wr = jnp.concatenate([w_router, jnp.zeros((d, LANES - N_EXPERTS), F32)], axis=1)
    wr_hi = wr.astype(BF16)
    wr_lo = (wr - wr_hi.astype(F32)).astype(BF16)
    br = jnp.concatenate([b_router, jnp.full((LANES - N_EXPERTS,), NEG_INF, F32)])
    wv_h = w_kv_up[..., QK_NOPE:]
    w_v_tiles = jnp.concatenate([wv_h, jnp.zeros((KV_LORA, N_HEADS, HEAD_PAD - V_DIM), F32)], axis=-1)
    v_ones = (jnp.arange(N_HEADS * HEAD_PAD) % HEAD_PAD == V_DIM).astype(F32).reshape(1, N_HEADS * HEAD_PAD)
    return {
        "w_v_tiles": w_v_tiles.reshape(KV_LORA, N_HEADS * HEAD_PAD).astype(BF16), "v_ones": v_ones,
        "w_router_hi": wr_hi, "w_router_lo": wr_lo,
        "w_in": w_in_r.astype(BF16), "g_norm_mix": g_norm_mix.reshape(1, d), "g_q_lat": g_q_lat.reshape(1, Q_LORA),
        "w_q_up": w_q_r.astype(BF16),
        "gain_q": jnp.concatenate([g_q_nope, g_q_rope, z32]).reshape(1, LANES),
        "m_grp": jnp.asarray(m_grp, BF16), "cos_t": cos_t, "sin_t": sin_t,
        "g_kv_lat": g_kv_lat.reshape(1, KV_LORA),
        "gain_k": jnp.concatenate([g_k_nope, z64]).reshape(1, LANES),
        "gain_kpe": jnp.concatenate([z64, g_k_rope, z32]).reshape(1, LANES),
        "w_k": w_k_r.astype(BF16), "w_v": wv.astype(BF16),
        "w_abs": absorb.reshape(N_HEADS * HEAD_PAD, 2 * LANES).astype(BF16), "w_kt": w_kt.astype(BF16),
        "g_conv_ln": g_conv_ln.reshape(1, CONV_CH), "b_conv_ln": b_conv_ln.reshape(1, CONV_CH),
        "g_out_attn": g_out_attn.reshape(1, ATTN_WIDTH), "g_out_conv": g_out_conv.reshape(1, CONV_CH),
        "w_out": w_out.astype(BF16), "g_norm_ffn": g_norm_ffn.reshape(1, d),
        "b_router": br.reshape(1, LANES),
    }


def _mod_table(mod_p, mod_s, t_new):
    return jnp.concatenate([jnp.broadcast_to(mod_p, (TOK_TILE, D_MODEL)), jnp.repeat(mod_s, t_new, axis=0)], axis=0)


def kernel(x_prompt, x_sample, cache_ckv, cache_kpe, state_conv, page_table, c_prompt, c_sample, w_ada, b_ada, g_norm_mix, g_norm_ffn, w_in, g_q_lat, w_q_up, g_q_nope, g_q_rope, g_kv_lat, g_k_rope, w_kv_up, g_k_nope, w_dw, b_dw, g_conv_ln, b_conv_ln, g_out_attn, g_out_conv, w_out, w_router, b_router, w_exp_up, b_exp_up, w_exp_down, b_exp_down):
    bsz, seq, d = x_prompt.shape
    n_seq, t_new = x_sample.shape[:2]
    depth = w_ada.shape[0]
    assert bsz == 1 and depth == 1 and t_new == 4 and d == D_MODEL
    n_p = bsz * seq
    n_s = n_seq * t_new
    n = n_p + n_s
    past = page_table.shape[1] * cache_ckv.shape[2]
    l = 0

    pos = np.concatenate([np.arange(seq), np.tile(past + np.arange(t_new), n_seq)])
    p = _prepare(pos, w_in[l], g_norm_mix[l], g_q_lat[l], w_q_up[l], g_q_nope[l], g_q_rope[l], g_kv_lat[l],
                 g_k_rope[l], w_kv_up[l], g_k_nope[l], g_conv_ln[l], b_conv_ln[l], g_out_attn[l], g_out_conv[l],
                 w_out[l], g_norm_ffn[l], w_router[l], b_router[l])

    n_c = 1 + n_seq
    c_all = jnp.concatenate([c_prompt, c_sample, jnp.zeros((-n_c % 8, d), F32)], axis=0)
    mod = _adaln(c_all, w_ada[l], b_ada[l])
    tabs = [_mod_table(mod[0:1, j * d:(j + 1) * d], mod[1:n_c, j * d:(j + 1) * d], t_new) for j in range(6)]
    sh_m, sc_m, gt_m, sh_f, sc_f, gt_f = tabs

    x_p = x_prompt.reshape(n_p, d)
    x_s = x_sample.reshape(n_s, d)
    q_all, ckv_p, ckv_s, kpe_tp, kpe_ts, u_all, k_all, v_all = _mixer_inputs(x_p, x_s, sh_m, sc_m, p)

    attn_p = _prompt_attention(q_all, k_all, v_all, seq)
    ckv_s = ckv_s.reshape(n_seq, t_new, KV_LORA)
    kpe_s = jnp.transpose(kpe_ts).reshape(n_seq, t_new, QK_ROPE)
    ckv_new_pad = jnp.pad(ckv_s, ((0, 0), (0, LANES - t_new), (0, 0)))
    kpe_new_t = jnp.swapaxes(jnp.pad(kpe_s, ((0, 0), (0, LANES - t_new), (0, 0))), 1, 2)
    q_s = q_all[n_p:].reshape(n_seq, t_new, N_HEADS * HEAD_PAD)
    attn_s = _sample_attention(page_table, q_s, ckv_new_pad, kpe_new_t, cache_ckv,
                               jnp.swapaxes(cache_kpe, 2, 3), p)

    conv_p = _prompt_conv(u_all, w_dw[l], b_dw[l].reshape(1, CONV_CH))
    u_s = u_all[n_p:].reshape(n_seq, t_new, CONV_CH)
    u_ext_s = jnp.concatenate([state_conv[l], u_s], axis=1)
    ext = CONV_W - 1 + t_new
    ext_pad = -ext % 8
    w_taps = jnp.stack([jnp.pad(w_dw[l], ((t, t_new - 1 - t + ext_pad), (0, 0))) for t in range(t_new)])
    conv_s = _sample_conv(jnp.pad(u_ext_s, ((0, 0), (0, ext_pad), (0, 0))), w_taps, b_dw[l].reshape(1, CONV_CH))
    conv_s = jnp.transpose(conv_s, (1, 0, 2)).reshape(n_s, CONV_CH)

    y_all, f_all, top_i, top_w = _merge_router(x_p, x_s, attn_p, attn_s.reshape(n_s, ATTN_WIDTH), conv_p, conv_s,
                                               gt_m, sh_f, sc_f, p)

    n_tiles = -(-(n * TOP_K + N_EXPERTS * (FFN_TILE - 1)) // FFN_TILE)
    n_slots = n_tiles * FFN_TILE
    pos_tok, meta = _route(top_i, n_tiles)
    slot = jnp.transpose(pos_tok[:, :TOP_K])
    piece_idx = (slot[:, None, :] + (jnp.arange(SC_PIECES, dtype=I32) * n_slots)[None, :, None]).reshape(-1)
    x_sorted = _sc_scatter_rows(f_all.reshape(SC_PIECES * n, SC_ROW), piece_idx, SC_PIECES * n_slots)
    h_sorted = _expert_ffn(meta[:n_tiles, 0], meta[0:1, 1], x_sorted.reshape(SC_PIECES, n_slots, SC_ROW),
                           w_exp_up[l], b_exp_up[l], w_exp_down[l], b_exp_down[l])
    gathered = _sc_gather_rows(h_sorted.reshape(SC_PIECES * n_slots, SC_ROW), piece_idx)
    y_p, y_s = _combine(y_all, gathered.reshape(TOP_K, SC_PIECES, n, SC_ROW), top_w, gt_f, n_p)
    y_p = y_p.reshape(bsz, seq, d)
    y_s = y_s.reshape(n_seq, t_new, d)
    ckv_prompt = ckv_p.reshape(1, bsz, seq, KV_LORA)
    kpe_prompt = jnp.swapaxes(kpe_tp.reshape(1, bsz, QK_ROPE, seq), 2, 3)
    conv_prompt = u_all[n_p - (CONV_W - 1):n_p].reshape(1, bsz, CONV_W - 1, CONV_CH)
    ckv_sample = ckv_s[None]
    kpe_sample = kpe_s[None]
    conv_sample = u_ext_s[:, t_new:][None]
    return (y_p, y_s, ckv_prompt, kpe_prompt, conv_prompt, ckv_sample, kpe_sample, conv_sample)
```

```python
import functools

import numpy as np
import jax
import jax.numpy as jnp
from jax import lax
from jax.experimental import pallas as pl
from jax.experimental.pallas import tpu as pltpu
from jax.experimental.pallas import tpu_sc as plsc

F32 = jnp.float32
BF16 = jnp.bfloat16
I32 = jnp.int32
HIGHEST = lax.Precision.HIGHEST

D_MODEL = 1024
N_HEADS = 8
QK_NOPE = 64
QK_ROPE = 32
V_DIM = 64
Q_LORA = 256
KV_LORA = 128
ATTN_WIDTH = N_HEADS * V_DIM
CONV_CH = D_MODEL - ATTN_WIDTH
CONV_W = 31
N_EXPERTS = 32
TOP_K = 4
D_FF = D_MODEL
SWIGLU_LIMIT = 7.0
SWIGLU_ALPHA = 1.702
EPS = 1e-6
NEG_INF = -1e30
ROPE_THETA = 10000.0
SM_SCALE = (QK_NOPE + QK_ROPE) ** -0.5
LOG2E = 1.4426950408889634
Q_SCALE = SM_SCALE * LOG2E

LANES = 128
SUBLANES = 8
HEAD_PAD = LANES
ROPE_LO = QK_NOPE
ROPE_HALF = QK_ROPE // 2

TOK_TILE = 512
FA_TQ = 1024
FA_TK = FA_TQ
FA_HEADS = 8
PAGES_PER_STEP = 64
FFN_TILE = 512
ROUTE_TILE = 512
SC_ROW = 256
PACKED_WIDTH = D_MODEL // 2
SC_PIECES = PACKED_WIDTH // SC_ROW
SC_WINDOW = 128
VMEM_LIMIT = 56 * 1024 * 1024


def _cparams(sem, vmem=None):
    return pltpu.CompilerParams(dimension_semantics=sem, vmem_limit_bytes=vmem)


def _rsqrt_mean(x, n):
    return lax.rsqrt(jnp.sum(x * x, axis=-1, keepdims=True) * (1.0 / n) + EPS)


def _pack_rows(x):
    half = x.shape[1] // 2
    hi = lax.bitcast_convert_type(x[:, :half].astype(BF16).astype(F32), I32)
    lo = lax.bitcast_convert_type(x[:, half:].astype(BF16).astype(F32), I32)
    return hi | lax.shift_right_logical(lo, jnp.full(lo.shape, 16, I32))


def _unpack_rows(w):
    hi = lax.bitcast_convert_type(w & jnp.int32(-65536), F32)
    lo = lax.bitcast_convert_type(lax.shift_left(w, jnp.full(w.shape, 16, I32)), F32)
    return jnp.concatenate([hi, lo], axis=1)


def _ada_kernel(c_ref, w_ref, b_ref, o_ref):
    c = c_ref[...]
    s = c * jax.nn.sigmoid(c)
    o_ref[...] = jnp.dot(s, w_ref[...], precision=HIGHEST, preferred_element_type=F32) + b_ref[...]


def _adaln(c_all, w_ada, b_ada):
    rows = c_all.shape[0]
    n_out = w_ada.shape[1]
    return pl.pallas_call(
        _ada_kernel,
        grid=(n_out // D_MODEL,),
        in_specs=[pl.BlockSpec((rows, D_MODEL), lambda j: (0, 0)),
                  pl.BlockSpec((D_MODEL, D_MODEL), lambda j: (0, j)),
                  pl.BlockSpec((1, D_MODEL), lambda j: (0, j))],
        out_specs=pl.BlockSpec((rows, D_MODEL), lambda j: (0, j)),
        out_shape=jax.ShapeDtypeStruct((rows, n_out), F32),
        compiler_params=_cparams(("arbitrary",)),
        name="adaln",
    )(c_all, w_ada, b_ada.reshape(1, n_out))


def _group_norm_rope(x, m_grp, gain, cos_t, sin_t, first_half):
    ms = jnp.dot((x * x).astype(BF16), m_grp, preferred_element_type=F32)
    xn = x * lax.rsqrt(ms + EPS) * gain
    swapped = jnp.where(first_half, pltpu.roll(xn, LANES - ROPE_HALF, 1), pltpu.roll(xn, ROPE_HALF, 1))
    return xn * cos_t + swapped * sin_t


def _pick_rows(n_prompt_tiles, prompt_ref, decode_ref):
    return jnp.where(pl.program_id(0) < n_prompt_tiles, prompt_ref[...], decode_ref[...])


def _mix_kernel(n_prompt_tiles, xp_ref, xs_ref, sh_ref, sc_ref, gmix_ref, win_ref, gql_ref, wq_ref, gq_ref, m_ref,
                cos_ref, sin_ref, gkv_ref, gk_ref, gkpe_ref, wk_ref, wv_ref, vone_ref,
                q_out, ckvp_out, ckvs_out, kpetp_out, kpets_out, u_out, k_out, v_out):
    x = _pick_rows(n_prompt_tiles, xp_ref, xs_ref)
    h = x * _rsqrt_mean(x, D_MODEL) * gmix_ref[...]
    h = h * (1.0 + sc_ref[...]) + sh_ref[...]
    proj = jnp.dot(h.astype(BF16), win_ref[...], preferred_element_type=F32)
    q_lat = proj[:, :Q_LORA]
    ckv_raw = proj[:, Q_LORA:Q_LORA + KV_LORA]
    kpe_blk = proj[:, Q_LORA + KV_LORA:Q_LORA + KV_LORA + LANES]
    glu_lo = Q_LORA + KV_LORA + LANES
    u_out[...] = proj[:, glu_lo:glu_lo + CONV_CH] * jax.nn.sigmoid(proj[:, glu_lo + CONV_CH:glu_lo + 2 * CONV_CH])

    m_grp = m_ref[...]
    cos_t = cos_ref[...]
    sin_t = sin_ref[...]
    lane = lax.broadcasted_iota(I32, (1, LANES), 1)
    first_half = lane < ROPE_LO + ROPE_HALF

    q_lat_n = q_lat * _rsqrt_mean(q_lat, Q_LORA) * gql_ref[...]
    q = jnp.dot(q_lat_n.astype(BF16), wq_ref[...], preferred_element_type=F32)
    gq = gq_ref[...]
    for hd in range(N_HEADS):
        qh = _group_norm_rope(q[:, hd * HEAD_PAD:(hd + 1) * HEAD_PAD], m_grp, gq, cos_t, sin_t, first_half)
        q_out[:, hd * HEAD_PAD:(hd + 1) * HEAD_PAD] = (qh * Q_SCALE).astype(BF16)

    ckv_n = ckv_raw * _rsqrt_mean(ckv_raw, KV_LORA) * gkv_ref[...]
    kpe_r = _group_norm_rope(kpe_blk, m_grp, gkpe_ref[...], cos_t, sin_t, first_half)
    kpe_t = jnp.transpose(kpe_r)[ROPE_LO:ROPE_LO + QK_ROPE, :]
    is_prompt = pl.program_id(0) < n_prompt_tiles

    @pl.when(is_prompt)
    def _():
        ckvp_out[...] = ckv_n
        kpetp_out[...] = kpe_t

    @pl.when(jnp.logical_not(is_prompt))
    def _():
        ckvs_out[...] = ckv_n
        kpets_out[...] = kpe_t

    ckv_b = ckv_n.astype(BF16)
    kexp = jnp.dot(ckv_b, wk_ref[...], preferred_element_type=F32)
    gk = gk_ref[...]
    for hd in range(N_HEADS):
        kh = kexp[:, hd * HEAD_PAD:(hd + 1) * HEAD_PAD]
        ms = jnp.dot((kh * kh).astype(BF16), m_grp, preferred_element_type=F32)
        k_out[:, hd * HEAD_PAD:(hd + 1) * HEAD_PAD] = (kh * lax.rsqrt(ms + EPS) * gk + kpe_r).astype(BF16)
    v_out[...] = (jnp.dot(ckv_b, wv_ref[...], preferred_element_type=F32) + vone_ref[...]).astype(BF16)


def _mod_index(n_prompt_tiles):
    return lambda i: (jnp.where(i < n_prompt_tiles, 0, i - n_prompt_tiles + 1), 0)


def _prompt_index(n_prompt_tiles):
    return lambda i: (jnp.minimum(i, n_prompt_tiles - 1), 0)


def _decode_index(n_prompt_tiles):
    return lambda i: (jnp.maximum(i - n_prompt_tiles, 0), 0)


def _mixer_inputs(x_p, x_s, sh, sc, p):
    n_prompt_tiles = x_p.shape[0] // TOK_TILE
    n = x_p.shape[0] + x_s.shape[0]
    tm = TOK_TILE
    const = lambda i: (0, 0)
    row = lambda i: (i, 0)
    mod = _mod_index(n_prompt_tiles)
    hw = N_HEADS * HEAD_PAD
    in_cols = p["w_in"].shape[1]
    return pl.pallas_call(
        functools.partial(_mix_kernel, n_prompt_tiles),
        grid=(n // tm,),
        in_specs=[pl.BlockSpec((tm, D_MODEL), _prompt_index(n_prompt_tiles)),
                  pl.BlockSpec((tm, D_MODEL), _decode_index(n_prompt_tiles)),
                  pl.BlockSpec((tm, D_MODEL), mod),
                  pl.BlockSpec((tm, D_MODEL), mod),
                  pl.BlockSpec((1, D_MODEL), const),
                  pl.BlockSpec((D_MODEL, in_cols), const),
                  pl.BlockSpec((1, Q_LORA), const),
                  pl.BlockSpec((Q_LORA, hw), const),
                  pl.BlockSpec((1, LANES), const),
                  pl.BlockSpec((LANES, LANES), const),
                  pl.BlockSpec((tm, LANES), row),
                  pl.BlockSpec((tm, LANES), row),
                  pl.BlockSpec((1, KV_LORA), const),
                  pl.BlockSpec((1, LANES), const),
                  pl.BlockSpec((1, LANES), const),
                  pl.BlockSpec((KV_LORA, hw), const),
                  pl.BlockSpec((KV_LORA, hw), const),
                  pl.BlockSpec((1, hw), const)],
        out_specs=[pl.BlockSpec((tm, hw), row),
                   pl.BlockSpec((tm, KV_LORA), _prompt_index(n_prompt_tiles)),
                   pl.BlockSpec((tm, KV_LORA), _decode_index(n_prompt_tiles)),
                   pl.BlockSpec((QK_ROPE, tm), lambda i: (0, jnp.minimum(i, n_prompt_tiles - 1))),
                   pl.BlockSpec((QK_ROPE, tm), lambda i: (0, jnp.maximum(i - n_prompt_tiles, 0))),
                   pl.BlockSpec((tm, CONV_CH), row),
                   pl.BlockSpec((tm, hw), row),
                   pl.BlockSpec((tm, hw), row)],
        out_shape=[jax.ShapeDtypeStruct((n, hw), BF16),
                   jax.ShapeDtypeStruct((x_p.shape[0], KV_LORA), F32),
                   jax.ShapeDtypeStruct((x_s.shape[0], KV_LORA), F32),
                   jax.ShapeDtypeStruct((QK_ROPE, x_p.shape[0]), F32),
                   jax.ShapeDtypeStruct((QK_ROPE, x_s.shape[0]), F32),
                   jax.ShapeDtypeStruct((n, CONV_CH), F32),
                   jax.ShapeDtypeStruct((n, hw), BF16),
                   jax.ShapeDtypeStruct((n, hw), BF16)],
        compiler_params=_cparams(("arbitrary",), VMEM_LIMIT),
        name="mixer_inputs",
    )(x_p, x_s, sh, sc, p["g_norm_mix"], p["w_in"], p["g_q_lat"], p["w_q_up"], p["gain_q"], p["m_grp"],
      p["cos_t"], p["sin_t"], p["g_kv_lat"], p["gain_k"], p["gain_kpe"], p["w_k"], p["w_v_tiles"], p["v_ones"])


def _fa_kernel(qt_ref, kt_ref, q_ref, k_ref, v_ref, o_ref, m_sc, acc_sc):
    t = pl.program_id(1)
    qi = qt_ref[t]
    ki = kt_ref[t]
    last_k = (qi + 1) * (FA_TQ // FA_TK) - 1

    @pl.when(ki == 0)
    def _():
        m_sc[...] = jnp.full(m_sc.shape, NEG_INF, F32)
        acc_sc[...] = jnp.zeros(acc_sc.shape, F32)

    def update(hh, r0, nr, k0, nk, masked):
        q = q_ref[pl.ds(r0, nr), hh * HEAD_PAD:(hh + 1) * HEAD_PAD]
        k = k_ref[pl.ds(k0, nk), hh * HEAD_PAD:(hh + 1) * HEAD_PAD]
        v = v_ref[pl.ds(k0, nk), hh * HEAD_PAD:(hh + 1) * HEAD_PAD]
        s = lax.dot_general(q, k, (((1,), (1,)), ((), ())), preferred_element_type=F32)
        if masked:
            col_minus_row = lax.broadcasted_iota(I32, (nr, nk), 1) - lax.broadcasted_iota(I32, (nr, nk), 0)
            s = jnp.where(col_minus_row <= (qi * FA_TQ + r0) - (ki * FA_TK + k0), s, NEG_INF)
        m_prev = m_sc[hh, pl.ds(r0, nr), :]
        m_new = jnp.maximum(m_prev, jnp.max(s, axis=-1, keepdims=True))
        alpha = jnp.exp2(m_prev - m_new)
        pr = jnp.exp2((s - jnp.concatenate([m_new] * (nk // LANES), axis=1)).astype(BF16))
        acc_sc[hh, pl.ds(r0, nr), :] = (alpha * acc_sc[hh, pl.ds(r0, nr), :]
                                        + jnp.dot(pr, v, preferred_element_type=F32))
        m_sc[hh, pl.ds(r0, nr), :] = m_new

    @pl.when(ki < qi)
    def _():
        for hh in range(FA_HEADS):
            update(hh, 0, FA_TQ, 0, FA_TK, False)

    @pl.when(ki == qi)
    def _():
        half = FA_TQ // 2
        for hh in range(FA_HEADS):
            update(hh, 0, half, 0, half, True)
            update(hh, half, half, 0, FA_TK, True)

    @pl.when(ki == last_k)
    def _():
        for hh in range(FA_HEADS):
            acc = acc_sc[hh]
            o_ref[:, hh * V_DIM:(hh + 1) * V_DIM] = acc[:, :V_DIM] / acc[:, V_DIM:V_DIM + 1]


def _prompt_attention(q_all, k_all, v_all, seq):
    nq = seq // FA_TQ
    ratio = FA_TQ // FA_TK
    qt, kt = [], []
    for qi in range(nq):
        for ki in range((qi + 1) * ratio):
            qt.append(qi)
            kt.append(ki)
    qt = jnp.asarray(np.array(qt, np.int32))
    kt = jnp.asarray(np.array(kt, np.int32))
    n_pairs = int(qt.shape[0])
    grid_spec = pltpu.PrefetchScalarGridSpec(
        num_scalar_prefetch=2,
        grid=(N_HEADS // FA_HEADS, n_pairs),
        in_specs=[pl.BlockSpec((FA_TQ, FA_HEADS * HEAD_PAD), lambda hp, t, qt, kt: (qt[t], hp)),
                  pl.BlockSpec((FA_TK, FA_HEADS * HEAD_PAD), lambda hp, t, qt, kt: (kt[t], hp)),
                  pl.BlockSpec((FA_TK, FA_HEADS * HEAD_PAD), lambda hp, t, qt, kt: (kt[t], hp))],
        out_specs=pl.BlockSpec((FA_TQ, FA_HEADS * V_DIM), lambda hp, t, qt, kt: (qt[t], hp)),
        scratch_shapes=[pltpu.VMEM((FA_HEADS, FA_TQ, LANES), F32),
                        pltpu.VMEM((FA_HEADS, FA_TQ, HEAD_PAD), F32)],
    )
    return pl.pallas_call(
        _fa_kernel,
        grid_spec=grid_spec,
        out_shape=jax.ShapeDtypeStruct((seq, ATTN_WIDTH), F32),
        compiler_params=_cparams(("parallel", "arbitrary"), VMEM_LIMIT),
        name="prompt_attention",
    )(qt, kt, q_all, k_all, v_all)


def _sattn_kernel(pt_ref, q_ref, wabs_ref, wkt_ref, wv_ref, ckvn_ref, kpen_ref, ckv_hbm, kpe_hbm,
                  o_ref, m_sc, l_sc, acc_sc, qa_sc, wall_sc, ckv_buf, kpe_buf, sem):
    kb = pl.program_id(1)
    n_kb = pl.num_programs(1)
    step = pl.program_id(0) * n_kb + kb
    n_steps = pl.num_programs(0) * n_kb
    slot = step % 2
    rows = 4 * N_HEADS

    def start_fetch(step_i, slot_i):
        for j in range(PAGES_PER_STEP):
            page = pt_ref[step_i * PAGES_PER_STEP + j]
            pltpu.make_async_copy(ckv_hbm.at[0, page], ckv_buf.at[slot_i, j], sem.at[0, slot_i]).start()
            pltpu.make_async_copy(kpe_hbm.at[0, page], kpe_buf.at[slot_i, j], sem.at[1, slot_i]).start()

    @pl.when(step == 0)
    def _():
        start_fetch(step, slot)

    @pl.when(step + 1 < n_steps)
    def _():
        start_fetch(step + 1, 1 - slot)

    @pl.when(kb == 0)
    def _():
        m_sc[...] = jnp.full(m_sc.shape, NEG_INF, F32)
        l_sc[...] = jnp.zeros(l_sc.shape, F32)
        acc_sc[...] = jnp.zeros(acc_sc.shape, F32)
        q4 = q_ref[...].astype(F32)
        head_of_lane = lax.broadcasted_iota(I32, (N_HEADS, N_HEADS * HEAD_PAD), 1) // HEAD_PAD
        head_of_row = lax.broadcasted_iota(I32, (N_HEADS, N_HEADS * HEAD_PAD), 0)
        own = head_of_lane == head_of_row
        qbd = jnp.concatenate(
            [jnp.where(own, jnp.broadcast_to(q4[qq:qq + 1, :], own.shape), 0.0) for qq in range(4)], axis=0)
        qa = jnp.dot(qbd.astype(BF16), wabs_ref[...], preferred_element_type=F32).astype(BF16)
        qa_sc[...] = qa
        wall_sc[pl.ds(0, N_HEADS * QK_NOPE), :] = wkt_ref[...]
        wall_sc[pl.ds(N_HEADS * QK_NOPE, rows), :] = qa[:, :KV_LORA]

    def attend(state, ckv_b, kpe_t, mask):
        m_prev, l_prev, acc_prev = state
        nt = (((1,), (1,)), ((), ()))
        kn_all = lax.dot_general(wall_sc[...], ckv_b, nt, preferred_element_type=F32)
        keys = kn_all.shape[1]
        kn_t = kn_all[:N_HEADS * QK_NOPE]
        ss = jnp.sum((kn_t * kn_t).reshape(N_HEADS, QK_NOPE, keys), axis=1)
        r8 = lax.rsqrt(ss * (1.0 / QK_NOPE) + EPS)
        rope = jnp.dot(qa_sc[:, KV_LORA:KV_LORA + QK_ROPE], kpe_t, preferred_element_type=F32)
        s = kn_all[N_HEADS * QK_NOPE:] * jnp.concatenate([r8] * 4, axis=0) + rope
        if mask is not None:
            s = jnp.where(mask, s, NEG_INF)
        m_new = jnp.maximum(m_prev, jnp.max(s, axis=-1, keepdims=True))
        alpha = jnp.exp2(m_prev - m_new)
        pr = jnp.exp2(s - m_new)
        l_new = alpha * l_prev + jnp.sum(pr, axis=-1, keepdims=True)
        acc_new = alpha * acc_prev + jnp.dot(pr.astype(BF16), ckv_b, preferred_element_type=F32)
        return m_new, l_new, acc_new

    pltpu.make_async_copy(ckv_hbm.at[0, pl.ds(0, PAGES_PER_STEP)], ckv_buf.at[slot], sem.at[0, slot]).wait()
    pltpu.make_async_copy(kpe_hbm.at[0, pl.ds(0, PAGES_PER_STEP)], kpe_buf.at[slot], sem.at[1, slot]).wait()

    page = ckv_buf.shape[2]
    ckv_b = ckv_buf[slot].reshape(PAGES_PER_STEP * page, KV_LORA).astype(BF16)
    kpe_t = jnp.concatenate([kpe_buf[slot, j].astype(BF16) for j in range(PAGES_PER_STEP)], axis=1)

    @pl.when(kb < n_kb - 1)
    def _():
        m_sc[...], l_sc[...], acc_sc[...] = attend((m_sc[...], l_sc[...], acc_sc[...]), ckv_b, kpe_t, None)

    @pl.when(kb == n_kb - 1)
    def _():
        n_cached = PAGES_PER_STEP * page
        key = lax.broadcasted_iota(I32, (rows, n_cached + LANES), 1) - n_cached
        qry = lax.broadcasted_iota(I32, (rows, n_cached + LANES), 0) // N_HEADS
        ckv_all = jnp.concatenate([ckv_b, ckvn_ref[...].astype(BF16)], axis=0)
        kpe_all = jnp.concatenate([kpe_t, kpen_ref[...].astype(BF16)], axis=1)
        _, l_fin, acc_fin = attend((m_sc[...], l_sc[...], acc_sc[...]), ckv_all, kpe_all, key <= qry)
        lat = acc_fin / l_fin
        o_all = jnp.dot(lat.astype(BF16), wv_ref[...], preferred_element_type=F32)
        head_of_col = lax.broadcasted_iota(I32, (rows, ATTN_WIDTH), 1) // V_DIM
        head_of_row = lax.broadcasted_iota(I32, (rows, ATTN_WIDTH), 0) % N_HEADS
        o_own = jnp.where(head_of_col == head_of_row, o_all, 0.0)
        o_ref[...] = jnp.sum(o_own.reshape(4, N_HEADS, ATTN_WIDTH), axis=1)


def _sample_attention(page_table, q_s, ckv_new_pad, kpe_new_t, cache_ckv, cache_kpe_t, p):
    n_seq, n_pages = page_table.shape
    page = cache_ckv.shape[2]
    n_kb = n_pages // PAGES_PER_STEP
    hw = N_HEADS * HEAD_PAD
    assert n_pages % PAGES_PER_STEP == 0
    per_seq3 = lambda b, kb, pt: (b, 0, 0)
    const = lambda b, kb, pt: (0, 0)
    in_specs = [pl.BlockSpec((None, 4, hw), per_seq3),
                pl.BlockSpec((hw, 2 * LANES), const),
                pl.BlockSpec((N_HEADS * QK_NOPE, KV_LORA), const),
                pl.BlockSpec((KV_LORA, ATTN_WIDTH), const),
                pl.BlockSpec((None, LANES, KV_LORA), per_seq3),
                pl.BlockSpec((None, QK_ROPE, LANES), per_seq3),
                pl.BlockSpec(memory_space=pl.ANY),
                pl.BlockSpec(memory_space=pl.ANY)]
    grid_spec = pltpu.PrefetchScalarGridSpec(
        num_scalar_prefetch=1,
        grid=(n_seq, n_kb),
        in_specs=in_specs,
        out_specs=pl.BlockSpec((None, 4, ATTN_WIDTH), per_seq3),
        scratch_shapes=[pltpu.VMEM((4 * N_HEADS, 1), F32),
                        pltpu.VMEM((4 * N_HEADS, 1), F32),
                        pltpu.VMEM((4 * N_HEADS, KV_LORA), F32),
                        pltpu.VMEM((4 * N_HEADS, 2 * LANES), BF16),
                        pltpu.VMEM((N_HEADS * QK_NOPE + 4 * N_HEADS, KV_LORA), BF16),
                        pltpu.VMEM((2, PAGES_PER_STEP, page, KV_LORA), F32),
                        pltpu.VMEM((2, PAGES_PER_STEP, QK_ROPE, page), F32),
                        pltpu.SemaphoreType.DMA((2, 2))],
    )
    return pl.pallas_call(
        _sattn_kernel,
        grid_spec=grid_spec,
        out_shape=jax.ShapeDtypeStruct((n_seq, 4, ATTN_WIDTH), F32),
        compiler_params=_cparams(("arbitrary", "arbitrary"), VMEM_LIMIT),
        name="decode_attention",
    )(page_table.reshape(-1), q_s, p["w_abs"], p["w_kt"], p["w_v"], ckv_new_pad, kpe_new_t, cache_ckv, cache_kpe_t)


CONV_HALO = 32
CONV_ROWS = 64


def _conv_kernel(halo_ref, u_ref, w_ref, b_ref, o_ref, ext_sc, sh_sc):
    i = pl.program_id(0)
    tm = u_ref.shape[0]
    ext_sc[pl.ds(0, CONV_HALO), :] = jnp.where(i == 0, 0.0, halo_ref[...])
    ext_sc[pl.ds(CONV_HALO, tm), :] = u_ref[...]
    first = CONV_HALO - (CONV_W - 1)
    span = tm + CONV_HALO - SUBLANES
    for sft in range(1, SUBLANES):
        sh_sc[sft - 1] = ext_sc[pl.ds(sft, span), :]
    for rc in range(tm // CONV_ROWS):
        acc = jnp.broadcast_to(b_ref[...], (CONV_ROWS, CONV_CH))
        for j in range(CONV_W):
            a8, sft = divmod(first + j, SUBLANES)
            src = ext_sc if sft == 0 else sh_sc.at[sft - 1]
            acc = acc + src[pl.ds(rc * CONV_ROWS + SUBLANES * a8, CONV_ROWS), :] * w_ref[j:j + 1, :]
        o_ref[pl.ds(rc * CONV_ROWS, CONV_ROWS), :] = acc


def _prompt_conv(u_all, w_dw, b_dw):
    n = u_all.shape[0]
    tm = TOK_TILE
    per = tm // CONV_HALO
    return pl.pallas_call(
        _conv_kernel,
        grid=(n // tm,),
        in_specs=[pl.BlockSpec((CONV_HALO, CONV_CH), lambda i: (jnp.maximum(i * per - 1, 0), 0)),
                  pl.BlockSpec((tm, CONV_CH), lambda i: (i, 0)),
                  pl.BlockSpec((CONV_W, CONV_CH), lambda i: (0, 0)),
                  pl.BlockSpec((1, CONV_CH), lambda i: (0, 0))],
        out_specs=pl.BlockSpec((tm, CONV_CH), lambda i: (i, 0)),
        out_shape=jax.ShapeDtypeStruct((n, CONV_CH), F32),
        scratch_shapes=[pltpu.VMEM((CONV_HALO + tm, CONV_CH), F32),
                        pltpu.VMEM((SUBLANES - 1, CONV_HALO + tm - SUBLANES, CONV_CH), F32)],
        compiler_params=_cparams(("parallel",)),
        name="prompt_conv",
    )(u_all, u_all, w_dw, b_dw)


def _sconv_kernel(u_ref, wt_ref, b_ref, o_ref):
    u = u_ref[...]
    for t in range(o_ref.shape[0]):
        o_ref[t] = jnp.sum(u * wt_ref[t][None, :, :], axis=1) + b_ref[...]


def _sample_conv(u_ext, w_taps, b_dw):
    n_seq, ext, _ = u_ext.shape
    t_new = w_taps.shape[0]
    sb = 8
    return pl.pallas_call(
        _sconv_kernel,
        grid=(n_seq // sb,),
        in_specs=[pl.BlockSpec((sb, ext, CONV_CH), lambda i: (i, 0, 0)),
                  pl.BlockSpec((t_new, ext, CONV_CH), lambda i: (0, 0, 0)),
                  pl.BlockSpec((1, CONV_CH), lambda i: (0, 0))],
        out_specs=pl.BlockSpec((t_new, sb, CONV_CH), lambda i: (0, i, 0)),
        out_shape=jax.ShapeDtypeStruct((t_new, n_seq, CONV_CH), F32),
        compiler_params=_cparams(("parallel",)),
        name="decode_conv",
    )(u_ext, w_taps, b_dw)


def _lane_pack(cols, dtype):
    lane = lax.broadcasted_iota(I32, (cols[0].shape[0], LANES), 1)
    out = jnp.zeros((cols[0].shape[0], LANES), dtype)
    for j, c in enumerate(cols):
        out = jnp.where(lane == j, c.astype(dtype), out)
    return out


def _merge_kernel(n_prompt_tiles, xp_ref, xs_ref, ap_ref, as_ref, cp_ref, cs_ref, gate_ref, shf_ref, scf_ref,
                  gln_ref, bln_ref, ga_ref, gc_ref, wout_ref, gffn_ref, wrh_ref, wrl_ref, br_ref,
                  y_out, f_out, ti_out, tw_out):
    yc = _pick_rows(n_prompt_tiles, cp_ref, cs_ref)
    mu = jnp.mean(yc, axis=-1, keepdims=True)
    xc = yc - mu
    var = jnp.mean(xc * xc, axis=-1, keepdims=True)
    ln = xc * lax.rsqrt(var + EPS) * gln_ref[...] + bln_ref[...]
    conv = ln * jax.nn.sigmoid(ln)
    attn = _pick_rows(n_prompt_tiles, ap_ref, as_ref)
    a_n = attn * _rsqrt_mean(attn, ATTN_WIDTH) * ga_ref[...]
    c_n = conv * _rsqrt_mean(conv, CONV_CH) * gc_ref[...]
    m = (jnp.dot(a_n.astype(BF16), wout_ref[:ATTN_WIDTH, :], preferred_element_type=F32)
         + jnp.dot(c_n.astype(BF16), wout_ref[ATTN_WIDTH:, :], preferred_element_type=F32))
    y = _pick_rows(n_prompt_tiles, xp_ref, xs_ref) + gate_ref[...] * m
    y_out[...] = y
    f = y * _rsqrt_mean(y, D_MODEL) * gffn_ref[...]
    f = f * (1.0 + scf_ref[...]) + shf_ref[...]
    f_words = _pack_rows(f)
    for j in range(SC_PIECES):
        f_out[j] = f_words[:, j * SC_ROW:(j + 1) * SC_ROW]
    f_hi = f.astype(BF16)
    f_lo = (f - f_hi.astype(F32)).astype(BF16)
    logits = (jnp.dot(f_hi, wrh_ref[...], preferred_element_type=F32)
              + jnp.dot(f_hi, wrl_ref[...], preferred_element_type=F32)
              + jnp.dot(f_lo, wrh_ref[...], preferred_element_type=F32)) + br_ref[...]
    lane = lax.broadcasted_iota(I32, logits.shape, 1)
    vals, idxs = [], []
    for _ in range(TOP_K):
        mx = jnp.max(logits, axis=-1, keepdims=True)
        ix = jnp.min(jnp.where(logits == mx, lane, LANES), axis=-1, keepdims=True)
        vals.append(mx)
        idxs.append(ix)
        logits = jnp.where(lane == ix, NEG_INF * 4.0, logits)
    exps = [jnp.exp(v - vals[0]) for v in vals]
    tot = exps[0] + exps[1] + exps[2] + exps[3]
    ti_out[...] = _lane_pack(idxs, I32)
    tw_out[...] = _lane_pack([e / tot for e in exps], F32)


def _merge_router(x_p, x_s, attn_p, attn_s, conv_p, conv_s, gate, shf, scf, p):
    n_prompt_tiles = x_p.shape[0] // TOK_TILE
    n = x_p.shape[0] + x_s.shape[0]
    tm = TOK_TILE
    const = lambda i: (0, 0)
    row = lambda i: (i, 0)
    mod = _mod_index(n_prompt_tiles)
    from_p = _prompt_index(n_prompt_tiles)
    from_s = _decode_index(n_prompt_tiles)
    return pl.pallas_call(
        functools.partial(_merge_kernel, n_prompt_tiles),
        grid=(n // tm,),
        in_specs=[pl.BlockSpec((tm, D_MODEL), from_p),
                  pl.BlockSpec((tm, D_MODEL), from_s),
                  pl.BlockSpec((tm, ATTN_WIDTH), from_p),
                  pl.BlockSpec((tm, ATTN_WIDTH), from_s),
                  pl.BlockSpec((tm, CONV_CH), from_p),
                  pl.BlockSpec((tm, CONV_CH), from_s),
                  pl.BlockSpec((tm, D_MODEL), mod),
                  pl.BlockSpec((tm, D_MODEL), mod),
                  pl.BlockSpec((tm, D_MODEL), mod),
                  pl.BlockSpec((1, CONV_CH), const),
                  pl.BlockSpec((1, CONV_CH), const),
                  pl.BlockSpec((1, ATTN_WIDTH), const),
                  pl.BlockSpec((1, CONV_CH), const),
                  pl.BlockSpec((D_MODEL, D_MODEL), const),
                  pl.BlockSpec((1, D_MODEL), const),
                  pl.BlockSpec((D_MODEL, LANES), const),
                  pl.BlockSpec((D_MODEL, LANES), const),
                  pl.BlockSpec((1, LANES), const)],
        out_specs=[pl.BlockSpec((tm, D_MODEL), row),
                   pl.BlockSpec((SC_PIECES, tm, SC_ROW), lambda i: (0, i, 0)),
                   pl.BlockSpec((tm, LANES), row),
                   pl.BlockSpec((tm, LANES), row)],
        out_shape=[jax.ShapeDtypeStruct((n, D_MODEL), F32),
                   jax.ShapeDtypeStruct((SC_PIECES, n, SC_ROW), I32),
                   jax.ShapeDtypeStruct((n, LANES), I32),
                   jax.ShapeDtypeStruct((n, LANES), F32)],
        compiler_params=_cparams(("parallel",), VMEM_LIMIT),
        name="merge_router",
    )(x_p, x_s, attn_p, attn_s, conv_p, conv_s, gate, shf, scf, p["g_conv_ln"], p["b_conv_ln"], p["g_out_attn"],
      p["g_out_conv"], p["w_out"], p["g_norm_ffn"], p["w_router_hi"], p["w_router_lo"], p["b_router"])


def _select_lane(table, idx_col, lane):
    return jnp.sum(jnp.where(lane == idx_col, table, 0.0), axis=-1, keepdims=True)


def _rank_kernel(ti_ref, rk_out, cnt_out, carry_sc):
    i = pl.program_id(0)

    @pl.when(i == 0)
    def _():
        carry_sc[...] = jnp.zeros(carry_sc.shape, F32)

    ti = ti_ref[...]
    tn = ti.shape[0]
    lane = lax.broadcasted_iota(I32, (tn, LANES), 1)
    sel = jnp.zeros((tn, LANES), F32)
    for k in range(TOP_K):
        sel = sel + (lane == ti[:, k:k + 1]).astype(F32)
    r_i = lax.broadcasted_iota(I32, (tn, tn), 0)
    c_i = lax.broadcasted_iota(I32, (tn, tn), 1)
    below = (c_i < r_i).astype(BF16)
    rank = carry_sc[...] + jnp.dot(below, sel.astype(BF16), preferred_element_type=F32)
    rk_out[...] = _lane_pack([_select_lane(rank, ti[:, k:k + 1], lane) for k in range(TOP_K)], F32)
    carry_sc[...] = carry_sc[...] + jnp.sum(sel, axis=0, keepdims=True)
    cnt_out[...] = jnp.broadcast_to(carry_sc[...], cnt_out.shape)


def _pos_kernel(cnt_ref, ti_ref, rk_ref, pos_out, meta_out):
    cnt = cnt_ref[...]
    padded = jnp.ceil(cnt * (1.0 / FFN_TILE)) * FFN_TILE
    r_i = lax.broadcasted_iota(I32, (LANES, LANES), 0)
    c_i = lax.broadcasted_iota(I32, (LANES, LANES), 1)
    before = (r_i < c_i).astype(F32)
    offs = jnp.dot(padded, before, precision=HIGHEST, preferred_element_type=F32)
    ends = offs + padded
    ti = ti_ref[...]
    tn = ti.shape[0]
    lane = lax.broadcasted_iota(I32, (tn, LANES), 1)
    off_row = offs[0:1, :]
    rk = rk_ref[...]
    pos = [_select_lane(jnp.broadcast_to(off_row, (tn, LANES)), ti[:, k:k + 1], lane) + rk[:, k:k + 1]
           for k in range(TOP_K)]
    pos_out[...] = _lane_pack(pos, F32).astype(I32)

    @pl.when(pl.program_id(0) == 0)
    def _():
        nt = meta_out.shape[0]
        start = (lax.broadcasted_iota(I32, (nt, LANES), 0) * FFN_TILE).astype(F32)
        elane = lax.broadcasted_iota(I32, (nt, LANES), 1)
        done = jnp.where((elane < N_EXPERTS) & (jnp.broadcast_to(ends[0:1, :], (nt, LANES)) <= start), 1.0, 0.0)
        expert = jnp.minimum(jnp.sum(done, axis=-1, keepdims=True), N_EXPERTS - 1.0)
        total = jnp.sum(jnp.where(elane < N_EXPERTS, jnp.broadcast_to(padded[0:1, :], (nt, LANES)), 0.0),
                        axis=-1, keepdims=True)
        filled = jnp.broadcast_to((offs + cnt)[0:1, :], (nt, LANES))
        end_e = jnp.sum(jnp.where(elane == expert.astype(I32), filled, 0.0), axis=-1, keepdims=True)
        rows = jnp.clip(end_e - start[:, 0:1], 0.0, float(FFN_TILE))
        meta_out[...] = _lane_pack([expert, total * (1.0 / FFN_TILE), rows], F32).astype(I32)


def _route(top_i, n_tiles_max):
    n = top_i.shape[0]
    tn = ROUTE_TILE
    rk, cnt = pl.pallas_call(
        _rank_kernel,
        grid=(n // tn,),
        in_specs=[pl.BlockSpec((tn, LANES), lambda i: (i, 0))],
        out_specs=[pl.BlockSpec((tn, LANES), lambda i: (i, 0)),
                   pl.BlockSpec((8, LANES), lambda i: (0, 0))],
        out_shape=[jax.ShapeDtypeStruct((n, LANES), F32),
                   jax.ShapeDtypeStruct((8, LANES), F32)],
        scratch_shapes=[pltpu.VMEM((1, LANES), F32)],
        compiler_params=_cparams(("arbitrary",)),
        name="route_rank",
    )(top_i)
    nt_pad = -(-n_tiles_max // 8) * 8
    pos, meta = pl.pallas_call(
        _pos_kernel,
        grid=(n // tn,),
        in_specs=[pl.BlockSpec((8, LANES), lambda i: (0, 0)),
                  pl.BlockSpec((tn, LANES), lambda i: (i, 0)),
                  pl.BlockSpec((tn, LANES), lambda i: (i, 0))],
        out_specs=[pl.BlockSpec((tn, LANES), lambda i: (i, 0)),
                   pl.BlockSpec((nt_pad, LANES), lambda i: (0, 0))],
        out_shape=[jax.ShapeDtypeStruct((n, LANES), I32),
                   jax.ShapeDtypeStruct((nt_pad, LANES), I32)],
        compiler_params=_cparams(("arbitrary",)),
        name="route_pos",
    )(cnt, top_i, rk)
    return pos, meta


def _sc_scatter_rows(x, idx, n_out):
    n_src = x.shape[0]
    n = idx.shape[0]
    n_src_blk = n_src // SC_WINDOW
    mesh = plsc.VectorSubcoreMesh(core_axis_name="c", subcore_axis_name="s")

    @pl.kernel(out_type=jax.ShapeDtypeStruct((n_out, SC_ROW), x.dtype), mesh=mesh)
    def k(x_hbm, i_hbm, o_hbm):
        def body(x_vmem, i_vmem):
            pltpu.sync_copy(x_vmem, o_hbm.at[i_vmem.at[0]])

        pltpu.emit_pipeline(
            body,
            grid=(n // SC_WINDOW,),
            in_specs=[pl.BlockSpec((SC_WINDOW, SC_ROW), index_map=lambda i: (i % n_src_blk, 0)),
                      pl.BlockSpec((1, SC_WINDOW), index_map=lambda i: (0, i))],
            out_specs=[],
            core_axis_name=("c", "s"),
            dimension_semantics=(pltpu.PARALLEL,),
        )(x_hbm, i_hbm)

    return k(x, idx.reshape(1, n))


def _sc_gather_rows(x, idx):
    n = idx.shape[0]
    mesh = plsc.VectorSubcoreMesh(core_axis_name="c", subcore_axis_name="s")

    @pl.kernel(out_type=jax.ShapeDtypeStruct((n, SC_ROW), x.dtype), mesh=mesh)
    def k(x_hbm, i_hbm, o_hbm):
        def body(i_vmem, o_vmem):
            pltpu.sync_copy(x_hbm.at[i_vmem.at[0]], o_vmem)

        pltpu.emit_pipeline(
            body,
            grid=(n // SC_WINDOW,),
            in_specs=[pl.BlockSpec((1, SC_WINDOW), index_map=lambda i: (0, i))],
            out_specs=[pl.BlockSpec((SC_WINDOW, SC_ROW), index_map=lambda i: (i, 0))],
            core_axis_name=("c", "s"),
            dimension_semantics=(pltpu.PARALLEL,),
        )(i_hbm, o_hbm)

    return k(x, idx.reshape(1, n))


def _ffn_kernel(te_ref, nv_ref, nr_ref, x_ref, wu_ref, bu_ref, wd_ref, bd_ref, o_ref, wu_sc, wd_sc):
    t = pl.program_id(0)
    valid = t < nv_ref[0]
    prev = te_ref[jnp.maximum(t - 1, 0)]
    fresh = jnp.logical_or(t == 0, te_ref[t] != prev)

    @pl.when(jnp.logical_and(valid, fresh))
    def _():
        wu_sc[...] = wu_ref[...].astype(BF16)
        wd_sc[...] = wd_ref[...].astype(BF16)

    def experts(n_rows):
        rows = pl.ds(0, n_rows)
        x = _unpack_rows(jnp.concatenate([x_ref[j, rows, :] for j in range(SC_PIECES)], axis=1))
        z = jnp.dot(x.astype(BF16), wu_sc[...], preferred_element_type=F32) + bu_ref[...]
        zg = jnp.minimum(z[:, :D_FF], SWIGLU_LIMIT)
        zl = jnp.clip(z[:, D_FF:], -SWIGLU_LIMIT, SWIGLU_LIMIT)
        act = zg * jax.nn.sigmoid(SWIGLU_ALPHA * zg) * (zl + 1.0)
        out = _pack_rows(jnp.dot(act.astype(BF16), wd_sc[...], preferred_element_type=F32) + bd_ref[...])
        for j in range(SC_PIECES):
            o_ref[j, rows, :] = out[:, j * SC_ROW:(j + 1) * SC_ROW]

    half = FFN_TILE // 2

    @pl.when(jnp.logical_and(valid, nr_ref[t] > half))
    def _():
        experts(FFN_TILE)

    @pl.when(jnp.logical_and(valid, nr_ref[t] <= half))
    def _():
        experts(half)


def _expert_ffn(tile_expert, n_valid, tile_rows, x_sorted, w_up, b_up, w_down, b_down):
    n_slots = x_sorted.shape[1]
    n_tiles = n_slots // FFN_TILE
    xmap = lambda t, te, nv, nr: (0, jnp.minimum(t, nv[0] - 1), 0)
    emap = lambda t, te, nv, nr: (te[t], 0, 0)
    grid_spec = pltpu.PrefetchScalarGridSpec(
        num_scalar_prefetch=3,
        grid=(n_tiles,),
        in_specs=[pl.BlockSpec((SC_PIECES, FFN_TILE, SC_ROW), xmap),
                  pl.BlockSpec((None, D_MODEL, 2 * D_FF), emap),
                  pl.BlockSpec((None, 1, 2 * D_FF), emap),
                  pl.BlockSpec((None, D_FF, D_MODEL), emap),
                  pl.BlockSpec((None, 1, D_MODEL), emap)],
        out_specs=pl.BlockSpec((SC_PIECES, FFN_TILE, SC_ROW), xmap),
        scratch_shapes=[pltpu.VMEM((D_MODEL, 2 * D_FF), BF16),
                        pltpu.VMEM((D_FF, D_MODEL), BF16)],
    )
    return pl.pallas_call(
        _ffn_kernel,
        grid_spec=grid_spec,
        out_shape=jax.ShapeDtypeStruct((SC_PIECES, n_slots, SC_ROW), I32),
        compiler_params=_cparams(("arbitrary",), VMEM_LIMIT),
        name="expert_ffn",
    )(tile_expert, n_valid, tile_rows, x_sorted, w_up, b_up.reshape(N_EXPERTS, 1, 2 * D_FF), w_down,
      b_down.reshape(N_EXPERTS, 1, D_MODEL))


def _combine_kernel(n_prompt_tiles, y_ref, g_ref, tw_ref, gate_ref, op_ref, os_ref):
    tw = tw_ref[...]
    rows = lambda k: _unpack_rows(jnp.concatenate([g_ref[k, j] for j in range(SC_PIECES)], axis=1))
    moe = rows(0) * tw[:, 0:1]
    for k in range(1, TOP_K):
        moe = moe + rows(k) * tw[:, k:k + 1]
    out = y_ref[...] + gate_ref[...] * moe
    i = pl.program_id(0)

    @pl.when(i < n_prompt_tiles)
    def _():
        op_ref[...] = out

    @pl.when(i >= n_prompt_tiles)
    def _():
        os_ref[...] = out


def _combine(y_all, gathered, top_w, gate, n_p):
    n = y_all.shape[0]
    tm = TOK_TILE
    n_prompt_tiles = n_p // tm
    row = lambda i: (i, 0)
    return pl.pallas_call(
        functools.partial(_combine_kernel, n_prompt_tiles),
        grid=(n // tm,),
        in_specs=[pl.BlockSpec((tm, D_MODEL), row),
                  pl.BlockSpec((TOP_K, SC_PIECES, tm, SC_ROW), lambda i: (0, 0, i, 0)),
                  pl.BlockSpec((tm, LANES), row),
                  pl.BlockSpec((tm, D_MODEL), _mod_index(n_prompt_tiles))],
        out_specs=[pl.BlockSpec((tm, D_MODEL), _prompt_index(n_prompt_tiles)),
                   pl.BlockSpec((tm, D_MODEL), _decode_index(n_prompt_tiles))],
        out_shape=[jax.ShapeDtypeStruct((n_p, D_MODEL), F32),
                   jax.ShapeDtypeStruct((n - n_p, D_MODEL), F32)],
        compiler_params=_cparams(("arbitrary",), VMEM_LIMIT),
        name="moe_combine",
    )(y_all, gathered, top_w, gate)


def _head_tiles(nope, rope):
    pad = jnp.zeros(nope.shape[:-1] + (HEAD_PAD - QK_NOPE - QK_ROPE,), nope.dtype)
    t = jnp.concatenate([nope, rope, pad], axis=-1)
    return t.reshape(t.shape[:-2] + (N_HEADS * HEAD_PAD,))


def _prepare(pos, w_in, g_norm_mix, g_q_lat, w_q_up, g_q_nope, g_q_rope, g_kv_lat, g_k_rope, w_kv_up, g_k_nope,
             g_conv_ln, b_conv_ln, g_out_attn, g_out_conv, w_out, g_norm_ffn, w_router, b_router):
    z32 = jnp.zeros((HEAD_PAD - QK_NOPE - QK_ROPE,), F32)
    z64 = jnp.zeros((QK_NOPE,), F32)
    d = D_MODEL
    kpe_cols = w_in[:, Q_LORA + KV_LORA:Q_LORA + KV_LORA + QK_ROPE]
    kpe_tile = jnp.concatenate([jnp.zeros((d, QK_NOPE), F32), kpe_cols, jnp.zeros((d, z32.shape[0]), F32)], axis=1)
    w_in_r = jnp.concatenate([w_in[:, :Q_LORA + KV_LORA], kpe_tile, w_in[:, Q_LORA + KV_LORA + QK_ROPE:]], axis=1)
    wq = w_q_up.reshape(Q_LORA, N_HEADS, QK_NOPE + QK_ROPE)
    w_q_r = _head_tiles(wq[..., :QK_NOPE], wq[..., QK_NOPE:])
    wk = w_kv_up[..., :QK_NOPE]
    wv = w_kv_up[..., QK_NOPE:].reshape(KV_LORA, ATTN_WIDTH)
    w_k_r = _head_tiles(wk, jnp.zeros((KV_LORA, N_HEADS, QK_ROPE), F32))
    lane = np.arange(LANES)
    grp = np.where(lane < QK_NOPE, 0, np.where(lane < QK_NOPE + QK_ROPE, 1, 2))
    m_grp = ((grp[:, None] == grp[None, :]) & (grp[:, None] < 2)).astype(np.float32)
    m_grp = m_grp / np.where(grp < 1, QK_NOPE, QK_ROPE)[None, :]
    inv = np.float32(ROPE_THETA) ** (-np.arange(0, QK_ROPE, 2, dtype=np.float32) / np.float32(QK_ROPE))
    ang = pos.astype(np.float32)[:, None] * inv[None, :].astype(np.float32)
    cs, sn = jnp.asarray(np.cos(ang), F32), jnp.asarray(np.sin(ang), F32)
    n = pos.shape[0]
    cos_t = jnp.concatenate([jnp.ones((n, QK_NOPE), F32), cs, cs, jnp.zeros((n, z32.shape[0]), F32)], axis=1)
    sin_t = jnp.concatenate([jnp.zeros((n, QK_NOPE), F32), -sn, sn, jnp.zeros((n, z32.shape[0]), F32)], axis=1)
    wk_g = wk * g_k_nope[None, None, :]
    absorb = jnp.zeros((N_HEADS, HEAD_PAD, 2 * LANES), F32)
    absorb = absorb.at[:, :QK_NOPE, :KV_LORA].set(jnp.transpose(wk_g, (1, 2, 0)))
    absorb = absorb.at[:, ROPE_LO:ROPE_LO + QK_ROPE, KV_LORA:KV_LORA + QK_ROPE].set(
        jnp.broadcast_to(jnp.eye(QK_ROPE, dtype=F32), (N_HEADS, QK_ROPE, QK_ROPE)))
    w_kt = jnp.transpose(wk, (1, 2, 0)).reshape(N_HEADS * QK_NOPE, KV_LORA)
    wr = jnp.concatenate([w_router, jnp.zeros((d, LANES - N_EXPERTS), F32)], axis=1)
    wr_hi = wr.astype(BF16)
    wr_lo = (wr - wr_hi.astype(F32)).astype(BF16)
    br = jnp.concatenate([b_router, jnp.full((LANES - N_EXPERTS,), NEG_INF, F32)])
    wv_h = w_kv_up[..., QK_NOPE:]
    w_v_tiles = jnp.concatenate([wv_h, jnp.zeros((KV_LORA, N_HEADS, HEAD_PAD - V_DIM), F32)], axis=-1)
    v_ones = (jnp.arange(N_HEADS * HEAD_PAD) % HEAD_PAD == V_DIM).astype(F32).reshape(1, N_HEADS * HEAD_PAD)
    return {
        "w_v_tiles": w_v_tiles.reshape(KV_LORA, N_HEADS * HEAD_PAD).astype(BF16), "v_ones": v_ones,
        "w_router_hi": wr_hi, "w_router_lo": wr_lo,
        "w_in": w_in_r.astype(BF16), "g_norm_mix": g_norm_mix.reshape(1, d), "g_q_lat": g_q_lat.reshape(1, Q_LORA),
        "w_q_up": w_q_r.astype(BF16),
        "gain_q": jnp.concatenate([g_q_nope, g_q_rope, z32]).reshape(1, LANES),
        "m_grp": jnp.asarray(m_grp, BF16), "cos_t": cos_t, "sin_t": sin_t,
        "g_kv_lat": g_kv_lat.reshape(1, KV_LORA),
        "gain_k": jnp.concatenate([g_k_nope, z64]).reshape(1, LANES),
        "gain_kpe": jnp.concatenate([z64, g_k_rope, z32]).reshape(1, LANES),
        "w_k": w_k_r.astype(BF16), "w_v": wv.astype(BF16),
        "w_abs": absorb.reshape(N_HEADS * HEAD_PAD, 2 * LANES).astype(BF16), "w_kt": w_kt.astype(BF16),
        "g_conv_ln": g_conv_ln.reshape(1, CONV_CH), "b_conv_ln": b_conv_ln.reshape(1, CONV_CH),
        "g_out_attn": g_out_attn.reshape(1, ATTN_WIDTH), "g_out_conv": g_out_conv.reshape(1, CONV_CH),
        "w_out": w_out.astype(BF16), "g_norm_ffn": g_norm_ffn.reshape(1, d),
        "b_router": br.reshape(1, LANES),
    }


def _mod_table(mod_p, mod_s, t_new):
    return jnp.concatenate([jnp.broadcast_to(mod_p, (TOK_TILE, D_MODEL)), jnp.repeat(mod_s, t_new, axis=0)], axis=0)


def kernel(x_prompt, x_sample, cache_ckv, cache_kpe, state_conv, page_table, c_prompt, c_sample, w_ada, b_ada, g_norm_mix, g_norm_ffn, w_in, g_q_lat, w_q_up, g_q_nope, g_q_rope, g_kv_lat, g_k_rope, w_kv_up, g_k_nope, w_dw, b_dw, g_conv_ln, b_conv_ln, g_out_attn, g_out_conv, w_out, w_router, b_router, w_exp_up, b_exp_up, w_exp_down, b_exp_down):
    bsz, seq, d = x_prompt.shape
    n_seq, t_new = x_sample.shape[:2]
    depth = w_ada.shape[0]
    assert bsz == 1 and depth == 1 and t_new == 4 and d == D_MODEL
    n_p = bsz * seq
    n_s = n_seq * t_new
    n = n_p + n_s
    past = page_table.shape[1] * cache_ckv.shape[2]
    l = 0

    pos = np.concatenate([np.arange(seq), np.tile(past + np.arange(t_new), n_seq)])
    p = _prepare(pos, w_in[l], g_norm_mix[l], g_q_lat[l], w_q_up[l], g_q_nope[l], g_q_rope[l], g_kv_lat[l],
                 g_k_rope[l], w_kv_up[l], g_k_nope[l], g_conv_ln[l], b_conv_ln[l], g_out_attn[l], g_out_conv[l],
                 w_out[l], g_norm_ffn[l], w_router[l], b_router[l])

    n_c = 1 + n_seq
    c_all = jnp.concatenate([c_prompt, c_sample, jnp.zeros((-n_c % 8, d), F32)], axis=0)
    mod = _adaln(c_all, w_ada[l], b_ada[l])
    tabs = [_mod_table(mod[0:1, j * d:(j + 1) * d], mod[1:n_c, j * d:(j + 1) * d], t_new) for j in range(6)]
    sh_m, sc_m, gt_m, sh_f, sc_f, gt_f = tabs

    x_p = x_prompt.reshape(n_p, d)
    x_s = x_sample.reshape(n_s, d)
    q_all, ckv_p, ckv_s, kpe_tp, kpe_ts, u_all, k_all, v_all = _mixer_inputs(x_p, x_s, sh_m, sc_m, p)

    attn_p = _prompt_attention(q_all, k_all, v_all, seq)
    ckv_s = ckv_s.reshape(n_seq, t_new, KV_LORA)
    kpe_s = jnp.transpose(kpe_ts).reshape(n_seq, t_new, QK_ROPE)
    ckv_new_pad = jnp.pad(ckv_s, ((0, 0), (0, LANES - t_new), (0, 0)))
    kpe_new_t = jnp.swapaxes(jnp.pad(kpe_s, ((0, 0), (0, LANES - t_new), (0, 0))), 1, 2)
    q_s = q_all[n_p:].reshape(n_seq, t_new, N_HEADS * HEAD_PAD)
    attn_s = _sample_attention(page_table, q_s, ckv_new_pad, kpe_new_t, cache_ckv,
                               jnp.swapaxes(cache_kpe, 2, 3), p)

    conv_p = _prompt_conv(u_all, w_dw[l], b_dw[l].reshape(1, CONV_CH))
    u_s = u_all[n_p:].reshape(n_seq, t_new, CONV_CH)
    u_ext_s = jnp.concatenate([state_conv[l], u_s], axis=1)
    ext = CONV_W - 1 + t_new
    ext_pad = -ext % 8
    w_taps = jnp.stack([jnp.pad(w_dw[l], ((t, t_new - 1 - t + ext_pad), (0, 0))) for t in range(t_new)])
    conv_s = _sample_conv(jnp.pad(u_ext_s, ((0, 0), (0, ext_pad), (0, 0))), w_taps, b_dw[l].reshape(1, CONV_CH))
    conv_s = jnp.transpose(conv_s, (1, 0, 2)).reshape(n_s, CONV_CH)

    y_all, f_all, top_i, top_w = _merge_router(x_p, x_s, attn_p, attn_s.reshape(n_s, ATTN_WIDTH), conv_p, conv_s,
                                               gt_m, sh_f, sc_f, p)

    n_tiles = -(-(n * TOP_K + N_EXPERTS * (FFN_TILE - 1)) // FFN_TILE)
    n_slots = n_tiles * FFN_TILE
    pos_tok, meta = _route(top_i, n_tiles)
    slot = jnp.transpose(pos_tok[:, :TOP_K])
    piece_idx = (slot[:, None, :] + (jnp.arange(SC_PIECES, dtype=I32) * n_slots)[None, :, None]).reshape(-1)
    x_sorted = _sc_scatter_rows(f_all.reshape(SC_PIECES * n, SC_ROW), piece_idx, SC_PIECES * n_slots)
    h_sorted = _expert_ffn(meta[:n_tiles, 0], meta[0:1, 1], meta[:n_tiles, 2],
                           x_sorted.reshape(SC_PIECES, n_slots, SC_ROW),
                           w_exp_up[l], b_exp_up[l], w_exp_down[l], b_exp_down[l])
    gathered = _sc_gather_rows(h_sorted.reshape(SC_PIECES * n_slots, SC_ROW), piece_idx)
    y_p, y_s = _combine(y_all, gathered.reshape(TOP_K, SC_PIECES, n, SC_ROW), top_w, gt_f, n_p)
    y_p = y_p.reshape(bsz, seq, d)
    y_s = y_s.reshape(n_seq, t_new, d)
    ckv_prompt = ckv_p.reshape(1, bsz, seq, KV_LORA)
    kpe_prompt = jnp.swapaxes(kpe_tp.reshape(1, bsz, QK_ROPE, seq), 2, 3)
    conv_prompt = u_all[n_p - (CONV_W - 1):n_p].reshape(1, bsz, CONV_W - 1, CONV_CH)
    ckv_sample = ckv_s[None]
    kpe_sample = kpe_s[None]
    conv_sample = u_ext_s[:, t_new:][None]
    return (y_p, y_s, ckv_prompt, kpe_prompt, conv_prompt, ckv_sample, kpe_sample, conv_sample)
```

```python
import functools

import numpy as np
import jax
import jax.numpy as jnp
from jax import lax
from jax.experimental import pallas as pl
from jax.experimental.pallas import tpu as pltpu
from jax.experimental.pallas import tpu_sc as plsc

F32 = jnp.float32
BF16 = jnp.bfloat16
I32 = jnp.int32
HIGHEST = lax.Precision.HIGHEST

D_MODEL = 1024
N_HEADS = 8
QK_NOPE = 64
QK_ROPE = 32
V_DIM = 64
Q_LORA = 256
KV_LORA = 128
ATTN_WIDTH = N_HEADS * V_DIM
CONV_CH = D_MODEL - ATTN_WIDTH
CONV_W = 31
N_EXPERTS = 32
TOP_K = 4
D_FF = D_MODEL
SWIGLU_LIMIT = 7.0
SWIGLU_ALPHA = 1.702
EPS = 1e-6
NEG_INF = -1e30
ROPE_THETA = 10000.0
SM_SCALE = (QK_NOPE + QK_ROPE) ** -0.5
LOG2E = 1.4426950408889634
Q_SCALE = SM_SCALE * LOG2E

LANES = 128
SUBLANES = 8
HEAD_PAD = LANES
ROPE_LO = QK_NOPE
ROPE_HALF = QK_ROPE // 2

TOK_TILE = 512
FA_TQ = 1024
FA_TK = FA_TQ
FA_HEADS = 8
PAGES_PER_STEP = 64
FFN_TILE = 512
ROUTE_TILE = 512
SC_ROW = 256
PACKED_WIDTH = D_MODEL // 2
SC_PIECES = PACKED_WIDTH // SC_ROW
SC_WINDOW = 128
VMEM_LIMIT = 56 * 1024 * 1024


def _cparams(sem, vmem=None):
    return pltpu.CompilerParams(dimension_semantics=sem, vmem_limit_bytes=vmem)


def _rsqrt_mean(x, n):
    return lax.rsqrt(jnp.sum(x * x, axis=-1, keepdims=True) * (1.0 / n) + EPS)


def _pack_rows(x):
    half = x.shape[1] // 2
    hi = lax.bitcast_convert_type(x[:, :half].astype(BF16).astype(F32), I32)
    lo = lax.bitcast_convert_type(x[:, half:].astype(BF16).astype(F32), I32)
    return hi | lax.shift_right_logical(lo, jnp.full(lo.shape, 16, I32))


def _unpack_rows(w):
    hi = lax.bitcast_convert_type(w & jnp.int32(-65536), F32)
    lo = lax.bitcast_convert_type(lax.shift_left(w, jnp.full(w.shape, 16, I32)), F32)
    return jnp.concatenate([hi, lo], axis=1)


def _ada_kernel(c_ref, w_ref, b_ref, o_ref):
    c = c_ref[...]
    s = c * jax.nn.sigmoid(c)
    o_ref[...] = jnp.dot(s, w_ref[...], precision=HIGHEST, preferred_element_type=F32) + b_ref[...]


def _adaln(c_all, w_ada, b_ada):
    rows = c_all.shape[0]
    n_out = w_ada.shape[1]
    return pl.pallas_call(
        _ada_kernel,
        grid=(n_out // D_MODEL,),
        in_specs=[pl.BlockSpec((rows, D_MODEL), lambda j: (0, 0)),
                  pl.BlockSpec((D_MODEL, D_MODEL), lambda j: (0, j)),
                  pl.BlockSpec((1, D_MODEL), lambda j: (0, j))],
        out_specs=pl.BlockSpec((rows, D_MODEL), lambda j: (0, j)),
        out_shape=jax.ShapeDtypeStruct((rows, n_out), F32),
        compiler_params=_cparams(("arbitrary",)),
        name="adaln",
    )(c_all, w_ada, b_ada.reshape(1, n_out))


def _group_norm_rope(x, m_grp, gain, cos_t, sin_t, first_half):
    ms = jnp.dot((x * x).astype(BF16), m_grp, preferred_element_type=F32)
    xn = x * lax.rsqrt(ms + EPS) * gain
    swapped = jnp.where(first_half, pltpu.roll(xn, LANES - ROPE_HALF, 1), pltpu.roll(xn, ROPE_HALF, 1))
    return xn * cos_t + swapped * sin_t


def _pick_rows(n_prompt_tiles, prompt_ref, decode_ref):
    return jnp.where(pl.program_id(0) < n_prompt_tiles, prompt_ref[...], decode_ref[...])


def _mix_kernel(n_prompt_tiles, xp_ref, xs_ref, sh_ref, sc_ref, gmix_ref, win_ref, gql_ref, wq_ref, gq_ref, m_ref,
                cos_ref, sin_ref, gkv_ref, gk_ref, gkpe_ref, wk_ref, wv_ref, vone_ref,
                q_out, ckvp_out, ckvs_out, kpetp_out, kpets_out, u_out, k_out, v_out):
    x = _pick_rows(n_prompt_tiles, xp_ref, xs_ref)
    h = x * _rsqrt_mean(x, D_MODEL) * gmix_ref[...]
    h = h * (1.0 + sc_ref[...]) + sh_ref[...]
    proj = jnp.dot(h.astype(BF16), win_ref[...], preferred_element_type=F32)
    q_lat = proj[:, :Q_LORA]
    ckv_raw = proj[:, Q_LORA:Q_LORA + KV_LORA]
    kpe_blk = proj[:, Q_LORA + KV_LORA:Q_LORA + KV_LORA + LANES]
    glu_lo = Q_LORA + KV_LORA + LANES
    u_out[...] = proj[:, glu_lo:glu_lo + CONV_CH] * jax.nn.sigmoid(proj[:, glu_lo + CONV_CH:glu_lo + 2 * CONV_CH])

    m_grp = m_ref[...]
    cos_t = cos_ref[...]
    sin_t = sin_ref[...]
    lane = lax.broadcasted_iota(I32, (1, LANES), 1)
    first_half = lane < ROPE_LO + ROPE_HALF

    q_lat_n = q_lat * _rsqrt_mean(q_lat, Q_LORA) * gql_ref[...]
    q = jnp.dot(q_lat_n.astype(BF16), wq_ref[...], preferred_element_type=F32)
    gq = gq_ref[...]
    for hd in range(N_HEADS):
        qh = _group_norm_rope(q[:, hd * HEAD_PAD:(hd + 1) * HEAD_PAD], m_grp, gq, cos_t, sin_t, first_half)
        q_out[:, hd * HEAD_PAD:(hd + 1) * HEAD_PAD] = (qh * Q_SCALE).astype(BF16)

    ckv_n = ckv_raw * _rsqrt_mean(ckv_raw, KV_LORA) * gkv_ref[...]
    kpe_r = _group_norm_rope(kpe_blk, m_grp, gkpe_ref[...], cos_t, sin_t, first_half)
    kpe_t = jnp.transpose(kpe_r)[ROPE_LO:ROPE_LO + QK_ROPE, :]
    is_prompt = pl.program_id(0) < n_prompt_tiles

    @pl.when(is_prompt)
    def _():
        ckvp_out[...] = ckv_n
        kpetp_out[...] = kpe_t

    @pl.when(jnp.logical_not(is_prompt))
    def _():
        ckvs_out[...] = ckv_n
        kpets_out[...] = kpe_t

    ckv_b = ckv_n.astype(BF16)
    kexp = jnp.dot(ckv_b, wk_ref[...], preferred_element_type=F32)
    gk = gk_ref[...]
    for hd in range(N_HEADS):
        kh = kexp[:, hd * HEAD_PAD:(hd + 1) * HEAD_PAD]
        ms = jnp.dot((kh * kh).astype(BF16), m_grp, preferred_element_type=F32)
        k_out[:, hd * HEAD_PAD:(hd + 1) * HEAD_PAD] = (kh * lax.rsqrt(ms + EPS) * gk + kpe_r).astype(BF16)
    v_out[...] = (jnp.dot(ckv_b, wv_ref[...], preferred_element_type=F32) + vone_ref[...]).astype(BF16)


def _mod_index(n_prompt_tiles):
    return lambda i: (jnp.where(i < n_prompt_tiles, 0, i - n_prompt_tiles + 1), 0)


def _prompt_index(n_prompt_tiles):
    return lambda i: (jnp.minimum(i, n_prompt_tiles - 1), 0)


def _decode_index(n_prompt_tiles):
    return lambda i: (jnp.maximum(i - n_prompt_tiles, 0), 0)


def _mixer_inputs(x_p, x_s, sh, sc, p):
    n_prompt_tiles = x_p.shape[0] // TOK_TILE
    n = x_p.shape[0] + x_s.shape[0]
    tm = TOK_TILE
    const = lambda i: (0, 0)
    row = lambda i: (i, 0)
    mod = _mod_index(n_prompt_tiles)
    hw = N_HEADS * HEAD_PAD
    in_cols = p["w_in"].shape[1]
    return pl.pallas_call(
        functools.partial(_mix_kernel, n_prompt_tiles),
        grid=(n // tm,),
        in_specs=[pl.BlockSpec((tm, D_MODEL), _prompt_index(n_prompt_tiles)),
                  pl.BlockSpec((tm, D_MODEL), _decode_index(n_prompt_tiles)),
                  pl.BlockSpec((tm, D_MODEL), mod),
                  pl.BlockSpec((tm, D_MODEL), mod),
                  pl.BlockSpec((1, D_MODEL), const),
                  pl.BlockSpec((D_MODEL, in_cols), const),
                  pl.BlockSpec((1, Q_LORA), const),
                  pl.BlockSpec((Q_LORA, hw), const),
                  pl.BlockSpec((1, LANES), const),
                  pl.BlockSpec((LANES, LANES), const),
                  pl.BlockSpec((tm, LANES), row),
                  pl.BlockSpec((tm, LANES), row),
                  pl.BlockSpec((1, KV_LORA), const),
                  pl.BlockSpec((1, LANES), const),
                  pl.BlockSpec((1, LANES), const),
                  pl.BlockSpec((KV_LORA, hw), const),
                  pl.BlockSpec((KV_LORA, hw), const),
                  pl.BlockSpec((1, hw), const)],
        out_specs=[pl.BlockSpec((tm, hw), row),
                   pl.BlockSpec((tm, KV_LORA), _prompt_index(n_prompt_tiles)),
                   pl.BlockSpec((tm, KV_LORA), _decode_index(n_prompt_tiles)),
                   pl.BlockSpec((QK_ROPE, tm), lambda i: (0, jnp.minimum(i, n_prompt_tiles - 1))),
                   pl.BlockSpec((QK_ROPE, tm), lambda i: (0, jnp.maximum(i - n_prompt_tiles, 0))),
                   pl.BlockSpec((tm, CONV_CH), row),
                   pl.BlockSpec((tm, hw), row),
                   pl.BlockSpec((tm, hw), row)],
        out_shape=[jax.ShapeDtypeStruct((n, hw), BF16),
                   jax.ShapeDtypeStruct((x_p.shape[0], KV_LORA), F32),
                   jax.ShapeDtypeStruct((x_s.shape[0], KV_LORA), F32),
                   jax.ShapeDtypeStruct((QK_ROPE, x_p.shape[0]), F32),
                   jax.ShapeDtypeStruct((QK_ROPE, x_s.shape[0]), F32),
                   jax.ShapeDtypeStruct((n, CONV_CH), F32),
                   jax.ShapeDtypeStruct((n, hw), BF16),
                   jax.ShapeDtypeStruct((n, hw), BF16)],
        compiler_params=_cparams(("arbitrary",), VMEM_LIMIT),
        name="mixer_inputs",
    )(x_p, x_s, sh, sc, p["g_norm_mix"], p["w_in"], p["g_q_lat"], p["w_q_up"], p["gain_q"], p["m_grp"],
      p["cos_t"], p["sin_t"], p["g_kv_lat"], p["gain_k"], p["gain_kpe"], p["w_k"], p["w_v_tiles"], p["v_ones"])


def _fa_kernel(qt_ref, kt_ref, q_ref, k_ref, v_ref, o_ref, m_sc, acc_sc):
    t = pl.program_id(1)
    qi = qt_ref[t]
    ki = kt_ref[t]
    last_k = (qi + 1) * (FA_TQ // FA_TK) - 1

    @pl.when(ki == 0)
    def _():
        m_sc[...] = jnp.full(m_sc.shape, NEG_INF, F32)
        acc_sc[...] = jnp.zeros(acc_sc.shape, F32)

    def update(hh, r0, nr, k0, nk, masked):
        q = q_ref[pl.ds(r0, nr), hh * HEAD_PAD:(hh + 1) * HEAD_PAD]
        k = k_ref[pl.ds(k0, nk), hh * HEAD_PAD:(hh + 1) * HEAD_PAD]
        v = v_ref[pl.ds(k0, nk), hh * HEAD_PAD:(hh + 1) * HEAD_PAD]
        s = lax.dot_general(q, k, (((1,), (1,)), ((), ())), preferred_element_type=F32)
        if masked:
            col_minus_row = lax.broadcasted_iota(I32, (nr, nk), 1) - lax.broadcasted_iota(I32, (nr, nk), 0)
            s = jnp.where(col_minus_row <= (qi * FA_TQ + r0) - (ki * FA_TK + k0), s, NEG_INF)
        m_prev = m_sc[hh, pl.ds(r0, nr), :]
        m_new = jnp.maximum(m_prev, jnp.max(s, axis=-1, keepdims=True))
        alpha = jnp.exp2(m_prev - m_new)
        pr = jnp.exp2((s - jnp.concatenate([m_new] * (nk // LANES), axis=1)).astype(BF16))
        acc_sc[hh, pl.ds(r0, nr), :] = (alpha * acc_sc[hh, pl.ds(r0, nr), :]
                                        + jnp.dot(pr, v, preferred_element_type=F32))
        m_sc[hh, pl.ds(r0, nr), :] = m_new

    @pl.when(ki < qi)
    def _():
        for hh in range(FA_HEADS):
            update(hh, 0, FA_TQ, 0, FA_TK, False)

    @pl.when(ki == qi)
    def _():
        half = FA_TQ // 2
        for hh in range(FA_HEADS):
            update(hh, 0, half, 0, half, True)
            update(hh, half, half, 0, FA_TK, True)

    @pl.when(ki == last_k)
    def _():
        for hh in range(FA_HEADS):
            acc = acc_sc[hh]
            o_ref[:, hh * V_DIM:(hh + 1) * V_DIM] = acc[:, :V_DIM] / acc[:, V_DIM:V_DIM + 1]


def _prompt_attention(q_all, k_all, v_all, seq):
    nq = seq // FA_TQ
    ratio = FA_TQ // FA_TK
    qt, kt = [], []
    for qi in range(nq):
        for ki in range((qi + 1) * ratio):
            qt.append(qi)
            kt.append(ki)
    qt = jnp.asarray(np.array(qt, np.int32))
    kt = jnp.asarray(np.array(kt, np.int32))
    n_pairs = int(qt.shape[0])
    grid_spec = pltpu.PrefetchScalarGridSpec(
        num_scalar_prefetch=2,
        grid=(N_HEADS // FA_HEADS, n_pairs),
        in_specs=[pl.BlockSpec((FA_TQ, FA_HEADS * HEAD_PAD), lambda hp, t, qt, kt: (qt[t], hp)),
                  pl.BlockSpec((FA_TK, FA_HEADS * HEAD_PAD), lambda hp, t, qt, kt: (kt[t], hp)),
                  pl.BlockSpec((FA_TK, FA_HEADS * HEAD_PAD), lambda hp, t, qt, kt: (kt[t], hp))],
        out_specs=pl.BlockSpec((FA_TQ, FA_HEADS * V_DIM), lambda hp, t, qt, kt: (qt[t], hp)),
        scratch_shapes=[pltpu.VMEM((FA_HEADS, FA_TQ, LANES), F32),
                        pltpu.VMEM((FA_HEADS, FA_TQ, HEAD_PAD), F32)],
    )
    return pl.pallas_call(
        _fa_kernel,
        grid_spec=grid_spec,
        out_shape=jax.ShapeDtypeStruct((seq, ATTN_WIDTH), F32),
        compiler_params=_cparams(("parallel", "arbitrary"), VMEM_LIMIT),
        name="prompt_attention",
    )(qt, kt, q_all, k_all, v_all)


def _sattn_kernel(pt_ref, q_ref, wabs_ref, wkt_ref, wv_ref, ckvn_ref, kpen_ref, ckv_hbm, kpe_hbm,
                  o_ref, m_sc, l_sc, acc_sc, qa_sc, wall_sc, ckv_buf, kpe_buf, sem):
    kb = pl.program_id(1)
    n_kb = pl.num_programs(1)
    step = pl.program_id(0) * n_kb + kb
    n_steps = pl.num_programs(0) * n_kb
    slot = step % 2
    rows = 4 * N_HEADS

    def start_fetch(step_i, slot_i):
        for j in range(PAGES_PER_STEP):
            page = pt_ref[step_i * PAGES_PER_STEP + j]
            pltpu.make_async_copy(ckv_hbm.at[0, page], ckv_buf.at[slot_i, j], sem.at[0, slot_i]).start(priority=j % 2)
            pltpu.make_async_copy(kpe_hbm.at[0, page], kpe_buf.at[slot_i, j],
                                  sem.at[1, slot_i]).start(priority=(j + 1) % 2)

    @pl.when(step == 0)
    def _():
        start_fetch(step, slot)

    @pl.when(step + 1 < n_steps)
    def _():
        start_fetch(step + 1, 1 - slot)

    @pl.when(kb == 0)
    def _():
        m_sc[...] = jnp.full(m_sc.shape, NEG_INF, F32)
        l_sc[...] = jnp.zeros(l_sc.shape, F32)
        acc_sc[...] = jnp.zeros(acc_sc.shape, F32)
        q4 = q_ref[...].astype(F32)
        head_of_lane = lax.broadcasted_iota(I32, (N_HEADS, N_HEADS * HEAD_PAD), 1) // HEAD_PAD
        head_of_row = lax.broadcasted_iota(I32, (N_HEADS, N_HEADS * HEAD_PAD), 0)
        own = head_of_lane == head_of_row
        qbd = jnp.concatenate(
            [jnp.where(own, jnp.broadcast_to(q4[qq:qq + 1, :], own.shape), 0.0) for qq in range(4)], axis=0)
        qa = jnp.dot(qbd.astype(BF16), wabs_ref[...], preferred_element_type=F32).astype(BF16)
        qa_sc[...] = qa
        wall_sc[pl.ds(0, N_HEADS * QK_NOPE), :] = wkt_ref[...]
        wall_sc[pl.ds(N_HEADS * QK_NOPE, rows), :] = qa[:, :KV_LORA]

    def attend(state, ckv_b, kpe_t, mask):
        m_prev, l_prev, acc_prev = state
        nt = (((1,), (1,)), ((), ()))
        kn_all = lax.dot_general(wall_sc[...], ckv_b, nt, preferred_element_type=F32)
        keys = kn_all.shape[1]
        kn_t = kn_all[:N_HEADS * QK_NOPE]
        ss = jnp.sum((kn_t * kn_t).reshape(N_HEADS, QK_NOPE, keys), axis=1)
        r8 = lax.rsqrt(ss * (1.0 / QK_NOPE) + EPS)
        rope = jnp.dot(qa_sc[:, KV_LORA:KV_LORA + QK_ROPE], kpe_t, preferred_element_type=F32)
        s = kn_all[N_HEADS * QK_NOPE:] * jnp.concatenate([r8] * 4, axis=0) + rope
        if mask is not None:
            s = jnp.where(mask, s, NEG_INF)
        m_new = jnp.maximum(m_prev, jnp.max(s, axis=-1, keepdims=True))
        alpha = jnp.exp2(m_prev - m_new)
        pr = jnp.exp2(s - m_new)
        l_new = alpha * l_prev + jnp.sum(pr, axis=-1, keepdims=True)
        acc_new = alpha * acc_prev + jnp.dot(pr.astype(BF16), ckv_b, preferred_element_type=F32)
        return m_new, l_new, acc_new

    pltpu.make_async_copy(ckv_hbm.at[0, pl.ds(0, PAGES_PER_STEP)], ckv_buf.at[slot], sem.at[0, slot]).wait()
    pltpu.make_async_copy(kpe_hbm.at[0, pl.ds(0, PAGES_PER_STEP)], kpe_buf.at[slot], sem.at[1, slot]).wait()

    page = ckv_buf.shape[2]
    ckv_b = ckv_buf[slot].reshape(PAGES_PER_STEP * page, KV_LORA).astype(BF16)
    kpe_t = jnp.concatenate([kpe_buf[slot, j].astype(BF16) for j in range(PAGES_PER_STEP)], axis=1)

    @pl.when(kb < n_kb - 1)
    def _():
        m_sc[...], l_sc[...], acc_sc[...] = attend((m_sc[...], l_sc[...], acc_sc[...]), ckv_b, kpe_t, None)

    @pl.when(kb == n_kb - 1)
    def _():
        n_cached = PAGES_PER_STEP * page
        key = lax.broadcasted_iota(I32, (rows, n_cached + LANES), 1) - n_cached
        qry = lax.broadcasted_iota(I32, (rows, n_cached + LANES), 0) // N_HEADS
        ckv_all = jnp.concatenate([ckv_b, ckvn_ref[...].astype(BF16)], axis=0)
        kpe_all = jnp.concatenate([kpe_t, kpen_ref[...].astype(BF16)], axis=1)
        _, l_fin, acc_fin = attend((m_sc[...], l_sc[...], acc_sc[...]), ckv_all, kpe_all, key <= qry)
        lat = acc_fin / l_fin
        o_all = jnp.dot(lat.astype(BF16), wv_ref[...], preferred_element_type=F32)
        head_of_col = lax.broadcasted_iota(I32, (rows, ATTN_WIDTH), 1) // V_DIM
        head_of_row = lax.broadcasted_iota(I32, (rows, ATTN_WIDTH), 0) % N_HEADS
        o_own = jnp.where(head_of_col == head_of_row, o_all, 0.0)
        o_ref[...] = jnp.sum(o_own.reshape(4, N_HEADS, ATTN_WIDTH), axis=1)


def _sample_attention(page_table, q_s, ckv_new_pad, kpe_new_t, cache_ckv, cache_kpe_t, p):
    n_seq, n_pages = page_table.shape
    page = cache_ckv.shape[2]
    n_kb = n_pages // PAGES_PER_STEP
    hw = N_HEADS * HEAD_PAD
    assert n_pages % PAGES_PER_STEP == 0
    per_seq3 = lambda b, kb, pt: (b, 0, 0)
    const = lambda b, kb, pt: (0, 0)
    in_specs = [pl.BlockSpec((None, 4, hw), per_seq3),
                pl.BlockSpec((hw, 2 * LANES), const),
                pl.BlockSpec((N_HEADS * QK_NOPE, KV_LORA), const),
                pl.BlockSpec((KV_LORA, ATTN_WIDTH), const),
                pl.BlockSpec((None, LANES, KV_LORA), per_seq3),
                pl.BlockSpec((None, QK_ROPE, LANES), per_seq3),
                pl.BlockSpec(memory_space=pl.ANY),
                pl.BlockSpec(memory_space=pl.ANY)]
    grid_spec = pltpu.PrefetchScalarGridSpec(
        num_scalar_prefetch=1,
        grid=(n_seq, n_kb),
        in_specs=in_specs,
        out_specs=pl.BlockSpec((None, 4, ATTN_WIDTH), per_seq3),
        scratch_shapes=[pltpu.VMEM((4 * N_HEADS, 1), F32),
                        pltpu.VMEM((4 * N_HEADS, 1), F32),
                        pltpu.VMEM((4 * N_HEADS, KV_LORA), F32),
                        pltpu.VMEM((4 * N_HEADS, 2 * LANES), BF16),
                        pltpu.VMEM((N_HEADS * QK_NOPE + 4 * N_HEADS, KV_LORA), BF16),
                        pltpu.VMEM((2, PAGES_PER_STEP, page, KV_LORA), F32),
                        pltpu.VMEM((2, PAGES_PER_STEP, QK_ROPE, page), F32),
                        pltpu.SemaphoreType.DMA((2, 2))],
    )
    return pl.pallas_call(
        _sattn_kernel,
        grid_spec=grid_spec,
        out_shape=jax.ShapeDtypeStruct((n_seq, 4, ATTN_WIDTH), F32),
        compiler_params=_cparams(("arbitrary", "arbitrary"), VMEM_LIMIT),
        name="decode_attention",
    )(page_table.reshape(-1), q_s, p["w_abs"], p["w_kt"], p["w_v"], ckv_new_pad, kpe_new_t, cache_ckv, cache_kpe_t)


CONV_HALO = 32
CONV_ROWS = 64


def _conv_kernel(halo_ref, u_ref, w_ref, b_ref, o_ref, ext_sc, sh_sc):
    i = pl.program_id(0)
    tm = u_ref.shape[0]
    ext_sc[pl.ds(0, CONV_HALO), :] = jnp.where(i == 0, 0.0, halo_ref[...])
    ext_sc[pl.ds(CONV_HALO, tm), :] = u_ref[...]
    first = CONV_HALO - (CONV_W - 1)
    span = tm + CONV_HALO - SUBLANES
    for sft in range(1, SUBLANES):
        sh_sc[sft - 1] = ext_sc[pl.ds(sft, span), :]
    for rc in range(tm // CONV_ROWS):
        acc = jnp.broadcast_to(b_ref[...], (CONV_ROWS, CONV_CH))
        for j in range(CONV_W):
            a8, sft = divmod(first + j, SUBLANES)
            src = ext_sc if sft == 0 else sh_sc.at[sft - 1]
            acc = acc + src[pl.ds(rc * CONV_ROWS + SUBLANES * a8, CONV_ROWS), :] * w_ref[j:j + 1, :]
        o_ref[pl.ds(rc * CONV_ROWS, CONV_ROWS), :] = acc


def _prompt_conv(u_all, w_dw, b_dw):
    n = u_all.shape[0]
    tm = TOK_TILE
    per = tm // CONV_HALO
    return pl.pallas_call(
        _conv_kernel,
        grid=(n // tm,),
        in_specs=[pl.BlockSpec((CONV_HALO, CONV_CH), lambda i: (jnp.maximum(i * per - 1, 0), 0)),
                  pl.BlockSpec((tm, CONV_CH), lambda i: (i, 0)),
                  pl.BlockSpec((CONV_W, CONV_CH), lambda i: (0, 0)),
                  pl.BlockSpec((1, CONV_CH), lambda i: (0, 0))],
        out_specs=pl.BlockSpec((tm, CONV_CH), lambda i: (i, 0)),
        out_shape=jax.ShapeDtypeStruct((n, CONV_CH), F32),
        scratch_shapes=[pltpu.VMEM((CONV_HALO + tm, CONV_CH), F32),
                        pltpu.VMEM((SUBLANES - 1, CONV_HALO + tm - SUBLANES, CONV_CH), F32)],
        compiler_params=_cparams(("parallel",)),
        name="prompt_conv",
    )(u_all, u_all, w_dw, b_dw)


def _sconv_kernel(u_ref, wt_ref, b_ref, o_ref):
    u = u_ref[...]
    for t in range(o_ref.shape[0]):
        o_ref[t] = jnp.sum(u * wt_ref[t][None, :, :], axis=1) + b_ref[...]


def _sample_conv(u_ext, w_taps, b_dw):
    n_seq, ext, _ = u_ext.shape
    t_new = w_taps.shape[0]
    sb = 8
    return pl.pallas_call(
        _sconv_kernel,
        grid=(n_seq // sb,),
        in_specs=[pl.BlockSpec((sb, ext, CONV_CH), lambda i: (i, 0, 0)),
                  pl.BlockSpec((t_new, ext, CONV_CH), lambda i: (0, 0, 0)),
                  pl.BlockSpec((1, CONV_CH), lambda i: (0, 0))],
        out_specs=pl.BlockSpec((t_new, sb, CONV_CH), lambda i: (0, i, 0)),
        out_shape=jax.ShapeDtypeStruct((t_new, n_seq, CONV_CH), F32),
        compiler_params=_cparams(("parallel",)),
        name="decode_conv",
    )(u_ext, w_taps, b_dw)


def _lane_pack(cols, dtype):
    lane = lax.broadcasted_iota(I32, (cols[0].shape[0], LANES), 1)
    out = jnp.zeros((cols[0].shape[0], LANES), dtype)
    for j, c in enumerate(cols):
        out = jnp.where(lane == j, c.astype(dtype), out)
    return out


def _merge_kernel(n_prompt_tiles, xp_ref, xs_ref, ap_ref, as_ref, cp_ref, cs_ref, gate_ref, shf_ref, scf_ref,
                  gln_ref, bln_ref, ga_ref, gc_ref, wout_ref, gffn_ref, wrh_ref, wrl_ref, br_ref,
                  y_out, f_out, ti_out, tw_out):
    yc = _pick_rows(n_prompt_tiles, cp_ref, cs_ref)
    mu = jnp.mean(yc, axis=-1, keepdims=True)
    xc = yc - mu
    var = jnp.mean(xc * xc, axis=-1, keepdims=True)
    ln = xc * lax.rsqrt(var + EPS) * gln_ref[...] + bln_ref[...]
    conv = ln * jax.nn.sigmoid(ln)
    attn = _pick_rows(n_prompt_tiles, ap_ref, as_ref)
    a_n = attn * _rsqrt_mean(attn, ATTN_WIDTH) * ga_ref[...]
    c_n = conv * _rsqrt_mean(conv, CONV_CH) * gc_ref[...]
    m = (jnp.dot(a_n.astype(BF16), wout_ref[:ATTN_WIDTH, :], preferred_element_type=F32)
         + jnp.dot(c_n.astype(BF16), wout_ref[ATTN_WIDTH:, :], preferred_element_type=F32))
    y = _pick_rows(n_prompt_tiles, xp_ref, xs_ref) + gate_ref[...] * m
    y_out[...] = y
    f = y * _rsqrt_mean(y, D_MODEL) * gffn_ref[...]
    f = f * (1.0 + scf_ref[...]) + shf_ref[...]
    f_words = _pack_rows(f)
    for j in range(SC_PIECES):
        f_out[j] = f_words[:, j * SC_ROW:(j + 1) * SC_ROW]
    f_hi = f.astype(BF16)
    f_lo = (f - f_hi.astype(F32)).astype(BF16)
    logits = (jnp.dot(f_hi, wrh_ref[...], preferred_element_type=F32)
              + jnp.dot(f_hi, wrl_ref[...], preferred_element_type=F32)
              + jnp.dot(f_lo, wrh_ref[...], preferred_element_type=F32)) + br_ref[...]
    lane = lax.broadcasted_iota(I32, logits.shape, 1)
    vals, idxs = [], []
    for _ in range(TOP_K):
        mx = jnp.max(logits, axis=-1, keepdims=True)
        ix = jnp.min(jnp.where(logits == mx, lane, LANES), axis=-1, keepdims=True)
        vals.append(mx)
        idxs.append(ix)
        logits = jnp.where(lane == ix, NEG_INF * 4.0, logits)
    exps = [jnp.exp(v - vals[0]) for v in vals]
    tot = exps[0] + exps[1] + exps[2] + exps[3]
    ti_out[...] = _lane_pack(idxs, I32)
    tw_out[...] = _lane_pack([e / tot for e in exps], F32)


def _merge_router(x_p, x_s, attn_p, attn_s, conv_p, conv_s, gate, shf, scf, p):
    n_prompt_tiles = x_p.shape[0] // TOK_TILE
    n = x_p.shape[0] + x_s.shape[0]
    tm = TOK_TILE
    const = lambda i: (0, 0)
    row = lambda i: (i, 0)
    mod = _mod_index(n_prompt_tiles)
    from_p = _prompt_index(n_prompt_tiles)
    from_s = _decode_index(n_prompt_tiles)
    return pl.pallas_call(
        functools.partial(_merge_kernel, n_prompt_tiles),
        grid=(n // tm,),
        in_specs=[pl.BlockSpec((tm, D_MODEL), from_p),
                  pl.BlockSpec((tm, D_MODEL), from_s),
                  pl.BlockSpec((tm, ATTN_WIDTH), from_p),
                  pl.BlockSpec((tm, ATTN_WIDTH), from_s),
                  pl.BlockSpec((tm, CONV_CH), from_p),
                  pl.BlockSpec((tm, CONV_CH), from_s),
                  pl.BlockSpec((tm, D_MODEL), mod),
                  pl.BlockSpec((tm, D_MODEL), mod),
                  pl.BlockSpec((tm, D_MODEL), mod),
                  pl.BlockSpec((1, CONV_CH), const),
                  pl.BlockSpec((1, CONV_CH), const),
                  pl.BlockSpec((1, ATTN_WIDTH), const),
                  pl.BlockSpec((1, CONV_CH), const),
                  pl.BlockSpec((D_MODEL, D_MODEL), const),
                  pl.BlockSpec((1, D_MODEL), const),
                  pl.BlockSpec((D_MODEL, LANES), const),
                  pl.BlockSpec((D_MODEL, LANES), const),
                  pl.BlockSpec((1, LANES), const)],
        out_specs=[pl.BlockSpec((tm, D_MODEL), row),
                   pl.BlockSpec((SC_PIECES, tm, SC_ROW), lambda i: (0, i, 0)),
                   pl.BlockSpec((tm, LANES), row),
                   pl.BlockSpec((tm, LANES), row)],
        out_shape=[jax.ShapeDtypeStruct((n, D_MODEL), F32),
                   jax.ShapeDtypeStruct((SC_PIECES, n, SC_ROW), I32),
                   jax.ShapeDtypeStruct((n, LANES), I32),
                   jax.ShapeDtypeStruct((n, LANES), F32)],
        compiler_params=_cparams(("parallel",), VMEM_LIMIT),
        name="merge_router",
    )(x_p, x_s, attn_p, attn_s, conv_p, conv_s, gate, shf, scf, p["g_conv_ln"], p["b_conv_ln"], p["g_out_attn"],
      p["g_out_conv"], p["w_out"], p["g_norm_ffn"], p["w_router_hi"], p["w_router_lo"], p["b_router"])


def _select_lane(table, idx_col, lane):
    return jnp.sum(jnp.where(lane == idx_col, table, 0.0), axis=-1, keepdims=True)


def _rank_kernel(ti_ref, rk_out, cnt_out, carry_sc):
    i = pl.program_id(0)

    @pl.when(i == 0)
    def _():
        carry_sc[...] = jnp.zeros(carry_sc.shape, F32)

    ti = ti_ref[...]
    tn = ti.shape[0]
    lane = lax.broadcasted_iota(I32, (tn, LANES), 1)
    sel = jnp.zeros((tn, LANES), F32)
    for k in range(TOP_K):
        sel = sel + (lane == ti[:, k:k + 1]).astype(F32)
    r_i = lax.broadcasted_iota(I32, (tn, tn), 0)
    c_i = lax.broadcasted_iota(I32, (tn, tn), 1)
    below = (c_i < r_i).astype(BF16)
    rank = carry_sc[...] + jnp.dot(below, sel.astype(BF16), preferred_element_type=F32)
    rk_out[...] = _lane_pack([_select_lane(rank, ti[:, k:k + 1], lane) for k in range(TOP_K)], F32)
    carry_sc[...] = carry_sc[...] + jnp.sum(sel, axis=0, keepdims=True)
    cnt_out[...] = jnp.broadcast_to(carry_sc[...], cnt_out.shape)


def _pos_kernel(cnt_ref, ti_ref, rk_ref, pos_out, meta_out):
    cnt = cnt_ref[...]
    padded = jnp.ceil(cnt * (1.0 / FFN_TILE)) * FFN_TILE
    r_i = lax.broadcasted_iota(I32, (LANES, LANES), 0)
    c_i = lax.broadcasted_iota(I32, (LANES, LANES), 1)
    before = (r_i < c_i).astype(F32)
    offs = jnp.dot(padded, before, precision=HIGHEST, preferred_element_type=F32)
    ends = offs + padded
    ti = ti_ref[...]
    tn = ti.shape[0]
    lane = lax.broadcasted_iota(I32, (tn, LANES), 1)
    off_row = offs[0:1, :]
    rk = rk_ref[...]
    pos = [_select_lane(jnp.broadcast_to(off_row, (tn, LANES)), ti[:, k:k + 1], lane) + rk[:, k:k + 1]
           for k in range(TOP_K)]
    pos_out[...] = _lane_pack(pos, F32).astype(I32)

    @pl.when(pl.program_id(0) == 0)
    def _():
        nt = meta_out.shape[0]
        start = (lax.broadcasted_iota(I32, (nt, LANES), 0) * FFN_TILE).astype(F32)
        elane = lax.broadcasted_iota(I32, (nt, LANES), 1)
        done = jnp.where((elane < N_EXPERTS) & (jnp.broadcast_to(ends[0:1, :], (nt, LANES)) <= start), 1.0, 0.0)
        expert = jnp.minimum(jnp.sum(done, axis=-1, keepdims=True), N_EXPERTS - 1.0)
        total = jnp.sum(jnp.where(elane < N_EXPERTS, jnp.broadcast_to(padded[0:1, :], (nt, LANES)), 0.0),
                        axis=-1, keepdims=True)
        filled = jnp.broadcast_to((offs + cnt)[0:1, :], (nt, LANES))
        end_e = jnp.sum(jnp.where(elane == expert.astype(I32), filled, 0.0), axis=-1, keepdims=True)
        rows = jnp.clip(end_e - start[:, 0:1], 0.0, float(FFN_TILE))
        meta_out[...] = _lane_pack([expert, total * (1.0 / FFN_TILE), rows], F32).astype(I32)


def _route(top_i, n_tiles_max):
    n = top_i.shape[0]
    tn = ROUTE_TILE
    rk, cnt = pl.pallas_call(
        _rank_kernel,
        grid=(n // tn,),
        in_specs=[pl.BlockSpec((tn, LANES), lambda i: (i, 0))],
        out_specs=[pl.BlockSpec((tn, LANES), lambda i: (i, 0)),
                   pl.BlockSpec((8, LANES), lambda i: (0, 0))],
        out_shape=[jax.ShapeDtypeStruct((n, LANES), F32),
                   jax.ShapeDtypeStruct((8, LANES), F32)],
        scratch_shapes=[pltpu.VMEM((1, LANES), F32)],
        compiler_params=_cparams(("arbitrary",)),
        name="route_rank",
    )(top_i)
    nt_pad = -(-n_tiles_max // 8) * 8
    pos, meta = pl.pallas_call(
        _pos_kernel,
        grid=(n // tn,),
        in_specs=[pl.BlockSpec((8, LANES), lambda i: (0, 0)),
                  pl.BlockSpec((tn, LANES), lambda i: (i, 0)),
                  pl.BlockSpec((tn, LANES), lambda i: (i, 0))],
        out_specs=[pl.BlockSpec((tn, LANES), lambda i: (i, 0)),
                   pl.BlockSpec((nt_pad, LANES), lambda i: (0, 0))],
        out_shape=[jax.ShapeDtypeStruct((n, LANES), I32),
                   jax.ShapeDtypeStruct((nt_pad, LANES), I32)],
        compiler_params=_cparams(("arbitrary",)),
        name="route_pos",
    )(cnt, top_i, rk)
    return pos, meta


def _sc_scatter_rows(x, idx, n_out):
    n_src = x.shape[0]
    n = idx.shape[0]
    n_src_blk = n_src // SC_WINDOW
    mesh = plsc.VectorSubcoreMesh(core_axis_name="c", subcore_axis_name="s")

    @pl.kernel(out_type=jax.ShapeDtypeStruct((n_out, SC_ROW), x.dtype), mesh=mesh)
    def k(x_hbm, i_hbm, o_hbm):
        def body(x_vmem, i_vmem):
            pltpu.sync_copy(x_vmem, o_hbm.at[i_vmem.at[0]])

        pltpu.emit_pipeline(
            body,
            grid=(n // SC_WINDOW,),
            in_specs=[pl.BlockSpec((SC_WINDOW, SC_ROW), index_map=lambda i: (i % n_src_blk, 0)),
                      pl.BlockSpec((1, SC_WINDOW), index_map=lambda i: (0, i))],
            out_specs=[],
            core_axis_name=("c", "s"),
            dimension_semantics=(pltpu.PARALLEL,),
        )(x_hbm, i_hbm)

    return k(x, idx.reshape(1, n))


def _sc_gather_rows(x, idx):
    n = idx.shape[0]
    mesh = plsc.VectorSubcoreMesh(core_axis_name="c", subcore_axis_name="s")

    @pl.kernel(out_type=jax.ShapeDtypeStruct((n, SC_ROW), x.dtype), mesh=mesh)
    def k(x_hbm, i_hbm, o_hbm):
        def body(i_vmem, o_vmem):
            pltpu.sync_copy(x_hbm.at[i_vmem.at[0]], o_vmem)

        pltpu.emit_pipeline(
            body,
            grid=(n // SC_WINDOW,),
            in_specs=[pl.BlockSpec((1, SC_WINDOW), index_map=lambda i: (0, i))],
            out_specs=[pl.BlockSpec((SC_WINDOW, SC_ROW), index_map=lambda i: (i, 0))],
            core_axis_name=("c", "s"),
            dimension_semantics=(pltpu.PARALLEL,),
        )(i_hbm, o_hbm)

    return k(x, idx.reshape(1, n))


def _ffn_kernel(te_ref, nv_ref, nr_ref, x_ref, wu_ref, bu_ref, wd_ref, bd_ref, o_ref, wu_sc, wd_sc):
    t = pl.program_id(0)
    valid = t < nv_ref[0]
    prev = te_ref[jnp.maximum(t - 1, 0)]
    fresh = jnp.logical_or(t == 0, te_ref[t] != prev)

    @pl.when(jnp.logical_and(valid, fresh))
    def _():
        wu_sc[...] = wu_ref[...].astype(BF16)
        wd_sc[...] = wd_ref[...].astype(BF16)

    def experts(n_rows):
        rows = pl.ds(0, n_rows)
        x = _unpack_rows(jnp.concatenate([x_ref[j, rows, :] for j in range(SC_PIECES)], axis=1))
        z = jnp.dot(x.astype(BF16), wu_sc[...], preferred_element_type=F32) + bu_ref[...]
        zg = jnp.minimum(z[:, :D_FF], SWIGLU_LIMIT)
        zl = jnp.clip(z[:, D_FF:], -SWIGLU_LIMIT, SWIGLU_LIMIT)
        act = zg * jax.nn.sigmoid(SWIGLU_ALPHA * zg) * (zl + 1.0)
        out = _pack_rows(jnp.dot(act.astype(BF16), wd_sc[...], preferred_element_type=F32) + bd_ref[...])
        for j in range(SC_PIECES):
            o_ref[j, rows, :] = out[:, j * SC_ROW:(j + 1) * SC_ROW]

    half = FFN_TILE // 2

    @pl.when(jnp.logical_and(valid, nr_ref[t] > half))
    def _():
        experts(FFN_TILE)

    @pl.when(jnp.logical_and(valid, nr_ref[t] <= half))
    def _():
        experts(half)


def _expert_ffn(tile_expert, n_valid, tile_rows, x_sorted, w_up, b_up, w_down, b_down):
    n_slots = x_sorted.shape[1]
    n_tiles = n_slots // FFN_TILE
    xmap = lambda t, te, nv, nr: (0, jnp.minimum(t, nv[0] - 1), 0)
    emap = lambda t, te, nv, nr: (te[t], 0, 0)
    grid_spec = pltpu.PrefetchScalarGridSpec(
        num_scalar_prefetch=3,
        grid=(n_tiles,),
        in_specs=[pl.BlockSpec((SC_PIECES, FFN_TILE, SC_ROW), xmap),
                  pl.BlockSpec((None, D_MODEL, 2 * D_FF), emap),
                  pl.BlockSpec((None, 1, 2 * D_FF), emap),
                  pl.BlockSpec((None, D_FF, D_MODEL), emap),
                  pl.BlockSpec((None, 1, D_MODEL), emap)],
        out_specs=pl.BlockSpec((SC_PIECES, FFN_TILE, SC_ROW), xmap),
        scratch_shapes=[pltpu.VMEM((D_MODEL, 2 * D_FF), BF16),
                        pltpu.VMEM((D_FF, D_MODEL), BF16)],
    )
    return pl.pallas_call(
        _ffn_kernel,
        grid_spec=grid_spec,
        out_shape=jax.ShapeDtypeStruct((SC_PIECES, n_slots, SC_ROW), I32),
        compiler_params=_cparams(("arbitrary",), VMEM_LIMIT),
        name="expert_ffn",
    )(tile_expert, n_valid, tile_rows, x_sorted, w_up, b_up.reshape(N_EXPERTS, 1, 2 * D_FF), w_down,
      b_down.reshape(N_EXPERTS, 1, D_MODEL))


def _combine_kernel(n_prompt_tiles, y_ref, g_ref, tw_ref, gate_ref, op_ref, os_ref):
    tw = tw_ref[...]
    rows = lambda k: _unpack_rows(jnp.concatenate([g_ref[k, j] for j in range(SC_PIECES)], axis=1))
    moe = rows(0) * tw[:, 0:1]
    for k in range(1, TOP_K):
        moe = moe + rows(k) * tw[:, k:k + 1]
    out = y_ref[...] + gate_ref[...] * moe
    i = pl.program_id(0)

    @pl.when(i < n_prompt_tiles)
    def _():
        op_ref[...] = out

    @pl.when(i >= n_prompt_tiles)
    def _():
        os_ref[...] = out


def _combine(y_all, gathered, top_w, gate, n_p):
    n = y_all.shape[0]
    tm = TOK_TILE
    n_prompt_tiles = n_p // tm
    row = lambda i: (i, 0)
    return pl.pallas_call(
        functools.partial(_combine_kernel, n_prompt_tiles),
        grid=(n // tm,),
        in_specs=[pl.BlockSpec((tm, D_MODEL), row),
                  pl.BlockSpec((TOP_K, SC_PIECES, tm, SC_ROW), lambda i: (0, 0, i, 0)),
                  pl.BlockSpec((tm, LANES), row),
                  pl.BlockSpec((tm, D_MODEL), _mod_index(n_prompt_tiles))],
        out_specs=[pl.BlockSpec((tm, D_MODEL), _prompt_index(n_prompt_tiles)),
                   pl.BlockSpec((tm, D_MODEL), _decode_index(n_prompt_tiles))],
        out_shape=[jax.ShapeDtypeStruct((n_p, D_MODEL), F32),
                   jax.ShapeDtypeStruct((n - n_p, D_MODEL), F32)],
        compiler_params=_cparams(("arbitrary",), VMEM_LIMIT),
        name="moe_combine",
    )(y_all, gathered, top_w, gate)


def _head_tiles(nope, rope):
    pad = jnp.zeros(nope.shape[:-1] + (HEAD_PAD - QK_NOPE - QK_ROPE,), nope.dtype)
    t = jnp.concatenate([nope, rope, pad], axis=-1)
    return t.reshape(t.shape[:-2] + (N_HEADS * HEAD_PAD,))


def _prepare(pos, w_in, g_norm_mix, g_q_lat, w_q_up, g_q_nope, g_q_rope, g_kv_lat, g_k_rope, w_kv_up, g_k_nope,
             g_conv_ln, b_conv_ln, g_out_attn, g_out_conv, w_out, g_norm_ffn, w_router, b_router):
    z32 = jnp.zeros((HEAD_PAD - QK_NOPE - QK_ROPE,), F32)
    z64 = jnp.zeros((QK_NOPE,), F32)
    d = D_MODEL
    kpe_cols = w_in[:, Q_LORA + KV_LORA:Q_LORA + KV_LORA + QK_ROPE]
    kpe_tile = jnp.concatenate([jnp.zeros((d, QK_NOPE), F32), kpe_cols, jnp.zeros((d, z32.shape[0]), F32)], axis=1)
    w_in_r = jnp.concatenate([w_in[:, :Q_LORA + KV_LORA], kpe_tile, w_in[:, Q_LORA + KV_LORA + QK_ROPE:]], axis=1)
    wq = w_q_up.reshape(Q_LORA, N_HEADS, QK_NOPE + QK_ROPE)
    w_q_r = _head_tiles(wq[..., :QK_NOPE], wq[..., QK_NOPE:])
    wk = w_kv_up[..., :QK_NOPE]
    wv = w_kv_up[..., QK_NOPE:].reshape(KV_LORA, ATTN_WIDTH)
    w_k_r = _head_tiles(wk, jnp.zeros((KV_LORA, N_HEADS, QK_ROPE), F32))
    lane = np.arange(LANES)
    grp = np.where(lane < QK_NOPE, 0, np.where(lane < QK_NOPE + QK_ROPE, 1, 2))
    m_grp = ((grp[:, None] == grp[None, :]) & (grp[:, None] < 2)).astype(np.float32)
    m_grp = m_grp / np.where(grp < 1, QK_NOPE, QK_ROPE)[None, :]
    inv = np.float32(ROPE_THETA) ** (-np.arange(0, QK_ROPE, 2, dtype=np.float32) / np.float32(QK_ROPE))
    ang = pos.astype(np.float32)[:, None] * inv[None, :].astype(np.float32)
    cs, sn = jnp.asarray(np.cos(ang), F32), jnp.asarray(np.sin(ang), F32)
    n = pos.shape[0]
    cos_t = jnp.concatenate([jnp.ones((n, QK_NOPE), F32), cs, cs, jnp.zeros((n, z32.shape[0]), F32)], axis=1)
    sin_t = jnp.concatenate([jnp.zeros((n, QK_NOPE), F32), -sn, sn, jnp.zeros((n, z32.shape[0]), F32)], axis=1)
    wk_g = wk * g_k_nope[None, None, :]
    absorb = jnp.zeros((N_HEADS, HEAD_PAD, 2 * LANES), F32)
    absorb = absorb.at[:, :QK_NOPE, :KV_LORA].set(jnp.transpose(wk_g, (1, 2, 0)))
    absorb = absorb.at[:, ROPE_LO:ROPE_LO + QK_ROPE, KV_LORA:KV_LORA + QK_ROPE].set(
        jnp.broadcast_to(jnp.eye(QK_ROPE, dtype=F32), (N_HEADS, QK_ROPE, QK_ROPE)))
    w_kt = jnp.transpose(wk, (1, 2, 0)).reshape(N_HEADS * QK_NOPE, KV_LORA)
    wr = jnp.concatenate([w_router, jnp.zeros((d, LANES - N_EXPERTS), F32)], axis=1)
    wr_hi = wr.astype(BF16)
    wr_lo = (wr - wr_hi.astype(F32)).astype(BF16)
    br = jnp.concatenate([b_router, jnp.full((LANES - N_EXPERTS,), NEG_INF, F32)])
    wv_h = w_kv_up[..., QK_NOPE:]
    w_v_tiles = jnp.concatenate([wv_h, jnp.zeros((KV_LORA, N_HEADS, HEAD_PAD - V_DIM), F32)], axis=-1)
    v_ones = (jnp.arange(N_HEADS * HEAD_PAD) % HEAD_PAD == V_DIM).astype(F32).reshape(1, N_HEADS * HEAD_PAD)
    return {
        "w_v_tiles": w_v_tiles.reshape(KV_LORA, N_HEADS * HEAD_PAD).astype(BF16), "v_ones": v_ones,
        "w_router_hi": wr_hi, "w_router_lo": wr_lo,
        "w_in": w_in_r.astype(BF16), "g_norm_mix": g_norm_mix.reshape(1, d), "g_q_lat": g_q_lat.reshape(1, Q_LORA),
        "w_q_up": w_q_r.astype(BF16),
        "gain_q": jnp.concatenate([g_q_nope, g_q_rope, z32]).reshape(1, LANES),
        "m_grp": jnp.asarray(m_grp, BF16), "cos_t": cos_t, "sin_t": sin_t,
        "g_kv_lat": g_kv_lat.reshape(1, KV_LORA),
        "gain_k": jnp.concatenate([g_k_nope, z64]).reshape(1, LANES),
        "gain_kpe": jnp.concatenate([z64, g_k_rope, z32]).reshape(1, LANES),
        "w_k": w_k_r.astype(BF16), "w_v": wv.astype(BF16),
        "w_abs": absorb.reshape(N_HEADS * HEAD_PAD, 2 * LANES).astype(BF16), "w_kt": w_kt.astype(BF16),
        "g_conv_ln": g_conv_ln.reshape(1, CONV_CH), "b_conv_ln": b_conv_ln.reshape(1, CONV_CH),
        "g_out_attn": g_out_attn.reshape(1, ATTN_WIDTH), "g_out_conv": g_out_conv.reshape(1, CONV_CH),
        "w_out": w_out.astype(BF16), "g_norm_ffn": g_norm_ffn.reshape(1, d),
        "b_router": br.reshape(1, LANES),
    }


def _mod_table(mod_p, mod_s, t_new):
    return jnp.concatenate([jnp.broadcast_to(mod_p, (TOK_TILE, D_MODEL)), jnp.repeat(mod_s, t_new, axis=0)], axis=0)


def kernel(x_prompt, x_sample, cache_ckv, cache_kpe, state_conv, page_table, c_prompt, c_sample, w_ada, b_ada, g_norm_mix, g_norm_ffn, w_in, g_q_lat, w_q_up, g_q_nope, g_q_rope, g_kv_lat, g_k_rope, w_kv_up, g_k_nope, w_dw, b_dw, g_conv_ln, b_conv_ln, g_out_attn, g_out_conv, w_out, w_router, b_router, w_exp_up, b_exp_up, w_exp_down, b_exp_down):
    bsz, seq, d = x_prompt.shape
    n_seq, t_new = x_sample.shape[:2]
    depth = w_ada.shape[0]
    assert bsz == 1 and depth == 1 and t_new == 4 and d == D_MODEL
    n_p = bsz * seq
    n_s = n_seq * t_new
    n = n_p + n_s
    past = page_table.shape[1] * cache_ckv.shape[2]
    l = 0

    pos = np.concatenate([np.arange(seq), np.tile(past + np.arange(t_new), n_seq)])
    p = _prepare(pos, w_in[l], g_norm_mix[l], g_q_lat[l], w_q_up[l], g_q_nope[l], g_q_rope[l], g_kv_lat[l],
                 g_k_rope[l], w_kv_up[l], g_k_nope[l], g_conv_ln[l], b_conv_ln[l], g_out_attn[l], g_out_conv[l],
                 w_out[l], g_norm_ffn[l], w_router[l], b_router[l])

    n_c = 1 + n_seq
    c_all = jnp.concatenate([c_prompt, c_sample, jnp.zeros((-n_c % 8, d), F32)], axis=0)
    mod = _adaln(c_all, w_ada[l], b_ada[l])
    tabs = [_mod_table(mod[0:1, j * d:(j + 1) * d], mod[1:n_c, j * d:(j + 1) * d], t_new) for j in range(6)]
    sh_m, sc_m, gt_m, sh_f, sc_f, gt_f = tabs

    x_p = x_prompt.reshape(n_p, d)
    x_s = x_sample.reshape(n_s, d)
    q_all, ckv_p, ckv_s, kpe_tp, kpe_ts, u_all, k_all, v_all = _mixer_inputs(x_p, x_s, sh_m, sc_m, p)

    attn_p = _prompt_attention(q_all, k_all, v_all, seq)
    ckv_s = ckv_s.reshape(n_seq, t_new, KV_LORA)
    kpe_s = jnp.transpose(kpe_ts).reshape(n_seq, t_new, QK_ROPE)
    ckv_new_pad = jnp.pad(ckv_s, ((0, 0), (0, LANES - t_new), (0, 0)))
    kpe_new_t = jnp.swapaxes(jnp.pad(kpe_s, ((0, 0), (0, LANES - t_new), (0, 0))), 1, 2)
    q_s = q_all[n_p:].reshape(n_seq, t_new, N_HEADS * HEAD_PAD)
    attn_s = _sample_attention(page_table, q_s, ckv_new_pad, kpe_new_t, cache_ckv,
                               jnp.swapaxes(cache_kpe, 2, 3), p)

    conv_p = _prompt_conv(u_all, w_dw[l], b_dw[l].reshape(1, CONV_CH))
    u_s = u_all[n_p:].reshape(n_seq, t_new, CONV_CH)
    u_ext_s = jnp.concatenate([state_conv[l], u_s], axis=1)
    ext = CONV_W - 1 + t_new
    ext_pad = -ext % SUBLANES
    w_taps = jnp.stack([jnp.pad(w_dw[l], ((t, t_new - 1 - t + ext_pad), (0, 0))) for t in range(t_new)])
    conv_s = _sample_conv(jnp.pad(u_ext_s, ((0, 0), (0, ext_pad), (0, 0))), w_taps, b_dw[l].reshape(1, CONV_CH))
    conv_s = jnp.transpose(conv_s, (1, 0, 2)).reshape(n_s, CONV_CH)

    y_all, f_all, top_i, top_w = _merge_router(x_p, x_s, attn_p, attn_s.reshape(n_s, ATTN_WIDTH), conv_p, conv_s,
                                               gt_m, sh_f, sc_f, p)

    n_tiles = -(-(n * TOP_K + N_EXPERTS * (FFN_TILE - 1)) // FFN_TILE)
    n_slots = n_tiles * FFN_TILE
    pos_tok, meta = _route(top_i, n_tiles)
    slot = jnp.transpose(pos_tok[:, :TOP_K])
    piece_idx = (slot[:, None, :] + (jnp.arange(SC_PIECES, dtype=I32) * n_slots)[None, :, None]).reshape(-1)
    x_sorted = _sc_scatter_rows(f_all.reshape(SC_PIECES * n, SC_ROW), piece_idx, SC_PIECES * n_slots)
    h_sorted = _expert_ffn(meta[:n_tiles, 0], meta[0:1, 1], meta[:n_tiles, 2],
                           x_sorted.reshape(SC_PIECES, n_slots, SC_ROW),
                           w_exp_up[l], b_exp_up[l], w_exp_down[l], b_exp_down[l])
    gathered = _sc_gather_rows(h_sorted.reshape(SC_PIECES * n_slots, SC_ROW), piece_idx)
    y_p, y_s = _combine(y_all, gathered.reshape(TOP_K, SC_PIECES, n, SC_ROW), top_w, gt_f, n_p)
    y_p = y_p.reshape(bsz, seq, d)
    y_s = y_s.reshape(n_seq, t_new, d)
    ckv_prompt = ckv_p.reshape(1, bsz, seq, KV_LORA)
    kpe_prompt = jnp.swapaxes(kpe_tp.reshape(1, bsz, QK_ROPE, seq), 2, 3)
    conv_prompt = u_all[n_p - (CONV_W - 1):n_p].reshape(1, bsz, CONV_W - 1, CONV_CH)
    ckv_sample = ckv_s[None]
    kpe_sample = kpe_s[None]
    conv_sample = u_ext_s[:, t_new:][None]
    return (y_p, y_s, ckv_prompt, kpe_prompt, conv_prompt, ckv_sample, kpe_sample, conv_sample)
```

```python
import functools

import numpy as np
import jax
import jax.numpy as jnp
from jax import lax
from jax.experimental import pallas as pl
from jax.experimental.pallas import tpu as pltpu
from jax.experimental.pallas import tpu_sc as plsc

F32 = jnp.float32
BF16 = jnp.bfloat16
I32 = jnp.int32
HIGHEST = lax.Precision.HIGHEST

D_MODEL = 1024
N_HEADS = 8
QK_NOPE = 64
QK_ROPE = 32
V_DIM = 64
Q_LORA = 256
KV_LORA = 128
ATTN_WIDTH = N_HEADS * V_DIM
CONV_CH = D_MODEL - ATTN_WIDTH
CONV_W = 31
N_EXPERTS = 32
TOP_K = 4
D_FF = D_MODEL
SWIGLU_LIMIT = 7.0
SWIGLU_ALPHA = 1.702
EPS = 1e-6
NEG_INF = -1e30
ROPE_THETA = 10000.0
SM_SCALE = (QK_NOPE + QK_ROPE) ** -0.5
LOG2E = 1.4426950408889634
Q_SCALE = SM_SCALE * LOG2E

LANES = 128
SUBLANES = 8
HEAD_PAD = LANES
ROPE_LO = QK_NOPE
ROPE_HALF = QK_ROPE // 2

TOK_TILE = 512
FA_TQ = 1024
FA_TK = FA_TQ
FA_HEADS = 8
PAGES_PER_STEP = 128
FFN_TILE = 512
ROUTE_TILE = 512
SC_ROW = 256
PACKED_WIDTH = D_MODEL // 2
SC_PIECES = PACKED_WIDTH // SC_ROW
SC_WINDOW = 128
VMEM_LIMIT = 56 * 1024 * 1024


def _cparams(sem, vmem=None):
    return pltpu.CompilerParams(dimension_semantics=sem, vmem_limit_bytes=vmem)


def _rsqrt_mean(x, n):
    return lax.rsqrt(jnp.sum(x * x, axis=-1, keepdims=True) * (1.0 / n) + EPS)


def _pack_rows(x):
    half = x.shape[1] // 2
    hi = lax.bitcast_convert_type(x[:, :half].astype(BF16).astype(F32), I32)
    lo = lax.bitcast_convert_type(x[:, half:].astype(BF16).astype(F32), I32)
    return hi | lax.shift_right_logical(lo, jnp.full(lo.shape, 16, I32))


def _unpack_rows(w):
    hi = lax.bitcast_convert_type(w & jnp.int32(-65536), F32)
    lo = lax.bitcast_convert_type(lax.shift_left(w, jnp.full(w.shape, 16, I32)), F32)
    return jnp.concatenate([hi, lo], axis=1)


def _ada_kernel(c_ref, w_ref, b_ref, o_ref):
    c = c_ref[...]
    s = c * jax.nn.sigmoid(c)
    o_ref[...] = jnp.dot(s, w_ref[...], precision=HIGHEST, preferred_element_type=F32) + b_ref[...]


def _adaln(c_all, w_ada, b_ada):
    rows = c_all.shape[0]
    n_out = w_ada.shape[1]
    return pl.pallas_call(
        _ada_kernel,
        grid=(n_out // D_MODEL,),
        in_specs=[pl.BlockSpec((rows, D_MODEL), lambda j: (0, 0)),
                  pl.BlockSpec((D_MODEL, D_MODEL), lambda j: (0, j)),
                  pl.BlockSpec((1, D_MODEL), lambda j: (0, j))],
        out_specs=pl.BlockSpec((rows, D_MODEL), lambda j: (0, j)),
        out_shape=jax.ShapeDtypeStruct((rows, n_out), F32),
        compiler_params=_cparams(("arbitrary",)),
        name="adaln",
    )(c_all, w_ada, b_ada.reshape(1, n_out))


def _group_norm_rope(x, m_grp, gain, cos_t, sin_t, first_half):
    ms = jnp.dot((x * x).astype(BF16), m_grp, preferred_element_type=F32)
    xn = x * lax.rsqrt(ms + EPS) * gain
    swapped = jnp.where(first_half, pltpu.roll(xn, LANES - ROPE_HALF, 1), pltpu.roll(xn, ROPE_HALF, 1))
    return xn * cos_t + swapped * sin_t


def _pick_rows(n_prompt_tiles, prompt_ref, decode_ref):
    return jnp.where(pl.program_id(0) < n_prompt_tiles, prompt_ref[...], decode_ref[...])


def _mix_kernel(n_prompt_tiles, xp_ref, xs_ref, sh_ref, sc_ref, gmix_ref, win_ref, gql_ref, wq_ref, gq_ref, m_ref,
                cos_ref, sin_ref, gkv_ref, gk_ref, gkpe_ref, wk_ref, wv_ref, vone_ref,
                q_out, ckvp_out, ckvs_out, kpetp_out, kpets_out, u_out, k_out, v_out):
    x = _pick_rows(n_prompt_tiles, xp_ref, xs_ref)
    h = x * _rsqrt_mean(x, D_MODEL) * gmix_ref[...]
    h = h * (1.0 + sc_ref[...]) + sh_ref[...]
    proj = jnp.dot(h.astype(BF16), win_ref[...], preferred_element_type=F32)
    q_lat = proj[:, :Q_LORA]
    ckv_raw = proj[:, Q_LORA:Q_LORA + KV_LORA]
    kpe_blk = proj[:, Q_LORA + KV_LORA:Q_LORA + KV_LORA + LANES]
    glu_lo = Q_LORA + KV_LORA + LANES
    u_out[...] = proj[:, glu_lo:glu_lo + CONV_CH] * jax.nn.sigmoid(proj[:, glu_lo + CONV_CH:glu_lo + 2 * CONV_CH])

    m_grp = m_ref[...]
    cos_t = cos_ref[...]
    sin_t = sin_ref[...]
    lane = lax.broadcasted_iota(I32, (1, LANES), 1)
    first_half = lane < ROPE_LO + ROPE_HALF

    q_lat_n = q_lat * _rsqrt_mean(q_lat, Q_LORA) * gql_ref[...]
    q = jnp.dot(q_lat_n.astype(BF16), wq_ref[...], preferred_element_type=F32)
    gq = gq_ref[...]
    for hd in range(N_HEADS):
        qh = _group_norm_rope(q[:, hd * HEAD_PAD:(hd + 1) * HEAD_PAD], m_grp, gq, cos_t, sin_t, first_half)
        q_out[:, hd * HEAD_PAD:(hd + 1) * HEAD_PAD] = (qh * Q_SCALE).astype(BF16)

    ckv_n = ckv_raw * _rsqrt_mean(ckv_raw, KV_LORA) * gkv_ref[...]
    kpe_r = _group_norm_rope(kpe_blk, m_grp, gkpe_ref[...], cos_t, sin_t, first_half)
    kpe_t = jnp.transpose(kpe_r)[ROPE_LO:ROPE_LO + QK_ROPE, :]
    is_prompt = pl.program_id(0) < n_prompt_tiles

    @pl.when(is_prompt)
    def _():
        ckvp_out[...] = ckv_n
        kpetp_out[...] = kpe_t

    @pl.when(jnp.logical_not(is_prompt))
    def _():
        ckvs_out[...] = ckv_n
        kpets_out[...] = kpe_t

    ckv_b = ckv_n.astype(BF16)
    kexp = jnp.dot(ckv_b, wk_ref[...], preferred_element_type=F32)
    gk = gk_ref[...]
    for hd in range(N_HEADS):
        kh = kexp[:, hd * HEAD_PAD:(hd + 1) * HEAD_PAD]
        ms = jnp.dot((kh * kh).astype(BF16), m_grp, preferred_element_type=F32)
        k_out[:, hd * HEAD_PAD:(hd + 1) * HEAD_PAD] = (kh * lax.rsqrt(ms + EPS) * gk + kpe_r).astype(BF16)
    v_out[...] = (jnp.dot(ckv_b, wv_ref[...], preferred_element_type=F32) + vone_ref[...]).astype(BF16)


def _mod_index(n_prompt_tiles):
    return lambda i: (jnp.where(i < n_prompt_tiles, 0, i - n_prompt_tiles + 1), 0)


def _prompt_index(n_prompt_tiles):
    return lambda i: (jnp.minimum(i, n_prompt_tiles - 1), 0)


def _decode_index(n_prompt_tiles):
    return lambda i: (jnp.maximum(i - n_prompt_tiles, 0), 0)


def _mixer_inputs(x_p, x_s, sh, sc, p):
    n_prompt_tiles = x_p.shape[0] // TOK_TILE
    n = x_p.shape[0] + x_s.shape[0]
    tm = TOK_TILE
    const = lambda i: (0, 0)
    row = lambda i: (i, 0)
    mod = _mod_index(n_prompt_tiles)
    hw = N_HEADS * HEAD_PAD
    in_cols = p["w_in"].shape[1]
    return pl.pallas_call(
        functools.partial(_mix_kernel, n_prompt_tiles),
        grid=(n // tm,),
        in_specs=[pl.BlockSpec((tm, D_MODEL), _prompt_index(n_prompt_tiles)),
                  pl.BlockSpec((tm, D_MODEL), _decode_index(n_prompt_tiles)),
                  pl.BlockSpec((tm, D_MODEL), mod),
                  pl.BlockSpec((tm, D_MODEL), mod),
                  pl.BlockSpec((1, D_MODEL), const),
                  pl.BlockSpec((D_MODEL, in_cols), const),
                  pl.BlockSpec((1, Q_LORA), const),
                  pl.BlockSpec((Q_LORA, hw), const),
                  pl.BlockSpec((1, LANES), const),
                  pl.BlockSpec((LANES, LANES), const),
                  pl.BlockSpec((tm, LANES), row),
                  pl.BlockSpec((tm, LANES), row),
                  pl.BlockSpec((1, KV_LORA), const),
                  pl.BlockSpec((1, LANES), const),
                  pl.BlockSpec((1, LANES), const),
                  pl.BlockSpec((KV_LORA, hw), const),
                  pl.BlockSpec((KV_LORA, hw), const),
                  pl.BlockSpec((1, hw), const)],
        out_specs=[pl.BlockSpec((tm, hw), row),
                   pl.BlockSpec((tm, KV_LORA), _prompt_index(n_prompt_tiles)),
                   pl.BlockSpec((tm, KV_LORA), _decode_index(n_prompt_tiles)),
                   pl.BlockSpec((QK_ROPE, tm), lambda i: (0, jnp.minimum(i, n_prompt_tiles - 1))),
                   pl.BlockSpec((QK_ROPE, tm), lambda i: (0, jnp.maximum(i - n_prompt_tiles, 0))),
                   pl.BlockSpec((tm, CONV_CH), row),
                   pl.BlockSpec((tm, hw), row),
                   pl.BlockSpec((tm, hw), row)],
        out_shape=[jax.ShapeDtypeStruct((n, hw), BF16),
                   jax.ShapeDtypeStruct((x_p.shape[0], KV_LORA), F32),
                   jax.ShapeDtypeStruct((x_s.shape[0], KV_LORA), F32),
                   jax.ShapeDtypeStruct((QK_ROPE, x_p.shape[0]), F32),
                   jax.ShapeDtypeStruct((QK_ROPE, x_s.shape[0]), F32),
                   jax.ShapeDtypeStruct((n, CONV_CH), F32),
                   jax.ShapeDtypeStruct((n, hw), BF16),
                   jax.ShapeDtypeStruct((n, hw), BF16)],
        compiler_params=_cparams(("arbitrary",), VMEM_LIMIT),
        name="mixer_inputs",
    )(x_p, x_s, sh, sc, p["g_norm_mix"], p["w_in"], p["g_q_lat"], p["w_q_up"], p["gain_q"], p["m_grp"],
      p["cos_t"], p["sin_t"], p["g_kv_lat"], p["gain_k"], p["gain_kpe"], p["w_k"], p["w_v_tiles"], p["v_ones"])


def _fa_kernel(qt_ref, kt_ref, q_ref, k_ref, v_ref, o_ref, m_sc, acc_sc):
    t = pl.program_id(1)
    qi = qt_ref[t]
    ki = kt_ref[t]
    last_k = (qi + 1) * (FA_TQ // FA_TK) - 1

    @pl.when(ki == 0)
    def _():
        m_sc[...] = jnp.full(m_sc.shape, NEG_INF, F32)
        acc_sc[...] = jnp.zeros(acc_sc.shape, F32)

    def update(hh, r0, nr, k0, nk, masked):
        q = q_ref[pl.ds(r0, nr), hh * HEAD_PAD:(hh + 1) * HEAD_PAD]
        k = k_ref[pl.ds(k0, nk), hh * HEAD_PAD:(hh + 1) * HEAD_PAD]
        v = v_ref[pl.ds(k0, nk), hh * HEAD_PAD:(hh + 1) * HEAD_PAD]
        s = lax.dot_general(q, k, (((1,), (1,)), ((), ())), preferred_element_type=F32)
        if masked:
            col_minus_row = lax.broadcasted_iota(I32, (nr, nk), 1) - lax.broadcasted_iota(I32, (nr, nk), 0)
            s = jnp.where(col_minus_row <= (qi * FA_TQ + r0) - (ki * FA_TK + k0), s, NEG_INF)
        m_prev = m_sc[hh, pl.ds(r0, nr), :]
        m_new = jnp.maximum(m_prev, jnp.max(s, axis=-1, keepdims=True))
        alpha = jnp.exp2(m_prev - m_new)
        pr = jnp.exp2((s - jnp.concatenate([m_new] * (nk // LANES), axis=1)).astype(BF16))
        acc_sc[hh, pl.ds(r0, nr), :] = (alpha * acc_sc[hh, pl.ds(r0, nr), :]
                                        + jnp.dot(pr, v, preferred_element_type=F32))
        m_sc[hh, pl.ds(r0, nr), :] = m_new

    @pl.when(ki < qi)
    def _():
        for hh in range(FA_HEADS):
            update(hh, 0, FA_TQ, 0, FA_TK, False)

    @pl.when(ki == qi)
    def _():
        half = FA_TQ // 2
        for hh in range(FA_HEADS):
            update(hh, 0, half, 0, half, True)
            update(hh, half, half, 0, FA_TK, True)

    @pl.when(ki == last_k)
    def _():
        for hh in range(FA_HEADS):
            acc = acc_sc[hh]
            o_ref[:, hh * V_DIM:(hh + 1) * V_DIM] = acc[:, :V_DIM] / acc[:, V_DIM:V_DIM + 1]


def _prompt_attention(q_all, k_all, v_all, seq):
    nq = seq // FA_TQ
    ratio = FA_TQ // FA_TK
    qt, kt = [], []
    for qi in range(nq):
        for ki in range((qi + 1) * ratio):
            qt.append(qi)
            kt.append(ki)
    qt = jnp.asarray(np.array(qt, np.int32))
    kt = jnp.asarray(np.array(kt, np.int32))
    n_pairs = int(qt.shape[0])
    grid_spec = pltpu.PrefetchScalarGridSpec(
        num_scalar_prefetch=2,
        grid=(N_HEADS // FA_HEADS, n_pairs),
        in_specs=[pl.BlockSpec((FA_TQ, FA_HEADS * HEAD_PAD), lambda hp, t, qt, kt: (qt[t], hp)),
                  pl.BlockSpec((FA_TK, FA_HEADS * HEAD_PAD), lambda hp, t, qt, kt: (kt[t], hp)),
                  pl.BlockSpec((FA_TK, FA_HEADS * HEAD_PAD), lambda hp, t, qt, kt: (kt[t], hp))],
        out_specs=pl.BlockSpec((FA_TQ, FA_HEADS * V_DIM), lambda hp, t, qt, kt: (qt[t], hp)),
        scratch_shapes=[pltpu.VMEM((FA_HEADS, FA_TQ, LANES), F32),
                        pltpu.VMEM((FA_HEADS, FA_TQ, HEAD_PAD), F32)],
    )
    return pl.pallas_call(
        _fa_kernel,
        grid_spec=grid_spec,
        out_shape=jax.ShapeDtypeStruct((seq, ATTN_WIDTH), F32),
        compiler_params=_cparams(("parallel", "arbitrary"), VMEM_LIMIT),
        name="prompt_attention",
    )(qt, kt, q_all, k_all, v_all)


def _sattn_kernel(pt_ref, q_ref, wabs_ref, wkt_ref, wv_ref, ckvn_ref, kpen_ref, ckv_hbm, kpe_hbm,
                  o_ref, m_sc, l_sc, acc_sc, qa_sc, wall_sc, ckv_buf, kpe_buf, sem):
    kb = pl.program_id(1)
    n_kb = pl.num_programs(1)
    step = pl.program_id(0) * n_kb + kb
    n_steps = pl.num_programs(0) * n_kb
    slot = step % 2
    rows = 4 * N_HEADS

    def start_fetch(step_i, slot_i):
        for j in range(PAGES_PER_STEP):
            page = pt_ref[step_i * PAGES_PER_STEP + j]
            pltpu.make_async_copy(ckv_hbm.at[0, page], ckv_buf.at[slot_i, j], sem.at[0, slot_i]).start(priority=j % 2)
            pltpu.make_async_copy(kpe_hbm.at[0, page], kpe_buf.at[slot_i, j],
                                  sem.at[1, slot_i]).start(priority=(j + 1) % 2)

    @pl.when(step == 0)
    def _():
        start_fetch(step, slot)

    @pl.when(step + 1 < n_steps)
    def _():
        start_fetch(step + 1, 1 - slot)

    @pl.when(kb == 0)
    def _():
        m_sc[...] = jnp.full(m_sc.shape, NEG_INF, F32)
        l_sc[...] = jnp.zeros(l_sc.shape, F32)
        acc_sc[...] = jnp.zeros(acc_sc.shape, F32)
        q4 = q_ref[...].astype(F32)
        head_of_lane = lax.broadcasted_iota(I32, (N_HEADS, N_HEADS * HEAD_PAD), 1) // HEAD_PAD
        head_of_row = lax.broadcasted_iota(I32, (N_HEADS, N_HEADS * HEAD_PAD), 0)
        own = head_of_lane == head_of_row
        qbd = jnp.concatenate(
            [jnp.where(own, jnp.broadcast_to(q4[qq:qq + 1, :], own.shape), 0.0) for qq in range(4)], axis=0)
        qa = jnp.dot(qbd.astype(BF16), wabs_ref[...], preferred_element_type=F32).astype(BF16)
        qa_sc[...] = qa
        wall_sc[pl.ds(0, N_HEADS * QK_NOPE), :] = wkt_ref[...]
        wall_sc[pl.ds(N_HEADS * QK_NOPE, rows), :] = qa[:, :KV_LORA]

    def attend(state, ckv_b, kpe_t, mask):
        m_prev, l_prev, acc_prev = state
        nt = (((1,), (1,)), ((), ()))
        kn_all = lax.dot_general(wall_sc[...], ckv_b, nt, preferred_element_type=F32)
        keys = kn_all.shape[1]
        kn_t = kn_all[:N_HEADS * QK_NOPE]
        ss = jnp.sum((kn_t * kn_t).reshape(N_HEADS, QK_NOPE, keys), axis=1)
        r8 = lax.rsqrt(ss * (1.0 / QK_NOPE) + EPS)
        rope = jnp.dot(qa_sc[:, KV_LORA:KV_LORA + QK_ROPE], kpe_t, preferred_element_type=F32)
        s = kn_all[N_HEADS * QK_NOPE:] * jnp.concatenate([r8] * 4, axis=0) + rope
        if mask is not None:
            s = jnp.where(mask, s, NEG_INF)
        m_new = jnp.maximum(m_prev, jnp.max(s, axis=-1, keepdims=True))
        alpha = jnp.exp2(m_prev - m_new)
        pr = jnp.exp2(s - m_new)
        l_new = alpha * l_prev + jnp.sum(pr, axis=-1, keepdims=True)
        acc_new = alpha * acc_prev + jnp.dot(pr.astype(BF16), ckv_b, preferred_element_type=F32)
        return m_new, l_new, acc_new

    pltpu.make_async_copy(ckv_hbm.at[0, pl.ds(0, PAGES_PER_STEP)], ckv_buf.at[slot], sem.at[0, slot]).wait()
    pltpu.make_async_copy(kpe_hbm.at[0, pl.ds(0, PAGES_PER_STEP)], kpe_buf.at[slot], sem.at[1, slot]).wait()

    page = ckv_buf.shape[2]
    ckv_b = ckv_buf[slot].reshape(PAGES_PER_STEP * page, KV_LORA).astype(BF16)
    kpe_t = jnp.concatenate([kpe_buf[slot, j].astype(BF16) for j in range(PAGES_PER_STEP)], axis=1)

    @pl.when(kb < n_kb - 1)
    def _():
        m_sc[...], l_sc[...], acc_sc[...] = attend((m_sc[...], l_sc[...], acc_sc[...]), ckv_b, kpe_t, None)

    @pl.when(kb == n_kb - 1)
    def _():
        n_cached = PAGES_PER_STEP * page
        key = lax.broadcasted_iota(I32, (rows, n_cached + LANES), 1) - n_cached
        qry = lax.broadcasted_iota(I32, (rows, n_cached + LANES), 0) // N_HEADS
        ckv_all = jnp.concatenate([ckv_b, ckvn_ref[...].astype(BF16)], axis=0)
        kpe_all = jnp.concatenate([kpe_t, kpen_ref[...].astype(BF16)], axis=1)
        _, l_fin, acc_fin = attend((m_sc[...], l_sc[...], acc_sc[...]), ckv_all, kpe_all, key <= qry)
        lat = acc_fin / l_fin
        o_all = jnp.dot(lat.astype(BF16), wv_ref[...], preferred_element_type=F32)
        head_of_col = lax.broadcasted_iota(I32, (rows, ATTN_WIDTH), 1) // V_DIM
        head_of_row = lax.broadcasted_iota(I32, (rows, ATTN_WIDTH), 0) % N_HEADS
        o_own = jnp.where(head_of_col == head_of_row, o_all, 0.0)
        o_ref[...] = jnp.sum(o_own.reshape(4, N_HEADS, ATTN_WIDTH), axis=1)


def _sample_attention(page_table, q_s, ckv_new_pad, kpe_new_t, cache_ckv, cache_kpe_t, p):
    n_seq, n_pages = page_table.shape
    page = cache_ckv.shape[2]
    n_kb = n_pages // PAGES_PER_STEP
    hw = N_HEADS * HEAD_PAD
    assert n_pages % PAGES_PER_STEP == 0
    per_seq3 = lambda b, kb, pt: (b, 0, 0)
    const = lambda b, kb, pt: (0, 0)
    in_specs = [pl.BlockSpec((None, 4, hw), per_seq3),
                pl.BlockSpec((hw, 2 * LANES), const),
                pl.BlockSpec((N_HEADS * QK_NOPE, KV_LORA), const),
                pl.BlockSpec((KV_LORA, ATTN_WIDTH), const),
                pl.BlockSpec((None, LANES, KV_LORA), per_seq3),
                pl.BlockSpec((None, QK_ROPE, LANES), per_seq3),
                pl.BlockSpec(memory_space=pl.ANY),
                pl.BlockSpec(memory_space=pl.ANY)]
    grid_spec = pltpu.PrefetchScalarGridSpec(
        num_scalar_prefetch=1,
        grid=(n_seq, n_kb),
        in_specs=in_specs,
        out_specs=pl.BlockSpec((None, 4, ATTN_WIDTH), per_seq3),
        scratch_shapes=[pltpu.VMEM((4 * N_HEADS, 1), F32),
                        pltpu.VMEM((4 * N_HEADS, 1), F32),
                        pltpu.VMEM((4 * N_HEADS, KV_LORA), F32),
                        pltpu.VMEM((4 * N_HEADS, 2 * LANES), BF16),
                        pltpu.VMEM((N_HEADS * QK_NOPE + 4 * N_HEADS, KV_LORA), BF16),
                        pltpu.VMEM((2, PAGES_PER_STEP, page, KV_LORA), F32),
                        pltpu.VMEM((2, PAGES_PER_STEP, QK_ROPE, page), F32),
                        pltpu.SemaphoreType.DMA((2, 2))],
    )
    return pl.pallas_call(
        _sattn_kernel,
        grid_spec=grid_spec,
        out_shape=jax.ShapeDtypeStruct((n_seq, 4, ATTN_WIDTH), F32),
        compiler_params=_cparams(("arbitrary", "arbitrary"), VMEM_LIMIT),
        name="decode_attention",
    )(page_table.reshape(-1), q_s, p["w_abs"], p["w_kt"], p["w_v"], ckv_new_pad, kpe_new_t, cache_ckv, cache_kpe_t)


CONV_HALO = 32
CONV_ROWS = 64


def _conv_kernel(halo_ref, u_ref, w_ref, b_ref, o_ref, ext_sc, sh_sc):
    i = pl.program_id(0)
    tm = u_ref.shape[0]
    ext_sc[pl.ds(0, CONV_HALO), :] = jnp.where(i == 0, 0.0, halo_ref[...])
    ext_sc[pl.ds(CONV_HALO, tm), :] = u_ref[...]
    first = CONV_HALO - (CONV_W - 1)
    span = tm + CONV_HALO - SUBLANES
    for sft in range(1, SUBLANES):
        sh_sc[sft - 1] = ext_sc[pl.ds(sft, span), :]
    for rc in range(tm // CONV_ROWS):
        acc = jnp.broadcast_to(b_ref[...], (CONV_ROWS, CONV_CH))
        for j in range(CONV_W):
            a8, sft = divmod(first + j, SUBLANES)
            src = ext_sc if sft == 0 else sh_sc.at[sft - 1]
            acc = acc + src[pl.ds(rc * CONV_ROWS + SUBLANES * a8, CONV_ROWS), :] * w_ref[j:j + 1, :]
        o_ref[pl.ds(rc * CONV_ROWS, CONV_ROWS), :] = acc


def _prompt_conv(u_all, w_dw, b_dw):
    n = u_all.shape[0]
    tm = TOK_TILE
    per = tm // CONV_HALO
    return pl.pallas_call(
        _conv_kernel,
        grid=(n // tm,),
        in_specs=[pl.BlockSpec((CONV_HALO, CONV_CH), lambda i: (jnp.maximum(i * per - 1, 0), 0)),
                  pl.BlockSpec((tm, CONV_CH), lambda i: (i, 0)),
                  pl.BlockSpec((CONV_W, CONV_CH), lambda i: (0, 0)),
                  pl.BlockSpec((1, CONV_CH), lambda i: (0, 0))],
        out_specs=pl.BlockSpec((tm, CONV_CH), lambda i: (i, 0)),
        out_shape=jax.ShapeDtypeStruct((n, CONV_CH), F32),
        scratch_shapes=[pltpu.VMEM((CONV_HALO + tm, CONV_CH), F32),
                        pltpu.VMEM((SUBLANES - 1, CONV_HALO + tm - SUBLANES, CONV_CH), F32)],
        compiler_params=_cparams(("parallel",)),
        name="prompt_conv",
    )(u_all, u_all, w_dw, b_dw)


def _sconv_kernel(u_ref, wt_ref, b_ref, o_ref):
    u = u_ref[...]
    for t in range(o_ref.shape[0]):
        o_ref[t] = jnp.sum(u * wt_ref[t][None, :, :], axis=1) + b_ref[...]


def _sample_conv(u_ext, w_taps, b_dw):
    n_seq, ext, _ = u_ext.shape
    t_new = w_taps.shape[0]
    sb = 8
    return pl.pallas_call(
        _sconv_kernel,
        grid=(n_seq // sb,),
        in_specs=[pl.BlockSpec((sb, ext, CONV_CH), lambda i: (i, 0, 0)),
                  pl.BlockSpec((t_new, ext, CONV_CH), lambda i: (0, 0, 0)),
                  pl.BlockSpec((1, CONV_CH), lambda i: (0, 0))],
        out_specs=pl.BlockSpec((t_new, sb, CONV_CH), lambda i: (0, i, 0)),
        out_shape=jax.ShapeDtypeStruct((t_new, n_seq, CONV_CH), F32),
        compiler_params=_cparams(("parallel",)),
        name="decode_conv",
    )(u_ext, w_taps, b_dw)


def _lane_pack(cols, dtype):
    lane = lax.broadcasted_iota(I32, (cols[0].shape[0], LANES), 1)
    out = jnp.zeros((cols[0].shape[0], LANES), dtype)
    for j, c in enumerate(cols):
        out = jnp.where(lane == j, c.astype(dtype), out)
    return out


def _merge_kernel(n_prompt_tiles, xp_ref, xs_ref, ap_ref, as_ref, cp_ref, cs_ref, gate_ref, shf_ref, scf_ref,
                  gln_ref, bln_ref, ga_ref, gc_ref, wout_ref, gffn_ref, wrh_ref, wrl_ref, br_ref,
                  y_out, f_out, ti_out, tw_out):
    yc = _pick_rows(n_prompt_tiles, cp_ref, cs_ref)
    mu = jnp.mean(yc, axis=-1, keepdims=True)
    xc = yc - mu
    var = jnp.mean(xc * xc, axis=-1, keepdims=True)
    ln = xc * lax.rsqrt(var + EPS) * gln_ref[...] + bln_ref[...]
    conv = ln * jax.nn.sigmoid(ln)
    attn = _pick_rows(n_prompt_tiles, ap_ref, as_ref)
    a_n = attn * _rsqrt_mean(attn, ATTN_WIDTH) * ga_ref[...]
    c_n = conv * _rsqrt_mean(conv, CONV_CH) * gc_ref[...]
    m = (jnp.dot(a_n.astype(BF16), wout_ref[:ATTN_WIDTH, :], preferred_element_type=F32)
         + jnp.dot(c_n.astype(BF16), wout_ref[ATTN_WIDTH:, :], preferred_element_type=F32))
    y = _pick_rows(n_prompt_tiles, xp_ref, xs_ref) + gate_ref[...] * m
    y_out[...] = y
    f = y * _rsqrt_mean(y, D_MODEL) * gffn_ref[...]
    f = f * (1.0 + scf_ref[...]) + shf_ref[...]
    f_words = _pack_rows(f)
    for j in range(SC_PIECES):
        f_out[j] = f_words[:, j * SC_ROW:(j + 1) * SC_ROW]
    f_hi = f.astype(BF16)
    f_lo = (f - f_hi.astype(F32)).astype(BF16)
    logits = (jnp.dot(f_hi, wrh_ref[...], preferred_element_type=F32)
              + jnp.dot(f_hi, wrl_ref[...], preferred_element_type=F32)
              + jnp.dot(f_lo, wrh_ref[...], preferred_element_type=F32)) + br_ref[...]
    lane = lax.broadcasted_iota(I32, logits.shape, 1)
    vals, idxs = [], []
    for _ in range(TOP_K):
        mx = jnp.max(logits, axis=-1, keepdims=True)
        ix = jnp.min(jnp.where(logits == mx, lane, LANES), axis=-1, keepdims=True)
        vals.append(mx)
        idxs.append(ix)
        logits = jnp.where(lane == ix, NEG_INF * 4.0, logits)
    exps = [jnp.exp(v - vals[0]) for v in vals]
    tot = exps[0] + exps[1] + exps[2] + exps[3]
    ti_out[...] = _lane_pack(idxs, I32)
    tw_out[...] = _lane_pack([e / tot for e in exps], F32)


def _merge_router(x_p, x_s, attn_p, attn_s, conv_p, conv_s, gate, shf, scf, p):
    n_prompt_tiles = x_p.shape[0] // TOK_TILE
    n = x_p.shape[0] + x_s.shape[0]
    tm = TOK_TILE
    const = lambda i: (0, 0)
    row = lambda i: (i, 0)
    mod = _mod_index(n_prompt_tiles)
    from_p = _prompt_index(n_prompt_tiles)
    from_s = _decode_index(n_prompt_tiles)
    return pl.pallas_call(
        functools.partial(_merge_kernel, n_prompt_tiles),
        grid=(n // tm,),
        in_specs=[pl.BlockSpec((tm, D_MODEL), from_p),
                  pl.BlockSpec((tm, D_MODEL), from_s),
                  pl.BlockSpec((tm, ATTN_WIDTH), from_p),
                  pl.BlockSpec((tm, ATTN_WIDTH), from_s),
                  pl.BlockSpec((tm, CONV_CH), from_p),
                  pl.BlockSpec((tm, CONV_CH), from_s),
                  pl.BlockSpec((tm, D_MODEL), mod),
                  pl.BlockSpec((tm, D_MODEL), mod),
                  pl.BlockSpec((tm, D_MODEL), mod),
                  pl.BlockSpec((1, CONV_CH), const),
                  pl.BlockSpec((1, CONV_CH), const),
                  pl.BlockSpec((1, ATTN_WIDTH), const),
                  pl.BlockSpec((1, CONV_CH), const),
                  pl.BlockSpec((D_MODEL, D_MODEL), const),
                  pl.BlockSpec((1, D_MODEL), const),
                  pl.BlockSpec((D_MODEL, LANES), const),
                  pl.BlockSpec((D_MODEL, LANES), const),
                  pl.BlockSpec((1, LANES), const)],
        out_specs=[pl.BlockSpec((tm, D_MODEL), row),
                   pl.BlockSpec((SC_PIECES, tm, SC_ROW), lambda i: (0, i, 0)),
                   pl.BlockSpec((tm, LANES), row),
                   pl.BlockSpec((tm, LANES), row)],
        out_shape=[jax.ShapeDtypeStruct((n, D_MODEL), F32),
                   jax.ShapeDtypeStruct((SC_PIECES, n, SC_ROW), I32),
                   jax.ShapeDtypeStruct((n, LANES), I32),
                   jax.ShapeDtypeStruct((n, LANES), F32)],
        compiler_params=_cparams(("parallel",), VMEM_LIMIT),
        name="merge_router",
    )(x_p, x_s, attn_p, attn_s, conv_p, conv_s, gate, shf, scf, p["g_conv_ln"], p["b_conv_ln"], p["g_out_attn"],
      p["g_out_conv"], p["w_out"], p["g_norm_ffn"], p["w_router_hi"], p["w_router_lo"], p["b_router"])


def _select_lane(table, idx_col, lane):
    return jnp.sum(jnp.where(lane == idx_col, table, 0.0), axis=-1, keepdims=True)


def _rank_kernel(ti_ref, rk_out, cnt_out, carry_sc):
    i = pl.program_id(0)

    @pl.when(i == 0)
    def _():
        carry_sc[...] = jnp.zeros(carry_sc.shape, F32)

    ti = ti_ref[...]
    tn = ti.shape[0]
    lane = lax.broadcasted_iota(I32, (tn, LANES), 1)
    sel = jnp.zeros((tn, LANES), F32)
    for k in range(TOP_K):
        sel = sel + (lane == ti[:, k:k + 1]).astype(F32)
    r_i = lax.broadcasted_iota(I32, (tn, tn), 0)
    c_i = lax.broadcasted_iota(I32, (tn, tn), 1)
    below = (c_i < r_i).astype(BF16)
    rank = carry_sc[...] + jnp.dot(below, sel.astype(BF16), preferred_element_type=F32)
    rk_out[...] = _lane_pack([_select_lane(rank, ti[:, k:k + 1], lane) for k in range(TOP_K)], F32)
    carry_sc[...] = carry_sc[...] + jnp.sum(sel, axis=0, keepdims=True)
    cnt_out[...] = jnp.broadcast_to(carry_sc[...], cnt_out.shape)


def _pos_kernel(cnt_ref, ti_ref, rk_ref, pos_out, meta_out):
    cnt = cnt_ref[...]
    padded = jnp.ceil(cnt * (1.0 / FFN_TILE)) * FFN_TILE
    r_i = lax.broadcasted_iota(I32, (LANES, LANES), 0)
    c_i = lax.broadcasted_iota(I32, (LANES, LANES), 1)
    before = (r_i < c_i).astype(F32)
    offs = jnp.dot(padded, before, precision=HIGHEST, preferred_element_type=F32)
    ends = offs + padded
    ti = ti_ref[...]
    tn = ti.shape[0]
    lane = lax.broadcasted_iota(I32, (tn, LANES), 1)
    off_row = offs[0:1, :]
    rk = rk_ref[...]
    pos = [_select_lane(jnp.broadcast_to(off_row, (tn, LANES)), ti[:, k:k + 1], lane) + rk[:, k:k + 1]
           for k in range(TOP_K)]
    pos_out[...] = _lane_pack(pos, F32).astype(I32)

    @pl.when(pl.program_id(0) == 0)
    def _():
        nt = meta_out.shape[0]
        start = (lax.broadcasted_iota(I32, (nt, LANES), 0) * FFN_TILE).astype(F32)
        elane = lax.broadcasted_iota(I32, (nt, LANES), 1)
        done = jnp.where((elane < N_EXPERTS) & (jnp.broadcast_to(ends[0:1, :], (nt, LANES)) <= start), 1.0, 0.0)
        expert = jnp.minimum(jnp.sum(done, axis=-1, keepdims=True), N_EXPERTS - 1.0)
        total = jnp.sum(jnp.where(elane < N_EXPERTS, jnp.broadcast_to(padded[0:1, :], (nt, LANES)), 0.0),
                        axis=-1, keepdims=True)
        filled = jnp.broadcast_to((offs + cnt)[0:1, :], (nt, LANES))
        end_e = jnp.sum(jnp.where(elane == expert.astype(I32), filled, 0.0), axis=-1, keepdims=True)
        rows = jnp.clip(end_e - start[:, 0:1], 0.0, float(FFN_TILE))
        meta_out[...] = _lane_pack([expert, total * (1.0 / FFN_TILE), rows], F32).astype(I32)


def _route(top_i, n_tiles_max):
    n = top_i.shape[0]
    tn = ROUTE_TILE
    rk, cnt = pl.pallas_call(
        _rank_kernel,
        grid=(n // tn,),
        in_specs=[pl.BlockSpec((tn, LANES), lambda i: (i, 0))],
        out_specs=[pl.BlockSpec((tn, LANES), lambda i: (i, 0)),
                   pl.BlockSpec((8, LANES), lambda i: (0, 0))],
        out_shape=[jax.ShapeDtypeStruct((n, LANES), F32),
                   jax.ShapeDtypeStruct((8, LANES), F32)],
        scratch_shapes=[pltpu.VMEM((1, LANES), F32)],
        compiler_params=_cparams(("arbitrary",)),
        name="route_rank",
    )(top_i)
    nt_pad = -(-n_tiles_max // 8) * 8
    pos, meta = pl.pallas_call(
        _pos_kernel,
        grid=(n // tn,),
        in_specs=[pl.BlockSpec((8, LANES), lambda i: (0, 0)),
                  pl.BlockSpec((tn, LANES), lambda i: (i, 0)),
                  pl.BlockSpec((tn, LANES), lambda i: (i, 0))],
        out_specs=[pl.BlockSpec((tn, LANES), lambda i: (i, 0)),
                   pl.BlockSpec((nt_pad, LANES), lambda i: (0, 0))],
        out_shape=[jax.ShapeDtypeStruct((n, LANES), I32),
                   jax.ShapeDtypeStruct((nt_pad, LANES), I32)],
        compiler_params=_cparams(("arbitrary",)),
        name="route_pos",
    )(cnt, top_i, rk)
    return pos, meta


def _sc_scatter_rows(x, idx, n_out):
    n_src = x.shape[0]
    n = idx.shape[0]
    n_src_blk = n_src // SC_WINDOW
    mesh = plsc.VectorSubcoreMesh(core_axis_name="c", subcore_axis_name="s")

    @pl.kernel(out_type=jax.ShapeDtypeStruct((n_out, SC_ROW), x.dtype), mesh=mesh)
    def k(x_hbm, i_hbm, o_hbm):
        def body(x_vmem, i_vmem):
            pltpu.sync_copy(x_vmem, o_hbm.at[i_vmem.at[0]])

        pltpu.emit_pipeline(
            body,
            grid=(n // SC_WINDOW,),
            in_specs=[pl.BlockSpec((SC_WINDOW, SC_ROW), index_map=lambda i: (i % n_src_blk, 0)),
                      pl.BlockSpec((1, SC_WINDOW), index_map=lambda i: (0, i))],
            out_specs=[],
            core_axis_name=("c", "s"),
            dimension_semantics=(pltpu.PARALLEL,),
        )(x_hbm, i_hbm)

    return k(x, idx.reshape(1, n))


def _sc_gather_rows(x, idx):
    n = idx.shape[0]
    mesh = plsc.VectorSubcoreMesh(core_axis_name="c", subcore_axis_name="s")

    @pl.kernel(out_type=jax.ShapeDtypeStruct((n, SC_ROW), x.dtype), mesh=mesh)
    def k(x_hbm, i_hbm, o_hbm):
        def body(i_vmem, o_vmem):
            pltpu.sync_copy(x_hbm.at[i_vmem.at[0]], o_vmem)

        pltpu.emit_pipeline(
            body,
            grid=(n // SC_WINDOW,),
            in_specs=[pl.BlockSpec((1, SC_WINDOW), index_map=lambda i: (0, i))],
            out_specs=[pl.BlockSpec((SC_WINDOW, SC_ROW), index_map=lambda i: (i, 0))],
            core_axis_name=("c", "s"),
            dimension_semantics=(pltpu.PARALLEL,),
        )(i_hbm, o_hbm)

    return k(x, idx.reshape(1, n))


def _ffn_kernel(te_ref, nv_ref, nr_ref, x_ref, wu_ref, bu_ref, wd_ref, bd_ref, o_ref, wu_sc, wd_sc):
    t = pl.program_id(0)
    valid = t < nv_ref[0]
    prev = te_ref[jnp.maximum(t - 1, 0)]
    fresh = jnp.logical_or(t == 0, te_ref[t] != prev)

    @pl.when(jnp.logical_and(valid, fresh))
    def _():
        wu_sc[...] = wu_ref[...].astype(BF16)
        wd_sc[...] = wd_ref[...].astype(BF16)

    def experts(n_rows):
        rows = pl.ds(0, n_rows)
        x = _unpack_rows(jnp.concatenate([x_ref[j, rows, :] for j in range(SC_PIECES)], axis=1))
        z = jnp.dot(x.astype(BF16), wu_sc[...], preferred_element_type=F32) + bu_ref[...]
        zg = jnp.minimum(z[:, :D_FF], SWIGLU_LIMIT)
        zl = jnp.clip(z[:, D_FF:], -SWIGLU_LIMIT, SWIGLU_LIMIT)
        act = zg * jax.nn.sigmoid(SWIGLU_ALPHA * zg) * (zl + 1.0)
        out = _pack_rows(jnp.dot(act.astype(BF16), wd_sc[...], preferred_element_type=F32) + bd_ref[...])
        for j in range(SC_PIECES):
            o_ref[j, rows, :] = out[:, j * SC_ROW:(j + 1) * SC_ROW]

    half = FFN_TILE // 2

    @pl.when(jnp.logical_and(valid, nr_ref[t] > half))
    def _():
        experts(FFN_TILE)

    @pl.when(jnp.logical_and(valid, nr_ref[t] <= half))
    def _():
        experts(half)


def _expert_ffn(tile_expert, n_valid, tile_rows, x_sorted, w_up, b_up, w_down, b_down):
    n_slots = x_sorted.shape[1]
    n_tiles = n_slots // FFN_TILE
    xmap = lambda t, te, nv, nr: (0, jnp.minimum(t, nv[0] - 1), 0)
    emap = lambda t, te, nv, nr: (te[t], 0, 0)
    grid_spec = pltpu.PrefetchScalarGridSpec(
        num_scalar_prefetch=3,
        grid=(n_tiles,),
        in_specs=[pl.BlockSpec((SC_PIECES, FFN_TILE, SC_ROW), xmap),
                  pl.BlockSpec((None, D_MODEL, 2 * D_FF), emap),
                  pl.BlockSpec((None, 1, 2 * D_FF), emap),
                  pl.BlockSpec((None, D_FF, D_MODEL), emap),
                  pl.BlockSpec((None, 1, D_MODEL), emap)],
        out_specs=pl.BlockSpec((SC_PIECES, FFN_TILE, SC_ROW), xmap),
        scratch_shapes=[pltpu.VMEM((D_MODEL, 2 * D_FF), BF16),
                        pltpu.VMEM((D_FF, D_MODEL), BF16)],
    )
    return pl.pallas_call(
        _ffn_kernel,
        grid_spec=grid_spec,
        out_shape=jax.ShapeDtypeStruct((SC_PIECES, n_slots, SC_ROW), I32),
        compiler_params=_cparams(("arbitrary",), VMEM_LIMIT),
        name="expert_ffn",
    )(tile_expert, n_valid, tile_rows, x_sorted, w_up, b_up.reshape(N_EXPERTS, 1, 2 * D_FF), w_down,
      b_down.reshape(N_EXPERTS, 1, D_MODEL))


def _combine_kernel(n_prompt_tiles, y_ref, g_ref, tw_ref, gate_ref, op_ref, os_ref):
    tw = tw_ref[...]
    rows = lambda k: _unpack_rows(jnp.concatenate([g_ref[k, j] for j in range(SC_PIECES)], axis=1))
    moe = rows(0) * tw[:, 0:1]
    for k in range(1, TOP_K):
        moe = moe + rows(k) * tw[:, k:k + 1]
    out = y_ref[...] + gate_ref[...] * moe
    i = pl.program_id(0)

    @pl.when(i < n_prompt_tiles)
    def _():
        op_ref[...] = out

    @pl.when(i >= n_prompt_tiles)
    def _():
        os_ref[...] = out


def _combine(y_all, gathered, top_w, gate, n_p):
    n = y_all.shape[0]
    tm = TOK_TILE
    n_prompt_tiles = n_p // tm
    row = lambda i: (i, 0)
    return pl.pallas_call(
        functools.partial(_combine_kernel, n_prompt_tiles),
        grid=(n // tm,),
        in_specs=[pl.BlockSpec((tm, D_MODEL), row),
                  pl.BlockSpec((TOP_K, SC_PIECES, tm, SC_ROW), lambda i: (0, 0, i, 0)),
                  pl.BlockSpec((tm, LANES), row),
                  pl.BlockSpec((tm, D_MODEL), _mod_index(n_prompt_tiles))],
        out_specs=[pl.BlockSpec((tm, D_MODEL), _prompt_index(n_prompt_tiles)),
                   pl.BlockSpec((tm, D_MODEL), _decode_index(n_prompt_tiles))],
        out_shape=[jax.ShapeDtypeStruct((n_p, D_MODEL), F32),
                   jax.ShapeDtypeStruct((n - n_p, D_MODEL), F32)],
        compiler_params=_cparams(("arbitrary",), VMEM_LIMIT),
        name="moe_combine",
    )(y_all, gathered, top_w, gate)


def _head_tiles(nope, rope):
    pad = jnp.zeros(nope.shape[:-1] + (HEAD_PAD - QK_NOPE - QK_ROPE,), nope.dtype)
    t = jnp.concatenate([nope, rope, pad], axis=-1)
    return t.reshape(t.shape[:-2] + (N_HEADS * HEAD_PAD,))


def _prepare(pos, w_in, g_norm_mix, g_q_lat, w_q_up, g_q_nope, g_q_rope, g_kv_lat, g_k_rope, w_kv_up, g_k_nope,
             g_conv_ln, b_conv_ln, g_out_attn, g_out_conv, w_out, g_norm_ffn, w_router, b_router):
    z32 = jnp.zeros((HEAD_PAD - QK_NOPE - QK_ROPE,), F32)
    z64 = jnp.zeros((QK_NOPE,), F32)
    d = D_MODEL
    kpe_cols = w_in[:, Q_LORA + KV_LORA:Q_LORA + KV_LORA + QK_ROPE]
    kpe_tile = jnp.concatenate([jnp.zeros((d, QK_NOPE), F32), kpe_cols, jnp.zeros((d, z32.shape[0]), F32)], axis=1)
    w_in_r = jnp.concatenate([w_in[:, :Q_LORA + KV_LORA], kpe_tile, w_in[:, Q_LORA + KV_LORA + QK_ROPE:]], axis=1)
    wq = w_q_up.reshape(Q_LORA, N_HEADS, QK_NOPE + QK_ROPE)
    w_q_r = _head_tiles(wq[..., :QK_NOPE], wq[..., QK_NOPE:])
    wk = w_kv_up[..., :QK_NOPE]
    wv = w_kv_up[..., QK_NOPE:].reshape(KV_LORA, ATTN_WIDTH)
    w_k_r = _head_tiles(wk, jnp.zeros((KV_LORA, N_HEADS, QK_ROPE), F32))
    lane = np.arange(LANES)
    grp = np.where(lane < QK_NOPE, 0, np.where(lane < QK_NOPE + QK_ROPE, 1, 2))
    m_grp = ((grp[:, None] == grp[None, :]) & (grp[:, None] < 2)).astype(np.float32)
    m_grp = m_grp / np.where(grp < 1, QK_NOPE, QK_ROPE)[None, :]
    inv = np.float32(ROPE_THETA) ** (-np.arange(0, QK_ROPE, 2, dtype=np.float32) / np.float32(QK_ROPE))
    ang = pos.astype(np.float32)[:, None] * inv[None, :].astype(np.float32)
    cs, sn = jnp.asarray(np.cos(ang), F32), jnp.asarray(np.sin(ang), F32)
    n = pos.shape[0]
    cos_t = jnp.concatenate([jnp.ones((n, QK_NOPE), F32), cs, cs, jnp.zeros((n, z32.shape[0]), F32)], axis=1)
    sin_t = jnp.concatenate([jnp.zeros((n, QK_NOPE), F32), -sn, sn, jnp.zeros((n, z32.shape[0]), F32)], axis=1)
    wk_g = wk * g_k_nope[None, None, :]
    absorb = jnp.zeros((N_HEADS, HEAD_PAD, 2 * LANES), F32)
    absorb = absorb.at[:, :QK_NOPE, :KV_LORA].set(jnp.transpose(wk_g, (1, 2, 0)))
    absorb = absorb.at[:, ROPE_LO:ROPE_LO + QK_ROPE, KV_LORA:KV_LORA + QK_ROPE].set(
        jnp.broadcast_to(jnp.eye(QK_ROPE, dtype=F32), (N_HEADS, QK_ROPE, QK_ROPE)))
    w_kt = jnp.transpose(wk, (1, 2, 0)).reshape(N_HEADS * QK_NOPE, KV_LORA)
    wr = jnp.concatenate([w_router, jnp.zeros((d, LANES - N_EXPERTS), F32)], axis=1)
    wr_hi = wr.astype(BF16)
    wr_lo = (wr - wr_hi.astype(F32)).astype(BF16)
    br = jnp.concatenate([b_router, jnp.full((LANES - N_EXPERTS,), NEG_INF, F32)])
    wv_h = w_kv_up[..., QK_NOPE:]
    w_v_tiles = jnp.concatenate([wv_h, jnp.zeros((KV_LORA, N_HEADS, HEAD_PAD - V_DIM), F32)], axis=-1)
    v_ones = (jnp.arange(N_HEADS * HEAD_PAD) % HEAD_PAD == V_DIM).astype(F32).reshape(1, N_HEADS * HEAD_PAD)
    return {
        "w_v_tiles": w_v_tiles.reshape(KV_LORA, N_HEADS * HEAD_PAD).astype(BF16), "v_ones": v_ones,
        "w_router_hi": wr_hi, "w_router_lo": wr_lo,
        "w_in": w_in_r.astype(BF16), "g_norm_mix": g_norm_mix.reshape(1, d), "g_q_lat": g_q_lat.reshape(1, Q_LORA),
        "w_q_up": w_q_r.astype(BF16),
        "gain_q": jnp.concatenate([g_q_nope, g_q_rope, z32]).reshape(1, LANES),
        "m_grp": jnp.asarray(m_grp, BF16), "cos_t": cos_t, "sin_t": sin_t,
        "g_kv_lat": g_kv_lat.reshape(1, KV_LORA),
        "gain_k": jnp.concatenate([g_k_nope, z64]).reshape(1, LANES),
        "gain_kpe": jnp.concatenate([z64, g_k_rope, z32]).reshape(1, LANES),
        "w_k": w_k_r.astype(BF16), "w_v": wv.astype(BF16),
        "w_abs": absorb.reshape(N_HEADS * HEAD_PAD, 2 * LANES).astype(BF16), "w_kt": w_kt.astype(BF16),
        "g_conv_ln": g_conv_ln.reshape(1, CONV_CH), "b_conv_ln": b_conv_ln.reshape(1, CONV_CH),
        "g_out_attn": g_out_attn.reshape(1, ATTN_WIDTH), "g_out_conv": g_out_conv.reshape(1, CONV_CH),
        "w_out": w_out.astype(BF16), "g_norm_ffn": g_norm_ffn.reshape(1, d),
        "b_router": br.reshape(1, LANES),
    }


def _mod_table(mod_p, mod_s, t_new):
    return jnp.concatenate([jnp.broadcast_to(mod_p, (TOK_TILE, D_MODEL)), jnp.repeat(mod_s, t_new, axis=0)], axis=0)


def kernel(x_prompt, x_sample, cache_ckv, cache_kpe, state_conv, page_table, c_prompt, c_sample, w_ada, b_ada, g_norm_mix, g_norm_ffn, w_in, g_q_lat, w_q_up, g_q_nope, g_q_rope, g_kv_lat, g_k_rope, w_kv_up, g_k_nope, w_dw, b_dw, g_conv_ln, b_conv_ln, g_out_attn, g_out_conv, w_out, w_router, b_router, w_exp_up, b_exp_up, w_exp_down, b_exp_down):
    bsz, seq, d = x_prompt.shape
    n_seq, t_new = x_sample.shape[:2]
    depth = w_ada.shape[0]
    assert bsz == 1 and depth == 1 and t_new == 4 and d == D_MODEL
    n_p = bsz * seq
    n_s = n_seq * t_new
    n = n_p + n_s
    past = page_table.shape[1] * cache_ckv.shape[2]
    l = 0

    pos = np.concatenate([np.arange(seq), np.tile(past + np.arange(t_new), n_seq)])
    p = _prepare(pos, w_in[l], g_norm_mix[l], g_q_lat[l], w_q_up[l], g_q_nope[l], g_q_rope[l], g_kv_lat[l],
                 g_k_rope[l], w_kv_up[l], g_k_nope[l], g_conv_ln[l], b_conv_ln[l], g_out_attn[l], g_out_conv[l],
                 w_out[l], g_norm_ffn[l], w_router[l], b_router[l])

    n_c = 1 + n_seq
    c_all = jnp.concatenate([c_prompt, c_sample, jnp.zeros((-n_c % 8, d), F32)], axis=0)
    mod = _adaln(c_all, w_ada[l], b_ada[l])
    tabs = [_mod_table(mod[0:1, j * d:(j + 1) * d], mod[1:n_c, j * d:(j + 1) * d], t_new) for j in range(6)]
    sh_m, sc_m, gt_m, sh_f, sc_f, gt_f = tabs

    x_p = x_prompt.reshape(n_p, d)
    x_s = x_sample.reshape(n_s, d)
    q_all, ckv_p, ckv_s, kpe_tp, kpe_ts, u_all, k_all, v_all = _mixer_inputs(x_p, x_s, sh_m, sc_m, p)

    attn_p = _prompt_attention(q_all, k_all, v_all, seq)
    ckv_s = ckv_s.reshape(n_seq, t_new, KV_LORA)
    kpe_s = jnp.transpose(kpe_ts).reshape(n_seq, t_new, QK_ROPE)
    ckv_new_pad = jnp.pad(ckv_s, ((0, 0), (0, LANES - t_new), (0, 0)))
    kpe_new_t = jnp.swapaxes(jnp.pad(kpe_s, ((0, 0), (0, LANES - t_new), (0, 0))), 1, 2)
    q_s = q_all[n_p:].reshape(n_seq, t_new, N_HEADS * HEAD_PAD)
    attn_s = _sample_attention(page_table, q_s, ckv_new_pad, kpe_new_t, cache_ckv,
                               jnp.swapaxes(cache_kpe, 2, 3), p)

    conv_p = _prompt_conv(u_all, w_dw[l], b_dw[l].reshape(1, CONV_CH))
    u_s = u_all[n_p:].reshape(n_seq, t_new, CONV_CH)
    u_ext_s = jnp.concatenate([state_conv[l], u_s], axis=1)
    ext = CONV_W - 1 + t_new
    ext_pad = -ext % SUBLANES
    w_taps = jnp.stack([jnp.pad(w_dw[l], ((t, t_new - 1 - t + ext_pad), (0, 0))) for t in range(t_new)])
    conv_s = _sample_conv(jnp.pad(u_ext_s, ((0, 0), (0, ext_pad), (0, 0))), w_taps, b_dw[l].reshape(1, CONV_CH))
    conv_s = jnp.transpose(conv_s, (1, 0, 2)).reshape(n_s, CONV_CH)

    y_all, f_all, top_i, top_w = _merge_router(x_p, x_s, attn_p, attn_s.reshape(n_s, ATTN_WIDTH), conv_p, conv_s,
                                               gt_m, sh_f, sc_f, p)

    n_tiles = -(-(n * TOP_K + N_EXPERTS * (FFN_TILE - 1)) // FFN_TILE)
    n_slots = n_tiles * FFN_TILE
    pos_tok, meta = _route(top_i, n_tiles)
    slot = jnp.transpose(pos_tok[:, :TOP_K])
    piece_idx = (slot[:, None, :] + (jnp.arange(SC_PIECES, dtype=I32) * n_slots)[None, :, None]).reshape(-1)
    x_sorted = _sc_scatter_rows(f_all.reshape(SC_PIECES * n, SC_ROW), piece_idx, SC_PIECES * n_slots)
    h_sorted = _expert_ffn(meta[:n_tiles, 0], meta[0:1, 1], meta[:n_tiles, 2],
                           x_sorted.reshape(SC_PIECES, n_slots, SC_ROW),
                           w_exp_up[l], b_exp_up[l], w_exp_down[l], b_exp_down[l])
    gathered = _sc_gather_rows(h_sorted.reshape(SC_PIECES * n_slots, SC_ROW), piece_idx)
    y_p, y_s = _combine(y_all, gathered.reshape(TOP_K, SC_PIECES, n, SC_ROW), top_w, gt_f, n_p)
    y_p = y_p.reshape(bsz, seq, d)
    y_s = y_s.reshape(n_seq, t_new, d)
    ckv_prompt = ckv_p.reshape(1, bsz, seq, KV_LORA)
    kpe_prompt = jnp.swapaxes(kpe_tp.reshape(1, bsz, QK_ROPE, seq), 2, 3)
    conv_prompt = u_all[n_p - (CONV_W - 1):n_p].reshape(1, bsz, CONV_W - 1, CONV_CH)
    ckv_sample = ckv_s[None]
    kpe_sample = kpe_s[None]
    conv_sample = u_ext_s[:, t_new:][None]
    return (y_p, y_s, ckv_prompt, kpe_prompt, conv_prompt, ckv_sample, kpe_sample, conv_sample)
```
